```python
import numpy as np
import jax
import jax.numpy as jnp
from jax import lax

D_MODEL = 2048
BATCH = 4
SEQ = 4096
DEPTH = 4

F32 = jnp.float32
HEAD_DIM = 64
D_MIX = D_MODEL
D_RWKV = (3 * D_MIX) // 8
D_POOL = D_MIX // 4
D_NSA = D_MIX - D_RWKV - D_POOL
RW_HEADS = D_RWKV // HEAD_DIM
RW_DECAY_LORA = 64
RW_A_LORA = 64
RW_GATE_LORA = 128
RW_GN_EPS = 64e-5
RW_COLS = 3 * D_RWKV + RW_DECAY_LORA + RW_A_LORA + RW_GATE_LORA
RW_SPLITS = (D_RWKV, 2 * D_RWKV, 3 * D_RWKV, 3 * D_RWKV + RW_DECAY_LORA,
             3 * D_RWKV + RW_DECAY_LORA + RW_A_LORA)
POOL_WINDOWS = (2, 4, 8, 16)
POOL_GROUP = D_POOL // 4
NSA_HEADS = D_NSA // HEAD_DIM
NSA_KV_HEADS = 3
NSA_GQA = NSA_HEADS // NSA_KV_HEADS
NSA_KV = NSA_KV_HEADS * HEAD_DIM
NSA_COLS = D_NSA + 6 * NSA_KV + 3 * NSA_HEADS
NSA_SPLITS = (D_NSA, D_NSA + NSA_KV, D_NSA + 2 * NSA_KV, D_NSA + 3 * NSA_KV,
              D_NSA + 4 * NSA_KV, D_NSA + 5 * NSA_KV, D_NSA + 6 * NSA_KV)
CMP_BLOCK = 32
CMP_STRIDE = 16
CMP_HIDDEN = 256
SEL_BLOCK = 64
SEL_TOPK = 16
FORCE_SCORE = 1e9
WINDOW = 512
Q_BLOCK = 128
SEL_Q_CHUNK = 64
ROPE_THETA = 500000.0
ROPE_DIM = HEAD_DIM // 4
D_FF = ((8 * D_MODEL // 3 + 127) // 128) * 128
IN_COLS = RW_COLS + D_POOL + NSA_COLS
ALPHA = (2 * DEPTH) ** 0.25
BETA = (8 * DEPTH) ** -0.25
LN_EPS = 1e-5

kernel_name = "hymba_rwkv7_pool_nsa_macaron_deepnorm"


def layer_norm(x, g, b):
    xf = x.astype(F32)
    mu = xf.mean(-1, keepdims=True)
    var = jnp.square(xf - mu).mean(-1, keepdims=True)
    return ((xf - mu) * lax.rsqrt(var + LN_EPS) * g + b).astype(x.dtype)


def swiglu(x, w_up, w_down):
    a, b = jnp.split(x @ w_up, 2, axis=-1)
    return (jax.nn.silu(a) * b) @ w_down


def token_shift(t):
    return jnp.pad(t[:, :-1], ((0, 0), (1, 0), (0, 0)))


def partial_rope(x, pos):
    half = ROPE_DIM // 2
    inv_freq = ROPE_THETA ** (-jnp.arange(half, dtype=F32) / half)
    ang = pos.astype(F32)[:, None] * inv_freq
    cos, sin = jnp.cos(ang)[:, None, :], jnp.sin(ang)[:, None, :]
    x1, x2, xp = x[..., :half], x[..., half:ROPE_DIM], x[..., ROPE_DIM:]
    rot = jnp.concatenate([x1 * cos - x2 * sin, x2 * cos + x1 * sin], axis=-1)
    return jnp.concatenate([rot.astype(x.dtype), xp], axis=-1)


def masked_softmax(s, mask):
    s = jnp.where(mask, s.astype(F32), -jnp.inf)
    m = jnp.max(s, axis=-1, keepdims=True)
    m = jnp.where(jnp.isfinite(m), m, 0.0)
    e = jnp.where(mask, jnp.exp(s - m), 0.0)
    return e / jnp.maximum(e.sum(-1, keepdims=True), jnp.finfo(F32).tiny)


def rwkv7_mix(p, mu, w0, w2, a0, a2, g2, k_k, k_a, r_k, gn_g, gn_b):
    B, S, _ = p.shape
    dt = p.dtype
    p = p + (token_shift(p) - p) * mu
    r, k, v, wl, al, gl = jnp.split(p, RW_SPLITS, axis=-1)
    w = -jax.nn.softplus(-(w0 + jnp.tanh(wl) @ w2).astype(F32)) - 0.5
    decay = jnp.exp(-jnp.exp(w))
    a = jax.nn.sigmoid((a0 + al @ a2).astype(F32))
    g = jax.nn.sigmoid(gl) @ g2

    def heads(t):
        return t.astype(F32).reshape(B, S, RW_HEADS, HEAD_DIM)

    kk = heads(k * k_k)
    kk = kk / jnp.maximum(jnp.linalg.norm(kk, axis=-1, keepdims=True), 1e-12)
    k_mod = k.astype(F32) * (1.0 + (a - 1.0) * k_a)
    r_h, k_h, v_h, w_h, a_h = heads(r), heads(k_mod), heads(v), heads(decay), heads(a)

    def step(state, inp):
        r_t, w_t, k_t, v_t, kk_t, a_t = inp
        sa = jnp.einsum('bhvk,bhk->bhv', state, kk_t)
        state = (state * w_t[:, :, None, :]
                 - sa[..., None] * (kk_t * a_t)[:, :, None, :]
                 + v_t[..., :, None] * k_t[..., None, :])
        return state, jnp.einsum('bhvk,bhk->bhv', state, r_t)

    xs = tuple(jnp.moveaxis(t, 1, 0) for t in (r_h, w_h, k_h, v_h, kk, a_h))
    state0 = jnp.zeros((B, RW_HEADS, HEAD_DIM, HEAD_DIM), F32)
    _, y = lax.scan(step, state0, xs)
    y = jnp.moveaxis(y, 0, 1)
    ym = y.mean(-1, keepdims=True)
    yv = jnp.square(y - ym).mean(-1, keepdims=True)
    y = ((y - ym) * lax.rsqrt(yv + RW_GN_EPS)).reshape(B, S, D_RWKV) * gn_g + gn_b
    bonus = (jnp.sum(r_h * k_h * r_k, axis=-1, keepdims=True) * v_h).reshape(B, S, D_RWKV)
    return ((y + bonus) * g).astype(dt)


def pool_mix(p, pool_w, pool_b, pool_scale):
    B, S, _ = p.shape
    n_g = len(POOL_WINDOWS)
    pf = p.astype(F32).reshape(B, S, n_g, POOL_GROUP)
    cs = jnp.cumsum(pf, axis=1)
    t1 = jnp.arange(1, S + 1, dtype=F32)
    pooled = []
    for gi, win in enumerate(POOL_WINDOWS):
        c = cs[:, :, gi]
        lagged = jnp.pad(c, ((0, 0), (win, 0), (0, 0)))[:, :S]
        count = jnp.minimum(t1, float(win))[None, :, None]
        pooled.append((c - lagged) / count - pf[:, :, gi])
    z = jnp.stack(pooled, axis=2)
    z = jnp.einsum('bsgc,gcd->bsgd', z, pool_w) + pool_b.reshape(n_g, POOL_GROUP)
    z = z * pool_scale.reshape(n_g, POOL_GROUP)
    return z.reshape(B, S, D_POOL).astype(p.dtype)


def compress(blocks, pe, w1, w2):
    h = jnp.einsum('bnlhd,ldf->bnhf', blocks + pe[:, None, :], w1)
    return jax.nn.gelu(h) @ w2


def nsa_mix(p, pos, cmp_pe_k, cmp_pe_v, cmp_k_w1, cmp_k_w2, cmp_v_w1, cmp_v_w2, gate_b):
    B, S, _ = p.shape
    dt = p.dtype
    q, kc, vc, ks, vs, kw, vw, gl = jnp.split(p, NSA_SPLITS, axis=-1)
    kv_shape = (B, S, NSA_KV_HEADS, HEAD_DIM)
    q = partial_rope(q.reshape(B, S, NSA_HEADS, HEAD_DIM), pos)
    q = q.reshape(B, S, NSA_KV_HEADS, NSA_GQA, HEAD_DIM) * (HEAD_DIM ** -0.5)
    kc, vc, vs, vw = (t.reshape(kv_shape) for t in (kc, vc, vs, vw))
    ks = partial_rope(ks.reshape(kv_shape), pos)
    kw = partial_rope(kw.reshape(kv_shape), pos)
    gates = jax.nn.sigmoid((gl + gate_b).astype(F32)).reshape(B, S, NSA_KV_HEADS, NSA_GQA, 3)

    n_cmp = (S - CMP_BLOCK) // CMP_STRIDE + 1
    cmp_start = np.arange(n_cmp) * CMP_STRIDE
    cmp_idx = cmp_start[:, None] + np.arange(CMP_BLOCK)[None, :]
    cmp_end = cmp_start + CMP_BLOCK - 1
    k_cmp = compress(kc[:, cmp_idx], cmp_pe_k, cmp_k_w1, cmp_k_w2)
    v_cmp = compress(vc[:, cmp_idx], cmp_pe_v, cmp_v_w1, cmp_v_w2)
    k_cmp = partial_rope(k_cmp, jnp.asarray(cmp_end))
    s_cmp = jnp.einsum('bshgd,bnhd->bshgn', q, k_cmp)
    mask_cmp = (jnp.asarray(cmp_end)[None, :] <= pos[:, None])[None, :, None, None, :]
    p_cmp = masked_softmax(s_cmp, mask_cmp)
    o_cmp = jnp.einsum('bshgn,bnhd->bshgd', p_cmp.astype(dt), v_cmp)

    n_sel = S // SEL_BLOCK
    k_sel = min(SEL_TOPK, n_sel)
    sel_start = np.arange(n_sel) * SEL_BLOCK
    overlap = np.clip(np.minimum(cmp_start[:, None] + CMP_BLOCK, sel_start[None, :] + SEL_BLOCK)
                      - np.maximum(cmp_start[:, None], sel_start[None, :]), 0, None) / CMP_BLOCK
    imp = jnp.einsum('bshgn,nj->bshj', p_cmp, jnp.asarray(overlap, F32))
    blk = jnp.arange(n_sel)[None, :]
    cur = (pos // SEL_BLOCK)[:, None]
    forced = (blk == 0) | (blk == cur) | (blk == cur - 1)
    score = jnp.where(forced[None, :, None, :], FORCE_SCORE, imp)
    score = jnp.where((blk <= cur)[None, :, None, :], score, -jnp.inf)
    top_val, top_idx = lax.top_k(score, k_sel)
    top_ok = top_val > -jnp.inf
    ks_blk = ks.reshape(B, n_sel, SEL_BLOCK, NSA_KV_HEADS, HEAD_DIM).transpose(0, 3, 1, 2, 4)
    vs_blk = vs.reshape(B, n_sel, SEL_BLOCK, NSA_KV_HEADS, HEAD_DIM).transpose(0, 3, 1, 2, 4)
    nc = S // SEL_Q_CHUNK
    b_ix = jnp.arange(B)[:, None, None, None]
    h_ix = jnp.arange(NSA_KV_HEADS)[None, None, :, None]
    tok = jnp.arange(SEL_BLOCK)

    def chunk_first(t):
        return jnp.moveaxis(t.reshape(B, nc, SEL_Q_CHUNK, *t.shape[2:]), 1, 0)

    def sel_chunk(args):
        qc, idx, ok, qpos = args
        kg = ks_blk[b_ix, h_ix, idx]
        vg = vs_blk[b_ix, h_ix, idx]
        kpos = idx[..., None] * SEL_BLOCK + tok
        mask = ok[..., None] & (kpos <= qpos[None, :, None, None, None])
        s = jnp.einsum('bchgd,bchkld->bchgkl', qc, kg)
        pr = masked_softmax(s.reshape(*s.shape[:4], -1), mask.reshape(*mask.shape[:3], 1, -1))
        return jnp.einsum('bchgn,bchnd->bchgd', pr.astype(dt),
                          vg.reshape(*vg.shape[:3], -1, HEAD_DIM))

    o_sel = lax.map(sel_chunk, (chunk_first(q), chunk_first(top_idx), chunk_first(top_ok),
                                pos.reshape(nc, SEL_Q_CHUNK)))
    o_sel = jnp.moveaxis(o_sel, 0, 1).reshape(B, S, NSA_KV_HEADS, NSA_GQA, HEAD_DIM)

    n_qb = S // Q_BLOCK
    span = WINDOW + Q_BLOCK
    win_idx = np.arange(n_qb)[:, None] * Q_BLOCK + np.arange(span)[None, :]
    pad = ((0, 0), (WINDOW, 0), (0, 0), (0, 0))
    kb = jnp.pad(kw, pad)[:, win_idx]
    vb = jnp.pad(vw, pad)[:, win_idx]
    kpos = (win_idx - WINDOW)[:, None, :]
    qpos = np.arange(S).reshape(n_qb, Q_BLOCK)[:, :, None]
    mask_w = (kpos >= 0) & (kpos <= qpos) & (kpos > qpos - WINDOW)
    qb = q.reshape(B, n_qb, Q_BLOCK, NSA_KV_HEADS, NSA_GQA, HEAD_DIM)
    s_w = jnp.einsum('bnqhgd,bnkhd->bnhgqk', qb, kb)
    p_w = masked_softmax(s_w, jnp.asarray(mask_w)[None, :, None, None])
    o_win = jnp.einsum('bnhgqk,bnkhd->bnqhgd', p_w.astype(dt), vb)
    o_win = o_win.reshape(B, S, NSA_KV_HEADS, NSA_GQA, HEAD_DIM)

    o = gates[..., 0:1] * o_cmp + gates[..., 1:2] * o_sel + gates[..., 2:3] * o_win
    return o.reshape(B, S, D_NSA).astype(dt)


def setup_inputs(seed: int = 0) -> dict:
    key = jax.random.key(seed)
    keys = iter(jax.random.split(key, 48))
    L = DEPTH

    def nrm(shape, scale):
        return scale * jax.random.normal(next(keys), shape, F32)

    def unif(shape, lo, hi):
        return jax.random.uniform(next(keys), shape, F32, lo, hi)

    return {
        "x": nrm((BATCH, SEQ, D_MODEL), 1.0),
        "ffn1_w_up": nrm((L, D_MODEL, 2 * D_FF), D_MODEL ** -0.5),
        "ffn1_w_down": nrm((L, D_FF, D_MODEL), BETA * D_FF ** -0.5),
        "ln1_g": 1.0 + nrm((L, D_MODEL), 0.02),
        "ln1_b": nrm((L, D_MODEL), 0.02),
        "w_in": nrm((L, D_MODEL, IN_COLS), D_MODEL ** -0.5),
        "rw_mu": unif((L, RW_COLS), 0.0, 1.0),
        "rw_w0": unif((L, D_RWKV), -6.0, -1.0),
        "rw_w2": nrm((L, RW_DECAY_LORA, D_RWKV), RW_DECAY_LORA ** -0.5),
        "rw_a0": nrm((L, D_RWKV), 0.1),
        "rw_a2": nrm((L, RW_A_LORA, D_RWKV), RW_A_LORA ** -0.5),
        "rw_g2": nrm((L, RW_GATE_LORA, D_RWKV), RW_GATE_LORA ** -0.5),
        "rw_k_k": 0.85 + nrm((L, D_RWKV), 0.02),
        "rw_k_a": 1.0 + nrm((L, D_RWKV), 0.02),
        "rw_r_k": nrm((L, RW_HEADS, HEAD_DIM), 0.1),
        "rw_gn_g": 1.0 + nrm((L, D_RWKV), 0.02),
        "rw_gn_b": nrm((L, D_RWKV), 0.02),
        "pool_w": nrm((L, len(POOL_WINDOWS), POOL_GROUP, POOL_GROUP), POOL_GROUP ** -0.5),
        "pool_b": nrm((L, D_POOL), 0.02),
        "pool_scale": 1.0 + nrm((L, D_POOL), 0.1),
        "nsa_cmp_pe_k": nrm((L, CMP_BLOCK, HEAD_DIM), 0.02),
        "nsa_cmp_pe_v": nrm((L, CMP_BLOCK, HEAD_DIM), 0.02),
        "nsa_cmp_k_w1": nrm((L, CMP_BLOCK, HEAD_DIM, CMP_HIDDEN), (CMP_BLOCK * HEAD_DIM) ** -0.5),
        "nsa_cmp_k_w2": nrm((L, CMP_HIDDEN, HEAD_DIM), CMP_HIDDEN ** -0.5),
        "nsa_cmp_v_w1": nrm((L, CMP_BLOCK, HEAD_DIM, CMP_HIDDEN), (CMP_BLOCK * HEAD_DIM) ** -0.5),
        "nsa_cmp_v_w2": nrm((L, CMP_HIDDEN, HEAD_DIM), CMP_HIDDEN ** -0.5),
        "nsa_gate_b": nrm((L, 3 * NSA_HEADS), 0.1),
        "w_out": nrm((L, D_MIX, D_MODEL), BETA * D_MIX ** -0.5),
        "ln2_g": 1.0 + nrm((L, D_MODEL), 0.02),
        "ln2_b": nrm((L, D_MODEL), 0.02),
        "ffn2_w_up": nrm((L, D_MODEL, 2 * D_FF), D_MODEL ** -0.5),
        "ffn2_w_down": nrm((L, D_FF, D_MODEL), BETA * D_FF ** -0.5),
        "ln3_g": 1.0 + nrm((L, D_MODEL), 0.02),
        "ln3_b": nrm((L, D_MODEL), 0.02),
    }


def reference(x, ffn1_w_up, ffn1_w_down, ln1_g, ln1_b, w_in, rw_mu, rw_w0, rw_w2, rw_a0,
              rw_a2, rw_g2, rw_k_k, rw_k_a, rw_r_k, rw_gn_g, rw_gn_b, pool_w, pool_b,
              pool_scale, nsa_cmp_pe_k, nsa_cmp_pe_v, nsa_cmp_k_w1, nsa_cmp_k_w2,
              nsa_cmp_v_w1, nsa_cmp_v_w2, nsa_gate_b, w_out, ln2_g, ln2_b, ffn2_w_up,
              ffn2_w_down, ln3_g, ln3_b):
    pos = jnp.arange(x.shape[1])
    for l in range(DEPTH):
        x = layer_norm(ALPHA * x + 0.5 * swiglu(x, ffn1_w_up[l], ffn1_w_down[l]), ln1_g[l], ln1_b[l])
        p = x @ w_in[l]
        p_rw, p_pool, p_nsa = jnp.split(p, (RW_COLS, RW_COLS + D_POOL), axis=-1)
        y_rw = rwkv7_mix(p_rw, rw_mu[l], rw_w0[l], rw_w2[l], rw_a0[l], rw_a2[l], rw_g2[l],
                         rw_k_k[l], rw_k_a[l], rw_r_k[l], rw_gn_g[l], rw_gn_b[l])
        y_pool = pool_mix(p_pool, pool_w[l], pool_b[l], pool_scale[l])
        y_nsa = nsa_mix(p_nsa, pos, nsa_cmp_pe_k[l], nsa_cmp_pe_v[l], nsa_cmp_k_w1[l],
                        nsa_cmp_k_w2[l], nsa_cmp_v_w1[l], nsa_cmp_v_w2[l], nsa_gate_b[l])
        y = jnp.concatenate([y_rw, y_pool, y_nsa], axis=-1) @ w_out[l]
        x = layer_norm(ALPHA * x + y, ln2_g[l], ln2_b[l])
        x = layer_norm(ALPHA * x + 0.5 * swiglu(x, ffn2_w_up[l], ffn2_w_down[l]), ln3_g[l], ln3_b[l])
    return x
```

```python
import functools

import numpy as np
import jax
import jax.numpy as jnp
from jax import lax
from jax.experimental import pallas as pl
from jax.experimental.pallas import tpu as pltpu

F32 = jnp.float32
BF16 = jnp.bfloat16

D_MODEL = 2048
DEPTH = 4
HEAD_DIM = 64
D_RWKV = 768
D_POOL = 512
D_NSA = 768
RW_HEADS = 12
RW_DECAY_LORA = 64
RW_A_LORA = 64
RW_GATE_LORA = 128
RW_GN_EPS = 64e-5
RW_COLS = 3 * D_RWKV + RW_DECAY_LORA + RW_A_LORA + RW_GATE_LORA
POOL_WINDOWS = (2, 4, 8, 16)
POOL_GROUP = 128
NSA_HEADS = 12
NSA_KV_HEADS = 3
NSA_GQA = 4
NSA_KV = 192
NSA_COLS = D_NSA + 6 * NSA_KV + 3 * NSA_HEADS
CMP_BLOCK = 32
CMP_STRIDE = 16
CMP_HIDDEN = 256
SEL_BLOCK = 64
SEL_TOPK = 16
FORCE_SCORE = 1e9
WINDOW = 512
ROPE_THETA = 500000.0
ROPE_DIM = 16
D_FF = 5504
IN_COLS = RW_COLS + D_POOL + NSA_COLS
ALPHA = (2 * DEPTH) ** 0.25
LN_EPS = 1e-5

LANES = 128
SUBLANES = 8
VMEM_LIMIT = 56 * 1024 * 1024

KV_SLOT = 2 * LANES
NSA_PAD = 2560
P_RW_OFF = 0
P_NSA_OFF = RW_COLS
P_POOL_OFF = RW_COLS + NSA_PAD
P_COLS = RW_COLS + NSA_PAD + D_POOL
NSA_GATE_OFF = D_NSA + 6 * KV_SLOT
D_FF_PAD = 5632

NEG = -1e30


def _params(*sem):
    return pltpu.CompilerParams(dimension_semantics=sem, vmem_limit_bytes=VMEM_LIMIT)


def _layer_norm(z, g, b):
    mu = jnp.mean(z, axis=-1, keepdims=True)
    zc = z - mu
    var = jnp.mean(zc * zc, axis=-1, keepdims=True)
    return zc * lax.rsqrt(var + LN_EPS) * g + b


def _dot(a, b):
    return jnp.dot(a, b, preferred_element_type=F32)


def _dot_nt(a, b):
    return lax.dot_general(a, b, (((1,), (1,)), ((), ())), preferred_element_type=F32)


def _dot_split(x, w):
    hi = x.astype(BF16)
    lo = (x - hi.astype(F32)).astype(BF16)
    return _dot(hi, w) + _dot(lo, w)


def _ffn_kernel(x_ref, wa_ref, wb_ref, wd_ref, g_ref, b_ref, o_ref, xb_ref, acc_ref):
    k = pl.program_id(1)

    @pl.when(k == 0)
    def _():
        xb_ref[...] = x_ref[...].astype(BF16)
        acc_ref[...] = jnp.zeros_like(acc_ref)

    xb = xb_ref[...]
    a = _dot(xb, wa_ref[...])
    b = _dot(xb, wb_ref[...])
    h = (a / (1.0 + jnp.exp(-a))) * b
    acc_ref[...] += _dot(h.astype(BF16), wd_ref[...])

    @pl.when(k == pl.num_programs(1) - 1)
    def _():
        z = ALPHA * x_ref[...] + 0.5 * acc_ref[...]
        o_ref[...] = _layer_norm(z, g_ref[...], b_ref[...])


def _ffn_ln(x, wa, wb, wd, g, b, layer, *, tm=512, tf=512):
    T, D = x.shape
    fp = wa.shape[-1]
    return pl.pallas_call(
        _ffn_kernel,
        out_shape=jax.ShapeDtypeStruct((T, D), F32),
        grid=(T // tm, fp // tf),
        in_specs=[
            pl.BlockSpec((tm, D), lambda i, k: (i, 0)),
            pl.BlockSpec((None, D, tf), lambda i, k: (layer, 0, k)),
            pl.BlockSpec((None, D, tf), lambda i, k: (layer, 0, k)),
            pl.BlockSpec((None, tf, D), lambda i, k: (layer, k, 0)),
            pl.BlockSpec((None, 1, D), lambda i, k: (layer, 0, 0)),
            pl.BlockSpec((None, 1, D), lambda i, k: (layer, 0, 0)),
        ],
        out_specs=pl.BlockSpec((tm, D), lambda i, k: (i, 0)),
        scratch_shapes=[pltpu.VMEM((tm, D), BF16), pltpu.VMEM((tm, D), F32)],
        compiler_params=_params("parallel", "arbitrary"),
        name="ffn_ln",
    )(x, wa, wb, wd, g, b)


def _inproj_kernel(x_ref, w_ref, o_ref):
    o_ref[...] = _dot(x_ref[...].astype(BF16), w_ref[...])


def _in_proj(x, w, layer, *, tm=512, tn=512):
    T, D = x.shape
    n = w.shape[-1]
    return pl.pallas_call(
        _inproj_kernel,
        out_shape=jax.ShapeDtypeStruct((T, n), F32),
        grid=(T // tm, n // tn),
        in_specs=[
            pl.BlockSpec((tm, D), lambda i, j: (i, 0)),
            pl.BlockSpec((None, D, tn), lambda i, j: (layer, 0, j)),
        ],
        out_specs=pl.BlockSpec((tm, tn), lambda i, j: (i, j)),
        compiler_params=_params("parallel", "arbitrary"),
        name="in_proj",
    )(x, w)


def _outproj_kernel(x_ref, yr_ref, yp_ref, yn_ref, wr_ref, wp_ref, wn_ref, g_ref, b_ref, o_ref):
    y = _dot(yr_ref[...], wr_ref[...]) + _dot(yp_ref[...], wp_ref[...]) + _dot(yn_ref[...], wn_ref[...])
    o_ref[...] = _layer_norm(ALPHA * x_ref[...] + y, g_ref[...], b_ref[...])


def _out_proj_ln(x, y_rw, y_pool, y_nsa, w_out, g, b, layer, *, tm=512):
    T, D = x.shape
    return pl.pallas_call(
        _outproj_kernel,
        out_shape=jax.ShapeDtypeStruct((T, D), F32),
        grid=(T // tm,),
        in_specs=[
            pl.BlockSpec((tm, D), lambda i: (i, 0)),
            pl.BlockSpec((tm, D_RWKV), lambda i: (i, 0)),
            pl.BlockSpec((tm, D_POOL), lambda i: (i, 0)),
            pl.BlockSpec((tm, D_NSA), lambda i: (i, 0)),
            pl.BlockSpec((None, D_RWKV, D), lambda i: (layer, 0, 0)),
            pl.BlockSpec((None, D_POOL, D), lambda i: (layer, 0, 0)),
            pl.BlockSpec((None, D_NSA, D), lambda i: (layer, 0, 0)),
            pl.BlockSpec((None, 1, D), lambda i: (layer, 0, 0)),
            pl.BlockSpec((None, 1, D), lambda i: (layer, 0, 0)),
        ],
        out_specs=pl.BlockSpec((tm, D), lambda i: (i, 0)),
        compiler_params=_params("parallel"),
        name="out_proj_ln",
    )(x, y_rw, y_pool, y_nsa, w_out[0], w_out[1], w_out[2], g, b)


POOL_HALO = 16


def _pool_kernel(p_ref, halo_ref, w_ref, b_ref, sc_ref, o_ref, xs_ref, *, ts):
    s = pl.program_id(1)
    x = p_ref[0]
    halo = jnp.where(s > 0, halo_ref[0], 0.0)
    xs_ref[0:POOL_HALO, :] = halo
    xs_ref[POOL_HALO:POOL_HALO + ts, :] = x
    t1 = (s * ts + 1 + lax.broadcasted_iota(jnp.int32, (ts, 1), 0)).astype(F32)
    for gi, win in enumerate(POOL_WINDOWS):
        c0 = gi * POOL_GROUP
        acc = x[:, c0:c0 + POOL_GROUP]
        for j in range(1, win):
            acc = acc + xs_ref[POOL_HALO - j:POOL_HALO - j + ts, c0:c0 + POOL_GROUP]
        pooled = acc / jnp.minimum(t1, float(win)) - x[:, c0:c0 + POOL_GROUP]
        z = _dot(pooled.astype(BF16), w_ref[gi]) + b_ref[:, c0:c0 + POOL_GROUP]
        o_ref[0, :, c0:c0 + POOL_GROUP] = (z * sc_ref[:, c0:c0 + POOL_GROUP]).astype(o_ref.dtype)


def _pool_mix(p_all, pool_w, pool_b, pool_scale, layer, *, ts=512):
    B, S, _ = p_all.shape
    cb = P_POOL_OFF // D_POOL
    hb = ts // POOL_HALO
    return pl.pallas_call(
        functools.partial(_pool_kernel, ts=ts),
        out_shape=jax.ShapeDtypeStruct((B, S, D_POOL), BF16),
        grid=(B, S // ts),
        in_specs=[
            pl.BlockSpec((1, ts, D_POOL), lambda b, s: (b, s, cb)),
            pl.BlockSpec((1, POOL_HALO, D_POOL), lambda b, s: (b, jnp.maximum(s * hb - 1, 0), cb)),
            pl.BlockSpec((None, 4, POOL_GROUP, POOL_GROUP), lambda b, s: (layer, 0, 0, 0)),
            pl.BlockSpec((None, 1, D_POOL), lambda b, s: (layer, 0, 0)),
            pl.BlockSpec((None, 1, D_POOL), lambda b, s: (layer, 0, 0)),
        ],
        out_specs=pl.BlockSpec((1, ts, D_POOL), lambda b, s: (b, s, 0)),
        scratch_shapes=[pltpu.VMEM((ts + POOL_HALO, D_POOL), F32)],
        compiler_params=_params("parallel", "parallel"),
        name="pool_mix",
    )(p_all, p_all, pool_w, pool_b, pool_scale)


def _softplus(z):
    return jnp.maximum(z, 0.0) + jnp.log1p(jnp.exp(-jnp.abs(z)))


def _sigmoid(z):
    return 1.0 / (1.0 + jnp.exp(-z))


def _rw_prep_kernel(p_ref, prev_ref, mu_ref, w0_ref, w2_ref, a0_ref, a2_ref, g2_ref, kk_ref, ka_ref,
                    rk_ref, ones_ref,
                    r_o, w_o, k_o, v_o, kk_o, be_o, bo_o, g_o, *, ts):
    s = pl.program_id(1)
    x = p_ref[0]
    last = jnp.where(s > 0, prev_ref[0][SUBLANES - 1:SUBLANES, :], 0.0)
    row = lax.broadcasted_iota(jnp.int32, (ts, 1), 0)
    shifted = jnp.where(row == 0, last, pltpu.roll(x, 1, 0))
    xm = x + (shifted - x) * mu_ref[...]
    c = D_RWKV
    r = xm[:, 0:c]
    k = xm[:, c:2 * c]
    v = xm[:, 2 * c:3 * c]
    lora = xm[:, 3 * c:3 * c + LANES]
    wl = lora[:, :RW_DECAY_LORA]
    al = lora[:, RW_DECAY_LORA:]
    gl = xm[:, 3 * c + LANES:]
    w = -_softplus(-(w0_ref[...] + _dot(jnp.tanh(wl).astype(BF16), w2_ref[...]))) - 0.5
    decay = jnp.exp(-jnp.exp(w))
    a = _sigmoid(a0_ref[...] + _dot(al.astype(BF16), a2_ref[...]))
    g = _dot(_sigmoid(gl).astype(BF16), g2_ref[...])
    ones = ones_ref[...]
    kk = k * kk_ref[...]
    nrm = jnp.sqrt(_dot_split(kk * kk, ones))
    kk = kk / jnp.maximum(nrm, 1e-12)
    k_mod = k * (1.0 + (a - 1.0) * ka_ref[...])
    bonus = _dot_split(r * k_mod * rk_ref[...], ones) * v
    r_o[0] = r
    w_o[0] = decay
    k_o[0] = k_mod
    v_o[0] = v
    kk_o[0] = kk
    be_o[0] = kk * a
    bo_o[0] = bonus
    g_o[0] = g


def _rw_prep(p_all, prm, layer, *, ts=256):
    B, S, _ = p_all.shape
    cb = P_RW_OFF // RW_COLS
    hb = ts // SUBLANES
    vec = lambda n: pl.BlockSpec((None, 1, n), lambda b, s: (layer, 0, 0))
    mat = lambda m, n: pl.BlockSpec((None, m, n), lambda b, s: (layer, 0, 0))
    out = jax.ShapeDtypeStruct((B, S, D_RWKV), F32)
    ospec = pl.BlockSpec((1, ts, D_RWKV), lambda b, s: (b, s, 0))
    return pl.pallas_call(
        functools.partial(_rw_prep_kernel, ts=ts),
        out_shape=[out] * 8,
        grid=(B, S // ts),
        in_specs=[
            pl.BlockSpec((1, ts, RW_COLS), lambda b, s: (b, s, cb)),
            pl.BlockSpec((1, SUBLANES, RW_COLS), lambda b, s: (b, jnp.maximum(s * hb - 1, 0), cb)),
            vec(RW_COLS), vec(D_RWKV), mat(RW_DECAY_LORA, D_RWKV), vec(D_RWKV), mat(RW_A_LORA, D_RWKV),
            mat(RW_GATE_LORA, D_RWKV), vec(D_RWKV), vec(D_RWKV), vec(D_RWKV),
            pl.BlockSpec((D_RWKV, D_RWKV), lambda b, s: (0, 0)),
        ],
        out_specs=[ospec] * 8,
        compiler_params=_params("parallel", "parallel"),
        name="rw_prep",
    )(p_all, p_all, prm["mu"], prm["w0"], prm["w2"], prm["a0"], prm["a2"], prm["g2"], prm["k_k"],
      prm["k_a"], prm["r_k"], prm["ones"])


RW_PAIRS = 2
RW_TBLK = 128


def _rw_scan_kernel(r_ref, w_ref, k_ref, v_ref, kk_ref, be_ref, bo_ref, g_ref, gng_ref, gnb_ref, ones_ref,
                    o_ref, st_ref, y_ref, xt_ref, *, ts):
    s = pl.program_id(2)

    @pl.when(s == 0)
    def _():
        st_ref[...] = jnp.zeros_like(st_ref)

    lane = lax.broadcasted_iota(jnp.int32, (HEAD_DIM, LANES), 1)
    hi_half = lane >= HEAD_DIM
    zeros = jnp.zeros((HEAD_DIM, LANES), jnp.int32)
    col_refs = (kk_ref, w_ref, be_ref, k_ref, r_ref)

    def pick(pp, j, t):
        idx = zeros + t
        lo = jnp.take_along_axis(xt_ref[pp, j, :HEAD_DIM, :], idx, axis=1)
        hi = jnp.take_along_axis(xt_ref[pp, j, HEAD_DIM:, :], idx, axis=1)
        return jnp.where(hi_half, hi, lo)

    for blk in range(ts // RW_TBLK):
        t0 = blk * RW_TBLK
        for pp in range(RW_PAIRS):
            for j, ref in enumerate(col_refs):
                xt_ref[pp, j] = ref[0, t0:t0 + RW_TBLK, pp * LANES:(pp + 1) * LANES].T

        def step(i, states):
            base = pl.multiple_of(t0 + i * SUBLANES, SUBLANES)
            states = list(states)
            for pp in range(RW_PAIRS):
                v8 = v_ref[0, pl.ds(base, SUBLANES), pp * LANES:(pp + 1) * LANES]
                st = states[pp]
                ys = []
                for j in range(SUBLANES):
                    t = i * SUBLANES + j
                    kkc, wc, bec, kc, rc = [pick(pp, c, t) for c in range(len(col_refs))]
                    sa = jnp.sum(st * kkc, axis=0, keepdims=True)
                    st = st * wc - bec * sa + kc * v8[j:j + 1, :]
                    ys.append(jnp.sum(st * rc, axis=0, keepdims=True))
                y_ref[pl.ds(base, SUBLANES), pp * LANES:(pp + 1) * LANES] = jnp.concatenate(ys, axis=0)
                states[pp] = st
            return tuple(states)

        states = lax.fori_loop(0, RW_TBLK // SUBLANES, step, tuple(st_ref[pp] for pp in range(RW_PAIRS)))
        for pp in range(RW_PAIRS):
            st_ref[pp] = states[pp]

    y = y_ref[...]
    ones = ones_ref[...]
    mean = _dot_split(y, ones) * (1.0 / HEAD_DIM)
    yc = y - mean
    var = _dot_split(yc * yc, ones) * (1.0 / HEAD_DIM)
    yn = yc * lax.rsqrt(var + RW_GN_EPS) * gng_ref[...] + gnb_ref[...]
    o_ref[0] = ((yn + bo_ref[0]) * g_ref[0]).astype(o_ref.dtype)


def _rw_scan(r, w, k, v, kk, be, bo, g, prm, layer, *, ts=512):
    B, S, _ = r.shape
    width = RW_PAIRS * LANES
    groups = D_RWKV // width
    seq = pl.BlockSpec((1, ts, width), lambda b, h, s: (b, s, h))
    vec = pl.BlockSpec((None, 1, width), lambda b, h, s: (layer, 0, h))
    return pl.pallas_call(
        functools.partial(_rw_scan_kernel, ts=ts),
        out_shape=jax.ShapeDtypeStruct((B, S, D_RWKV), BF16),
        grid=(B, groups, S // ts),
        in_specs=[seq] * 8 + [vec, vec, pl.BlockSpec((width, width), lambda b, h, s: (0, 0))],
        out_specs=seq,
        scratch_shapes=[pltpu.VMEM((RW_PAIRS, HEAD_DIM, LANES), F32), pltpu.VMEM((ts, width), F32),
                        pltpu.VMEM((RW_PAIRS, 5, RW_TBLK, LANES), F32)],
        compiler_params=_params("parallel", "parallel", "arbitrary"),
        name="rw_scan",
    )(r, w, k, v, kk, be, bo, g, prm["gn_g"], prm["gn_b"], prm["ones"][:width, :width])


def _rope(x, cos, sin):
    half = ROPE_DIM // 2
    w = x.shape[-1]
    d = lax.broadcasted_iota(jnp.int32, x.shape, 1) % HEAD_DIM
    partner = jnp.where(d < half, pltpu.roll(x, w - half, 1), pltpu.roll(x, half, 1))
    return x * cos + partner * sin


def _nsa_prep_kernel(p_ref, cos_ref, sin_ref, gb_ref, q_o, kc_o, vc_o, ks_o, vs_o, kw_o, vw_o, gt_o):
    cos = cos_ref[...]
    sin = sin_ref[...]
    for hq in range(NSA_HEADS // 4):
        x = p_ref[0, :, hq * KV_SLOT:(hq + 1) * KV_SLOT]
        qr = _rope(x, cos, sin) * (HEAD_DIM ** -0.5)
        for j in range(4):
            q_o[0, 4 * hq + j] = qr[:, j * HEAD_DIM:(j + 1) * HEAD_DIM].astype(BF16)
    for i, ref in enumerate((kc_o, vc_o, ks_o, vs_o, kw_o, vw_o)):
        part = p_ref[0, :, D_NSA + i * KV_SLOT:D_NSA + (i + 1) * KV_SLOT]
        if i in (2, 4):
            part = _rope(part, cos, sin)
        for h in range(NSA_KV_HEADS):
            ref[0, h] = part[:, h * HEAD_DIM:(h + 1) * HEAD_DIM].astype(BF16)
    gts = _sigmoid(p_ref[0, :, NSA_GATE_OFF:NSA_GATE_OFF + LANES] + gb_ref[...])
    for h in range(NSA_KV_HEADS):
        gt_o[0, h] = gts[:, h * 12:(h + 1) * 12]


def _nsa_prep(p_all, cos, sin, gate_b, layer, *, ts=256):
    B, S, _ = p_all.shape
    kv_shape = jax.ShapeDtypeStruct((B, NSA_KV_HEADS, S, HEAD_DIM), BF16)
    kv_spec = pl.BlockSpec((1, NSA_KV_HEADS, ts, HEAD_DIM), lambda b, s: (b, 0, s, 0))
    return pl.pallas_call(
        _nsa_prep_kernel,
        out_shape=[jax.ShapeDtypeStruct((B, NSA_HEADS, S, HEAD_DIM), BF16)] + [kv_shape] * 6
        + [jax.ShapeDtypeStruct((B, NSA_KV_HEADS, S, 12), F32)],
        grid=(B, S // ts),
        in_specs=[
            pl.BlockSpec((1, ts, NSA_PAD), lambda b, s: (b, s, P_NSA_OFF // NSA_PAD)),
            pl.BlockSpec((ts, KV_SLOT), lambda b, s: (s, 0)),
            pl.BlockSpec((ts, KV_SLOT), lambda b, s: (s, 0)),
            pl.BlockSpec((None, 1, LANES), lambda b, s: (layer, 0, 0)),
        ],
        out_specs=[pl.BlockSpec((1, NSA_HEADS, ts, HEAD_DIM), lambda b, s: (b, 0, s, 0))] + [kv_spec] * 6
        + [pl.BlockSpec((1, NSA_KV_HEADS, ts, 12), lambda b, s: (b, 0, s, 0))],
        compiler_params=_params("parallel", "parallel"),
        name="nsa_prep",
    )(p_all, cos, sin, gate_b)


def _gelu_tanh(x):
    return 0.5 * x * (1.0 + jnp.tanh(float(np.sqrt(2.0 / np.pi)) * (x + 0.044715 * (x * x * x))))


def _nsa_cmp_kernel(kc_ref, vc_ref, kw1_ref, kw2_ref, kpe_ref, vw1_ref, vw2_ref, vpe_ref, cos_ref, sin_ref,
                    k_o, v_o):
    def compress(g, w1_ref, w2_ref, pe_ref):
        half = CMP_STRIDE * HEAD_DIM
        first = _dot(g, w1_ref[:half, :])
        second = _dot(g, w1_ref[half:, :])
        n = first.shape[0]
        bias = _dot(pe_ref[...], w1_ref[...])[0:1]
        h = first + pltpu.roll(second, n - 1, 0) + bias
        return _dot(_gelu_tanh(h).astype(BF16), w2_ref[...])

    k = compress(kc_ref[0, 0], kw1_ref, kw2_ref, kpe_ref)
    k_o[0, 0] = _rope(k, cos_ref[...], sin_ref[...])[:, :HEAD_DIM].astype(BF16)
    v_o[0, 0] = compress(vc_ref[0, 0], vw1_ref, vw2_ref, vpe_ref)[:, :HEAD_DIM].astype(BF16)


def _nsa_compress(kc, vc, prm, cos_c, sin_c, layer):
    B, H, S, _ = kc.shape
    ng = S // CMP_STRIDE
    gw = CMP_STRIDE * HEAD_DIM
    g_k = kc.reshape(B, H, ng, gw)
    g_v = vc.reshape(B, H, ng, gw)
    gspec = pl.BlockSpec((1, 1, ng, gw), lambda b, h: (b, h, 0, 0))
    w1 = pl.BlockSpec((None, 2 * gw, CMP_HIDDEN), lambda b, h: (layer, 0, 0))
    w2 = pl.BlockSpec((None, CMP_HIDDEN, LANES), lambda b, h: (layer, 0, 0))
    pe = pl.BlockSpec((None, SUBLANES, 2 * gw), lambda b, h: (layer, 0, 0))
    tab = pl.BlockSpec((ng, LANES), lambda b, h: (0, 0))
    out = jax.ShapeDtypeStruct((B, H, ng, HEAD_DIM), BF16)
    ospec = pl.BlockSpec((1, 1, ng, HEAD_DIM), lambda b, h: (b, h, 0, 0))
    return pl.pallas_call(
        _nsa_cmp_kernel,
        out_shape=[out, out],
        grid=(B, H),
        in_specs=[gspec, gspec, w1, w2, pe, w1, w2, pe, tab, tab],
        out_specs=[ospec, ospec],
        compiler_params=_params("parallel", "parallel"),
        name="nsa_compress",
    )(g_k, g_v, prm["k_w1"], prm["k_w2"], prm["k_pe"], prm["v_w1"], prm["v_w2"], prm["v_pe"], cos_c, sin_c)


def _nsa_attn_kernel(q_ref, kcmp_ref, vcmp_ref, ks_ref, vs_ref, kw_ref, vw_ref, gt_ref, ov_ref, o_ref,
                     m_ref, l_ref, acc_ref, *, tq, tk, n_cmp):
    G = NSA_GQA
    q0 = pl.program_id(2) * tq
    q = q_ref[0].reshape(G * tq, HEAD_DIM)
    pos = q0 + lax.broadcasted_iota(jnp.int32, (tq, 1), 0)
    tiny = jnp.finfo(F32).tiny

    ncp = kcmp_ref.shape[2]
    s = _dot_nt(q, kcmp_ref[0, 0]).reshape(G, tq, ncp)
    n_idx = lax.broadcasted_iota(jnp.int32, (1, ncp), 1)
    cmask = ((n_idx * CMP_STRIDE + (CMP_BLOCK - 1)) <= pos) & (n_idx < n_cmp)
    s = jnp.where(cmask[None], s, NEG)
    m = jnp.max(s, axis=-1, keepdims=True)
    e = jnp.where(cmask[None], jnp.exp(s - m), 0.0)
    p = e / jnp.maximum(jnp.sum(e, axis=-1, keepdims=True), tiny)
    o_cmp = _dot(p.reshape(G * tq, ncp).astype(BF16), vcmp_ref[0, 0])
    psum = p[0] + p[1] + p[2] + p[3]
    imp = _dot_split(psum, ov_ref[...])

    n_sel = ov_ref.shape[1]
    blk = lax.broadcasted_iota(jnp.int32, (1, n_sel), 1)
    cur = pos // SEL_BLOCK
    forced = (blk == 0) | (blk == cur) | (blk == cur - 1)
    valid = blk <= cur
    score = jnp.where(valid, jnp.where(forced, FORCE_SCORE, imp), -jnp.inf)
    rank = jnp.zeros((tq, n_sel), F32)
    for i in range(n_sel):
        ci = score[:, i:i + 1]
        beats = (ci > score) | ((ci == score) & (blk > i))
        rank = rank + jnp.where(beats, 1.0, 0.0)
    sel = jnp.where((rank < float(SEL_TOPK)) & valid, 1.0, 0.0).astype(BF16)

    def flash(k_ref, v_ref, lo, hi, mask_fn):
        m_ref[...] = jnp.full_like(m_ref, NEG)
        l_ref[...] = jnp.zeros_like(l_ref)
        acc_ref[...] = jnp.zeros_like(acc_ref)

        def body(kt, carry):
            k0 = pl.multiple_of(kt * tk, tk)
            kb = k_ref[0, 0, pl.ds(k0, tk), :]
            vb = v_ref[0, 0, pl.ds(k0, tk), :]
            sc = _dot_nt(q, kb).reshape(G, tq, tk)
            kpos = k0 + lax.broadcasted_iota(jnp.int32, (1, tk), 1)
            mask = mask_fn(kt, kpos)[None]
            sc = jnp.where(mask, sc, NEG)
            m_prev = m_ref[...]
            m_new = jnp.maximum(m_prev, jnp.max(sc, axis=-1, keepdims=True))
            pr = jnp.where(mask, jnp.exp(sc - m_new), 0.0)
            corr = jnp.exp(m_prev - m_new)
            l_ref[...] = corr * l_ref[...] + jnp.sum(pr, axis=-1, keepdims=True)
            pv = _dot(pr.reshape(G * tq, tk).astype(BF16), vb).reshape(G, tq, HEAD_DIM)
            acc_ref[...] = corr * acc_ref[...] + pv
            m_ref[...] = m_new
            return carry

        lax.fori_loop(lo, hi, body, 0)
        return acc_ref[...] / jnp.maximum(l_ref[...], tiny)

    bpt = tk // SEL_BLOCK
    erow = lax.broadcasted_iota(jnp.int32, (n_sel, tk), 0)
    ecol = lax.broadcasted_iota(jnp.int32, (n_sel, tk), 1) // SEL_BLOCK

    def sel_mask(kt, kpos):
        expand = jnp.where(erow == ecol + kt * bpt, 1.0, 0.0).astype(BF16)
        chosen = _dot(sel, expand)
        return (chosen > 0.5) & (kpos <= pos)

    n_hi = (q0 + tq + tk - 1) // tk
    o_sel = flash(ks_ref, vs_ref, 0, n_hi, sel_mask)

    def win_mask(kt, kpos):
        return (kpos <= pos) & (kpos > pos - WINDOW)

    w_lo = jnp.maximum(q0 - WINDOW + 1, 0) // tk
    o_win = flash(kw_ref, vw_ref, w_lo, n_hi, win_mask)

    gt = gt_ref[0, 0]
    o_cmp = o_cmp.reshape(G, tq, HEAD_DIM)
    outs = []
    for g in range(G):
        outs.append(gt[:, g:g + 1] * o_cmp[g] + gt[:, G + g:G + g + 1] * o_sel[g]
                    + gt[:, 2 * G + g:2 * G + g + 1] * o_win[g])
    o_ref[0] = jnp.concatenate(outs, axis=1).astype(o_ref.dtype)


def _nsa_attention(q, k_cmp, v_cmp, ks, vs, kw, vw, gates, overlap, *, tq=128, tk=256):
    B, _, S, _ = q.shape
    H, G = NSA_KV_HEADS, NSA_GQA
    tq = min(tq, S)
    tk = min(tk, S)
    ncp = k_cmp.shape[2]
    n_cmp = (S - CMP_BLOCK) // CMP_STRIDE + 1
    full = pl.BlockSpec((1, 1, S, HEAD_DIM), lambda b, h, i: (b, h, 0, 0))
    cmp_spec = pl.BlockSpec((1, 1, ncp, HEAD_DIM), lambda b, h, i: (b, h, 0, 0))
    return pl.pallas_call(
        functools.partial(_nsa_attn_kernel, tq=tq, tk=tk, n_cmp=n_cmp),
        out_shape=jax.ShapeDtypeStruct((B, S, D_NSA), BF16),
        grid=(B, H, S // tq),
        in_specs=[
            pl.BlockSpec((1, G, tq, HEAD_DIM), lambda b, h, i: (b, h, i, 0)),
            cmp_spec, cmp_spec, full, full, full, full,
            pl.BlockSpec((1, 1, tq, 12), lambda b, h, i: (b, h, i, 0)),
            pl.BlockSpec(overlap.shape, lambda b, h, i: (0, 0)),
        ],
        out_specs=pl.BlockSpec((1, tq, G * HEAD_DIM), lambda b, h, i: (b, i, h)),
        scratch_shapes=[pltpu.VMEM((G, tq, 1), F32), pltpu.VMEM((G, tq, 1), F32),
                        pltpu.VMEM((G, tq, HEAD_DIM), F32)],
        compiler_params=_params("parallel", "parallel", "arbitrary"),
        name="nsa_attention",
    )(q, k_cmp, v_cmp, ks, vs, kw, vw, gates, overlap)


def _rope_tables(pos, heads):
    half = ROPE_DIM // 2
    inv_freq = ROPE_THETA ** (-jnp.arange(half, dtype=F32) / half)
    ang = pos.astype(F32)[:, None] * inv_freq
    cos, sin = jnp.cos(ang), jnp.sin(ang)
    n = pos.shape[0]
    rest = HEAD_DIM - ROPE_DIM
    cos_h = jnp.concatenate([cos, cos, jnp.ones((n, rest), F32)], axis=1)
    sin_h = jnp.concatenate([-sin, sin, jnp.zeros((n, rest), F32)], axis=1)
    return jnp.tile(cos_h, (1, heads)), jnp.tile(sin_h, (1, heads))


def _overlap_matrix(S, ncp):
    n_cmp = (S - CMP_BLOCK) // CMP_STRIDE + 1
    n_sel = S // SEL_BLOCK
    cmp_start = np.arange(n_cmp) * CMP_STRIDE
    sel_start = np.arange(n_sel) * SEL_BLOCK
    ov = np.clip(np.minimum(cmp_start[:, None] + CMP_BLOCK, sel_start[None, :] + SEL_BLOCK)
                 - np.maximum(cmp_start[:, None], sel_start[None, :]), 0, None) / CMP_BLOCK
    full = np.zeros((ncp, n_sel), np.float32)
    full[:n_cmp] = ov
    return jnp.asarray(full, BF16)


def _block_ones(n):
    idx = np.arange(n) // HEAD_DIM
    return jnp.asarray((idx[:, None] == idx[None, :]).astype(np.float32), BF16)


def _w_in_layout(w_in):
    L = w_in.shape[0]
    rw = w_in[:, :, :RW_COLS]
    pool = w_in[:, :, RW_COLS:RW_COLS + D_POOL]
    nsa = w_in[:, :, RW_COLS + D_POOL:]
    zeros = lambda n: jnp.zeros((L, D_MODEL, n), w_in.dtype)
    segs = [nsa[:, :, :D_NSA]]
    for i in range(6):
        segs += [nsa[:, :, D_NSA + i * NSA_KV:D_NSA + (i + 1) * NSA_KV], zeros(KV_SLOT - NSA_KV)]
    gates = nsa[:, :, D_NSA + 6 * NSA_KV:]
    gates = gates.reshape(L, D_MODEL, NSA_KV_HEADS, NSA_GQA, 3).transpose(0, 1, 2, 4, 3)
    segs += [gates.reshape(L, D_MODEL, 3 * NSA_HEADS),
             zeros(NSA_PAD - NSA_GATE_OFF - 3 * NSA_HEADS)]
    return jnp.concatenate([rw] + segs + [pool], axis=-1).astype(BF16)


def _gate_bias_layout(gate_b):
    L = gate_b.shape[0]
    gb = gate_b.reshape(L, NSA_KV_HEADS, NSA_GQA, 3).transpose(0, 1, 3, 2).reshape(L, 1, 3 * NSA_HEADS)
    return jnp.pad(gb, ((0, 0), (0, 0), (0, LANES - 3 * NSA_HEADS)))


def kernel(x, ffn1_w_up, ffn1_w_down, ln1_g, ln1_b, w_in, rw_mu, rw_w0, rw_w2, rw_a0, rw_a2, rw_g2, rw_k_k,
           rw_k_a, rw_r_k, rw_gn_g, rw_gn_b, pool_w, pool_b, pool_scale, nsa_cmp_pe_k, nsa_cmp_pe_v,
           nsa_cmp_k_w1, nsa_cmp_k_w2, nsa_cmp_v_w1, nsa_cmp_v_w2, nsa_gate_b, w_out, ln2_g, ln2_b,
           ffn2_w_up, ffn2_w_down, ln3_g, ln3_b):
    prm = _prepare(x.shape[1], ffn1_w_up, ffn1_w_down, ln1_g, ln1_b, w_in, rw_mu, rw_w0, rw_w2, rw_a0, rw_a2,
                   rw_g2, rw_k_k, rw_k_a, rw_r_k, rw_gn_g, rw_gn_b, pool_w, pool_b, pool_scale, nsa_cmp_pe_k,
                   nsa_cmp_pe_v, nsa_cmp_k_w1, nsa_cmp_k_w2, nsa_cmp_v_w1, nsa_cmp_v_w2, nsa_gate_b, w_out,
                   ln2_g, ln2_b, ffn2_w_up, ffn2_w_down, ln3_g, ln3_b)
    B, S, D = x.shape
    h = x.reshape(B * S, D)
    for l in range(w_in.shape[0]):
        h = _layer(h, prm, l, B, S)
    return h.reshape(B, S, D)


def _prepare(S, ffn1_w_up, ffn1_w_down, ln1_g, ln1_b, w_in, rw_mu, rw_w0, rw_w2, rw_a0, rw_a2, rw_g2, rw_k_k,
             rw_k_a, rw_r_k, rw_gn_g, rw_gn_b, pool_w, pool_b, pool_scale, nsa_cmp_pe_k, nsa_cmp_pe_v,
             nsa_cmp_k_w1, nsa_cmp_k_w2, nsa_cmp_v_w1, nsa_cmp_v_w2, nsa_gate_b, w_out, ln2_g, ln2_b,
             ffn2_w_up, ffn2_w_down, ln3_g, ln3_b):
    L = w_in.shape[0]
    fpad = D_FF_PAD - D_FF

    def up(w):
        a = jnp.pad(w[:, :, :D_FF], ((0, 0), (0, 0), (0, fpad))).astype(BF16)
        b = jnp.pad(w[:, :, D_FF:], ((0, 0), (0, 0), (0, fpad))).astype(BF16)
        return a, b

    def down(w):
        return jnp.pad(w, ((0, 0), (0, fpad), (0, 0))).astype(BF16)

    row = lambda v: v[:, None, :]
    f1a, f1b = up(ffn1_w_up)
    f2a, f2b = up(ffn2_w_up)
    w_out_b = w_out.astype(BF16)
    gw = CMP_BLOCK * HEAD_DIM
    pad_w2 = lambda w: jnp.pad(w, ((0, 0), (0, 0), (0, LANES - HEAD_DIM))).astype(BF16)
    pe_rows = lambda pe: jnp.broadcast_to(pe.reshape(L, 1, gw), (L, SUBLANES, gw)).astype(BF16)
    ncp = S // CMP_STRIDE
    cos_t, sin_t = _rope_tables(jnp.arange(S), 4)
    cos_c, sin_c = _rope_tables(jnp.arange(ncp) * CMP_STRIDE + (CMP_BLOCK - 1), 2)
    return dict(
        ffn1=(f1a, f1b, down(ffn1_w_down), row(ln1_g), row(ln1_b)),
        ffn2=(f2a, f2b, down(ffn2_w_down), row(ln3_g), row(ln3_b)),
        w_in=_w_in_layout(w_in),
        w_out=(w_out_b[:, :D_RWKV], w_out_b[:, D_RWKV:D_RWKV + D_POOL], w_out_b[:, D_RWKV + D_POOL:]),
        ln2=(row(ln2_g), row(ln2_b)),
        rw=dict(mu=row(rw_mu), w0=row(rw_w0), w2=rw_w2.astype(BF16), a0=row(rw_a0), a2=rw_a2.astype(BF16),
                g2=rw_g2.astype(BF16), k_k=row(rw_k_k), k_a=row(rw_k_a), r_k=rw_r_k.reshape(L, 1, D_RWKV),
                gn_g=row(rw_gn_g), gn_b=row(rw_gn_b), ones=_block_ones(D_RWKV)),
        cmp=dict(k_w1=nsa_cmp_k_w1.reshape(L, gw, CMP_HIDDEN).astype(BF16), k_w2=pad_w2(nsa_cmp_k_w2),
                 k_pe=pe_rows(nsa_cmp_pe_k),
                 v_w1=nsa_cmp_v_w1.reshape(L, gw, CMP_HIDDEN).astype(BF16), v_w2=pad_w2(nsa_cmp_v_w2),
                 v_pe=pe_rows(nsa_cmp_pe_v)),
        gate_b=_gate_bias_layout(nsa_gate_b),
        pool=(pool_w.astype(BF16), row(pool_b), row(pool_scale)),
        rope=(cos_t, sin_t), rope_cmp=(cos_c, sin_c), overlap=_overlap_matrix(S, ncp))


def _mixers(p_all, prm, l):
    r, w, k, v, kk, be, bo, g = _rw_prep(p_all, prm["rw"], l)
    y_rw = _rw_scan(r, w, k, v, kk, be, bo, g, prm["rw"], l)
    y_pool = _pool_mix(p_all, *prm["pool"], l)
    q, kc, vc, ks, vs, kw, vw, gates = _nsa_prep(p_all, *prm["rope"], prm["gate_b"], l)
    k_cmp, v_cmp = _nsa_compress(kc, vc, prm["cmp"], *prm["rope_cmp"], l)
    y_nsa = _nsa_attention(q, k_cmp, v_cmp, ks, vs, kw, vw, gates, prm["overlap"])
    return y_rw, y_pool, y_nsa


def _layer(h, prm, l, B, S):
    T = B * S
    h = _ffn_ln(h, *prm["ffn1"], l)
    p_all = _in_proj(h, prm["w_in"], l).reshape(B, S, P_COLS)
    y_rw, y_pool, y_nsa = _mixers(p_all, prm, l)
    h = _out_proj_ln(h, y_rw.reshape(T, D_RWKV), y_pool.reshape(T, D_POOL), y_nsa.reshape(T, D_NSA),
                     prm["w_out"], *prm["ln2"], l)
    return _ffn_ln(h, *prm["ffn2"], l)
```

```python
import functools

import numpy as np
import jax
import jax.numpy as jnp
from jax import lax
from jax.experimental import pallas as pl
from jax.experimental.pallas import tpu as pltpu

F32 = jnp.float32
BF16 = jnp.bfloat16

D_MODEL = 2048
DEPTH = 4
HEAD_DIM = 64
D_RWKV = 768
D_POOL = 512
D_NSA = 768
RW_HEADS = 12
RW_DECAY_LORA = 64
RW_A_LORA = 64
RW_GATE_LORA = 128
RW_GN_EPS = 64e-5
RW_COLS = 3 * D_RWKV + RW_DECAY_LORA + RW_A_LORA + RW_GATE_LORA
POOL_WINDOWS = (2, 4, 8, 16)
POOL_GROUP = 128
NSA_HEADS = 12
NSA_KV_HEADS = 3
NSA_GQA = 4
NSA_KV = 192
NSA_COLS = D_NSA + 6 * NSA_KV + 3 * NSA_HEADS
CMP_BLOCK = 32
CMP_STRIDE = 16
CMP_HIDDEN = 256
SEL_BLOCK = 64
SEL_TOPK = 16
FORCE_SCORE = 1e9
WINDOW = 512
ROPE_THETA = 500000.0
ROPE_DIM = 16
D_FF = 5504
IN_COLS = RW_COLS + D_POOL + NSA_COLS
ALPHA = (2 * DEPTH) ** 0.25
LN_EPS = 1e-5

LANES = 128
SUBLANES = 8
VMEM_LIMIT = 56 * 1024 * 1024

KV_SLOT = 2 * LANES
NSA_PAD = 2560
P_RW_OFF = 0
P_NSA_OFF = RW_COLS
P_POOL_OFF = RW_COLS + NSA_PAD
P_COLS = RW_COLS + NSA_PAD + D_POOL
NSA_GATE_OFF = D_NSA + 6 * KV_SLOT
D_FF_PAD = 5632

NEG = -1e30


def _params(*sem):
    return pltpu.CompilerParams(dimension_semantics=sem, vmem_limit_bytes=VMEM_LIMIT)


def _layer_norm(z, g, b):
    mu = jnp.mean(z, axis=-1, keepdims=True)
    zc = z - mu
    var = jnp.mean(zc * zc, axis=-1, keepdims=True)
    return zc * lax.rsqrt(var + LN_EPS) * g + b


def _dot(a, b):
    return jnp.dot(a, b, preferred_element_type=F32)


def _dot_nt(a, b):
    return lax.dot_general(a, b, (((1,), (1,)), ((), ())), preferred_element_type=F32)


def _dot_split(x, w):
    hi = x.astype(BF16)
    lo = (x - hi.astype(F32)).astype(BF16)
    return _dot(hi, w) + _dot(lo, w)


def _ffn_kernel(x_ref, wa_ref, wb_ref, wd_ref, g_ref, b_ref, o_ref, xb_ref, acc_ref):
    k = pl.program_id(1)

    @pl.when(k == 0)
    def _():
        xb_ref[...] = x_ref[...].astype(BF16)
        acc_ref[...] = jnp.zeros_like(acc_ref)

    xb = xb_ref[...]
    a = _dot(xb, wa_ref[...])
    b = _dot(xb, wb_ref[...])
    h = (a / (1.0 + jnp.exp(-a))) * b
    acc_ref[...] += _dot(h.astype(BF16), wd_ref[...])

    @pl.when(k == pl.num_programs(1) - 1)
    def _():
        z = ALPHA * x_ref[...] + 0.5 * acc_ref[...]
        o_ref[...] = _layer_norm(z, g_ref[...], b_ref[...])


def _ffn_ln(x, wa, wb, wd, g, b, layer, *, tm=512, tf=512):
    T, D = x.shape
    fp = wa.shape[-1]
    return pl.pallas_call(
        _ffn_kernel,
        out_shape=jax.ShapeDtypeStruct((T, D), F32),
        grid=(T // tm, fp // tf),
        in_specs=[
            pl.BlockSpec((tm, D), lambda i, k: (i, 0)),
            pl.BlockSpec((None, D, tf), lambda i, k: (layer, 0, k)),
            pl.BlockSpec((None, D, tf), lambda i, k: (layer, 0, k)),
            pl.BlockSpec((None, tf, D), lambda i, k: (layer, k, 0)),
            pl.BlockSpec((None, 1, D), lambda i, k: (layer, 0, 0)),
            pl.BlockSpec((None, 1, D), lambda i, k: (layer, 0, 0)),
        ],
        out_specs=pl.BlockSpec((tm, D), lambda i, k: (i, 0)),
        scratch_shapes=[pltpu.VMEM((tm, D), BF16), pltpu.VMEM((tm, D), F32)],
        compiler_params=_params("parallel", "arbitrary"),
        name="ffn_ln",
    )(x, wa, wb, wd, g, b)


def _inproj_kernel(x_ref, w_ref, o_ref):
    o_ref[...] = _dot(x_ref[...].astype(BF16), w_ref[...])


def _in_proj(x, w, layer, *, tm=512, tn=512):
    T, D = x.shape
    n = w.shape[-1]
    return pl.pallas_call(
        _inproj_kernel,
        out_shape=jax.ShapeDtypeStruct((T, n), F32),
        grid=(T // tm, n // tn),
        in_specs=[
            pl.BlockSpec((tm, D), lambda i, j: (i, 0)),
            pl.BlockSpec((None, D, tn), lambda i, j: (layer, 0, j)),
        ],
        out_specs=pl.BlockSpec((tm, tn), lambda i, j: (i, j)),
        compiler_params=_params("parallel", "arbitrary"),
        name="in_proj",
    )(x, w)


def _outproj_kernel(x_ref, yr_ref, yp_ref, yn_ref, wr_ref, wp_ref, wn_ref, g_ref, b_ref, o_ref):
    y = _dot(yr_ref[...], wr_ref[...]) + _dot(yp_ref[...], wp_ref[...]) + _dot(yn_ref[...], wn_ref[...])
    o_ref[...] = _layer_norm(ALPHA * x_ref[...] + y, g_ref[...], b_ref[...])


def _out_proj_ln(x, y_rw, y_pool, y_nsa, w_out, g, b, layer, *, tm=512):
    T, D = x.shape
    return pl.pallas_call(
        _outproj_kernel,
        out_shape=jax.ShapeDtypeStruct((T, D), F32),
        grid=(T // tm,),
        in_specs=[
            pl.BlockSpec((tm, D), lambda i: (i, 0)),
            pl.BlockSpec((tm, D_RWKV), lambda i: (i, 0)),
            pl.BlockSpec((tm, D_POOL), lambda i: (i, 0)),
            pl.BlockSpec((tm, D_NSA), lambda i: (i, 0)),
            pl.BlockSpec((None, D_RWKV, D), lambda i: (layer, 0, 0)),
            pl.BlockSpec((None, D_POOL, D), lambda i: (layer, 0, 0)),
            pl.BlockSpec((None, D_NSA, D), lambda i: (layer, 0, 0)),
            pl.BlockSpec((None, 1, D), lambda i: (layer, 0, 0)),
            pl.BlockSpec((None, 1, D), lambda i: (layer, 0, 0)),
        ],
        out_specs=pl.BlockSpec((tm, D), lambda i: (i, 0)),
        compiler_params=_params("parallel"),
        name="out_proj_ln",
    )(x, y_rw, y_pool, y_nsa, w_out[0], w_out[1], w_out[2], g, b)


POOL_HALO = 16


def _pool_kernel(p_ref, halo_ref, w_ref, b_ref, sc_ref, o_ref, xs_ref, *, ts):
    s = pl.program_id(1)
    x = p_ref[0]
    halo = jnp.where(s > 0, halo_ref[0], 0.0)
    xs_ref[0:POOL_HALO, :] = halo
    xs_ref[POOL_HALO:POOL_HALO + ts, :] = x
    t1 = (s * ts + 1 + lax.broadcasted_iota(jnp.int32, (ts, 1), 0)).astype(F32)
    for gi, win in enumerate(POOL_WINDOWS):
        c0 = gi * POOL_GROUP
        acc = x[:, c0:c0 + POOL_GROUP]
        for j in range(1, win):
            acc = acc + xs_ref[POOL_HALO - j:POOL_HALO - j + ts, c0:c0 + POOL_GROUP]
        pooled = acc / jnp.minimum(t1, float(win)) - x[:, c0:c0 + POOL_GROUP]
        z = _dot(pooled.astype(BF16), w_ref[gi]) + b_ref[:, c0:c0 + POOL_GROUP]
        o_ref[0, :, c0:c0 + POOL_GROUP] = (z * sc_ref[:, c0:c0 + POOL_GROUP]).astype(o_ref.dtype)


def _pool_mix(p_all, pool_w, pool_b, pool_scale, layer, *, ts=512):
    B, S, _ = p_all.shape
    cb = P_POOL_OFF // D_POOL
    hb = ts // POOL_HALO
    return pl.pallas_call(
        functools.partial(_pool_kernel, ts=ts),
        out_shape=jax.ShapeDtypeStruct((B, S, D_POOL), BF16),
        grid=(B, S // ts),
        in_specs=[
            pl.BlockSpec((1, ts, D_POOL), lambda b, s: (b, s, cb)),
            pl.BlockSpec((1, POOL_HALO, D_POOL), lambda b, s: (b, jnp.maximum(s * hb - 1, 0), cb)),
            pl.BlockSpec((None, 4, POOL_GROUP, POOL_GROUP), lambda b, s: (layer, 0, 0, 0)),
            pl.BlockSpec((None, 1, D_POOL), lambda b, s: (layer, 0, 0)),
            pl.BlockSpec((None, 1, D_POOL), lambda b, s: (layer, 0, 0)),
        ],
        out_specs=pl.BlockSpec((1, ts, D_POOL), lambda b, s: (b, s, 0)),
        scratch_shapes=[pltpu.VMEM((ts + POOL_HALO, D_POOL), F32)],
        compiler_params=_params("parallel", "parallel"),
        name="pool_mix",
    )(p_all, p_all, pool_w, pool_b, pool_scale)


def _softplus(z):
    return jnp.maximum(z, 0.0) + jnp.log1p(jnp.exp(-jnp.abs(z)))


def _sigmoid(z):
    return 1.0 / (1.0 + jnp.exp(-z))


def _rw_prep_kernel(p_ref, prev_ref, mu_ref, w0_ref, w2_ref, a0_ref, a2_ref, g2_ref, kk_ref, ka_ref,
                    rk_ref, ones_ref,
                    r_o, w_o, k_o, v_o, kk_o, be_o, bo_o, g_o, *, ts):
    s = pl.program_id(1)
    x = p_ref[0]
    last = jnp.where(s > 0, prev_ref[0][SUBLANES - 1:SUBLANES, :], 0.0)
    row = lax.broadcasted_iota(jnp.int32, (ts, 1), 0)
    shifted = jnp.where(row == 0, last, pltpu.roll(x, 1, 0))
    xm = x + (shifted - x) * mu_ref[...]
    c = D_RWKV
    r = xm[:, 0:c]
    k = xm[:, c:2 * c]
    v = xm[:, 2 * c:3 * c]
    lora = xm[:, 3 * c:3 * c + LANES]
    wl = lora[:, :RW_DECAY_LORA]
    al = lora[:, RW_DECAY_LORA:]
    gl = xm[:, 3 * c + LANES:]
    w = -_softplus(-(w0_ref[...] + _dot(jnp.tanh(wl).astype(BF16), w2_ref[...]))) - 0.5
    decay = jnp.exp(-jnp.exp(w))
    a = _sigmoid(a0_ref[...] + _dot(al.astype(BF16), a2_ref[...]))
    g = _dot(_sigmoid(gl).astype(BF16), g2_ref[...])
    ones = ones_ref[...]
    kk = k * kk_ref[...]
    nrm = jnp.sqrt(_dot_split(kk * kk, ones))
    kk = kk / jnp.maximum(nrm, 1e-12)
    k_mod = k * (1.0 + (a - 1.0) * ka_ref[...])
    bonus = _dot_split(r * k_mod * rk_ref[...], ones) * v
    r_o[0] = r
    w_o[0] = decay
    k_o[0] = k_mod
    v_o[0] = v
    kk_o[0] = kk
    be_o[0] = kk * a
    bo_o[0] = bonus
    g_o[0] = g


def _rw_prep(p_all, prm, layer, *, ts=256):
    B, S, _ = p_all.shape
    cb = P_RW_OFF // RW_COLS
    hb = ts // SUBLANES
    vec = lambda n: pl.BlockSpec((None, 1, n), lambda b, s: (layer, 0, 0))
    mat = lambda m, n: pl.BlockSpec((None, m, n), lambda b, s: (layer, 0, 0))
    out = jax.ShapeDtypeStruct((B, S, D_RWKV), F32)
    ospec = pl.BlockSpec((1, ts, D_RWKV), lambda b, s: (b, s, 0))
    return pl.pallas_call(
        functools.partial(_rw_prep_kernel, ts=ts),
        out_shape=[out] * 8,
        grid=(B, S // ts),
        in_specs=[
            pl.BlockSpec((1, ts, RW_COLS), lambda b, s: (b, s, cb)),
            pl.BlockSpec((1, SUBLANES, RW_COLS), lambda b, s: (b, jnp.maximum(s * hb - 1, 0), cb)),
            vec(RW_COLS), vec(D_RWKV), mat(RW_DECAY_LORA, D_RWKV), vec(D_RWKV), mat(RW_A_LORA, D_RWKV),
            mat(RW_GATE_LORA, D_RWKV), vec(D_RWKV), vec(D_RWKV), vec(D_RWKV),
            pl.BlockSpec((D_RWKV, D_RWKV), lambda b, s: (0, 0)),
        ],
        out_specs=[ospec] * 8,
        compiler_params=_params("parallel", "parallel"),
        name="rw_prep",
    )(p_all, p_all, prm["mu"], prm["w0"], prm["w2"], prm["a0"], prm["a2"], prm["g2"], prm["k_k"],
      prm["k_a"], prm["r_k"], prm["ones"])


RW_PAIRS = D_RWKV // LANES


def _rw_scan_kernel(r_ref, w_ref, k_ref, v_ref, kk_ref, be_ref, bo_ref, g_ref, gng_ref, gnb_ref, ones_ref,
                    pair_ref, o_ref, st_ref, vc_ref, sr_ref, y_ref, *, ts, nb):
    s = pl.program_id(1)

    @pl.when(s == 0)
    def _():
        st_ref[...] = jnp.zeros_like(st_ref)

    rows = RW_PAIRS * HEAD_DIM
    lane = lax.broadcasted_iota(jnp.int32, (rows, LANES), 1)
    row = lax.broadcasted_iota(jnp.int32, (rows, LANES), 0)
    eye = jnp.where(lane % HEAD_DIM == row % HEAD_DIM, 1.0, 0.0)
    ones_pair = pair_ref[...]

    def spread(x8, j):
        return jnp.concatenate([jnp.broadcast_to(x8[j:j + 1, pp * LANES:(pp + 1) * LANES], (HEAD_DIM, LANES))
                                for pp in range(RW_PAIRS)], axis=0)

    def step(i, carry):
        base = pl.multiple_of(i * SUBLANES, SUBLANES)
        tiles = [[ref[b, pl.ds(base, SUBLANES), :] for ref in (kk_ref, w_ref, be_ref, k_ref, r_ref, v_ref)]
                 for b in range(nb)]
        for b in range(nb):
            lhs = jnp.concatenate([eye * spread(tiles[b][5], j) for j in range(SUBLANES)], axis=0)
            vc_ref[b] = _dot(lhs.astype(BF16), ones_pair)
        sts = [st_ref[b] for b in range(nb)]
        for j in range(SUBLANES):
            blk = slice(j * rows, (j + 1) * rows)
            for b in range(nb):
                kk8, w8, be8, k8, r8, _ = tiles[b]
                st = sts[b]
                sa = _dot((st * spread(kk8, j)).astype(BF16), ones_pair)
                st = st * spread(w8, j) - sa * spread(be8, j) + vc_ref[b, blk, :] * spread(k8, j)
                sr_ref[b, blk, :] = (st * spread(r8, j)).astype(BF16)
                sts[b] = st
        for b in range(nb):
            st_ref[b] = sts[b]
            yb = _dot(sr_ref[b], ones_pair)
            out_rows = []
            for j in range(SUBLANES):
                diag = yb[j * rows:(j + 1) * rows] * eye
                out_rows.append(jnp.concatenate(
                    [jnp.sum(diag[pp * HEAD_DIM:(pp + 1) * HEAD_DIM], axis=0, keepdims=True)
                     for pp in range(RW_PAIRS)], axis=1))
            y_ref[b, pl.ds(base, SUBLANES), :] = jnp.concatenate(out_rows, axis=0)
        return carry

    lax.fori_loop(0, ts // SUBLANES, step, 0)

    ones = ones_ref[...]
    for b in range(nb):
        y = y_ref[b]
        mean = _dot_split(y, ones) * (1.0 / HEAD_DIM)
        yc = y - mean
        var = _dot_split(yc * yc, ones) * (1.0 / HEAD_DIM)
        yn = yc * lax.rsqrt(var + RW_GN_EPS) * gng_ref[...] + gnb_ref[...]
        o_ref[b] = ((yn + bo_ref[b]) * g_ref[b]).astype(o_ref.dtype)


def _rw_scan(r, w, k, v, kk, be, bo, g, prm, layer, *, ts=256):
    B, S, _ = r.shape
    nb = 2 if B % 2 == 0 else 1
    rows = RW_PAIRS * HEAD_DIM
    seq = pl.BlockSpec((nb, ts, D_RWKV), lambda b, s: (b, s, 0))
    vec = pl.BlockSpec((None, 1, D_RWKV), lambda b, s: (layer, 0, 0))
    return pl.pallas_call(
        functools.partial(_rw_scan_kernel, ts=ts, nb=nb),
        out_shape=jax.ShapeDtypeStruct((B, S, D_RWKV), BF16),
        grid=(B // nb, S // ts),
        in_specs=[seq] * 8 + [vec, vec, pl.BlockSpec((D_RWKV, D_RWKV), lambda b, s: (0, 0)),
                              pl.BlockSpec((LANES, LANES), lambda b, s: (0, 0))],
        out_specs=seq,
        scratch_shapes=[pltpu.VMEM((nb, rows, LANES), F32), pltpu.VMEM((nb, SUBLANES * rows, LANES), F32),
                        pltpu.VMEM((nb, SUBLANES * rows, LANES), BF16), pltpu.VMEM((nb, ts, D_RWKV), F32)],
        compiler_params=_params("parallel", "arbitrary"),
        name="rw_scan",
    )(r, w, k, v, kk, be, bo, g, prm["gn_g"], prm["gn_b"], prm["ones"], prm["ones"][:LANES, :LANES])


def _rope(x, cos, sin):
    half = ROPE_DIM // 2
    w = x.shape[-1]
    d = lax.broadcasted_iota(jnp.int32, x.shape, 1) % HEAD_DIM
    partner = jnp.where(d < half, pltpu.roll(x, w - half, 1), pltpu.roll(x, half, 1))
    return x * cos + partner * sin


def _nsa_prep_kernel(p_ref, cos_ref, sin_ref, gb_ref, q_o, kc_o, vc_o, ks_o, vs_o, kw_o, vw_o, gt_o):
    cos = cos_ref[...]
    sin = sin_ref[...]
    for hq in range(NSA_HEADS // 4):
        x = p_ref[0, :, hq * KV_SLOT:(hq + 1) * KV_SLOT]
        qr = _rope(x, cos, sin) * (HEAD_DIM ** -0.5)
        for j in range(4):
            q_o[0, 4 * hq + j] = qr[:, j * HEAD_DIM:(j + 1) * HEAD_DIM].astype(BF16)
    for i, ref in enumerate((kc_o, vc_o, ks_o, vs_o, kw_o, vw_o)):
        part = p_ref[0, :, D_NSA + i * KV_SLOT:D_NSA + (i + 1) * KV_SLOT]
        if i in (2, 4):
            part = _rope(part, cos, sin)
        if i in (3, 5):
            part = part.T
            for h in range(NSA_KV_HEADS):
                ref[0, h] = part[h * HEAD_DIM:(h + 1) * HEAD_DIM, :].astype(BF16)
        else:
            for h in range(NSA_KV_HEADS):
                ref[0, h] = part[:, h * HEAD_DIM:(h + 1) * HEAD_DIM].astype(BF16)
    gts = _sigmoid(p_ref[0, :, NSA_GATE_OFF:NSA_GATE_OFF + LANES] + gb_ref[...]).T
    for h in range(NSA_KV_HEADS):
        gt_o[0, h] = gts[h * 12:(h + 1) * 12, :]


def _nsa_prep(p_all, cos, sin, gate_b, layer, *, ts=256):
    B, S, _ = p_all.shape
    H = NSA_KV_HEADS
    k_shape = jax.ShapeDtypeStruct((B, H, S, HEAD_DIM), BF16)
    k_spec = pl.BlockSpec((1, H, ts, HEAD_DIM), lambda b, s: (b, 0, s, 0))
    vt_shape = jax.ShapeDtypeStruct((B, H, HEAD_DIM, S), BF16)
    vt_spec = pl.BlockSpec((1, H, HEAD_DIM, ts), lambda b, s: (b, 0, 0, s))
    return pl.pallas_call(
        _nsa_prep_kernel,
        out_shape=[jax.ShapeDtypeStruct((B, NSA_HEADS, S, HEAD_DIM), BF16),
                   k_shape, k_shape, k_shape, vt_shape, k_shape, vt_shape,
                   jax.ShapeDtypeStruct((B, H, 12, S), F32)],
        grid=(B, S // ts),
        in_specs=[
            pl.BlockSpec((1, ts, NSA_PAD), lambda b, s: (b, s, P_NSA_OFF // NSA_PAD)),
            pl.BlockSpec((ts, KV_SLOT), lambda b, s: (s, 0)),
            pl.BlockSpec((ts, KV_SLOT), lambda b, s: (s, 0)),
            pl.BlockSpec((None, 1, LANES), lambda b, s: (layer, 0, 0)),
        ],
        out_specs=[pl.BlockSpec((1, NSA_HEADS, ts, HEAD_DIM), lambda b, s: (b, 0, s, 0)),
                   k_spec, k_spec, k_spec, vt_spec, k_spec, vt_spec,
                   pl.BlockSpec((1, H, 12, ts), lambda b, s: (b, 0, 0, s))],
        compiler_params=_params("parallel", "parallel"),
        name="nsa_prep",
    )(p_all, cos, sin, gate_b)


def _gelu_tanh(x):
    return 0.5 * x * (1.0 + jnp.tanh(float(np.sqrt(2.0 / np.pi)) * (x + 0.044715 * (x * x * x))))


def _nsa_cmp_kernel(kc_ref, vc_ref, kw1_ref, kw2_ref, kpe_ref, vw1_ref, vw2_ref, vpe_ref, cos_ref, sin_ref,
                    k_o, v_o):
    def hidden(g, w1_ref, pe_ref):
        half = CMP_STRIDE * HEAD_DIM
        first = _dot(g, w1_ref[:half, :])
        second = _dot(g, w1_ref[half:, :])
        n = first.shape[0]
        bias = _dot(pe_ref[...], w1_ref[...])[0:1]
        return _gelu_tanh(first + pltpu.roll(second, n - 1, 0) + bias).astype(BF16)

    k = _dot(hidden(kc_ref[0, 0], kw1_ref, kpe_ref), kw2_ref[...])
    k_o[0, 0] = _rope(k, cos_ref[...], sin_ref[...])[:, :HEAD_DIM].astype(BF16)
    v_o[0, 0] = _dot_nt(vw2_ref[...], hidden(vc_ref[0, 0], vw1_ref, vpe_ref)).astype(BF16)


def _nsa_compress(kc, vc, prm, cos_c, sin_c, layer):
    B, H, S, _ = kc.shape
    ng = S // CMP_STRIDE
    gw = CMP_STRIDE * HEAD_DIM
    g_k = kc.reshape(B, H, ng, gw)
    g_v = vc.reshape(B, H, ng, gw)
    gspec = pl.BlockSpec((1, 1, ng, gw), lambda b, h: (b, h, 0, 0))
    w1 = pl.BlockSpec((None, 2 * gw, CMP_HIDDEN), lambda b, h: (layer, 0, 0))
    w2 = pl.BlockSpec((None, CMP_HIDDEN, LANES), lambda b, h: (layer, 0, 0))
    pe = pl.BlockSpec((None, SUBLANES, 2 * gw), lambda b, h: (layer, 0, 0))
    tab = pl.BlockSpec((ng, LANES), lambda b, h: (0, 0))
    return pl.pallas_call(
        _nsa_cmp_kernel,
        out_shape=[jax.ShapeDtypeStruct((B, H, ng, HEAD_DIM), BF16), jax.ShapeDtypeStruct((B, H, HEAD_DIM, ng), BF16)],
        grid=(B, H),
        in_specs=[gspec, gspec, w1, w2, pe, w1,
                  pl.BlockSpec((None, HEAD_DIM, CMP_HIDDEN), lambda b, h: (layer, 0, 0)), pe, tab, tab],
        out_specs=[pl.BlockSpec((1, 1, ng, HEAD_DIM), lambda b, h: (b, h, 0, 0)),
                   pl.BlockSpec((1, 1, HEAD_DIM, ng), lambda b, h: (b, h, 0, 0))],
        compiler_params=_params("parallel", "parallel"),
        name="nsa_compress",
    )(g_k, g_v, prm["k_w1"], prm["k_w2"], prm["k_pe"], prm["v_w1"], prm["v_w2"], prm["v_pe"], cos_c, sin_c)


def _nsa_attn_kernel(q_ref, kcmp_ref, vcmp_ref, ks_ref, vs_ref, kw_ref, vw_ref, gt_ref, ov_ref, o_ref,
                     m_ref, l_ref, acc_ref, *, tq, n_cmp):
    G = NSA_GQA
    tk = tq
    qi = pl.program_id(2)
    q0 = qi * tq
    qs = [q_ref[0, g] for g in range(G)]
    pos = q0 + lax.broadcasted_iota(jnp.int32, (1, tq), 1)
    tiny = jnp.finfo(F32).tiny

    ncp = kcmp_ref.shape[2]
    n_idx = lax.broadcasted_iota(jnp.int32, (ncp, 1), 0)
    cmask = ((n_idx * CMP_STRIDE + (CMP_BLOCK - 1)) <= pos) & (n_idx < n_cmp)
    kc = kcmp_ref[0, 0]
    vct = vcmp_ref[0, 0]
    o_cmp = []
    psum = jnp.zeros((ncp, tq), F32)
    for g in range(G):
        s = jnp.where(cmask, _dot_nt(kc, qs[g]), NEG)
        m = jnp.max(s, axis=0, keepdims=True)
        e = jnp.where(cmask, jnp.exp(s - m), 0.0)
        p = e / jnp.maximum(jnp.sum(e, axis=0, keepdims=True), tiny)
        o_cmp.append(_dot(vct, p.astype(BF16)))
        psum = psum + p
    p_hi = psum.astype(BF16)
    p_lo = (psum - p_hi.astype(F32)).astype(BF16)
    imp = _dot(ov_ref[...], p_hi) + _dot(ov_ref[...], p_lo)

    n_sel = ov_ref.shape[0]
    blk = lax.broadcasted_iota(jnp.int32, (n_sel, 1), 0)
    cur = pos // SEL_BLOCK
    forced = (blk == 0) | (blk == cur) | (blk == cur - 1)
    valid = blk <= cur
    score = jnp.where(valid, jnp.where(forced, FORCE_SCORE, imp), -jnp.inf)
    rank = jnp.zeros((n_sel, tq), F32)
    for i in range(n_sel):
        ci = score[i:i + 1, :]
        beats = (ci > score) | ((ci == score) & (blk > i))
        rank = rank + jnp.where(beats, 1.0, 0.0)
    chosen = (rank < float(SEL_TOPK)) & valid
    sel_bias = jnp.where(chosen, 0.0, NEG).astype(BF16)

    kcol = lax.broadcasted_iota(jnp.int32, (tk, 1), 0)

    def reset():
        m_ref[...] = jnp.full_like(m_ref, NEG)
        l_ref[...] = jnp.zeros_like(l_ref)
        acc_ref[...] = jnp.zeros_like(acc_ref)

    def tile(k_ref, vt_ref, kt, bias):
        k0 = pl.multiple_of(kt * tk, tk)
        kb = k_ref[0, 0, pl.ds(k0, tk), :]
        vt = vt_ref[0, 0, :, pl.ds(k0, tk)]
        for g in range(G):
            s = _dot_nt(kb, qs[g])
            if bias is not None:
                s = s + bias
            m_prev = m_ref[g]
            m_new = jnp.maximum(m_prev, jnp.max(s, axis=0, keepdims=True))
            p = jnp.exp(s - m_new)
            corr = jnp.exp(m_prev - m_new)
            l_ref[g] = corr * l_ref[g] + jnp.sum(p, axis=0, keepdims=True)
            acc_ref[g] = corr * acc_ref[g] + _dot(vt, p.astype(BF16))
            m_ref[g] = m_new

    def result():
        return [acc_ref[g] / jnp.maximum(l_ref[g], tiny) for g in range(G)]

    causal = jnp.where(q0 + kcol <= pos, 0.0, NEG)

    erow = lax.broadcasted_iota(jnp.int32, (tk, n_sel), 0) // SEL_BLOCK
    ecol = lax.broadcasted_iota(jnp.int32, (tk, n_sel), 1)

    def sel_tile(kt, extra):
        expand = jnp.where(erow + kt * (tk // SEL_BLOCK) == ecol, 1.0, 0.0).astype(BF16)
        bias = _dot(expand, sel_bias)
        tile(ks_ref, vs_ref, kt, bias if extra is None else bias + extra)

    reset()

    def sel_body(kt, carry):
        sel_tile(kt, None)
        return carry

    lax.fori_loop(0, qi, sel_body, 0)
    sel_tile(qi, causal)
    o_sel = result()

    reset()

    @pl.when(qi >= 2)
    def _():
        far = jnp.where((q0 - 2 * tk) + kcol > pos - WINDOW, 0.0, NEG)
        tile(kw_ref, vw_ref, qi - 2, far)

    @pl.when(qi >= 1)
    def _():
        tile(kw_ref, vw_ref, qi - 1, None)

    tile(kw_ref, vw_ref, qi, causal)
    o_win = result()

    gt = gt_ref[0, 0]
    outs = [gt[g:g + 1] * o_cmp[g] + gt[G + g:G + g + 1] * o_sel[g] + gt[2 * G + g:2 * G + g + 1] * o_win[g]
            for g in range(G)]
    for half in range(G // 2):
        pair = jnp.concatenate(outs[2 * half:2 * half + 2], axis=0)
        o_ref[0, :, half * LANES:(half + 1) * LANES] = pair.T.astype(o_ref.dtype)


def _nsa_attention(q, k_cmp, v_cmp_t, ks, vs_t, kw, vw_t, gates_t, overlap_t, *, tq=256):
    B, _, S, _ = q.shape
    H, G = NSA_KV_HEADS, NSA_GQA
    tq = min(tq, S)
    assert WINDOW == 2 * tq or S <= tq
    ncp = k_cmp.shape[2]
    n_cmp = (S - CMP_BLOCK) // CMP_STRIDE + 1
    keys = pl.BlockSpec((1, 1, S, HEAD_DIM), lambda b, h, i: (b, h, 0, 0))
    vals = pl.BlockSpec((1, 1, HEAD_DIM, S), lambda b, h, i: (b, h, 0, 0))
    return pl.pallas_call(
        functools.partial(_nsa_attn_kernel, tq=tq, n_cmp=n_cmp),
        out_shape=jax.ShapeDtypeStruct((B, S, D_NSA), BF16),
        grid=(B, H, S // tq),
        in_specs=[
            pl.BlockSpec((1, G, tq, HEAD_DIM), lambda b, h, i: (b, h, i, 0)),
            pl.BlockSpec((1, 1, ncp, HEAD_DIM), lambda b, h, i: (b, h, 0, 0)),
            pl.BlockSpec((1, 1, HEAD_DIM, ncp), lambda b, h, i: (b, h, 0, 0)),
            keys, vals, keys, vals,
            pl.BlockSpec((1, 1, 12, tq), lambda b, h, i: (b, h, 0, i)),
            pl.BlockSpec(overlap_t.shape, lambda b, h, i: (0, 0)),
        ],
        out_specs=pl.BlockSpec((1, tq, G * HEAD_DIM), lambda b, h, i: (b, i, h)),
        scratch_shapes=[pltpu.VMEM((G, 1, tq), F32), pltpu.VMEM((G, 1, tq), F32),
                        pltpu.VMEM((G, HEAD_DIM, tq), F32)],
        compiler_params=_params("parallel", "parallel", "arbitrary"),
        name="nsa_attention",
    )(q, k_cmp, v_cmp_t, ks, vs_t, kw, vw_t, gates_t, overlap_t)


def _rope_tables(pos, heads):
    half = ROPE_DIM // 2
    inv_freq = ROPE_THETA ** (-jnp.arange(half, dtype=F32) / half)
    ang = pos.astype(F32)[:, None] * inv_freq
    cos, sin = jnp.cos(ang), jnp.sin(ang)
    n = pos.shape[0]
    rest = HEAD_DIM - ROPE_DIM
    cos_h = jnp.concatenate([cos, cos, jnp.ones((n, rest), F32)], axis=1)
    sin_h = jnp.concatenate([-sin, sin, jnp.zeros((n, rest), F32)], axis=1)
    return jnp.tile(cos_h, (1, heads)), jnp.tile(sin_h, (1, heads))


def _overlap_matrix(S, ncp):
    n_cmp = (S - CMP_BLOCK) // CMP_STRIDE + 1
    n_sel = S // SEL_BLOCK
    cmp_start = np.arange(n_cmp) * CMP_STRIDE
    sel_start = np.arange(n_sel) * SEL_BLOCK
    ov = np.clip(np.minimum(cmp_start[:, None] + CMP_BLOCK, sel_start[None, :] + SEL_BLOCK)
                 - np.maximum(cmp_start[:, None], sel_start[None, :]), 0, None) / CMP_BLOCK
    full = np.zeros((n_sel, ncp), np.float32)
    full[:, :n_cmp] = ov.T
    return jnp.asarray(full, BF16)


def _block_ones(n):
    idx = np.arange(n) // HEAD_DIM
    return jnp.asarray((idx[:, None] == idx[None, :]).astype(np.float32), BF16)


def _w_in_layout(w_in):
    L = w_in.shape[0]
    rw = w_in[:, :, :RW_COLS]
    pool = w_in[:, :, RW_COLS:RW_COLS + D_POOL]
    nsa = w_in[:, :, RW_COLS + D_POOL:]
    zeros = lambda n: jnp.zeros((L, D_MODEL, n), w_in.dtype)
    segs = [nsa[:, :, :D_NSA]]
    for i in range(6):
        segs += [nsa[:, :, D_NSA + i * NSA_KV:D_NSA + (i + 1) * NSA_KV], zeros(KV_SLOT - NSA_KV)]
    gates = nsa[:, :, D_NSA + 6 * NSA_KV:]
    gates = gates.reshape(L, D_MODEL, NSA_KV_HEADS, NSA_GQA, 3).transpose(0, 1, 2, 4, 3)
    segs += [gates.reshape(L, D_MODEL, 3 * NSA_HEADS),
             zeros(NSA_PAD - NSA_GATE_OFF - 3 * NSA_HEADS)]
    return jnp.concatenate([rw] + segs + [pool], axis=-1).astype(BF16)


def _gate_bias_layout(gate_b):
    L = gate_b.shape[0]
    gb = gate_b.reshape(L, NSA_KV_HEADS, NSA_GQA, 3).transpose(0, 1, 3, 2).reshape(L, 1, 3 * NSA_HEADS)
    return jnp.pad(gb, ((0, 0), (0, 0), (0, LANES - 3 * NSA_HEADS)))


def kernel(x, ffn1_w_up, ffn1_w_down, ln1_g, ln1_b, w_in, rw_mu, rw_w0, rw_w2, rw_a0, rw_a2, rw_g2, rw_k_k,
           rw_k_a, rw_r_k, rw_gn_g, rw_gn_b, pool_w, pool_b, pool_scale, nsa_cmp_pe_k, nsa_cmp_pe_v,
           nsa_cmp_k_w1, nsa_cmp_k_w2, nsa_cmp_v_w1, nsa_cmp_v_w2, nsa_gate_b, w_out, ln2_g, ln2_b,
           ffn2_w_up, ffn2_w_down, ln3_g, ln3_b):
    prm = _prepare(x.shape[1], ffn1_w_up, ffn1_w_down, ln1_g, ln1_b, w_in, rw_mu, rw_w0, rw_w2, rw_a0, rw_a2,
                   rw_g2, rw_k_k, rw_k_a, rw_r_k, rw_gn_g, rw_gn_b, pool_w, pool_b, pool_scale, nsa_cmp_pe_k,
                   nsa_cmp_pe_v, nsa_cmp_k_w1, nsa_cmp_k_w2, nsa_cmp_v_w1, nsa_cmp_v_w2, nsa_gate_b, w_out,
                   ln2_g, ln2_b, ffn2_w_up, ffn2_w_down, ln3_g, ln3_b)
    B, S, D = x.shape
    h = x.reshape(B * S, D)
    for l in range(w_in.shape[0]):
        h = _layer(h, prm, l, B, S)
    return h.reshape(B, S, D)


def _prepare(S, ffn1_w_up, ffn1_w_down, ln1_g, ln1_b, w_in, rw_mu, rw_w0, rw_w2, rw_a0, rw_a2, rw_g2, rw_k_k,
             rw_k_a, rw_r_k, rw_gn_g, rw_gn_b, pool_w, pool_b, pool_scale, nsa_cmp_pe_k, nsa_cmp_pe_v,
             nsa_cmp_k_w1, nsa_cmp_k_w2, nsa_cmp_v_w1, nsa_cmp_v_w2, nsa_gate_b, w_out, ln2_g, ln2_b,
             ffn2_w_up, ffn2_w_down, ln3_g, ln3_b):
    L = w_in.shape[0]
    fpad = D_FF_PAD - D_FF

    def up(w):
        a = jnp.pad(w[:, :, :D_FF], ((0, 0), (0, 0), (0, fpad))).astype(BF16)
        b = jnp.pad(w[:, :, D_FF:], ((0, 0), (0, 0), (0, fpad))).astype(BF16)
        return a, b

    def down(w):
        return jnp.pad(w, ((0, 0), (0, fpad), (0, 0))).astype(BF16)

    row = lambda v: v[:, None, :]
    f1a, f1b = up(ffn1_w_up)
    f2a, f2b = up(ffn2_w_up)
    w_out_b = w_out.astype(BF16)
    gw = CMP_BLOCK * HEAD_DIM
    pad_w2 = lambda w: jnp.pad(w, ((0, 0), (0, 0), (0, LANES - HEAD_DIM))).astype(BF16)
    pe_rows = lambda pe: jnp.broadcast_to(pe.reshape(L, 1, gw), (L, SUBLANES, gw)).astype(BF16)
    ncp = S // CMP_STRIDE
    cos_t, sin_t = _rope_tables(jnp.arange(S), 4)
    cos_c, sin_c = _rope_tables(jnp.arange(ncp) * CMP_STRIDE + (CMP_BLOCK - 1), 2)
    return dict(
        ffn1=(f1a, f1b, down(ffn1_w_down), row(ln1_g), row(ln1_b)),
        ffn2=(f2a, f2b, down(ffn2_w_down), row(ln3_g), row(ln3_b)),
        w_in=_w_in_layout(w_in),
        w_out=(w_out_b[:, :D_RWKV], w_out_b[:, D_RWKV:D_RWKV + D_POOL], w_out_b[:, D_RWKV + D_POOL:]),
        ln2=(row(ln2_g), row(ln2_b)),
        rw=dict(mu=row(rw_mu), w0=row(rw_w0), w2=rw_w2.astype(BF16), a0=row(rw_a0), a2=rw_a2.astype(BF16),
                g2=rw_g2.astype(BF16), k_k=row(rw_k_k), k_a=row(rw_k_a), r_k=rw_r_k.reshape(L, 1, D_RWKV),
                gn_g=row(rw_gn_g), gn_b=row(rw_gn_b), ones=_block_ones(D_RWKV)),
        cmp=dict(k_w1=nsa_cmp_k_w1.reshape(L, gw, CMP_HIDDEN).astype(BF16), k_w2=pad_w2(nsa_cmp_k_w2),
                 k_pe=pe_rows(nsa_cmp_pe_k),
                 v_w1=nsa_cmp_v_w1.reshape(L, gw, CMP_HIDDEN).astype(BF16), v_w2=nsa_cmp_v_w2.transpose(0, 2, 1).astype(BF16),
                 v_pe=pe_rows(nsa_cmp_pe_v)),
        gate_b=_gate_bias_layout(nsa_gate_b),
        pool=(pool_w.astype(BF16), row(pool_b), row(pool_scale)),
        rope=(cos_t, sin_t), rope_cmp=(cos_c, sin_c), overlap=_overlap_matrix(S, ncp))


def _mixers(p_all, prm, l):
    r, w, k, v, kk, be, bo, g = _rw_prep(p_all, prm["rw"], l)
    y_rw = _rw_scan(r, w, k, v, kk, be, bo, g, prm["rw"], l)
    y_pool = _pool_mix(p_all, *prm["pool"], l)
    q, kc, vc, ks, vs, kw, vw, gates = _nsa_prep(p_all, *prm["rope"], prm["gate_b"], l)
    k_cmp, v_cmp = _nsa_compress(kc, vc, prm["cmp"], *prm["rope_cmp"], l)
    y_nsa = _nsa_attention(q, k_cmp, v_cmp, ks, vs, kw, vw, gates, prm["overlap"])
    return y_rw, y_pool, y_nsa


def _layer(h, prm, l, B, S):
    T = B * S
    h = _ffn_ln(h, *prm["ffn1"], l)
    p_all = _in_proj(h, prm["w_in"], l).reshape(B, S, P_COLS)
    y_rw, y_pool, y_nsa = _mixers(p_all, prm, l)
    h = _out_proj_ln(h, y_rw.reshape(T, D_RWKV), y_pool.reshape(T, D_POOL), y_nsa.reshape(T, D_NSA),
                     prm["w_out"], *prm["ln2"], l)
    return _ffn_ln(h, *prm["ffn2"], l)
```

```python
import functools

import numpy as np
import jax
import jax.numpy as jnp
from jax import lax
from jax.experimental import pallas as pl
from jax.experimental.pallas import tpu as pltpu

F32 = jnp.float32
BF16 = jnp.bfloat16

D_MODEL = 2048
DEPTH = 4
HEAD_DIM = 64
D_RWKV = 768
D_POOL = 512
D_NSA = 768
RW_HEADS = 12
RW_DECAY_LORA = 64
RW_A_LORA = 64
RW_GATE_LORA = 128
RW_GN_EPS = 64e-5
RW_COLS = 3 * D_RWKV + RW_DECAY_LORA + RW_A_LORA + RW_GATE_LORA
POOL_WINDOWS = (2, 4, 8, 16)
POOL_GROUP = 128
NSA_HEADS = 12
NSA_KV_HEADS = 3
NSA_GQA = 4
NSA_KV = 192
NSA_COLS = D_NSA + 6 * NSA_KV + 3 * NSA_HEADS
CMP_BLOCK = 32
CMP_STRIDE = 16
CMP_HIDDEN = 256
SEL_BLOCK = 64
SEL_TOPK = 16
FORCE_SCORE = 1e9
WINDOW = 512
ROPE_THETA = 500000.0
ROPE_DIM = 16
D_FF = 5504
IN_COLS = RW_COLS + D_POOL + NSA_COLS
ALPHA = (2 * DEPTH) ** 0.25
LN_EPS = 1e-5

LANES = 128
SUBLANES = 8
VMEM_LIMIT = 56 * 1024 * 1024

KV_SLOT = 2 * LANES
NSA_PAD = 2560
P_RW_OFF = 0
P_NSA_OFF = RW_COLS
P_POOL_OFF = RW_COLS + NSA_PAD
P_COLS = RW_COLS + NSA_PAD + D_POOL
NSA_GATE_OFF = D_NSA + 6 * KV_SLOT
D_FF_PAD = 5632

NEG = -1e30


def _params(*sem):
    return pltpu.CompilerParams(dimension_semantics=sem, vmem_limit_bytes=VMEM_LIMIT)


def _layer_norm(z, g, b):
    mu = jnp.mean(z, axis=-1, keepdims=True)
    zc = z - mu
    var = jnp.mean(zc * zc, axis=-1, keepdims=True)
    return zc * lax.rsqrt(var + LN_EPS) * g + b


def _dot(a, b):
    return jnp.dot(a, b, preferred_element_type=F32)


def _dot_nt(a, b):
    return lax.dot_general(a, b, (((1,), (1,)), ((), ())), preferred_element_type=F32)


def _dot_split(x, w):
    hi = x.astype(BF16)
    lo = (x - hi.astype(F32)).astype(BF16)
    return _dot(hi, w) + _dot(lo, w)


FFN_LN_ROWS = 256


def _ffn_kernel(x_ref, wa_ref, wb_ref, wd_ref, g_ref, b_ref, o_ref, xb_ref):
    k = pl.program_id(1)

    @pl.when(k == 0)
    def _():
        xb_ref[...] = x_ref[...].astype(BF16)
        o_ref[...] = jnp.zeros_like(o_ref)

    xb = xb_ref[...]
    a = _dot(xb, wa_ref[...])
    b = _dot(xb, wb_ref[...])
    h = (a / (1.0 + jnp.exp(-a))) * b
    o_ref[...] += _dot(h.astype(BF16), wd_ref[...])

    @pl.when(k == pl.num_programs(1) - 1)
    def _():
        for c in range(o_ref.shape[0] // FFN_LN_ROWS):
            rows = slice(c * FFN_LN_ROWS, (c + 1) * FFN_LN_ROWS)
            z = ALPHA * x_ref[rows, :] + 0.5 * o_ref[rows, :]
            o_ref[rows, :] = _layer_norm(z, g_ref[...], b_ref[...])


def _ffn_ln(x, wa, wb, wd, g, b, layer, *, tm=1024, tf=256):
    T, D = x.shape
    fp = wa.shape[-1]
    tm = min(tm, T)
    return pl.pallas_call(
        _ffn_kernel,
        out_shape=jax.ShapeDtypeStruct((T, D), F32),
        grid=(T // tm, fp // tf),
        in_specs=[
            pl.BlockSpec((tm, D), lambda i, k: (i, 0)),
            pl.BlockSpec((None, D, tf), lambda i, k: (layer, 0, k)),
            pl.BlockSpec((None, D, tf), lambda i, k: (layer, 0, k)),
            pl.BlockSpec((None, tf, D), lambda i, k: (layer, k, 0)),
            pl.BlockSpec((None, 1, D), lambda i, k: (layer, 0, 0)),
            pl.BlockSpec((None, 1, D), lambda i, k: (layer, 0, 0)),
        ],
        out_specs=pl.BlockSpec((tm, D), lambda i, k: (i, 0)),
        scratch_shapes=[pltpu.VMEM((tm, D), BF16)],
        compiler_params=_params("parallel", "arbitrary"),
        name="ffn_ln",
    )(x, wa, wb, wd, g, b)


def _inproj_kernel(x_ref, w_ref, o_ref):
    o_ref[...] = _dot(x_ref[...].astype(BF16), w_ref[...])


def _in_proj(x, w, layer, *, tm=1024, tn=512):
    T, D = x.shape
    n = w.shape[-1]
    tm = min(tm, T)
    return pl.pallas_call(
        _inproj_kernel,
        out_shape=jax.ShapeDtypeStruct((T, n), F32),
        grid=(T // tm, n // tn),
        in_specs=[
            pl.BlockSpec((tm, D), lambda i, j: (i, 0)),
            pl.BlockSpec((None, D, tn), lambda i, j: (layer, 0, j)),
        ],
        out_specs=pl.BlockSpec((tm, tn), lambda i, j: (i, j)),
        compiler_params=_params("parallel", "arbitrary"),
        name="in_proj",
    )(x, w)


def _outproj_kernel(x_ref, yr_ref, yp_ref, yn_ref, wr_ref, wp_ref, wn_ref, g_ref, b_ref, o_ref):
    y = _dot(yr_ref[...], wr_ref[...]) + _dot(yp_ref[...], wp_ref[...]) + _dot(yn_ref[...], wn_ref[...])
    o_ref[...] = _layer_norm(ALPHA * x_ref[...] + y, g_ref[...], b_ref[...])


def _out_proj_ln(x, y_rw, y_pool, y_nsa, w_out, g, b, layer, *, tm=512):
    T, D = x.shape
    return pl.pallas_call(
        _outproj_kernel,
        out_shape=jax.ShapeDtypeStruct((T, D), F32),
        grid=(T // tm,),
        in_specs=[
            pl.BlockSpec((tm, D), lambda i: (i, 0)),
            pl.BlockSpec((tm, D_RWKV), lambda i: (i, 0)),
            pl.BlockSpec((tm, D_POOL), lambda i: (i, 0)),
            pl.BlockSpec((tm, D_NSA), lambda i: (i, 0)),
            pl.BlockSpec((None, D_RWKV, D), lambda i: (layer, 0, 0)),
            pl.BlockSpec((None, D_POOL, D), lambda i: (layer, 0, 0)),
            pl.BlockSpec((None, D_NSA, D), lambda i: (layer, 0, 0)),
            pl.BlockSpec((None, 1, D), lambda i: (layer, 0, 0)),
            pl.BlockSpec((None, 1, D), lambda i: (layer, 0, 0)),
        ],
        out_specs=pl.BlockSpec((tm, D), lambda i: (i, 0)),
        compiler_params=_params("parallel"),
        name="out_proj_ln",
    )(x, y_rw, y_pool, y_nsa, w_out[0], w_out[1], w_out[2], g, b)


POOL_HALO = 16


def _pool_kernel(p_ref, halo_ref, w_ref, b_ref, sc_ref, o_ref, xs_ref, *, ts):
    s = pl.program_id(1)
    x = p_ref[0]
    halo = jnp.where(s > 0, halo_ref[0], 0.0)
    xs_ref[0:POOL_HALO, :] = halo
    xs_ref[POOL_HALO:POOL_HALO + ts, :] = x
    t1 = (s * ts + 1 + lax.broadcasted_iota(jnp.int32, (ts, 1), 0)).astype(F32)
    for gi, win in enumerate(POOL_WINDOWS):
        c0 = gi * POOL_GROUP
        acc = x[:, c0:c0 + POOL_GROUP]
        for j in range(1, win):
            acc = acc + xs_ref[POOL_HALO - j:POOL_HALO - j + ts, c0:c0 + POOL_GROUP]
        pooled = acc / jnp.minimum(t1, float(win)) - x[:, c0:c0 + POOL_GROUP]
        z = _dot(pooled.astype(BF16), w_ref[gi]) + b_ref[:, c0:c0 + POOL_GROUP]
        o_ref[0, :, c0:c0 + POOL_GROUP] = (z * sc_ref[:, c0:c0 + POOL_GROUP]).astype(o_ref.dtype)


def _pool_mix(p_all, pool_w, pool_b, pool_scale, layer, *, ts=512):
    B, S, _ = p_all.shape
    cb = P_POOL_OFF // D_POOL
    hb = ts // POOL_HALO
    return pl.pallas_call(
        functools.partial(_pool_kernel, ts=ts),
        out_shape=jax.ShapeDtypeStruct((B, S, D_POOL), BF16),
        grid=(B, S // ts),
        in_specs=[
            pl.BlockSpec((1, ts, D_POOL), lambda b, s: (b, s, cb)),
            pl.BlockSpec((1, POOL_HALO, D_POOL), lambda b, s: (b, jnp.maximum(s * hb - 1, 0), cb)),
            pl.BlockSpec((None, 4, POOL_GROUP, POOL_GROUP), lambda b, s: (layer, 0, 0, 0)),
            pl.BlockSpec((None, 1, D_POOL), lambda b, s: (layer, 0, 0)),
            pl.BlockSpec((None, 1, D_POOL), lambda b, s: (layer, 0, 0)),
        ],
        out_specs=pl.BlockSpec((1, ts, D_POOL), lambda b, s: (b, s, 0)),
        scratch_shapes=[pltpu.VMEM((ts + POOL_HALO, D_POOL), F32)],
        compiler_params=_params("parallel", "parallel"),
        name="pool_mix",
    )(p_all, p_all, pool_w, pool_b, pool_scale)


def _softplus(z):
    return jnp.maximum(z, 0.0) + jnp.log1p(jnp.exp(-jnp.abs(z)))


def _sigmoid(z):
    return 1.0 / (1.0 + jnp.exp(-z))


def _rw_prep_kernel(p_ref, prev_ref, mu_ref, w0_ref, w2_ref, a0_ref, a2_ref, g2_ref, kk_ref, ka_ref,
                    rk_ref, ones_ref,
                    r_o, w_o, k_o, v_o, kk_o, be_o, bo_o, g_o, *, ts):
    s = pl.program_id(1)
    x = p_ref[0]
    last = jnp.where(s > 0, prev_ref[0][SUBLANES - 1:SUBLANES, :], 0.0)
    row = lax.broadcasted_iota(jnp.int32, (ts, 1), 0)
    shifted = jnp.where(row == 0, last, pltpu.roll(x, 1, 0))
    xm = x + (shifted - x) * mu_ref[...]
    c = D_RWKV
    r = xm[:, 0:c]
    k = xm[:, c:2 * c]
    v = xm[:, 2 * c:3 * c]
    lora = xm[:, 3 * c:3 * c + LANES]
    wl = lora[:, :RW_DECAY_LORA]
    al = lora[:, RW_DECAY_LORA:]
    gl = xm[:, 3 * c + LANES:]
    w = -_softplus(-(w0_ref[...] + _dot(jnp.tanh(wl).astype(BF16), w2_ref[...]))) - 0.5
    decay = jnp.exp(-jnp.exp(w))
    a = _sigmoid(a0_ref[...] + _dot(al.astype(BF16), a2_ref[...]))
    g = _dot(_sigmoid(gl).astype(BF16), g2_ref[...])
    ones = ones_ref[...]
    kk = k * kk_ref[...]
    nrm = jnp.sqrt(_dot_split(kk * kk, ones))
    kk = kk / jnp.maximum(nrm, 1e-12)
    k_mod = k * (1.0 + (a - 1.0) * ka_ref[...])
    bonus = _dot_split(r * k_mod * rk_ref[...], ones) * v
    r_o[0] = r
    w_o[0] = decay
    k_o[0] = k_mod
    v_o[0] = v
    kk_o[0] = kk
    be_o[0] = kk * a
    bo_o[0] = bonus
    g_o[0] = g


def _rw_prep(p_all, prm, layer, *, ts=256):
    B, S, _ = p_all.shape
    cb = P_RW_OFF // RW_COLS
    hb = ts // SUBLANES
    vec = lambda n: pl.BlockSpec((None, 1, n), lambda b, s: (layer, 0, 0))
    mat = lambda m, n: pl.BlockSpec((None, m, n), lambda b, s: (layer, 0, 0))
    out = jax.ShapeDtypeStruct((B, S, D_RWKV), F32)
    ospec = pl.BlockSpec((1, ts, D_RWKV), lambda b, s: (b, s, 0))
    return pl.pallas_call(
        functools.partial(_rw_prep_kernel, ts=ts),
        out_shape=[out] * 8,
        grid=(B, S // ts),
        in_specs=[
            pl.BlockSpec((1, ts, RW_COLS), lambda b, s: (b, s, cb)),
            pl.BlockSpec((1, SUBLANES, RW_COLS), lambda b, s: (b, jnp.maximum(s * hb - 1, 0), cb)),
            vec(RW_COLS), vec(D_RWKV), mat(RW_DECAY_LORA, D_RWKV), vec(D_RWKV), mat(RW_A_LORA, D_RWKV),
            mat(RW_GATE_LORA, D_RWKV), vec(D_RWKV), vec(D_RWKV), vec(D_RWKV),
            pl.BlockSpec((D_RWKV, D_RWKV), lambda b, s: (0, 0)),
        ],
        out_specs=[ospec] * 8,
        compiler_params=_params("parallel", "parallel"),
        name="rw_prep",
    )(p_all, p_all, prm["mu"], prm["w0"], prm["w2"], prm["a0"], prm["a2"], prm["g2"], prm["k_k"],
      prm["k_a"], prm["r_k"], prm["ones"])


RW_PAIRS = D_RWKV // LANES


def _rw_scan_kernel(r_ref, w_ref, k_ref, v_ref, kk_ref, be_ref, bo_ref, g_ref, gng_ref, gnb_ref, ones_ref,
                    pair_ref, o_ref, st_ref, vc_ref, sr_ref, y_ref, *, ts, nb):
    s = pl.program_id(1)

    @pl.when(s == 0)
    def _():
        st_ref[...] = jnp.zeros_like(st_ref)

    rows = RW_PAIRS * HEAD_DIM
    lane = lax.broadcasted_iota(jnp.int32, (rows, LANES), 1)
    row = lax.broadcasted_iota(jnp.int32, (rows, LANES), 0)
    eye = jnp.where(lane % HEAD_DIM == row % HEAD_DIM, 1.0, 0.0)
    ones_pair = pair_ref[...]

    def spread(x8, j):
        return jnp.concatenate([jnp.broadcast_to(x8[j:j + 1, pp * LANES:(pp + 1) * LANES], (HEAD_DIM, LANES))
                                for pp in range(RW_PAIRS)], axis=0)

    def step(i, carry):
        base = pl.multiple_of(i * SUBLANES, SUBLANES)
        tiles = [[ref[b, pl.ds(base, SUBLANES), :] for ref in (kk_ref, w_ref, be_ref, k_ref, r_ref, v_ref)]
                 for b in range(nb)]
        for b in range(nb):
            lhs = jnp.concatenate([eye * spread(tiles[b][5], j) for j in range(SUBLANES)], axis=0)
            vc_ref[b] = _dot(lhs.astype(BF16), ones_pair)
        sts = [st_ref[b] for b in range(nb)]
        for j in range(SUBLANES):
            blk = slice(j * rows, (j + 1) * rows)
            for b in range(nb):
                kk8, w8, be8, k8, r8, _ = tiles[b]
                st = sts[b]
                sa = _dot((st * spread(kk8, j)).astype(BF16), ones_pair)
                st = st * spread(w8, j) - sa * spread(be8, j) + vc_ref[b, blk, :] * spread(k8, j)
                sr_ref[b, blk, :] = (st * spread(r8, j)).astype(BF16)
                sts[b] = st
        for b in range(nb):
            st_ref[b] = sts[b]
            yb = _dot(sr_ref[b], ones_pair)
            out_rows = []
            for j in range(SUBLANES):
                diag = yb[j * rows:(j + 1) * rows] * eye
                out_rows.append(jnp.concatenate(
                    [jnp.sum(diag[pp * HEAD_DIM:(pp + 1) * HEAD_DIM], axis=0, keepdims=True)
                     for pp in range(RW_PAIRS)], axis=1))
            y_ref[b, pl.ds(base, SUBLANES), :] = jnp.concatenate(out_rows, axis=0)
        return carry

    lax.fori_loop(0, ts // SUBLANES, step, 0)

    ones = ones_ref[...]
    for b in range(nb):
        y = y_ref[b]
        mean = _dot_split(y, ones) * (1.0 / HEAD_DIM)
        yc = y - mean
        var = _dot_split(yc * yc, ones) * (1.0 / HEAD_DIM)
        yn = yc * lax.rsqrt(var + RW_GN_EPS) * gng_ref[...] + gnb_ref[...]
        o_ref[b] = ((yn + bo_ref[b]) * g_ref[b]).astype(o_ref.dtype)


def _rw_scan(r, w, k, v, kk, be, bo, g, prm, layer, *, ts=256):
    B, S, _ = r.shape
    nb = 2 if B % 2 == 0 else 1
    rows = RW_PAIRS * HEAD_DIM
    seq = pl.BlockSpec((nb, ts, D_RWKV), lambda b, s: (b, s, 0))
    vec = pl.BlockSpec((None, 1, D_RWKV), lambda b, s: (layer, 0, 0))
    return pl.pallas_call(
        functools.partial(_rw_scan_kernel, ts=ts, nb=nb),
        out_shape=jax.ShapeDtypeStruct((B, S, D_RWKV), BF16),
        grid=(B // nb, S // ts),
        in_specs=[seq] * 8 + [vec, vec, pl.BlockSpec((D_RWKV, D_RWKV), lambda b, s: (0, 0)),
                              pl.BlockSpec((LANES, LANES), lambda b, s: (0, 0))],
        out_specs=seq,
        scratch_shapes=[pltpu.VMEM((nb, rows, LANES), F32), pltpu.VMEM((nb, SUBLANES * rows, LANES), F32),
                        pltpu.VMEM((nb, SUBLANES * rows, LANES), BF16), pltpu.VMEM((nb, ts, D_RWKV), F32)],
        compiler_params=_params("parallel", "arbitrary"),
        name="rw_scan",
    )(r, w, k, v, kk, be, bo, g, prm["gn_g"], prm["gn_b"], prm["ones"], prm["ones"][:LANES, :LANES])


def _rope(x, cos, sin):
    half = ROPE_DIM // 2
    w = x.shape[-1]
    d = lax.broadcasted_iota(jnp.int32, x.shape, 1) % HEAD_DIM
    partner = jnp.where(d < half, pltpu.roll(x, w - half, 1), pltpu.roll(x, half, 1))
    return x * cos + partner * sin


def _nsa_prep_kernel(p_ref, cos_ref, sin_ref, gb_ref, q_o, kc_o, vc_o, ks_o, vs_o, kw_o, vw_o, gt_o):
    cos = cos_ref[...]
    sin = sin_ref[...]
    for hq in range(NSA_HEADS // 4):
        x = p_ref[0, :, hq * KV_SLOT:(hq + 1) * KV_SLOT]
        qr = _rope(x, cos, sin) * (HEAD_DIM ** -0.5)
        for j in range(4):
            q_o[0, 4 * hq + j] = qr[:, j * HEAD_DIM:(j + 1) * HEAD_DIM].astype(BF16)
    for i, ref in enumerate((kc_o, vc_o, ks_o, vs_o, kw_o, vw_o)):
        part = p_ref[0, :, D_NSA + i * KV_SLOT:D_NSA + (i + 1) * KV_SLOT]
        if i in (2, 4):
            part = _rope(part, cos, sin)
        if i in (3, 5):
            part = part.T
            for h in range(NSA_KV_HEADS):
                ref[0, h] = part[h * HEAD_DIM:(h + 1) * HEAD_DIM, :].astype(BF16)
        else:
            for h in range(NSA_KV_HEADS):
                ref[0, h] = part[:, h * HEAD_DIM:(h + 1) * HEAD_DIM].astype(BF16)
    gts = _sigmoid(p_ref[0, :, NSA_GATE_OFF:NSA_GATE_OFF + LANES] + gb_ref[...]).T
    for h in range(NSA_KV_HEADS):
        gt_o[0, h] = gts[h * 12:(h + 1) * 12, :]


def _nsa_prep(p_all, cos, sin, gate_b, layer, *, ts=256):
    B, S, _ = p_all.shape
    H = NSA_KV_HEADS
    k_shape = jax.ShapeDtypeStruct((B, H, S, HEAD_DIM), BF16)
    k_spec = pl.BlockSpec((1, H, ts, HEAD_DIM), lambda b, s: (b, 0, s, 0))
    vt_shape = jax.ShapeDtypeStruct((B, H, HEAD_DIM, S), BF16)
    vt_spec = pl.BlockSpec((1, H, HEAD_DIM, ts), lambda b, s: (b, 0, 0, s))
    return pl.pallas_call(
        _nsa_prep_kernel,
        out_shape=[jax.ShapeDtypeStruct((B, NSA_HEADS, S, HEAD_DIM), BF16),
                   k_shape, k_shape, k_shape, vt_shape, k_shape, vt_shape,
                   jax.ShapeDtypeStruct((B, H, 12, S), F32)],
        grid=(B, S // ts),
        in_specs=[
            pl.BlockSpec((1, ts, NSA_PAD), lambda b, s: (b, s, P_NSA_OFF // NSA_PAD)),
            pl.BlockSpec((ts, KV_SLOT), lambda b, s: (s, 0)),
            pl.BlockSpec((ts, KV_SLOT), lambda b, s: (s, 0)),
            pl.BlockSpec((None, 1, LANES), lambda b, s: (layer, 0, 0)),
        ],
        out_specs=[pl.BlockSpec((1, NSA_HEADS, ts, HEAD_DIM), lambda b, s: (b, 0, s, 0)),
                   k_spec, k_spec, k_spec, vt_spec, k_spec, vt_spec,
                   pl.BlockSpec((1, H, 12, ts), lambda b, s: (b, 0, 0, s))],
        compiler_params=_params("parallel", "parallel"),
        name="nsa_prep",
    )(p_all, cos, sin, gate_b)


def _gelu_tanh(x):
    return 0.5 * x * (1.0 + jnp.tanh(float(np.sqrt(2.0 / np.pi)) * (x + 0.044715 * (x * x * x))))


def _nsa_cmp_kernel(kc_ref, vc_ref, kw1_ref, kw2_ref, kpe_ref, vw1_ref, vw2_ref, vpe_ref, cos_ref, sin_ref,
                    k_o, v_o):
    def hidden(g, w1_ref, pe_ref):
        half = CMP_STRIDE * HEAD_DIM
        first = _dot(g, w1_ref[:half, :])
        second = _dot(g, w1_ref[half:, :])
        n = first.shape[0]
        bias = _dot(pe_ref[...], w1_ref[...])[0:1]
        return _gelu_tanh(first + pltpu.roll(second, n - 1, 0) + bias).astype(BF16)

    k = _dot(hidden(kc_ref[0, 0], kw1_ref, kpe_ref), kw2_ref[...])
    k_o[0, 0] = _rope(k, cos_ref[...], sin_ref[...])[:, :HEAD_DIM].astype(BF16)
    v_o[0, 0] = _dot_nt(vw2_ref[...], hidden(vc_ref[0, 0], vw1_ref, vpe_ref)).astype(BF16)


def _nsa_compress(kc, vc, prm, cos_c, sin_c, layer):
    B, H, S, _ = kc.shape
    ng = S // CMP_STRIDE
    gw = CMP_STRIDE * HEAD_DIM
    g_k = kc.reshape(B, H, ng, gw)
    g_v = vc.reshape(B, H, ng, gw)
    gspec = pl.BlockSpec((1, 1, ng, gw), lambda b, h: (b, h, 0, 0))
    w1 = pl.BlockSpec((None, 2 * gw, CMP_HIDDEN), lambda b, h: (layer, 0, 0))
    w2 = pl.BlockSpec((None, CMP_HIDDEN, LANES), lambda b, h: (layer, 0, 0))
    pe = pl.BlockSpec((None, SUBLANES, 2 * gw), lambda b, h: (layer, 0, 0))
    tab = pl.BlockSpec((ng, LANES), lambda b, h: (0, 0))
    return pl.pallas_call(
        _nsa_cmp_kernel,
        out_shape=[jax.ShapeDtypeStruct((B, H, ng, HEAD_DIM), BF16), jax.ShapeDtypeStruct((B, H, HEAD_DIM, ng), BF16)],
        grid=(B, H),
        in_specs=[gspec, gspec, w1, w2, pe, w1,
                  pl.BlockSpec((None, HEAD_DIM, CMP_HIDDEN), lambda b, h: (layer, 0, 0)), pe, tab, tab],
        out_specs=[pl.BlockSpec((1, 1, ng, HEAD_DIM), lambda b, h: (b, h, 0, 0)),
                   pl.BlockSpec((1, 1, HEAD_DIM, ng), lambda b, h: (b, h, 0, 0))],
        compiler_params=_params("parallel", "parallel"),
        name="nsa_compress",
    )(g_k, g_v, prm["k_w1"], prm["k_w2"], prm["k_pe"], prm["v_w1"], prm["v_w2"], prm["v_pe"], cos_c, sin_c)


def _nsa_attn_kernel(q_ref, kcmp_ref, vcmp_ref, ks_ref, vs_ref, kw_ref, vw_ref, gt_ref, ov_ref, o_ref,
                     m_ref, l_ref, acc_ref, sb_ref, *, tq):
    G = NSA_GQA
    tk = tq
    qi = pl.program_id(2)
    q0 = qi * tq
    q_all = q_ref[0].reshape(G * tq, HEAD_DIM)
    pos = q0 + lax.broadcasted_iota(jnp.int32, (1, tq), 1)
    tiny = jnp.finfo(F32).tiny
    heads = lambda x: jnp.tile(x, (1, G))
    split = lambda x: [x[:, g * tq:(g + 1) * tq] for g in range(G)]

    ncp = kcmp_ref.shape[2]
    n_idx = lax.broadcasted_iota(jnp.int32, (ncp, 1), 0)
    cbias = jnp.where((n_idx * CMP_STRIDE + (CMP_BLOCK - 1)) <= pos, 0.0, NEG)
    any_cmp = jnp.where(pos >= CMP_BLOCK - 1, 1.0, 0.0)
    s = _dot_nt(kcmp_ref[0, 0], q_all) + heads(cbias)
    e = jnp.exp(s - jnp.max(s, axis=0, keepdims=True))
    p = e * (heads(any_cmp) / jnp.maximum(jnp.sum(e, axis=0, keepdims=True), tiny))
    o_cmp = split(_dot(vcmp_ref[0, 0], p.astype(BF16)))
    ps = split(p)
    psum = (ps[0] + ps[1]) + (ps[2] + ps[3])
    p_hi = psum.astype(BF16)
    p_lo = (psum - p_hi.astype(F32)).astype(BF16)
    imp = _dot(ov_ref[...], p_hi) + _dot(ov_ref[...], p_lo)

    n_sel = ov_ref.shape[0]
    blk = lax.broadcasted_iota(jnp.int32, (n_sel, 1), 0)
    cur = pos // SEL_BLOCK
    forced = (blk == 0) | (blk == cur) | (blk == cur - 1)
    valid = blk <= cur
    score = jnp.where(valid, jnp.where(forced, FORCE_SCORE, imp), -jnp.inf)
    rank = jnp.zeros((n_sel, tq), F32)
    for i in range(n_sel):
        ci = score[i:i + 1, :]
        beats = (ci > score) | ((ci == score) & (blk > i))
        rank = rank + jnp.where(beats, 1.0, 0.0)
    chosen = (rank < float(SEL_TOPK)) & valid
    sel_bias = jnp.where(chosen, 0.0, NEG)
    for j in range(n_sel):
        sb_ref[j] = jnp.broadcast_to(sel_bias[j:j + 1, :], (SUBLANES, tq))

    kcol = lax.broadcasted_iota(jnp.int32, (tk, 1), 0)

    def scores(kt):
        k0 = pl.multiple_of(kt * tk, tk)
        return _dot_nt(ks_ref[0, 0, pl.ds(k0, tk), :], q_all)

    def sel_update(kt, s, extra=None):
        bpt = tk // SEL_BLOCK
        bias = jnp.concatenate([jnp.tile(sb_ref[kt * bpt + j], (SEL_BLOCK // SUBLANES, 1)) for j in range(bpt)],
                               axis=0)
        if extra is not None:
            bias = bias + extra
        k0 = pl.multiple_of(kt * tk, tk)
        vt = vs_ref[0, 0, :, pl.ds(k0, tk)]
        s = s + heads(bias)
        m_prev = m_ref[...]
        m_new = jnp.maximum(m_prev, jnp.max(s, axis=0, keepdims=True))
        p = jnp.exp(s - m_new)
        corr = jnp.exp(m_prev - m_new)
        l_ref[...] = corr * l_ref[...] + jnp.sum(p, axis=0, keepdims=True)
        acc_ref[...] = corr * acc_ref[...] + _dot(vt, p.astype(BF16))
        m_ref[...] = m_new

    m_ref[...] = jnp.full_like(m_ref, NEG)
    l_ref[...] = jnp.zeros_like(l_ref)
    acc_ref[...] = jnp.zeros_like(acc_ref)

    def sel_body(i, carry):
        s_a, s_b = scores(2 * i), scores(2 * i + 1)
        sel_update(2 * i, s_a)
        sel_update(2 * i + 1, s_b)
        return carry

    lax.fori_loop(0, qi // 2, sel_body, 0)

    @pl.when(qi % 2 == 1)
    def _():
        sel_update(qi - 1, scores(qi - 1))

    causal = jnp.where(q0 + kcol <= pos, 0.0, NEG)
    sel_update(qi, scores(qi), causal)
    o_sel = split(acc_ref[...] / jnp.maximum(l_ref[...], tiny))

    wk = min(WINDOW + tq, ks_ref.shape[2])
    w0 = pl.multiple_of(jnp.maximum(q0 + tq - wk, 0), tk)
    wpos = w0 + lax.broadcasted_iota(jnp.int32, (wk, 1), 0)
    wbias = jnp.where((wpos <= pos) & (wpos > pos - WINDOW), 0.0, NEG)
    s = _dot_nt(kw_ref[0, 0, pl.ds(w0, wk), :], q_all) + heads(wbias)
    e = jnp.exp(s - jnp.max(s, axis=0, keepdims=True))
    pv = _dot(vw_ref[0, 0, :, pl.ds(w0, wk)], e.astype(BF16))
    o_win = split(pv / jnp.maximum(jnp.sum(e, axis=0, keepdims=True), tiny))

    gt = gt_ref[0, 0]
    outs = [gt[g:g + 1] * o_cmp[g] + gt[G + g:G + g + 1] * o_sel[g] + gt[2 * G + g:2 * G + g + 1] * o_win[g]
            for g in range(G)]
    for half in range(G // 2):
        pair = jnp.concatenate(outs[2 * half:2 * half + 2], axis=0)
        o_ref[0, :, half * LANES:(half + 1) * LANES] = pair.T.astype(o_ref.dtype)


def _nsa_attention(q, k_cmp, v_cmp_t, ks, vs_t, kw, vw_t, gates_t, overlap_t, *, tq=256):
    B, _, S, _ = q.shape
    H, G = NSA_KV_HEADS, NSA_GQA
    tq = min(tq, S)
    assert WINDOW == 2 * tq or S <= tq
    ncp = k_cmp.shape[2]
    n_cmp = (S - CMP_BLOCK) // CMP_STRIDE + 1
    keys = pl.BlockSpec((1, 1, S, HEAD_DIM), lambda b, h, i: (b, h, 0, 0))
    vals = pl.BlockSpec((1, 1, HEAD_DIM, S), lambda b, h, i: (b, h, 0, 0))
    return pl.pallas_call(
        functools.partial(_nsa_attn_kernel, tq=tq),
        out_shape=jax.ShapeDtypeStruct((B, S, D_NSA), BF16),
        grid=(B, H, S // tq),
        in_specs=[
            pl.BlockSpec((1, G, tq, HEAD_DIM), lambda b, h, i: (b, h, i, 0)),
            pl.BlockSpec((1, 1, ncp, HEAD_DIM), lambda b, h, i: (b, h, 0, 0)),
            pl.BlockSpec((1, 1, HEAD_DIM, ncp), lambda b, h, i: (b, h, 0, 0)),
            keys, vals, keys, vals,
            pl.BlockSpec((1, 1, 12, tq), lambda b, h, i: (b, h, 0, i)),
            pl.BlockSpec(overlap_t.shape, lambda b, h, i: (0, 0)),
        ],
        out_specs=pl.BlockSpec((1, tq, G * HEAD_DIM), lambda b, h, i: (b, i, h)),
        scratch_shapes=[pltpu.VMEM((1, G * tq), F32), pltpu.VMEM((1, G * tq), F32),
                        pltpu.VMEM((HEAD_DIM, G * tq), F32), pltpu.VMEM((S // SEL_BLOCK, SUBLANES, tq), F32)],
        compiler_params=_params("parallel", "parallel", "arbitrary"),
        name="nsa_attention",
    )(q, k_cmp, v_cmp_t, ks, vs_t, kw, vw_t, gates_t, overlap_t)


def _rope_tables(pos, heads):
    half = ROPE_DIM // 2
    inv_freq = ROPE_THETA ** (-jnp.arange(half, dtype=F32) / half)
    ang = pos.astype(F32)[:, None] * inv_freq
    cos, sin = jnp.cos(ang), jnp.sin(ang)
    n = pos.shape[0]
    rest = HEAD_DIM - ROPE_DIM
    cos_h = jnp.concatenate([cos, cos, jnp.ones((n, rest), F32)], axis=1)
    sin_h = jnp.concatenate([-sin, sin, jnp.zeros((n, rest), F32)], axis=1)
    return jnp.tile(cos_h, (1, heads)), jnp.tile(sin_h, (1, heads))


def _overlap_matrix(S, ncp):
    n_cmp = (S - CMP_BLOCK) // CMP_STRIDE + 1
    n_sel = S // SEL_BLOCK
    cmp_start = np.arange(n_cmp) * CMP_STRIDE
    sel_start = np.arange(n_sel) * SEL_BLOCK
    ov = np.clip(np.minimum(cmp_start[:, None] + CMP_BLOCK, sel_start[None, :] + SEL_BLOCK)
                 - np.maximum(cmp_start[:, None], sel_start[None, :]), 0, None) / CMP_BLOCK
    full = np.zeros((n_sel, ncp), np.float32)
    full[:, :n_cmp] = ov.T
    return jnp.asarray(full, BF16)


def _block_ones(n):
    idx = np.arange(n) // HEAD_DIM
    return jnp.asarray((idx[:, None] == idx[None, :]).astype(np.float32), BF16)


def _w_in_layout(w_in):
    L = w_in.shape[0]
    rw = w_in[:, :, :RW_COLS]
    pool = w_in[:, :, RW_COLS:RW_COLS + D_POOL]
    nsa = w_in[:, :, RW_COLS + D_POOL:]
    zeros = lambda n: jnp.zeros((L, D_MODEL, n), w_in.dtype)
    segs = [nsa[:, :, :D_NSA]]
    for i in range(6):
        segs += [nsa[:, :, D_NSA + i * NSA_KV:D_NSA + (i + 1) * NSA_KV], zeros(KV_SLOT - NSA_KV)]
    gates = nsa[:, :, D_NSA + 6 * NSA_KV:]
    gates = gates.reshape(L, D_MODEL, NSA_KV_HEADS, NSA_GQA, 3).transpose(0, 1, 2, 4, 3)
    segs += [gates.reshape(L, D_MODEL, 3 * NSA_HEADS),
             zeros(NSA_PAD - NSA_GATE_OFF - 3 * NSA_HEADS)]
    return jnp.concatenate([rw] + segs + [pool], axis=-1).astype(BF16)


def _gate_bias_layout(gate_b):
    L = gate_b.shape[0]
    gb = gate_b.reshape(L, NSA_KV_HEADS, NSA_GQA, 3).transpose(0, 1, 3, 2).reshape(L, 1, 3 * NSA_HEADS)
    return jnp.pad(gb, ((0, 0), (0, 0), (0, LANES - 3 * NSA_HEADS)))


def kernel(x, ffn1_w_up, ffn1_w_down, ln1_g, ln1_b, w_in, rw_mu, rw_w0, rw_w2, rw_a0, rw_a2, rw_g2, rw_k_k,
           rw_k_a, rw_r_k, rw_gn_g, rw_gn_b, pool_w, pool_b, pool_scale, nsa_cmp_pe_k, nsa_cmp_pe_v,
           nsa_cmp_k_w1, nsa_cmp_k_w2, nsa_cmp_v_w1, nsa_cmp_v_w2, nsa_gate_b, w_out, ln2_g, ln2_b,
           ffn2_w_up, ffn2_w_down, ln3_g, ln3_b):
    prm = _prepare(x.shape[1], ffn1_w_up, ffn1_w_down, ln1_g, ln1_b, w_in, rw_mu, rw_w0, rw_w2, rw_a0, rw_a2,
                   rw_g2, rw_k_k, rw_k_a, rw_r_k, rw_gn_g, rw_gn_b, pool_w, pool_b, pool_scale, nsa_cmp_pe_k,
                   nsa_cmp_pe_v, nsa_cmp_k_w1, nsa_cmp_k_w2, nsa_cmp_v_w1, nsa_cmp_v_w2, nsa_gate_b, w_out,
                   ln2_g, ln2_b, ffn2_w_up, ffn2_w_down, ln3_g, ln3_b)
    B, S, D = x.shape
    h = x.reshape(B * S, D)
    for l in range(w_in.shape[0]):
        h = _layer(h, prm, l, B, S)
    return h.reshape(B, S, D)


def _prepare(S, ffn1_w_up, ffn1_w_down, ln1_g, ln1_b, w_in, rw_mu, rw_w0, rw_w2, rw_a0, rw_a2, rw_g2, rw_k_k,
             rw_k_a, rw_r_k, rw_gn_g, rw_gn_b, pool_w, pool_b, pool_scale, nsa_cmp_pe_k, nsa_cmp_pe_v,
             nsa_cmp_k_w1, nsa_cmp_k_w2, nsa_cmp_v_w1, nsa_cmp_v_w2, nsa_gate_b, w_out, ln2_g, ln2_b,
             ffn2_w_up, ffn2_w_down, ln3_g, ln3_b):
    L = w_in.shape[0]
    fpad = D_FF_PAD - D_FF

    def up(w):
        a = jnp.pad(w[:, :, :D_FF], ((0, 0), (0, 0), (0, fpad))).astype(BF16)
        b = jnp.pad(w[:, :, D_FF:], ((0, 0), (0, 0), (0, fpad))).astype(BF16)
        return a, b

    def down(w):
        return jnp.pad(w, ((0, 0), (0, fpad), (0, 0))).astype(BF16)

    row = lambda v: v[:, None, :]
    f1a, f1b = up(ffn1_w_up)
    f2a, f2b = up(ffn2_w_up)
    w_out_b = w_out.astype(BF16)
    gw = CMP_BLOCK * HEAD_DIM
    pad_w2 = lambda w: jnp.pad(w, ((0, 0), (0, 0), (0, LANES - HEAD_DIM))).astype(BF16)
    pe_rows = lambda pe: jnp.broadcast_to(pe.reshape(L, 1, gw), (L, SUBLANES, gw)).astype(BF16)
    ncp = S // CMP_STRIDE
    cos_t, sin_t = _rope_tables(jnp.arange(S), 4)
    cos_c, sin_c = _rope_tables(jnp.arange(ncp) * CMP_STRIDE + (CMP_BLOCK - 1), 2)
    return dict(
        ffn1=(f1a, f1b, down(ffn1_w_down), row(ln1_g), row(ln1_b)),
        ffn2=(f2a, f2b, down(ffn2_w_down), row(ln3_g), row(ln3_b)),
        w_in=_w_in_layout(w_in),
        w_out=(w_out_b[:, :D_RWKV], w_out_b[:, D_RWKV:D_RWKV + D_POOL], w_out_b[:, D_RWKV + D_POOL:]),
        ln2=(row(ln2_g), row(ln2_b)),
        rw=dict(mu=row(rw_mu), w0=row(rw_w0), w2=rw_w2.astype(BF16), a0=row(rw_a0), a2=rw_a2.astype(BF16),
                g2=rw_g2.astype(BF16), k_k=row(rw_k_k), k_a=row(rw_k_a), r_k=rw_r_k.reshape(L, 1, D_RWKV),
                gn_g=row(rw_gn_g), gn_b=row(rw_gn_b), ones=_block_ones(D_RWKV)),
        cmp=dict(k_w1=nsa_cmp_k_w1.reshape(L, gw, CMP_HIDDEN).astype(BF16), k_w2=pad_w2(nsa_cmp_k_w2),
                 k_pe=pe_rows(nsa_cmp_pe_k),
                 v_w1=nsa_cmp_v_w1.reshape(L, gw, CMP_HIDDEN).astype(BF16), v_w2=nsa_cmp_v_w2.transpose(0, 2, 1).astype(BF16),
                 v_pe=pe_rows(nsa_cmp_pe_v)),
        gate_b=_gate_bias_layout(nsa_gate_b),
        pool=(pool_w.astype(BF16), row(pool_b), row(pool_scale)),
        rope=(cos_t, sin_t), rope_cmp=(cos_c, sin_c), overlap=_overlap_matrix(S, ncp))


def _mixers(p_all, prm, l):
    r, w, k, v, kk, be, bo, g = _rw_prep(p_all, prm["rw"], l)
    y_rw = _rw_scan(r, w, k, v, kk, be, bo, g, prm["rw"], l)
    y_pool = _pool_mix(p_all, *prm["pool"], l)
    q, kc, vc, ks, vs, kw, vw, gates = _nsa_prep(p_all, *prm["rope"], prm["gate_b"], l)
    k_cmp, v_cmp = _nsa_compress(kc, vc, prm["cmp"], *prm["rope_cmp"], l)
    y_nsa = _nsa_attention(q, k_cmp, v_cmp, ks, vs, kw, vw, gates, prm["overlap"])
    return y_rw, y_pool, y_nsa


def _layer(h, prm, l, B, S):
    T = B * S
    h = _ffn_ln(h, *prm["ffn1"], l)
    p_all = _in_proj(h, prm["w_in"], l).reshape(B, S, P_COLS)
    y_rw, y_pool, y_nsa = _mixers(p_all, prm, l)
    h = _out_proj_ln(h, y_rw.reshape(T, D_RWKV), y_pool.reshape(T, D_POOL), y_nsa.reshape(T, D_NSA),
                     prm["w_out"], *prm["ln2"], l)
    return _ffn_ln(h, *prm["ffn2"], l)
```

```python
import functools

import numpy as np
import jax
import jax.numpy as jnp
from jax import lax
from jax.experimental import pallas as pl
from jax.experimental.pallas import tpu as pltpu

F32 = jnp.float32
BF16 = jnp.bfloat16

D_MODEL = 2048
DEPTH = 4
HEAD_DIM = 64
D_RWKV = 768
D_POOL = 512
D_NSA = 768
RW_HEADS = 12
RW_DECAY_LORA = 64
RW_A_LORA = 64
RW_GATE_LORA = 128
RW_GN_EPS = 64e-5
RW_COLS = 3 * D_RWKV + RW_DECAY_LORA + RW_A_LORA + RW_GATE_LORA
POOL_WINDOWS = (2, 4, 8, 16)
POOL_GROUP = 128
NSA_HEADS = 12
NSA_KV_HEADS = 3
NSA_GQA = 4
NSA_KV = 192
NSA_COLS = D_NSA + 6 * NSA_KV + 3 * NSA_HEADS
CMP_BLOCK = 32
CMP_STRIDE = 16
CMP_HIDDEN = 256
SEL_BLOCK = 64
SEL_TOPK = 16
FORCE_SCORE = 1e9
WINDOW = 512
ROPE_THETA = 500000.0
ROPE_DIM = 16
D_FF = 5504
IN_COLS = RW_COLS + D_POOL + NSA_COLS
ALPHA = (2 * DEPTH) ** 0.25
LN_EPS = 1e-5

LANES = 128
SUBLANES = 8
VMEM_LIMIT = 56 * 1024 * 1024

KV_SLOT = 2 * LANES
NSA_PAD = 2560
P_RW_OFF = 0
P_NSA_OFF = RW_COLS
P_POOL_OFF = RW_COLS + NSA_PAD
P_COLS = RW_COLS + NSA_PAD + D_POOL
NSA_GATE_OFF = D_NSA + 6 * KV_SLOT
D_FF_PAD = 5632

NEG = -1e30


def _params(*sem):
    return pltpu.CompilerParams(dimension_semantics=sem, vmem_limit_bytes=VMEM_LIMIT)


def _layer_norm(z, g, b):
    mu = jnp.mean(z, axis=-1, keepdims=True)
    zc = z - mu
    var = jnp.mean(zc * zc, axis=-1, keepdims=True)
    return zc * lax.rsqrt(var + LN_EPS) * g + b


def _dot(a, b):
    return jnp.dot(a, b, preferred_element_type=F32)


def _dot_nt(a, b):
    return lax.dot_general(a, b, (((1,), (1,)), ((), ())), preferred_element_type=F32)


def _dot_split(x, w):
    hi = x.astype(BF16)
    lo = (x - hi.astype(F32)).astype(BF16)
    return _dot(hi, w) + _dot(lo, w)


FFN_LN_ROWS = 256


def _ffn_kernel(x_ref, wa_ref, wb_ref, wd_ref, g_ref, b_ref, o_ref, xb_ref):
    k = pl.program_id(1)

    @pl.when(k == 0)
    def _():
        xb_ref[...] = x_ref[...].astype(BF16)
        o_ref[...] = jnp.zeros_like(o_ref)

    xb = xb_ref[...]
    a = _dot(xb, wa_ref[...])
    b = _dot(xb, wb_ref[...])
    h = (a / (1.0 + jnp.exp(-a))) * b
    o_ref[...] += _dot(h.astype(BF16), wd_ref[...])

    @pl.when(k == pl.num_programs(1) - 1)
    def _():
        for c in range(o_ref.shape[0] // FFN_LN_ROWS):
            rows = slice(c * FFN_LN_ROWS, (c + 1) * FFN_LN_ROWS)
            z = ALPHA * x_ref[rows, :] + 0.5 * o_ref[rows, :]
            o_ref[rows, :] = _layer_norm(z, g_ref[...], b_ref[...])


def _ffn_ln(x, wa, wb, wd, g, b, layer, *, tm=1024, tf=256):
    T, D = x.shape
    fp = wa.shape[-1]
    tm = min(tm, T)
    return pl.pallas_call(
        _ffn_kernel,
        out_shape=jax.ShapeDtypeStruct((T, D), F32),
        grid=(T // tm, fp // tf),
        in_specs=[
            pl.BlockSpec((tm, D), lambda i, k: (i, 0)),
            pl.BlockSpec((None, D, tf), lambda i, k: (layer, 0, k)),
            pl.BlockSpec((None, D, tf), lambda i, k: (layer, 0, k)),
            pl.BlockSpec((None, tf, D), lambda i, k: (layer, k, 0)),
            pl.BlockSpec((None, 1, D), lambda i, k: (layer, 0, 0)),
            pl.BlockSpec((None, 1, D), lambda i, k: (layer, 0, 0)),
        ],
        out_specs=pl.BlockSpec((tm, D), lambda i, k: (i, 0)),
        scratch_shapes=[pltpu.VMEM((tm, D), BF16)],
        compiler_params=_params("parallel", "arbitrary"),
        name="ffn_ln",
    )(x, wa, wb, wd, g, b)


def _inproj_kernel(x_ref, w_ref, o_ref):
    o_ref[...] = _dot(x_ref[...].astype(BF16), w_ref[...])


def _in_proj(x, w, layer, *, tm=1024, tn=512):
    T, D = x.shape
    n = w.shape[-1]
    tm = min(tm, T)
    return pl.pallas_call(
        _inproj_kernel,
        out_shape=jax.ShapeDtypeStruct((T, n), F32),
        grid=(T // tm, n // tn),
        in_specs=[
            pl.BlockSpec((tm, D), lambda i, j: (i, 0)),
            pl.BlockSpec((None, D, tn), lambda i, j: (layer, 0, j)),
        ],
        out_specs=pl.BlockSpec((tm, tn), lambda i, j: (i, j)),
        compiler_params=_params("parallel", "arbitrary"),
        name="in_proj",
    )(x, w)


def _outproj_kernel(x_ref, yr_ref, yp_ref, yn_ref, wr_ref, wp_ref, wn_ref, g_ref, b_ref, o_ref):
    y = _dot(yr_ref[...], wr_ref[...]) + _dot(yp_ref[...], wp_ref[...]) + _dot(yn_ref[...], wn_ref[...])
    o_ref[...] = _layer_norm(ALPHA * x_ref[...] + y, g_ref[...], b_ref[...])


def _out_proj_ln(x, y_rw, y_pool, y_nsa, w_out, g, b, layer, *, tm=512):
    T, D = x.shape
    return pl.pallas_call(
        _outproj_kernel,
        out_shape=jax.ShapeDtypeStruct((T, D), F32),
        grid=(T // tm,),
        in_specs=[
            pl.BlockSpec((tm, D), lambda i: (i, 0)),
            pl.BlockSpec((tm, D_RWKV), lambda i: (i, 0)),
            pl.BlockSpec((tm, D_POOL), lambda i: (i, 0)),
            pl.BlockSpec((tm, D_NSA), lambda i: (i, 0)),
            pl.BlockSpec((None, D_RWKV, D), lambda i: (layer, 0, 0)),
            pl.BlockSpec((None, D_POOL, D), lambda i: (layer, 0, 0)),
            pl.BlockSpec((None, D_NSA, D), lambda i: (layer, 0, 0)),
            pl.BlockSpec((None, 1, D), lambda i: (layer, 0, 0)),
            pl.BlockSpec((None, 1, D), lambda i: (layer, 0, 0)),
        ],
        out_specs=pl.BlockSpec((tm, D), lambda i: (i, 0)),
        compiler_params=_params("parallel"),
        name="out_proj_ln",
    )(x, y_rw, y_pool, y_nsa, w_out[0], w_out[1], w_out[2], g, b)


POOL_HALO = 16


def _pool_kernel(p_ref, halo_ref, w_ref, b_ref, sc_ref, o_ref, xs_ref, *, ts):
    s = pl.program_id(1)
    x = p_ref[0]
    halo = jnp.where(s > 0, halo_ref[0], 0.0)
    xs_ref[0:POOL_HALO, :] = halo
    xs_ref[POOL_HALO:POOL_HALO + ts, :] = x
    t1 = (s * ts + 1 + lax.broadcasted_iota(jnp.int32, (ts, 1), 0)).astype(F32)
    for gi, win in enumerate(POOL_WINDOWS):
        c0 = gi * POOL_GROUP
        acc = x[:, c0:c0 + POOL_GROUP]
        for j in range(1, win):
            acc = acc + xs_ref[POOL_HALO - j:POOL_HALO - j + ts, c0:c0 + POOL_GROUP]
        pooled = acc / jnp.minimum(t1, float(win)) - x[:, c0:c0 + POOL_GROUP]
        z = _dot(pooled.astype(BF16), w_ref[gi]) + b_ref[:, c0:c0 + POOL_GROUP]
        o_ref[0, :, c0:c0 + POOL_GROUP] = (z * sc_ref[:, c0:c0 + POOL_GROUP]).astype(o_ref.dtype)


def _pool_mix(p_all, pool_w, pool_b, pool_scale, layer, *, ts=512):
    B, S, _ = p_all.shape
    cb = P_POOL_OFF // D_POOL
    hb = ts // POOL_HALO
    return pl.pallas_call(
        functools.partial(_pool_kernel, ts=ts),
        out_shape=jax.ShapeDtypeStruct((B, S, D_POOL), BF16),
        grid=(B, S // ts),
        in_specs=[
            pl.BlockSpec((1, ts, D_POOL), lambda b, s: (b, s, cb)),
            pl.BlockSpec((1, POOL_HALO, D_POOL), lambda b, s: (b, jnp.maximum(s * hb - 1, 0), cb)),
            pl.BlockSpec((None, 4, POOL_GROUP, POOL_GROUP), lambda b, s: (layer, 0, 0, 0)),
            pl.BlockSpec((None, 1, D_POOL), lambda b, s: (layer, 0, 0)),
            pl.BlockSpec((None, 1, D_POOL), lambda b, s: (layer, 0, 0)),
        ],
        out_specs=pl.BlockSpec((1, ts, D_POOL), lambda b, s: (b, s, 0)),
        scratch_shapes=[pltpu.VMEM((ts + POOL_HALO, D_POOL), F32)],
        compiler_params=_params("parallel", "parallel"),
        name="pool_mix",
    )(p_all, p_all, pool_w, pool_b, pool_scale)


def _softplus(z):
    return jnp.maximum(z, 0.0) + jnp.log1p(jnp.exp(-jnp.abs(z)))


def _sigmoid(z):
    return 1.0 / (1.0 + jnp.exp(-z))


def _lanes_to_parity_major(x):
    n_pairs = x.shape[1] // LANES
    low = lax.broadcasted_iota(jnp.int32, (1, LANES), 1) < HEAD_DIM

    def chunk(e, to_low):
        pair, parity = e % n_pairs, e // n_pairs
        src = x[:, pair * LANES:(pair + 1) * LANES]
        return src if (parity == 0) == to_low else pltpu.roll(src, HEAD_DIM, 1)

    return jnp.concatenate([jnp.where(low, chunk(2 * d, True), chunk(2 * d + 1, False))
                            for d in range(n_pairs)], axis=1)


def _rw_prep_kernel(p_ref, prev_ref, mu_ref, w0_ref, w2_ref, a0_ref, a2_ref, g2_ref, kk_ref, ka_ref,
                    rk_ref, ones_ref,
                    r_o, w_o, k_o, v_o, kk_o, be_o, bo_o, g_o, *, ts):
    s = pl.program_id(1)
    x = p_ref[0]
    last = jnp.where(s > 0, prev_ref[0][SUBLANES - 1:SUBLANES, :], 0.0)
    row = lax.broadcasted_iota(jnp.int32, (ts, 1), 0)
    shifted = jnp.where(row == 0, last, pltpu.roll(x, 1, 0))
    xm = x + (shifted - x) * mu_ref[...]
    c = D_RWKV
    r = xm[:, 0:c]
    k = xm[:, c:2 * c]
    v = xm[:, 2 * c:3 * c]
    lora = xm[:, 3 * c:3 * c + LANES]
    wl = lora[:, :RW_DECAY_LORA]
    al = lora[:, RW_DECAY_LORA:]
    gl = xm[:, 3 * c + LANES:]
    w = -_softplus(-(w0_ref[...] + _dot(jnp.tanh(wl).astype(BF16), w2_ref[...]))) - 0.5
    decay = jnp.exp(-jnp.exp(w))
    a = _sigmoid(a0_ref[...] + _dot(al.astype(BF16), a2_ref[...]))
    g = _dot(_sigmoid(gl).astype(BF16), g2_ref[...])
    ones = ones_ref[...]
    kk = k * kk_ref[...]
    nrm = jnp.sqrt(_dot_split(kk * kk, ones))
    kk = kk / jnp.maximum(nrm, 1e-12)
    k_mod = k * (1.0 + (a - 1.0) * ka_ref[...])
    bonus = _dot_split(r * k_mod * rk_ref[...], ones) * v
    r_o[0] = r
    w_o[0] = decay
    k_o[0] = k_mod
    v_o[0] = v
    kk_o[0] = kk
    be_o[0] = kk * a
    bo_o[0] = _lanes_to_parity_major(bonus)
    g_o[0] = g


def _rw_prep(p_all, prm, layer, *, ts=256):
    B, S, _ = p_all.shape
    cb = P_RW_OFF // RW_COLS
    hb = ts // SUBLANES
    vec = lambda n: pl.BlockSpec((None, 1, n), lambda b, s: (layer, 0, 0))
    mat = lambda m, n: pl.BlockSpec((None, m, n), lambda b, s: (layer, 0, 0))
    out = jax.ShapeDtypeStruct((B, S, D_RWKV), F32)
    ospec = pl.BlockSpec((1, ts, D_RWKV), lambda b, s: (b, s, 0))
    return pl.pallas_call(
        functools.partial(_rw_prep_kernel, ts=ts),
        out_shape=[out] * 8,
        grid=(B, S // ts),
        in_specs=[
            pl.BlockSpec((1, ts, RW_COLS), lambda b, s: (b, s, cb)),
            pl.BlockSpec((1, SUBLANES, RW_COLS), lambda b, s: (b, jnp.maximum(s * hb - 1, 0), cb)),
            vec(RW_COLS), vec(D_RWKV), mat(RW_DECAY_LORA, D_RWKV), vec(D_RWKV), mat(RW_A_LORA, D_RWKV),
            mat(RW_GATE_LORA, D_RWKV), vec(D_RWKV), vec(D_RWKV), vec(D_RWKV),
            pl.BlockSpec((D_RWKV, D_RWKV), lambda b, s: (0, 0)),
        ],
        out_specs=[ospec] * 8,
        compiler_params=_params("parallel", "parallel"),
        name="rw_prep",
    )(p_all, p_all, prm["mu"], prm["w0"], prm["w2"], prm["a0"], prm["a2"], prm["g2"], prm["k_k"],
      prm["k_a"], prm["r_k"], prm["ones"])


RW_PAIRS = D_RWKV // LANES


def _rw_scan_kernel(r_ref, w_ref, k_ref, v_ref, kk_ref, be_ref, bo_ref, g_ref, gng_ref, gnb_ref, ones_ref,
                    pair_ref, half_ref, o_ref, st_ref, vc_ref, sr_ref, y_ref, *, ts, nb):
    s = pl.program_id(1)

    @pl.when(s == 0)
    def _():
        st_ref[...] = jnp.zeros_like(st_ref)

    side, stack = 2, RW_PAIRS // 2
    rows, width = stack * HEAD_DIM, side * LANES
    lane = lax.broadcasted_iota(jnp.int32, (rows, width), 1)
    row = lax.broadcasted_iota(jnp.int32, (rows, width), 0)
    eye = jnp.where(lane % HEAD_DIM == row % HEAD_DIM, 1.0, 0.0)
    ones_blk = pair_ref[...]

    def spread(x8, j):
        blocks = []
        for st in range(stack):
            lanes = jnp.concatenate([x8[j:j + 1, (sd * stack + st) * LANES:(sd * stack + st + 1) * LANES]
                                     for sd in range(side)], axis=1)
            blocks.append(jnp.broadcast_to(lanes, (HEAD_DIM, width)))
        return jnp.concatenate(blocks, axis=0)

    def step(i, carry):
        base = pl.multiple_of(i * SUBLANES, SUBLANES)
        tiles = [[ref[b, pl.ds(base, SUBLANES), :] for ref in (kk_ref, w_ref, be_ref, k_ref, r_ref, v_ref)]
                 for b in range(nb)]
        for b in range(nb):
            lhs = jnp.concatenate([eye * spread(tiles[b][5], j) for j in range(SUBLANES)], axis=0)
            vc_ref[b] = _dot(lhs.astype(BF16), ones_blk)
        sts = [st_ref[b] for b in range(nb)]
        for j in range(SUBLANES):
            for b in range(nb):
                kk8, w8, be8, k8, r8, _ = tiles[b]
                st = sts[b]
                sa = _dot((st * spread(kk8, j)).astype(BF16), ones_blk)
                st = (st * spread(w8, j) - sa * spread(be8, j)
                      + vc_ref[b, j * rows:(j + 1) * rows, :] * spread(k8, j))
                sr = (st * spread(r8, j)).astype(BF16)
                for sd in range(side):
                    lo = (j * side + sd) * rows
                    sr_ref[b, lo:lo + rows, :] = sr[:, sd * LANES:(sd + 1) * LANES]
                sts[b] = st
        for b in range(nb):
            st_ref[b] = sts[b]
            yt = _dot_nt(half_ref[...], sr_ref[b])
            per = side * rows
            y_ref[b, pl.ds(base, SUBLANES), :] = jnp.concatenate(
                [jnp.concatenate([yt[0:1, j * per:(j + 1) * per], yt[1:2, j * per:(j + 1) * per]], axis=1)
                 for j in range(SUBLANES)], axis=0)
        return carry

    lax.fori_loop(0, ts // SUBLANES, step, 0)

    ones = ones_ref[...]
    for b in range(nb):
        y = y_ref[b]
        mean = _dot_split(y, ones) * (1.0 / HEAD_DIM)
        yc = y - mean
        var = _dot_split(yc * yc, ones) * (1.0 / HEAD_DIM)
        yn = yc * lax.rsqrt(var + RW_GN_EPS) * gng_ref[...] + gnb_ref[...]
        o_ref[b] = ((yn + bo_ref[b]) * g_ref[b]).astype(o_ref.dtype)


def _rw_scan(r, w, k, v, kk, be, bo, g, prm, layer, *, ts=128):
    B, S, _ = r.shape
    nb = 4 if B % 4 == 0 else (2 if B % 2 == 0 else 1)
    rows, width = RW_PAIRS // 2 * HEAD_DIM, 2 * LANES
    seq = pl.BlockSpec((nb, ts, D_RWKV), lambda b, s: (b, s, 0))
    vec = pl.BlockSpec((None, 1, D_RWKV), lambda b, s: (layer, 0, 0))
    return pl.pallas_call(
        functools.partial(_rw_scan_kernel, ts=ts, nb=nb),
        out_shape=jax.ShapeDtypeStruct((B, S, D_RWKV), BF16),
        grid=(B // nb, S // ts),
        in_specs=[seq] * 8 + [vec, vec, pl.BlockSpec((D_RWKV, D_RWKV), lambda b, s: (0, 0)),
                              pl.BlockSpec((width, width), lambda b, s: (0, 0)),
                              pl.BlockSpec((SUBLANES, LANES), lambda b, s: (0, 0))],
        out_specs=seq,
        scratch_shapes=[pltpu.VMEM((nb, rows, width), F32), pltpu.VMEM((nb, SUBLANES * rows, width), F32),
                        pltpu.VMEM((nb, SUBLANES * RW_PAIRS * HEAD_DIM, LANES), BF16),
                        pltpu.VMEM((nb, ts, D_RWKV), F32)],
        compiler_params=_params("parallel", "arbitrary"),
        name="rw_scan",
    )(r, w, k, v, kk, be, bo, g, prm["gn_g"], prm["gn_b"], prm["ones"], prm["ones"][:width, :width],
      prm["ones"][:SUBLANES * HEAD_DIM:HEAD_DIM, :LANES])


def _rope(x, cos, sin):
    half = ROPE_DIM // 2
    w = x.shape[-1]
    d = lax.broadcasted_iota(jnp.int32, x.shape, 1) % HEAD_DIM
    partner = jnp.where(d < half, pltpu.roll(x, w - half, 1), pltpu.roll(x, half, 1))
    return x * cos + partner * sin


def _nsa_prep_kernel(p_ref, cos_ref, sin_ref, gb_ref, q_o, kc_o, vc_o, ks_o, vs_o, kw_o, vw_o, gt_o):
    cos = cos_ref[...]
    sin = sin_ref[...]
    for hq in range(NSA_HEADS // 4):
        x = p_ref[0, :, hq * KV_SLOT:(hq + 1) * KV_SLOT]
        qr = _rope(x, cos, sin) * (HEAD_DIM ** -0.5)
        for j in range(4):
            q_o[0, 4 * hq + j] = qr[:, j * HEAD_DIM:(j + 1) * HEAD_DIM].astype(BF16)
    for i, ref in enumerate((kc_o, vc_o, ks_o, vs_o, kw_o, vw_o)):
        part = p_ref[0, :, D_NSA + i * KV_SLOT:D_NSA + (i + 1) * KV_SLOT]
        if i in (2, 4):
            part = _rope(part, cos, sin)
        if i in (3, 5):
            part = part.T
            for h in range(NSA_KV_HEADS):
                ref[0, h] = part[h * HEAD_DIM:(h + 1) * HEAD_DIM, :].astype(BF16)
        else:
            for h in range(NSA_KV_HEADS):
                ref[0, h] = part[:, h * HEAD_DIM:(h + 1) * HEAD_DIM].astype(BF16)
    gts = _sigmoid(p_ref[0, :, NSA_GATE_OFF:NSA_GATE_OFF + LANES] + gb_ref[...]).T
    for h in range(NSA_KV_HEADS):
        gt_o[0, h] = gts[h * 12:(h + 1) * 12, :]


def _nsa_prep(p_all, cos, sin, gate_b, layer, *, ts=256):
    B, S, _ = p_all.shape
    H = NSA_KV_HEADS
    k_shape = jax.ShapeDtypeStruct((B, H, S, HEAD_DIM), BF16)
    k_spec = pl.BlockSpec((1, H, ts, HEAD_DIM), lambda b, s: (b, 0, s, 0))
    vt_shape = jax.ShapeDtypeStruct((B, H, HEAD_DIM, S), BF16)
    vt_spec = pl.BlockSpec((1, H, HEAD_DIM, ts), lambda b, s: (b, 0, 0, s))
    return pl.pallas_call(
        _nsa_prep_kernel,
        out_shape=[jax.ShapeDtypeStruct((B, NSA_HEADS, S, HEAD_DIM), BF16),
                   k_shape, k_shape, k_shape, vt_shape, k_shape, vt_shape,
                   jax.ShapeDtypeStruct((B, H, 12, S), F32)],
        grid=(B, S // ts),
        in_specs=[
            pl.BlockSpec((1, ts, NSA_PAD), lambda b, s: (b, s, P_NSA_OFF // NSA_PAD)),
            pl.BlockSpec((ts, KV_SLOT), lambda b, s: (s, 0)),
            pl.BlockSpec((ts, KV_SLOT), lambda b, s: (s, 0)),
            pl.BlockSpec((None, 1, LANES), lambda b, s: (layer, 0, 0)),
        ],
        out_specs=[pl.BlockSpec((1, NSA_HEADS, ts, HEAD_DIM), lambda b, s: (b, 0, s, 0)),
                   k_spec, k_spec, k_spec, vt_spec, k_spec, vt_spec,
                   pl.BlockSpec((1, H, 12, ts), lambda b, s: (b, 0, 0, s))],
        compiler_params=_params("parallel", "parallel"),
        name="nsa_prep",
    )(p_all, cos, sin, gate_b)


def _gelu_tanh(x):
    return 0.5 * x * (1.0 + jnp.tanh(float(np.sqrt(2.0 / np.pi)) * (x + 0.044715 * (x * x * x))))


def _nsa_cmp_kernel(kc_ref, vc_ref, kw1_ref, kw2_ref, kpe_ref, vw1_ref, vw2_ref, vpe_ref, cos_ref, sin_ref,
                    k_o, v_o):
    def hidden(g, w1_ref, pe_ref):
        half = CMP_STRIDE * HEAD_DIM
        first = _dot(g, w1_ref[:half, :])
        second = _dot(g, w1_ref[half:, :])
        n = first.shape[0]
        bias = _dot(pe_ref[...], w1_ref[...])[0:1]
        return _gelu_tanh(first + pltpu.roll(second, n - 1, 0) + bias).astype(BF16)

    k = _dot(hidden(kc_ref[0, 0], kw1_ref, kpe_ref), kw2_ref[...])
    k_o[0, 0] = _rope(k, cos_ref[...], sin_ref[...])[:, :HEAD_DIM].astype(BF16)
    v_o[0, 0] = _dot_nt(vw2_ref[...], hidden(vc_ref[0, 0], vw1_ref, vpe_ref)).astype(BF16)


def _nsa_compress(kc, vc, prm, cos_c, sin_c, layer):
    B, H, S, _ = kc.shape
    ng = S // CMP_STRIDE
    gw = CMP_STRIDE * HEAD_DIM
    g_k = kc.reshape(B, H, ng, gw)
    g_v = vc.reshape(B, H, ng, gw)
    gspec = pl.BlockSpec((1, 1, ng, gw), lambda b, h: (b, h, 0, 0))
    w1 = pl.BlockSpec((None, 2 * gw, CMP_HIDDEN), lambda b, h: (layer, 0, 0))
    w2 = pl.BlockSpec((None, CMP_HIDDEN, LANES), lambda b, h: (layer, 0, 0))
    pe = pl.BlockSpec((None, SUBLANES, 2 * gw), lambda b, h: (layer, 0, 0))
    tab = pl.BlockSpec((ng, LANES), lambda b, h: (0, 0))
    return pl.pallas_call(
        _nsa_cmp_kernel,
        out_shape=[jax.ShapeDtypeStruct((B, H, ng, HEAD_DIM), BF16), jax.ShapeDtypeStruct((B, H, HEAD_DIM, ng), BF16)],
        grid=(B, H),
        in_specs=[gspec, gspec, w1, w2, pe, w1,
                  pl.BlockSpec((None, HEAD_DIM, CMP_HIDDEN), lambda b, h: (layer, 0, 0)), pe, tab, tab],
        out_specs=[pl.BlockSpec((1, 1, ng, HEAD_DIM), lambda b, h: (b, h, 0, 0)),
                   pl.BlockSpec((1, 1, HEAD_DIM, ng), lambda b, h: (b, h, 0, 0))],
        compiler_params=_params("parallel", "parallel"),
        name="nsa_compress",
    )(g_k, g_v, prm["k_w1"], prm["k_w2"], prm["k_pe"], prm["v_w1"], prm["v_w2"], prm["v_pe"], cos_c, sin_c)


def _nsa_attn_kernel(q_ref, kcmp_ref, vcmp_ref, ks_ref, vs_ref, kw_ref, vw_ref, gt_ref, ov_ref, o_ref,
                     m_ref, l_ref, acc_ref, sb_ref, *, tq):
    G = NSA_GQA
    tk = tq
    qi = pl.program_id(2)
    q0 = qi * tq
    q_all = q_ref[0].reshape(G * tq, HEAD_DIM)
    pos = q0 + lax.broadcasted_iota(jnp.int32, (1, tq), 1)
    tiny = jnp.finfo(F32).tiny
    heads = lambda x: jnp.tile(x, (1, G))
    split = lambda x: [x[:, g * tq:(g + 1) * tq] for g in range(G)]

    ncp = kcmp_ref.shape[2]
    n_idx = lax.broadcasted_iota(jnp.int32, (ncp, 1), 0)
    cbias = jnp.where((n_idx * CMP_STRIDE + (CMP_BLOCK - 1)) <= pos, 0.0, NEG)
    any_cmp = jnp.where(pos >= CMP_BLOCK - 1, 1.0, 0.0)
    s = _dot_nt(kcmp_ref[0, 0], q_all) + heads(cbias)
    e = jnp.exp(s - jnp.max(s, axis=0, keepdims=True))
    p = e * (heads(any_cmp) / jnp.maximum(jnp.sum(e, axis=0, keepdims=True), tiny))
    o_cmp = split(_dot(vcmp_ref[0, 0], p.astype(BF16)))
    ps = split(p)
    psum = (ps[0] + ps[1]) + (ps[2] + ps[3])
    p_hi = psum.astype(BF16)
    p_lo = (psum - p_hi.astype(F32)).astype(BF16)
    imp = _dot(ov_ref[...], p_hi) + _dot(ov_ref[...], p_lo)

    n_sel = ov_ref.shape[0]
    blk = lax.broadcasted_iota(jnp.int32, (n_sel, 1), 0)
    cur = pos // SEL_BLOCK
    forced = (blk == 0) | (blk == cur) | (blk == cur - 1)
    valid = blk <= cur
    score = jnp.where(valid, jnp.where(forced, FORCE_SCORE, imp), -jnp.inf)
    rank = jnp.zeros((n_sel, tq), F32)
    for i in range(n_sel):
        ci = score[i:i + 1, :]
        beats = (ci > score) | ((ci == score) & (blk > i))
        rank = rank + jnp.where(beats, 1.0, 0.0)
    chosen = (rank < float(SEL_TOPK)) & valid
    sel_bias = jnp.where(chosen, 0.0, NEG)
    for j in range(n_sel):
        sb_ref[j] = jnp.broadcast_to(sel_bias[j:j + 1, :], (SUBLANES, tq))

    kcol = lax.broadcasted_iota(jnp.int32, (tk, 1), 0)

    def scores(kt):
        k0 = pl.multiple_of(kt * tk, tk)
        return _dot_nt(ks_ref[0, 0, pl.ds(k0, tk), :], q_all)

    def sel_update(kt, s, extra=None):
        bpt = tk // SEL_BLOCK
        bias = jnp.concatenate([jnp.tile(sb_ref[kt * bpt + j], (SEL_BLOCK // SUBLANES, 1)) for j in range(bpt)],
                               axis=0)
        if extra is not None:
            bias = bias + extra
        k0 = pl.multiple_of(kt * tk, tk)
        vt = vs_ref[0, 0, :, pl.ds(k0, tk)]
        s = s + heads(bias)
        m_prev = m_ref[...]
        m_new = jnp.maximum(m_prev, jnp.max(s, axis=0, keepdims=True))
        p = jnp.exp(s - m_new)
        corr = jnp.exp(m_prev - m_new)
        l_ref[...] = corr * l_ref[...] + jnp.sum(p, axis=0, keepdims=True)
        acc_ref[...] = corr * acc_ref[...] + _dot(vt, p.astype(BF16))
        m_ref[...] = m_new

    m_ref[...] = jnp.full_like(m_ref, NEG)
    l_ref[...] = jnp.zeros_like(l_ref)
    acc_ref[...] = jnp.zeros_like(acc_ref)

    def sel_body(i, carry):
        s_a, s_b = scores(2 * i), scores(2 * i + 1)
        sel_update(2 * i, s_a)
        sel_update(2 * i + 1, s_b)
        return carry

    lax.fori_loop(0, qi // 2, sel_body, 0)

    @pl.when(qi % 2 == 1)
    def _():
        sel_update(qi - 1, scores(qi - 1))

    causal = jnp.where(q0 + kcol <= pos, 0.0, NEG)
    sel_update(qi, scores(qi), causal)
    o_sel = split(acc_ref[...] / jnp.maximum(l_ref[...], tiny))

    wk = min(WINDOW + tq, ks_ref.shape[2])
    w0 = pl.multiple_of(jnp.maximum(q0 + tq - wk, 0), tk)
    wpos = w0 + lax.broadcasted_iota(jnp.int32, (wk, 1), 0)
    wbias = jnp.where((wpos <= pos) & (wpos > pos - WINDOW), 0.0, NEG)
    s = _dot_nt(kw_ref[0, 0, pl.ds(w0, wk), :], q_all) + heads(wbias)
    e = jnp.exp(s - jnp.max(s, axis=0, keepdims=True))
    pv = _dot(vw_ref[0, 0, :, pl.ds(w0, wk)], e.astype(BF16))
    o_win = split(pv / jnp.maximum(jnp.sum(e, axis=0, keepdims=True), tiny))

    gt = gt_ref[0, 0]
    outs = [gt[g:g + 1] * o_cmp[g] + gt[G + g:G + g + 1] * o_sel[g] + gt[2 * G + g:2 * G + g + 1] * o_win[g]
            for g in range(G)]
    for half in range(G // 2):
        pair = jnp.concatenate(outs[2 * half:2 * half + 2], axis=0)
        o_ref[0, :, half * LANES:(half + 1) * LANES] = pair.T.astype(o_ref.dtype)


def _nsa_attention(q, k_cmp, v_cmp_t, ks, vs_t, kw, vw_t, gates_t, overlap_t, *, tq=256):
    B, _, S, _ = q.shape
    H, G = NSA_KV_HEADS, NSA_GQA
    tq = min(tq, S)
    assert WINDOW == 2 * tq or S <= tq
    ncp = k_cmp.shape[2]
    n_cmp = (S - CMP_BLOCK) // CMP_STRIDE + 1
    keys = pl.BlockSpec((1, 1, S, HEAD_DIM), lambda b, h, i: (b, h, 0, 0))
    vals = pl.BlockSpec((1, 1, HEAD_DIM, S), lambda b, h, i: (b, h, 0, 0))
    return pl.pallas_call(
        functools.partial(_nsa_attn_kernel, tq=tq),
        out_shape=jax.ShapeDtypeStruct((B, S, D_NSA), BF16),
        grid=(B, H, S // tq),
        in_specs=[
            pl.BlockSpec((1, G, tq, HEAD_DIM), lambda b, h, i: (b, h, i, 0)),
            pl.BlockSpec((1, 1, ncp, HEAD_DIM), lambda b, h, i: (b, h, 0, 0)),
            pl.BlockSpec((1, 1, HEAD_DIM, ncp), lambda b, h, i: (b, h, 0, 0)),
            keys, vals, keys, vals,
            pl.BlockSpec((1, 1, 12, tq), lambda b, h, i: (b, h, 0, i)),
            pl.BlockSpec(overlap_t.shape, lambda b, h, i: (0, 0)),
        ],
        out_specs=pl.BlockSpec((1, tq, G * HEAD_DIM), lambda b, h, i: (b, i, h)),
        scratch_shapes=[pltpu.VMEM((1, G * tq), F32), pltpu.VMEM((1, G * tq), F32),
                        pltpu.VMEM((HEAD_DIM, G * tq), F32), pltpu.VMEM((S // SEL_BLOCK, SUBLANES, tq), F32)],
        compiler_params=_params("parallel", "parallel", "arbitrary"),
        name="nsa_attention",
    )(q, k_cmp, v_cmp_t, ks, vs_t, kw, vw_t, gates_t, overlap_t)


def _rope_tables(pos, heads):
    half = ROPE_DIM // 2
    inv_freq = ROPE_THETA ** (-jnp.arange(half, dtype=F32) / half)
    ang = pos.astype(F32)[:, None] * inv_freq
    cos, sin = jnp.cos(ang), jnp.sin(ang)
    n = pos.shape[0]
    rest = HEAD_DIM - ROPE_DIM
    cos_h = jnp.concatenate([cos, cos, jnp.ones((n, rest), F32)], axis=1)
    sin_h = jnp.concatenate([-sin, sin, jnp.zeros((n, rest), F32)], axis=1)
    return jnp.tile(cos_h, (1, heads)), jnp.tile(sin_h, (1, heads))


def _overlap_matrix(S, ncp):
    n_cmp = (S - CMP_BLOCK) // CMP_STRIDE + 1
    n_sel = S // SEL_BLOCK
    cmp_start = np.arange(n_cmp) * CMP_STRIDE
    sel_start = np.arange(n_sel) * SEL_BLOCK
    ov = np.clip(np.minimum(cmp_start[:, None] + CMP_BLOCK, sel_start[None, :] + SEL_BLOCK)
                 - np.maximum(cmp_start[:, None], sel_start[None, :]), 0, None) / CMP_BLOCK
    full = np.zeros((n_sel, ncp), np.float32)
    full[:, :n_cmp] = ov.T
    return jnp.asarray(full, BF16)


RW_HEAD_ORDER = tuple(2 * (e % RW_PAIRS) + e // RW_PAIRS for e in range(RW_HEADS))


def _parity_major(a, axis):
    axis = axis % a.ndim
    shape = a.shape
    a = a.reshape(shape[:axis] + (RW_HEADS, HEAD_DIM) + shape[axis + 1:])
    a = jnp.take(a, jnp.asarray(RW_HEAD_ORDER), axis=axis)
    return a.reshape(shape)


def _block_ones(n):
    idx = np.arange(n) // HEAD_DIM
    return jnp.asarray((idx[:, None] == idx[None, :]).astype(np.float32), BF16)


def _w_in_layout(w_in):
    L = w_in.shape[0]
    rw = w_in[:, :, :RW_COLS]
    pool = w_in[:, :, RW_COLS:RW_COLS + D_POOL]
    nsa = w_in[:, :, RW_COLS + D_POOL:]
    zeros = lambda n: jnp.zeros((L, D_MODEL, n), w_in.dtype)
    segs = [nsa[:, :, :D_NSA]]
    for i in range(6):
        segs += [nsa[:, :, D_NSA + i * NSA_KV:D_NSA + (i + 1) * NSA_KV], zeros(KV_SLOT - NSA_KV)]
    gates = nsa[:, :, D_NSA + 6 * NSA_KV:]
    gates = gates.reshape(L, D_MODEL, NSA_KV_HEADS, NSA_GQA, 3).transpose(0, 1, 2, 4, 3)
    segs += [gates.reshape(L, D_MODEL, 3 * NSA_HEADS),
             zeros(NSA_PAD - NSA_GATE_OFF - 3 * NSA_HEADS)]
    return jnp.concatenate([rw] + segs + [pool], axis=-1).astype(BF16)


def _gate_bias_layout(gate_b):
    L = gate_b.shape[0]
    gb = gate_b.reshape(L, NSA_KV_HEADS, NSA_GQA, 3).transpose(0, 1, 3, 2).reshape(L, 1, 3 * NSA_HEADS)
    return jnp.pad(gb, ((0, 0), (0, 0), (0, LANES - 3 * NSA_HEADS)))


def kernel(x, ffn1_w_up, ffn1_w_down, ln1_g, ln1_b, w_in, rw_mu, rw_w0, rw_w2, rw_a0, rw_a2, rw_g2, rw_k_k,
           rw_k_a, rw_r_k, rw_gn_g, rw_gn_b, pool_w, pool_b, pool_scale, nsa_cmp_pe_k, nsa_cmp_pe_v,
           nsa_cmp_k_w1, nsa_cmp_k_w2, nsa_cmp_v_w1, nsa_cmp_v_w2, nsa_gate_b, w_out, ln2_g, ln2_b,
           ffn2_w_up, ffn2_w_down, ln3_g, ln3_b):
    prm = _prepare(x.shape[1], ffn1_w_up, ffn1_w_down, ln1_g, ln1_b, w_in, rw_mu, rw_w0, rw_w2, rw_a0, rw_a2,
                   rw_g2, rw_k_k, rw_k_a, rw_r_k, rw_gn_g, rw_gn_b, pool_w, pool_b, pool_scale, nsa_cmp_pe_k,
                   nsa_cmp_pe_v, nsa_cmp_k_w1, nsa_cmp_k_w2, nsa_cmp_v_w1, nsa_cmp_v_w2, nsa_gate_b, w_out,
                   ln2_g, ln2_b, ffn2_w_up, ffn2_w_down, ln3_g, ln3_b)
    B, S, D = x.shape
    h = x.reshape(B * S, D)
    for l in range(w_in.shape[0]):
        h = _layer(h, prm, l, B, S)
    return h.reshape(B, S, D)


def _prepare(S, ffn1_w_up, ffn1_w_down, ln1_g, ln1_b, w_in, rw_mu, rw_w0, rw_w2, rw_a0, rw_a2, rw_g2, rw_k_k,
             rw_k_a, rw_r_k, rw_gn_g, rw_gn_b, pool_w, pool_b, pool_scale, nsa_cmp_pe_k, nsa_cmp_pe_v,
             nsa_cmp_k_w1, nsa_cmp_k_w2, nsa_cmp_v_w1, nsa_cmp_v_w2, nsa_gate_b, w_out, ln2_g, ln2_b,
             ffn2_w_up, ffn2_w_down, ln3_g, ln3_b):
    L = w_in.shape[0]
    fpad = D_FF_PAD - D_FF

    def up(w):
        a = jnp.pad(w[:, :, :D_FF], ((0, 0), (0, 0), (0, fpad))).astype(BF16)
        b = jnp.pad(w[:, :, D_FF:], ((0, 0), (0, 0), (0, fpad))).astype(BF16)
        return a, b

    def down(w):
        return jnp.pad(w, ((0, 0), (0, fpad), (0, 0))).astype(BF16)

    row = lambda v: v[:, None, :]
    f1a, f1b = up(ffn1_w_up)
    f2a, f2b = up(ffn2_w_up)
    w_out_b = w_out.astype(BF16)
    gw = CMP_BLOCK * HEAD_DIM
    pad_w2 = lambda w: jnp.pad(w, ((0, 0), (0, 0), (0, LANES - HEAD_DIM))).astype(BF16)
    pe_rows = lambda pe: jnp.broadcast_to(pe.reshape(L, 1, gw), (L, SUBLANES, gw)).astype(BF16)
    ncp = S // CMP_STRIDE
    cos_t, sin_t = _rope_tables(jnp.arange(S), 4)
    cos_c, sin_c = _rope_tables(jnp.arange(ncp) * CMP_STRIDE + (CMP_BLOCK - 1), 2)
    return dict(
        ffn1=(f1a, f1b, down(ffn1_w_down), row(ln1_g), row(ln1_b)),
        ffn2=(f2a, f2b, down(ffn2_w_down), row(ln3_g), row(ln3_b)),
        w_in=_w_in_layout(w_in),
        w_out=(_parity_major(w_out_b[:, :D_RWKV], 1), w_out_b[:, D_RWKV:D_RWKV + D_POOL],
               w_out_b[:, D_RWKV + D_POOL:]),
        ln2=(row(ln2_g), row(ln2_b)),
        rw=dict(mu=row(rw_mu), w0=row(rw_w0), w2=rw_w2.astype(BF16), a0=row(rw_a0), a2=rw_a2.astype(BF16),
                g2=_parity_major(rw_g2, -1).astype(BF16), k_k=row(rw_k_k), k_a=row(rw_k_a),
                r_k=rw_r_k.reshape(L, 1, D_RWKV), gn_g=row(_parity_major(rw_gn_g, -1)),
                gn_b=row(_parity_major(rw_gn_b, -1)), ones=_block_ones(D_RWKV)),
        cmp=dict(k_w1=nsa_cmp_k_w1.reshape(L, gw, CMP_HIDDEN).astype(BF16), k_w2=pad_w2(nsa_cmp_k_w2),
                 k_pe=pe_rows(nsa_cmp_pe_k),
                 v_w1=nsa_cmp_v_w1.reshape(L, gw, CMP_HIDDEN).astype(BF16), v_w2=nsa_cmp_v_w2.transpose(0, 2, 1).astype(BF16),
                 v_pe=pe_rows(nsa_cmp_pe_v)),
        gate_b=_gate_bias_layout(nsa_gate_b),
        pool=(pool_w.astype(BF16), row(pool_b), row(pool_scale)),
        rope=(cos_t, sin_t), rope_cmp=(cos_c, sin_c), overlap=_overlap_matrix(S, ncp))


def _mixers(p_all, prm, l):
    r, w, k, v, kk, be, bo, g = _rw_prep(p_all, prm["rw"], l)
    y_rw = _rw_scan(r, w, k, v, kk, be, bo, g, prm["rw"], l)
    y_pool = _pool_mix(p_all, *prm["pool"], l)
    q, kc, vc, ks, vs, kw, vw, gates = _nsa_prep(p_all, *prm["rope"], prm["gate_b"], l)
    k_cmp, v_cmp = _nsa_compress(kc, vc, prm["cmp"], *prm["rope_cmp"], l)
    y_nsa = _nsa_attention(q, k_cmp, v_cmp, ks, vs, kw, vw, gates, prm["overlap"])
    return y_rw, y_pool, y_nsa


def _layer(h, prm, l, B, S):
    T = B * S
    h = _ffn_ln(h, *prm["ffn1"], l)
    p_all = _in_proj(h, prm["w_in"], l).reshape(B, S, P_COLS)
    y_rw, y_pool, y_nsa = _mixers(p_all, prm, l)
    h = _out_proj_ln(h, y_rw.reshape(T, D_RWKV), y_pool.reshape(T, D_POOL), y_nsa.reshape(T, D_NSA),
                     prm["w_out"], *prm["ln2"], l)
    return _ffn_ln(h, *prm["ffn2"], l)
```

```python
import functools

import numpy as np
import jax
import jax.numpy as jnp
from jax import lax
from jax.experimental import pallas as pl
from jax.experimental.pallas import tpu as pltpu

F32 = jnp.float32
BF16 = jnp.bfloat16

D_MODEL = 2048
DEPTH = 4
HEAD_DIM = 64
D_RWKV = 768
D_POOL = 512
D_NSA = 768
RW_HEADS = 12
RW_DECAY_LORA = 64
RW_A_LORA = 64
RW_GATE_LORA = 128
RW_GN_EPS = 64e-5
RW_COLS = 3 * D_RWKV + RW_DECAY_LORA + RW_A_LORA + RW_GATE_LORA
POOL_WINDOWS = (2, 4, 8, 16)
POOL_GROUP = 128
NSA_HEADS = 12
NSA_KV_HEADS = 3
NSA_GQA = 4
NSA_KV = 192
NSA_COLS = D_NSA + 6 * NSA_KV + 3 * NSA_HEADS
CMP_BLOCK = 32
CMP_STRIDE = 16
CMP_HIDDEN = 256
SEL_BLOCK = 64
SEL_TOPK = 16
FORCE_SCORE = 1e9
WINDOW = 512
ROPE_THETA = 500000.0
ROPE_DIM = 16
D_FF = 5504
IN_COLS = RW_COLS + D_POOL + NSA_COLS
ALPHA = (2 * DEPTH) ** 0.25
LN_EPS = 1e-5

LANES = 128
SUBLANES = 8
VMEM_LIMIT = 56 * 1024 * 1024

KV_SLOT = 2 * LANES
NSA_PAD = 2560
P_RW_OFF = 0
P_NSA_OFF = RW_COLS
P_POOL_OFF = RW_COLS + NSA_PAD
P_COLS = RW_COLS + NSA_PAD + D_POOL
NSA_GATE_OFF = D_NSA + 6 * KV_SLOT
D_FF_PAD = 5632

NEG = -1e30


def _params(*sem):
    return pltpu.CompilerParams(dimension_semantics=sem, vmem_limit_bytes=VMEM_LIMIT)


def _layer_norm(z, g, b):
    mu = jnp.mean(z, axis=-1, keepdims=True)
    zc = z - mu
    var = jnp.mean(zc * zc, axis=-1, keepdims=True)
    return zc * lax.rsqrt(var + LN_EPS) * g + b


def _dot(a, b):
    return jnp.dot(a, b, preferred_element_type=F32)


def _dot_nt(a, b):
    return lax.dot_general(a, b, (((1,), (1,)), ((), ())), preferred_element_type=F32)


def _dot_split(x, w):
    hi = x.astype(BF16)
    lo = (x - hi.astype(F32)).astype(BF16)
    return _dot(hi, w) + _dot(lo, w)


FFN_LN_ROWS = 256


def _ffn_kernel(x_ref, wa_ref, wb_ref, wd_ref, g_ref, b_ref, o_ref, xb_ref):
    k = pl.program_id(1)

    @pl.when(k == 0)
    def _():
        xb_ref[...] = x_ref[...].astype(BF16)
        o_ref[...] = jnp.zeros_like(o_ref)

    xb = xb_ref[...]
    a = _dot(xb, wa_ref[...])
    b = _dot(xb, wb_ref[...])
    h = (a / (1.0 + jnp.exp(-a))) * b
    o_ref[...] += _dot(h.astype(BF16), wd_ref[...])

    @pl.when(k == pl.num_programs(1) - 1)
    def _():
        for c in range(o_ref.shape[0] // FFN_LN_ROWS):
            rows = slice(c * FFN_LN_ROWS, (c + 1) * FFN_LN_ROWS)
            z = ALPHA * x_ref[rows, :] + 0.5 * o_ref[rows, :]
            o_ref[rows, :] = _layer_norm(z, g_ref[...], b_ref[...])


def _ffn_ln(x, wa, wb, wd, g, b, layer, *, tm=1024, tf=256):
    T, D = x.shape
    fp = wa.shape[-1]
    tm = min(tm, T)
    return pl.pallas_call(
        _ffn_kernel,
        out_shape=jax.ShapeDtypeStruct((T, D), F32),
        grid=(T // tm, fp // tf),
        in_specs=[
            pl.BlockSpec((tm, D), lambda i, k: (i, 0)),
            pl.BlockSpec((None, D, tf), lambda i, k: (layer, 0, k)),
            pl.BlockSpec((None, D, tf), lambda i, k: (layer, 0, k)),
            pl.BlockSpec((None, tf, D), lambda i, k: (layer, k, 0)),
            pl.BlockSpec((None, 1, D), lambda i, k: (layer, 0, 0)),
            pl.BlockSpec((None, 1, D), lambda i, k: (layer, 0, 0)),
        ],
        out_specs=pl.BlockSpec((tm, D), lambda i, k: (i, 0)),
        scratch_shapes=[pltpu.VMEM((tm, D), BF16)],
        compiler_params=_params("parallel", "arbitrary"),
        name="ffn_ln",
    )(x, wa, wb, wd, g, b)


def _inproj_kernel(x_ref, w_ref, o_ref, xb_ref):
    @pl.when(pl.program_id(1) == 0)
    def _():
        xb_ref[...] = x_ref[...].astype(BF16)

    o_ref[...] = _dot(xb_ref[...], w_ref[...])


def _in_proj(x, w, layer, *, tm=1024, tn=512):
    T, D = x.shape
    n = w.shape[-1]
    tm = min(tm, T)
    return pl.pallas_call(
        _inproj_kernel,
        out_shape=jax.ShapeDtypeStruct((T, n), F32),
        grid=(T // tm, n // tn),
        in_specs=[
            pl.BlockSpec((tm, D), lambda i, j: (i, 0)),
            pl.BlockSpec((None, D, tn), lambda i, j: (layer, 0, j)),
        ],
        out_specs=pl.BlockSpec((tm, tn), lambda i, j: (i, j)),
        scratch_shapes=[pltpu.VMEM((tm, D), BF16)],
        compiler_params=_params("parallel", "arbitrary"),
        name="in_proj",
    )(x, w)


def _outproj_kernel(x_ref, yr_ref, yp_ref, yn_ref, wr_ref, wp_ref, wn_ref, g_ref, b_ref, o_ref):
    y = _dot(yr_ref[...], wr_ref[...]) + _dot(yp_ref[...], wp_ref[...]) + _dot(yn_ref[...], wn_ref[...])
    o_ref[...] = _layer_norm(ALPHA * x_ref[...] + y, g_ref[...], b_ref[...])


def _out_proj_ln(x, y_rw, y_pool, y_nsa, w_out, g, b, layer, *, tm=512):
    T, D = x.shape
    return pl.pallas_call(
        _outproj_kernel,
        out_shape=jax.ShapeDtypeStruct((T, D), F32),
        grid=(T // tm,),
        in_specs=[
            pl.BlockSpec((tm, D), lambda i: (i, 0)),
            pl.BlockSpec((tm, D_RWKV), lambda i: (i, 0)),
            pl.BlockSpec((tm, D_POOL), lambda i: (i, 0)),
            pl.BlockSpec((tm, D_NSA), lambda i: (i, 0)),
            pl.BlockSpec((None, D_RWKV, D), lambda i: (layer, 0, 0)),
            pl.BlockSpec((None, D_POOL, D), lambda i: (layer, 0, 0)),
            pl.BlockSpec((None, D_NSA, D), lambda i: (layer, 0, 0)),
            pl.BlockSpec((None, 1, D), lambda i: (layer, 0, 0)),
            pl.BlockSpec((None, 1, D), lambda i: (layer, 0, 0)),
        ],
        out_specs=pl.BlockSpec((tm, D), lambda i: (i, 0)),
        compiler_params=_params("parallel"),
        name="out_proj_ln",
    )(x, y_rw, y_pool, y_nsa, w_out[0], w_out[1], w_out[2], g, b)


POOL_HALO = 16


def _pool_kernel(p_ref, halo_ref, w_ref, b_ref, sc_ref, o_ref, xs_ref, *, ts):
    s = pl.program_id(1)
    x = p_ref[0]
    halo = jnp.where(s > 0, halo_ref[0], 0.0)
    xs_ref[0:POOL_HALO, :] = halo
    xs_ref[POOL_HALO:POOL_HALO + ts, :] = x
    t1 = (s * ts + 1 + lax.broadcasted_iota(jnp.int32, (ts, 1), 0)).astype(F32)
    for gi, win in enumerate(POOL_WINDOWS):
        c0 = gi * POOL_GROUP
        acc = x[:, c0:c0 + POOL_GROUP]
        for j in range(1, win):
            acc = acc + xs_ref[POOL_HALO - j:POOL_HALO - j + ts, c0:c0 + POOL_GROUP]
        pooled = acc / jnp.minimum(t1, float(win)) - x[:, c0:c0 + POOL_GROUP]
        z = _dot(pooled.astype(BF16), w_ref[gi]) + b_ref[:, c0:c0 + POOL_GROUP]
        o_ref[0, :, c0:c0 + POOL_GROUP] = (z * sc_ref[:, c0:c0 + POOL_GROUP]).astype(o_ref.dtype)


def _pool_mix(p_all, pool_w, pool_b, pool_scale, layer, *, ts=512):
    B, S, _ = p_all.shape
    cb = P_POOL_OFF // D_POOL
    hb = ts // POOL_HALO
    return pl.pallas_call(
        functools.partial(_pool_kernel, ts=ts),
        out_shape=jax.ShapeDtypeStruct((B, S, D_POOL), BF16),
        grid=(B, S // ts),
        in_specs=[
            pl.BlockSpec((1, ts, D_POOL), lambda b, s: (b, s, cb)),
            pl.BlockSpec((1, POOL_HALO, D_POOL), lambda b, s: (b, jnp.maximum(s * hb - 1, 0), cb)),
            pl.BlockSpec((None, 4, POOL_GROUP, POOL_GROUP), lambda b, s: (layer, 0, 0, 0)),
            pl.BlockSpec((None, 1, D_POOL), lambda b, s: (layer, 0, 0)),
            pl.BlockSpec((None, 1, D_POOL), lambda b, s: (layer, 0, 0)),
        ],
        out_specs=pl.BlockSpec((1, ts, D_POOL), lambda b, s: (b, s, 0)),
        scratch_shapes=[pltpu.VMEM((ts + POOL_HALO, D_POOL), F32)],
        compiler_params=_params("parallel", "parallel"),
        name="pool_mix",
    )(p_all, p_all, pool_w, pool_b, pool_scale)


def _softplus(z):
    return jnp.maximum(z, 0.0) + jnp.log1p(jnp.exp(-jnp.abs(z)))


def _sigmoid(z):
    return 1.0 / (1.0 + jnp.exp(-z))


def _lanes_to_parity_major(x):
    n_pairs = x.shape[1] // LANES
    low = lax.broadcasted_iota(jnp.int32, (1, LANES), 1) < HEAD_DIM

    def chunk(e, to_low):
        pair, parity = e % n_pairs, e // n_pairs
        src = x[:, pair * LANES:(pair + 1) * LANES]
        return src if (parity == 0) == to_low else pltpu.roll(src, HEAD_DIM, 1)

    return jnp.concatenate([jnp.where(low, chunk(2 * d, True), chunk(2 * d + 1, False))
                            for d in range(n_pairs)], axis=1)


def _rw_prep_kernel(p_ref, prev_ref, mu_ref, w0_ref, w2_ref, a0_ref, a2_ref, g2_ref, kk_ref, ka_ref,
                    rk_ref, ones_ref,
                    r_o, w_o, k_o, v_o, kk_o, be_o, bo_o, g_o, *, ts):
    s = pl.program_id(1)
    x = p_ref[0]
    last = jnp.where(s > 0, prev_ref[0][SUBLANES - 1:SUBLANES, :], 0.0)
    row = lax.broadcasted_iota(jnp.int32, (ts, 1), 0)
    shifted = jnp.where(row == 0, last, pltpu.roll(x, 1, 0))
    xm = x + (shifted - x) * mu_ref[...]
    c = D_RWKV
    r = xm[:, 0:c]
    k = xm[:, c:2 * c]
    v = xm[:, 2 * c:3 * c]
    lora = xm[:, 3 * c:3 * c + LANES]
    wl = lora[:, :RW_DECAY_LORA]
    al = lora[:, RW_DECAY_LORA:]
    gl = xm[:, 3 * c + LANES:]
    w = -_softplus(-(w0_ref[...] + _dot(jnp.tanh(wl).astype(BF16), w2_ref[...]))) - 0.5
    decay = jnp.exp(-jnp.exp(w))
    a = _sigmoid(a0_ref[...] + _dot(al.astype(BF16), a2_ref[...]))
    g = _dot(_sigmoid(gl).astype(BF16), g2_ref[...])
    ones = ones_ref[...]
    kk = k * kk_ref[...]
    nrm = jnp.sqrt(_dot_split(kk * kk, ones))
    kk = kk / jnp.maximum(nrm, 1e-12)
    k_mod = k * (1.0 + (a - 1.0) * ka_ref[...])
    bonus = _dot_split(r * k_mod * rk_ref[...], ones) * v
    r_o[0] = r
    w_o[0] = decay
    k_o[0] = k_mod
    v_o[0] = v
    kk_o[0] = kk
    be_o[0] = kk * a
    bo_o[0] = _lanes_to_parity_major(bonus).astype(bo_o.dtype)
    g_o[0] = g.astype(g_o.dtype)


def _rw_prep(p_all, prm, layer, *, ts=256):
    B, S, _ = p_all.shape
    cb = P_RW_OFF // RW_COLS
    hb = ts // SUBLANES
    vec = lambda n: pl.BlockSpec((None, 1, n), lambda b, s: (layer, 0, 0))
    mat = lambda m, n: pl.BlockSpec((None, m, n), lambda b, s: (layer, 0, 0))
    out = jax.ShapeDtypeStruct((B, S, D_RWKV), F32)
    ospec = pl.BlockSpec((1, ts, D_RWKV), lambda b, s: (b, s, 0))
    return pl.pallas_call(
        functools.partial(_rw_prep_kernel, ts=ts),
        out_shape=[out] * 6 + [jax.ShapeDtypeStruct((B, S, D_RWKV), BF16)] * 2,
        grid=(B, S // ts),
        in_specs=[
            pl.BlockSpec((1, ts, RW_COLS), lambda b, s: (b, s, cb)),
            pl.BlockSpec((1, SUBLANES, RW_COLS), lambda b, s: (b, jnp.maximum(s * hb - 1, 0), cb)),
            vec(RW_COLS), vec(D_RWKV), mat(RW_DECAY_LORA, D_RWKV), vec(D_RWKV), mat(RW_A_LORA, D_RWKV),
            mat(RW_GATE_LORA, D_RWKV), vec(D_RWKV), vec(D_RWKV), vec(D_RWKV),
            pl.BlockSpec((D_RWKV, D_RWKV), lambda b, s: (0, 0)),
        ],
        out_specs=[ospec] * 8,
        compiler_params=_params("parallel", "parallel"),
        name="rw_prep",
    )(p_all, p_all, prm["mu"], prm["w0"], prm["w2"], prm["a0"], prm["a2"], prm["g2"], prm["k_k"],
      prm["k_a"], prm["r_k"], prm["ones"])


RW_PAIRS = D_RWKV // LANES


def _rw_scan_kernel(r_ref, w_ref, k_ref, v_ref, kk_ref, be_ref, bo_ref, g_ref, gng_ref, gnb_ref, ones_ref,
                    pair_ref, half_ref, o_ref, st_ref, vc_ref, sr_ref, y_ref, *, ts, nb):
    s = pl.program_id(1)

    @pl.when(s == 0)
    def _():
        st_ref[...] = jnp.zeros_like(st_ref)

    side, stack = 2, RW_PAIRS // 2
    rows, width = stack * HEAD_DIM, side * LANES
    lane = lax.broadcasted_iota(jnp.int32, (rows, width), 1)
    row = lax.broadcasted_iota(jnp.int32, (rows, width), 0)
    eye = jnp.where(lane % HEAD_DIM == row % HEAD_DIM, 1.0, 0.0)
    ones_blk = pair_ref[...]

    def spread(x8, j):
        blocks = []
        for st in range(stack):
            lanes = jnp.concatenate([x8[j:j + 1, (sd * stack + st) * LANES:(sd * stack + st + 1) * LANES]
                                     for sd in range(side)], axis=1)
            blocks.append(jnp.broadcast_to(lanes, (HEAD_DIM, width)))
        return jnp.concatenate(blocks, axis=0)

    def step(i, carry):
        base = pl.multiple_of(i * SUBLANES, SUBLANES)
        tiles = [[ref[b, pl.ds(base, SUBLANES), :] for ref in (kk_ref, w_ref, be_ref, k_ref, r_ref, v_ref)]
                 for b in range(nb)]
        for b in range(nb):
            lhs = jnp.concatenate([eye * spread(tiles[b][5], j) for j in range(SUBLANES)], axis=0)
            vc_ref[b] = _dot(lhs.astype(BF16), ones_blk)
        sts = [st_ref[b] for b in range(nb)]
        for j in range(SUBLANES):
            for b in range(nb):
                kk8, w8, be8, k8, r8, _ = tiles[b]
                st = sts[b]
                sa = _dot((st * spread(kk8, j)).astype(BF16), ones_blk)
                st = (st * spread(w8, j) - sa * spread(be8, j)
                      + vc_ref[b, j * rows:(j + 1) * rows, :] * spread(k8, j))
                sr = (st * spread(r8, j)).astype(BF16)
                for sd in range(side):
                    lo = (j * side + sd) * rows
                    sr_ref[b, lo:lo + rows, :] = sr[:, sd * LANES:(sd + 1) * LANES]
                sts[b] = st
        for b in range(nb):
            st_ref[b] = sts[b]
            yt = _dot_nt(half_ref[...], sr_ref[b])
            per = side * rows
            y_ref[b, pl.ds(base, SUBLANES), :] = jnp.concatenate(
                [jnp.concatenate([yt[0:1, j * per:(j + 1) * per], yt[1:2, j * per:(j + 1) * per]], axis=1)
                 for j in range(SUBLANES)], axis=0)
        return carry

    lax.fori_loop(0, ts // SUBLANES, step, 0)

    ones = ones_ref[...]
    for b in range(nb):
        y = y_ref[b]
        mean = _dot_split(y, ones) * (1.0 / HEAD_DIM)
        yc = y - mean
        var = _dot_split(yc * yc, ones) * (1.0 / HEAD_DIM)
        yn = yc * lax.rsqrt(var + RW_GN_EPS) * gng_ref[...] + gnb_ref[...]
        o_ref[b] = ((yn + bo_ref[b]) * g_ref[b]).astype(o_ref.dtype)


def _rw_scan(r, w, k, v, kk, be, bo, g, prm, layer, *, ts=128):
    B, S, _ = r.shape
    nb = 4 if B % 4 == 0 else (2 if B % 2 == 0 else 1)
    rows, width = RW_PAIRS // 2 * HEAD_DIM, 2 * LANES
    seq = pl.BlockSpec((nb, ts, D_RWKV), lambda b, s: (b, s, 0))
    vec = pl.BlockSpec((None, 1, D_RWKV), lambda b, s: (layer, 0, 0))
    return pl.pallas_call(
        functools.partial(_rw_scan_kernel, ts=ts, nb=nb),
        out_shape=jax.ShapeDtypeStruct((B, S, D_RWKV), BF16),
        grid=(B // nb, S // ts),
        in_specs=[seq] * 8 + [vec, vec, pl.BlockSpec((D_RWKV, D_RWKV), lambda b, s: (0, 0)),
                              pl.BlockSpec((width, width), lambda b, s: (0, 0)),
                              pl.BlockSpec((SUBLANES, LANES), lambda b, s: (0, 0))],
        out_specs=seq,
        scratch_shapes=[pltpu.VMEM((nb, rows, width), F32), pltpu.VMEM((nb, SUBLANES * rows, width), F32),
                        pltpu.VMEM((nb, SUBLANES * RW_PAIRS * HEAD_DIM, LANES), BF16),
                        pltpu.VMEM((nb, ts, D_RWKV), F32)],
        compiler_params=_params("parallel", "arbitrary"),
        name="rw_scan",
    )(r, w, k, v, kk, be, bo, g, prm["gn_g"], prm["gn_b"], prm["ones"], prm["ones"][:width, :width],
      prm["ones"][:SUBLANES * HEAD_DIM:HEAD_DIM, :LANES])


def _rope(x, cos, sin):
    half = ROPE_DIM // 2
    w = x.shape[-1]
    d = lax.broadcasted_iota(jnp.int32, x.shape, 1) % HEAD_DIM
    partner = jnp.where(d < half, pltpu.roll(x, w - half, 1), pltpu.roll(x, half, 1))
    return x * cos + partner * sin


def _nsa_prep_kernel(p_ref, cos_ref, sin_ref, gb_ref, q_o, kc_o, vc_o, ks_o, vs_o, kw_o, vw_o, gt_o):
    cos = cos_ref[...]
    sin = sin_ref[...]
    for hq in range(NSA_HEADS // 4):
        x = p_ref[0, :, hq * KV_SLOT:(hq + 1) * KV_SLOT]
        qr = _rope(x, cos, sin) * (HEAD_DIM ** -0.5)
        for j in range(4):
            q_o[0, 4 * hq + j] = qr[:, j * HEAD_DIM:(j + 1) * HEAD_DIM].astype(BF16)
    for i, ref in enumerate((kc_o, vc_o, ks_o, vs_o, kw_o, vw_o)):
        part = p_ref[0, :, D_NSA + i * KV_SLOT:D_NSA + (i + 1) * KV_SLOT]
        if i in (2, 4):
            part = _rope(part, cos, sin)
        if i in (3, 5):
            part = part.T
            for h in range(NSA_KV_HEADS):
                ref[0, h] = part[h * HEAD_DIM:(h + 1) * HEAD_DIM, :].astype(BF16)
        else:
            for h in range(NSA_KV_HEADS):
                ref[0, h] = part[:, h * HEAD_DIM:(h + 1) * HEAD_DIM].astype(BF16)
    gts = _sigmoid(p_ref[0, :, NSA_GATE_OFF:NSA_GATE_OFF + LANES] + gb_ref[...]).T
    for h in range(NSA_KV_HEADS):
        gt_o[0, h] = gts[h * 12:(h + 1) * 12, :]


def _nsa_prep(p_all, cos, sin, gate_b, layer, *, ts=256):
    B, S, _ = p_all.shape
    H = NSA_KV_HEADS
    k_shape = jax.ShapeDtypeStruct((B, H, S, HEAD_DIM), BF16)
    k_spec = pl.BlockSpec((1, H, ts, HEAD_DIM), lambda b, s: (b, 0, s, 0))
    vt_shape = jax.ShapeDtypeStruct((B, H, HEAD_DIM, S), BF16)
    vt_spec = pl.BlockSpec((1, H, HEAD_DIM, ts), lambda b, s: (b, 0, 0, s))
    return pl.pallas_call(
        _nsa_prep_kernel,
        out_shape=[jax.ShapeDtypeStruct((B, NSA_HEADS, S, HEAD_DIM), BF16),
                   k_shape, k_shape, k_shape, vt_shape, k_shape, vt_shape,
                   jax.ShapeDtypeStruct((B, H, 12, S), F32)],
        grid=(B, S // ts),
        in_specs=[
            pl.BlockSpec((1, ts, NSA_PAD), lambda b, s: (b, s, P_NSA_OFF // NSA_PAD)),
            pl.BlockSpec((ts, KV_SLOT), lambda b, s: (s, 0)),
            pl.BlockSpec((ts, KV_SLOT), lambda b, s: (s, 0)),
            pl.BlockSpec((None, 1, LANES), lambda b, s: (layer, 0, 0)),
        ],
        out_specs=[pl.BlockSpec((1, NSA_HEADS, ts, HEAD_DIM), lambda b, s: (b, 0, s, 0)),
                   k_spec, k_spec, k_spec, vt_spec, k_spec, vt_spec,
                   pl.BlockSpec((1, H, 12, ts), lambda b, s: (b, 0, 0, s))],
        compiler_params=_params("parallel", "parallel"),
        name="nsa_prep",
    )(p_all, cos, sin, gate_b)


def _gelu_tanh(x):
    return 0.5 * x * (1.0 + jnp.tanh(float(np.sqrt(2.0 / np.pi)) * (x + 0.044715 * (x * x * x))))


def _nsa_cmp_kernel(kc_ref, vc_ref, kw1_ref, kw2_ref, kpe_ref, vw1_ref, vw2_ref, vpe_ref, cos_ref, sin_ref,
                    k_o, v_o):
    def hidden(g, w1_ref, pe_ref):
        half = CMP_STRIDE * HEAD_DIM
        first = _dot(g, w1_ref[:half, :])
        second = _dot(g, w1_ref[half:, :])
        n = first.shape[0]
        bias = _dot(pe_ref[...], w1_ref[...])[0:1]
        return _gelu_tanh(first + pltpu.roll(second, n - 1, 0) + bias).astype(BF16)

    k = _dot(hidden(kc_ref[0, 0], kw1_ref, kpe_ref), kw2_ref[...])
    k_o[0, 0] = _rope(k, cos_ref[...], sin_ref[...])[:, :HEAD_DIM].astype(BF16)
    v_o[0, 0] = _dot_nt(vw2_ref[...], hidden(vc_ref[0, 0], vw1_ref, vpe_ref)).astype(BF16)


def _nsa_compress(kc, vc, prm, cos_c, sin_c, layer):
    B, H, S, _ = kc.shape
    ng = S // CMP_STRIDE
    gw = CMP_STRIDE * HEAD_DIM
    g_k = kc.reshape(B, H, ng, gw)
    g_v = vc.reshape(B, H, ng, gw)
    gspec = pl.BlockSpec((1, 1, ng, gw), lambda b, h: (b, h, 0, 0))
    w1 = pl.BlockSpec((None, 2 * gw, CMP_HIDDEN), lambda b, h: (layer, 0, 0))
    w2 = pl.BlockSpec((None, CMP_HIDDEN, LANES), lambda b, h: (layer, 0, 0))
    pe = pl.BlockSpec((None, SUBLANES, 2 * gw), lambda b, h: (layer, 0, 0))
    tab = pl.BlockSpec((ng, LANES), lambda b, h: (0, 0))
    return pl.pallas_call(
        _nsa_cmp_kernel,
        out_shape=[jax.ShapeDtypeStruct((B, H, ng, HEAD_DIM), BF16), jax.ShapeDtypeStruct((B, H, HEAD_DIM, ng), BF16)],
        grid=(B, H),
        in_specs=[gspec, gspec, w1, w2, pe, w1,
                  pl.BlockSpec((None, HEAD_DIM, CMP_HIDDEN), lambda b, h: (layer, 0, 0)), pe, tab, tab],
        out_specs=[pl.BlockSpec((1, 1, ng, HEAD_DIM), lambda b, h: (b, h, 0, 0)),
                   pl.BlockSpec((1, 1, HEAD_DIM, ng), lambda b, h: (b, h, 0, 0))],
        compiler_params=_params("parallel", "parallel"),
        name="nsa_compress",
    )(g_k, g_v, prm["k_w1"], prm["k_w2"], prm["k_pe"], prm["v_w1"], prm["v_w2"], prm["v_pe"], cos_c, sin_c)


def _nsa_attn_kernel(q_ref, kcmp_ref, vcmp_ref, ks_ref, vs_ref, kw_ref, vw_ref, gt_ref, ov_ref, o_ref,
                     m_ref, l_ref, acc_ref, sb_ref, *, tq):
    G = NSA_GQA
    tk = tq
    qi = pl.program_id(2)
    q0 = qi * tq
    q_all = q_ref[0].reshape(G * tq, HEAD_DIM)
    pos = q0 + lax.broadcasted_iota(jnp.int32, (1, tq), 1)
    tiny = jnp.finfo(F32).tiny
    heads = lambda x: jnp.tile(x, (1, G))
    split = lambda x: [x[:, g * tq:(g + 1) * tq] for g in range(G)]

    ncp = kcmp_ref.shape[2]
    n_idx = lax.broadcasted_iota(jnp.int32, (ncp, 1), 0)
    cbias = jnp.where((n_idx * CMP_STRIDE + (CMP_BLOCK - 1)) <= pos, 0.0, NEG)
    any_cmp = jnp.where(pos >= CMP_BLOCK - 1, 1.0, 0.0)
    s = _dot_nt(kcmp_ref[0, 0], q_all) + heads(cbias)
    e = jnp.exp(s - jnp.max(s, axis=0, keepdims=True))
    p = e * (heads(any_cmp) / jnp.maximum(jnp.sum(e, axis=0, keepdims=True), tiny))
    o_cmp = split(_dot(vcmp_ref[0, 0], p.astype(BF16)))
    ps = split(p)
    psum = (ps[0] + ps[1]) + (ps[2] + ps[3])
    p_hi = psum.astype(BF16)
    p_lo = (psum - p_hi.astype(F32)).astype(BF16)
    imp = _dot(ov_ref[...], p_hi) + _dot(ov_ref[...], p_lo)

    n_sel = ov_ref.shape[0]
    blk = lax.broadcasted_iota(jnp.int32, (n_sel, 1), 0)
    cur = pos // SEL_BLOCK
    forced = (blk == 0) | (blk == cur) | (blk == cur - 1)
    valid = blk <= cur
    score = jnp.where(valid, jnp.where(forced, FORCE_SCORE, imp), -jnp.inf)
    rank = jnp.zeros((n_sel, tq), F32)
    for i in range(n_sel):
        ci = score[i:i + 1, :]
        beats = (ci > score) | ((ci == score) & (blk > i))
        rank = rank + jnp.where(beats, 1.0, 0.0)
    chosen = (rank < float(SEL_TOPK)) & valid
    sel_bias = jnp.where(chosen, 0.0, NEG)
    for j in range(n_sel):
        sb_ref[j] = jnp.broadcast_to(sel_bias[j:j + 1, :], (SUBLANES, tq))

    kcol = lax.broadcasted_iota(jnp.int32, (tk, 1), 0)

    def scores(kt):
        k0 = pl.multiple_of(kt * tk, tk)
        return _dot_nt(ks_ref[0, 0, pl.ds(k0, tk), :], q_all)

    def sel_update(kt, s, extra=None):
        bpt = tk // SEL_BLOCK
        bias = jnp.concatenate([jnp.tile(sb_ref[kt * bpt + j], (SEL_BLOCK // SUBLANES, 1)) for j in range(bpt)],
                               axis=0)
        if extra is not None:
            bias = bias + extra
        k0 = pl.multiple_of(kt * tk, tk)
        vt = vs_ref[0, 0, :, pl.ds(k0, tk)]
        s = s + heads(bias)
        m_prev = m_ref[...]
        m_new = jnp.maximum(m_prev, jnp.max(s, axis=0, keepdims=True))
        p = jnp.exp(s - m_new)
        corr = jnp.exp(m_prev - m_new)
        l_ref[...] = corr * l_ref[...] + jnp.sum(p, axis=0, keepdims=True)
        acc_ref[...] = corr * acc_ref[...] + _dot(vt, p.astype(BF16))
        m_ref[...] = m_new

    m_ref[...] = jnp.full_like(m_ref, NEG)
    l_ref[...] = jnp.zeros_like(l_ref)
    acc_ref[...] = jnp.zeros_like(acc_ref)

    def sel_body(i, carry):
        s_a, s_b = scores(2 * i), scores(2 * i + 1)
        sel_update(2 * i, s_a)
        sel_update(2 * i + 1, s_b)
        return carry

    lax.fori_loop(0, qi // 2, sel_body, 0)

    @pl.when(qi % 2 == 1)
    def _():
        sel_update(qi - 1, scores(qi - 1))

    causal = jnp.where(q0 + kcol <= pos, 0.0, NEG)
    sel_update(qi, scores(qi), causal)
    o_sel = split(acc_ref[...] / jnp.maximum(l_ref[...], tiny))

    wk = min(WINDOW + tq, ks_ref.shape[2])
    w0 = pl.multiple_of(jnp.maximum(q0 + tq - wk, 0), tk)
    wpos = w0 + lax.broadcasted_iota(jnp.int32, (wk, 1), 0)
    wbias = jnp.where((wpos <= pos) & (wpos > pos - WINDOW), 0.0, NEG)
    s = _dot_nt(kw_ref[0, 0, pl.ds(w0, wk), :], q_all) + heads(wbias)
    e = jnp.exp(s - jnp.max(s, axis=0, keepdims=True))
    pv = _dot(vw_ref[0, 0, :, pl.ds(w0, wk)], e.astype(BF16))
    o_win = split(pv / jnp.maximum(jnp.sum(e, axis=0, keepdims=True), tiny))

    gt = gt_ref[0, 0]
    outs = [gt[g:g + 1] * o_cmp[g] + gt[G + g:G + g + 1] * o_sel[g] + gt[2 * G + g:2 * G + g + 1] * o_win[g]
            for g in range(G)]
    for half in range(G // 2):
        pair = jnp.concatenate(outs[2 * half:2 * half + 2], axis=0)
        o_ref[0, :, half * LANES:(half + 1) * LANES] = pair.T.astype(o_ref.dtype)


def _nsa_attention(q, k_cmp, v_cmp_t, ks, vs_t, kw, vw_t, gates_t, overlap_t, *, tq=256):
    B, _, S, _ = q.shape
    H, G = NSA_KV_HEADS, NSA_GQA
    tq = min(tq, S)
    assert WINDOW == 2 * tq or S <= tq
    ncp = k_cmp.shape[2]
    n_cmp = (S - CMP_BLOCK) // CMP_STRIDE + 1
    keys = pl.BlockSpec((1, 1, S, HEAD_DIM), lambda b, h, i: (b, h, 0, 0))
    vals = pl.BlockSpec((1, 1, HEAD_DIM, S), lambda b, h, i: (b, h, 0, 0))
    return pl.pallas_call(
        functools.partial(_nsa_attn_kernel, tq=tq),
        out_shape=jax.ShapeDtypeStruct((B, S, D_NSA), BF16),
        grid=(B, H, S // tq),
        in_specs=[
            pl.BlockSpec((1, G, tq, HEAD_DIM), lambda b, h, i: (b, h, i, 0)),
            pl.BlockSpec((1, 1, ncp, HEAD_DIM), lambda b, h, i: (b, h, 0, 0)),
            pl.BlockSpec((1, 1, HEAD_DIM, ncp), lambda b, h, i: (b, h, 0, 0)),
            keys, vals, keys, vals,
            pl.BlockSpec((1, 1, 12, tq), lambda b, h, i: (b, h, 0, i)),
            pl.BlockSpec(overlap_t.shape, lambda b, h, i: (0, 0)),
        ],
        out_specs=pl.BlockSpec((1, tq, G * HEAD_DIM), lambda b, h, i: (b, i, h)),
        scratch_shapes=[pltpu.VMEM((1, G * tq), F32), pltpu.VMEM((1, G * tq), F32),
                        pltpu.VMEM((HEAD_DIM, G * tq), F32), pltpu.VMEM((S // SEL_BLOCK, SUBLANES, tq), F32)],
        compiler_params=_params("parallel", "parallel", "arbitrary"),
        name="nsa_attention",
    )(q, k_cmp, v_cmp_t, ks, vs_t, kw, vw_t, gates_t, overlap_t)


def _rope_tables(pos, heads):
    half = ROPE_DIM // 2
    inv_freq = ROPE_THETA ** (-jnp.arange(half, dtype=F32) / half)
    ang = pos.astype(F32)[:, None] * inv_freq
    cos, sin = jnp.cos(ang), jnp.sin(ang)
    n = pos.shape[0]
    rest = HEAD_DIM - ROPE_DIM
    cos_h = jnp.concatenate([cos, cos, jnp.ones((n, rest), F32)], axis=1)
    sin_h = jnp.concatenate([-sin, sin, jnp.zeros((n, rest), F32)], axis=1)
    return jnp.tile(cos_h, (1, heads)), jnp.tile(sin_h, (1, heads))


def _overlap_matrix(S, ncp):
    n_cmp = (S - CMP_BLOCK) // CMP_STRIDE + 1
    n_sel = S // SEL_BLOCK
    cmp_start = np.arange(n_cmp) * CMP_STRIDE
    sel_start = np.arange(n_sel) * SEL_BLOCK
    ov = np.clip(np.minimum(cmp_start[:, None] + CMP_BLOCK, sel_start[None, :] + SEL_BLOCK)
                 - np.maximum(cmp_start[:, None], sel_start[None, :]), 0, None) / CMP_BLOCK
    full = np.zeros((n_sel, ncp), np.float32)
    full[:, :n_cmp] = ov.T
    return jnp.asarray(full, BF16)


RW_HEAD_ORDER = tuple(2 * (e % RW_PAIRS) + e // RW_PAIRS for e in range(RW_HEADS))


def _parity_major(a, axis):
    axis = axis % a.ndim
    shape = a.shape
    a = a.reshape(shape[:axis] + (RW_HEADS, HEAD_DIM) + shape[axis + 1:])
    a = jnp.take(a, jnp.asarray(RW_HEAD_ORDER), axis=axis)
    return a.reshape(shape)


def _block_ones(n):
    idx = np.arange(n) // HEAD_DIM
    return jnp.asarray((idx[:, None] == idx[None, :]).astype(np.float32), BF16)


def _w_in_layout(w_in):
    L = w_in.shape[0]
    rw = w_in[:, :, :RW_COLS]
    pool = w_in[:, :, RW_COLS:RW_COLS + D_POOL]
    nsa = w_in[:, :, RW_COLS + D_POOL:]
    zeros = lambda n: jnp.zeros((L, D_MODEL, n), w_in.dtype)
    segs = [nsa[:, :, :D_NSA]]
    for i in range(6):
        segs += [nsa[:, :, D_NSA + i * NSA_KV:D_NSA + (i + 1) * NSA_KV], zeros(KV_SLOT - NSA_KV)]
    gates = nsa[:, :, D_NSA + 6 * NSA_KV:]
    gates = gates.reshape(L, D_MODEL, NSA_KV_HEADS, NSA_GQA, 3).transpose(0, 1, 2, 4, 3)
    segs += [gates.reshape(L, D_MODEL, 3 * NSA_HEADS),
             zeros(NSA_PAD - NSA_GATE_OFF - 3 * NSA_HEADS)]
    return jnp.concatenate([rw] + segs + [pool], axis=-1).astype(BF16)


def _gate_bias_layout(gate_b):
    L = gate_b.shape[0]
    gb = gate_b.reshape(L, NSA_KV_HEADS, NSA_GQA, 3).transpose(0, 1, 3, 2).reshape(L, 1, 3 * NSA_HEADS)
    return jnp.pad(gb, ((0, 0), (0, 0), (0, LANES - 3 * NSA_HEADS)))


def kernel(x, ffn1_w_up, ffn1_w_down, ln1_g, ln1_b, w_in, rw_mu, rw_w0, rw_w2, rw_a0, rw_a2, rw_g2, rw_k_k,
           rw_k_a, rw_r_k, rw_gn_g, rw_gn_b, pool_w, pool_b, pool_scale, nsa_cmp_pe_k, nsa_cmp_pe_v,
           nsa_cmp_k_w1, nsa_cmp_k_w2, nsa_cmp_v_w1, nsa_cmp_v_w2, nsa_gate_b, w_out, ln2_g, ln2_b,
           ffn2_w_up, ffn2_w_down, ln3_g, ln3_b):
    prm = _prepare(x.shape[1], ffn1_w_up, ffn1_w_down, ln1_g, ln1_b, w_in, rw_mu, rw_w0, rw_w2, rw_a0, rw_a2,
                   rw_g2, rw_k_k, rw_k_a, rw_r_k, rw_gn_g, rw_gn_b, pool_w, pool_b, pool_scale, nsa_cmp_pe_k,
                   nsa_cmp_pe_v, nsa_cmp_k_w1, nsa_cmp_k_w2, nsa_cmp_v_w1, nsa_cmp_v_w2, nsa_gate_b, w_out,
                   ln2_g, ln2_b, ffn2_w_up, ffn2_w_down, ln3_g, ln3_b)
    B, S, D = x.shape
    h = x.reshape(B * S, D)
    for l in range(w_in.shape[0]):
        h = _layer(h, prm, l, B, S)
    return h.reshape(B, S, D)


def _prepare(S, ffn1_w_up, ffn1_w_down, ln1_g, ln1_b, w_in, rw_mu, rw_w0, rw_w2, rw_a0, rw_a2, rw_g2, rw_k_k,
             rw_k_a, rw_r_k, rw_gn_g, rw_gn_b, pool_w, pool_b, pool_scale, nsa_cmp_pe_k, nsa_cmp_pe_v,
             nsa_cmp_k_w1, nsa_cmp_k_w2, nsa_cmp_v_w1, nsa_cmp_v_w2, nsa_gate_b, w_out, ln2_g, ln2_b,
             ffn2_w_up, ffn2_w_down, ln3_g, ln3_b):
    L = w_in.shape[0]
    fpad = D_FF_PAD - D_FF

    def up(w):
        a = jnp.pad(w[:, :, :D_FF], ((0, 0), (0, 0), (0, fpad))).astype(BF16)
        b = jnp.pad(w[:, :, D_FF:], ((0, 0), (0, 0), (0, fpad))).astype(BF16)
        return a, b

    def down(w):
        return jnp.pad(w, ((0, 0), (0, fpad), (0, 0))).astype(BF16)

    row = lambda v: v[:, None, :]
    f1a, f1b = up(ffn1_w_up)
    f2a, f2b = up(ffn2_w_up)
    w_out_b = w_out.astype(BF16)
    gw = CMP_BLOCK * HEAD_DIM
    pad_w2 = lambda w: jnp.pad(w, ((0, 0), (0, 0), (0, LANES - HEAD_DIM))).astype(BF16)
    pe_rows = lambda pe: jnp.broadcast_to(pe.reshape(L, 1, gw), (L, SUBLANES, gw)).astype(BF16)
    ncp = S // CMP_STRIDE
    cos_t, sin_t = _rope_tables(jnp.arange(S), 4)
    cos_c, sin_c = _rope_tables(jnp.arange(ncp) * CMP_STRIDE + (CMP_BLOCK - 1), 2)
    return dict(
        ffn1=(f1a, f1b, down(ffn1_w_down), row(ln1_g), row(ln1_b)),
        ffn2=(f2a, f2b, down(ffn2_w_down), row(ln3_g), row(ln3_b)),
        w_in=_w_in_layout(w_in),
        w_out=(_parity_major(w_out_b[:, :D_RWKV], 1), w_out_b[:, D_RWKV:D_RWKV + D_POOL],
               w_out_b[:, D_RWKV + D_POOL:]),
        ln2=(row(ln2_g), row(ln2_b)),
        rw=dict(mu=row(rw_mu), w0=row(rw_w0), w2=rw_w2.astype(BF16), a0=row(rw_a0), a2=rw_a2.astype(BF16),
                g2=_parity_major(rw_g2, -1).astype(BF16), k_k=row(rw_k_k), k_a=row(rw_k_a),
                r_k=rw_r_k.reshape(L, 1, D_RWKV), gn_g=row(_parity_major(rw_gn_g, -1)),
                gn_b=row(_parity_major(rw_gn_b, -1)), ones=_block_ones(D_RWKV)),
        cmp=dict(k_w1=nsa_cmp_k_w1.reshape(L, gw, CMP_HIDDEN).astype(BF16), k_w2=pad_w2(nsa_cmp_k_w2),
                 k_pe=pe_rows(nsa_cmp_pe_k),
                 v_w1=nsa_cmp_v_w1.reshape(L, gw, CMP_HIDDEN).astype(BF16), v_w2=nsa_cmp_v_w2.transpose(0, 2, 1).astype(BF16),
                 v_pe=pe_rows(nsa_cmp_pe_v)),
        gate_b=_gate_bias_layout(nsa_gate_b),
        pool=(pool_w.astype(BF16), row(pool_b), row(pool_scale)),
        rope=(cos_t, sin_t), rope_cmp=(cos_c, sin_c), overlap=_overlap_matrix(S, ncp))


def _mixers(p_all, prm, l):
    r, w, k, v, kk, be, bo, g = _rw_prep(p_all, prm["rw"], l)
    y_rw = _rw_scan(r, w, k, v, kk, be, bo, g, prm["rw"], l)
    y_pool = _pool_mix(p_all, *prm["pool"], l)
    q, kc, vc, ks, vs, kw, vw, gates = _nsa_prep(p_all, *prm["rope"], prm["gate_b"], l)
    k_cmp, v_cmp = _nsa_compress(kc, vc, prm["cmp"], *prm["rope_cmp"], l)
    y_nsa = _nsa_attention(q, k_cmp, v_cmp, ks, vs, kw, vw, gates, prm["overlap"])
    return y_rw, y_pool, y_nsa


def _layer(h, prm, l, B, S):
    T = B * S
    h = _ffn_ln(h, *prm["ffn1"], l)
    p_all = _in_proj(h, prm["w_in"], l).reshape(B, S, P_COLS)
    y_rw, y_pool, y_nsa = _mixers(p_all, prm, l)
    h = _out_proj_ln(h, y_rw.reshape(T, D_RWKV), y_pool.reshape(T, D_POOL), y_nsa.reshape(T, D_NSA),
                     prm["w_out"], *prm["ln2"], l)
    return _ffn_ln(h, *prm["ffn2"], l)
```

```python
import functools

import numpy as np
import jax
import jax.numpy as jnp
from jax import lax
from jax.experimental import pallas as pl
from jax.experimental.pallas import tpu as pltpu

F32 = jnp.float32
BF16 = jnp.bfloat16

D_MODEL = 2048
DEPTH = 4
HEAD_DIM = 64
D_RWKV = 768
D_POOL = 512
D_NSA = 768
RW_HEADS = 12
RW_DECAY_LORA = 64
RW_A_LORA = 64
RW_GATE_LORA = 128
RW_GN_EPS = 64e-5
RW_COLS = 3 * D_RWKV + RW_DECAY_LORA + RW_A_LORA + RW_GATE_LORA
POOL_WINDOWS = (2, 4, 8, 16)
POOL_GROUP = 128
NSA_HEADS = 12
NSA_KV_HEADS = 3
NSA_GQA = 4
NSA_KV = 192
NSA_COLS = D_NSA + 6 * NSA_KV + 3 * NSA_HEADS
CMP_BLOCK = 32
CMP_STRIDE = 16
CMP_HIDDEN = 256
SEL_BLOCK = 64
SEL_TOPK = 16
FORCE_SCORE = 1e9
WINDOW = 512
ROPE_THETA = 500000.0
ROPE_DIM = 16
D_FF = 5504
IN_COLS = RW_COLS + D_POOL + NSA_COLS
ALPHA = (2 * DEPTH) ** 0.25
LN_EPS = 1e-5

LANES = 128
SUBLANES = 8
VMEM_LIMIT = 56 * 1024 * 1024

KV_SLOT = 2 * LANES
NSA_PAD = 2560
P_RW_OFF = 0
P_NSA_OFF = RW_COLS
P_POOL_OFF = RW_COLS + NSA_PAD
P_COLS = RW_COLS + NSA_PAD + D_POOL
NSA_GATE_OFF = D_NSA + 6 * KV_SLOT
D_FF_PAD = 5632

NEG = -1e30


def _params(*sem):
    return pltpu.CompilerParams(dimension_semantics=sem, vmem_limit_bytes=VMEM_LIMIT)


def _layer_norm(z, g, b):
    mu = jnp.mean(z, axis=-1, keepdims=True)
    zc = z - mu
    var = jnp.mean(zc * zc, axis=-1, keepdims=True)
    return zc * lax.rsqrt(var + LN_EPS) * g + b


def _dot(a, b):
    return jnp.dot(a, b, preferred_element_type=F32)


def _dot_nt(a, b):
    return lax.dot_general(a, b, (((1,), (1,)), ((), ())), preferred_element_type=F32)


def _dot_split(x, w):
    hi = x.astype(BF16)
    lo = (x - hi.astype(F32)).astype(BF16)
    return _dot(hi, w) + _dot(lo, w)


FFN_LN_ROWS = 256


def _ffn_kernel(x_ref, wa_ref, wb_ref, wd_ref, g_ref, b_ref, o_ref, xb_ref):
    k = pl.program_id(1)

    @pl.when(k == 0)
    def _():
        xb_ref[...] = x_ref[...].astype(BF16)
        o_ref[...] = jnp.zeros_like(o_ref)

    xb = xb_ref[...]
    a = _dot(xb, wa_ref[...])
    b = _dot(xb, wb_ref[...])
    h = (a / (1.0 + jnp.exp(-a))) * b
    o_ref[...] += _dot(h.astype(BF16), wd_ref[...])

    @pl.when(k == pl.num_programs(1) - 1)
    def _():
        for c in range(o_ref.shape[0] // FFN_LN_ROWS):
            rows = slice(c * FFN_LN_ROWS, (c + 1) * FFN_LN_ROWS)
            z = ALPHA * x_ref[rows, :] + 0.5 * o_ref[rows, :]
            o_ref[rows, :] = _layer_norm(z, g_ref[...], b_ref[...])


def _ffn_ln(x, wa, wb, wd, g, b, layer, *, tm=1024, tf=256):
    T, D = x.shape
    fp = wa.shape[-1]
    tm = min(tm, T)
    return pl.pallas_call(
        _ffn_kernel,
        out_shape=jax.ShapeDtypeStruct((T, D), F32),
        grid=(T // tm, fp // tf),
        in_specs=[
            pl.BlockSpec((tm, D), lambda i, k: (i, 0)),
            pl.BlockSpec((None, D, tf), lambda i, k: (layer, 0, k)),
            pl.BlockSpec((None, D, tf), lambda i, k: (layer, 0, k)),
            pl.BlockSpec((None, tf, D), lambda i, k: (layer, k, 0)),
            pl.BlockSpec((None, 1, D), lambda i, k: (layer, 0, 0)),
            pl.BlockSpec((None, 1, D), lambda i, k: (layer, 0, 0)),
        ],
        out_specs=pl.BlockSpec((tm, D), lambda i, k: (i, 0)),
        scratch_shapes=[pltpu.VMEM((tm, D), BF16)],
        compiler_params=_params("parallel", "arbitrary"),
        name="ffn_ln",
    )(x, wa, wb, wd, g, b)


def _inproj_kernel(x_ref, w_ref, o_ref, xb_ref):
    @pl.when(pl.program_id(1) == 0)
    def _():
        xb_ref[...] = x_ref[...].astype(BF16)

    o_ref[...] = _dot(xb_ref[...], w_ref[...])


def _in_proj(x, w, layer, *, tm=1024, tn=512):
    T, D = x.shape
    n = w.shape[-1]
    tm = min(tm, T)
    return pl.pallas_call(
        _inproj_kernel,
        out_shape=jax.ShapeDtypeStruct((T, n), F32),
        grid=(T // tm, n // tn),
        in_specs=[
            pl.BlockSpec((tm, D), lambda i, j: (i, 0)),
            pl.BlockSpec((None, D, tn), lambda i, j: (layer, 0, j)),
        ],
        out_specs=pl.BlockSpec((tm, tn), lambda i, j: (i, j)),
        scratch_shapes=[pltpu.VMEM((tm, D), BF16)],
        compiler_params=_params("parallel", "arbitrary"),
        name="in_proj",
    )(x, w)


def _outproj_kernel(x_ref, yr_ref, yp_ref, yn_ref, wr_ref, wp_ref, wn_ref, g_ref, b_ref, o_ref):
    y = _dot(yr_ref[...], wr_ref[...]) + _dot(yp_ref[...], wp_ref[...]) + _dot(yn_ref[...], wn_ref[...])
    o_ref[...] = _layer_norm(ALPHA * x_ref[...] + y, g_ref[...], b_ref[...])


def _out_proj_ln(x, y_rw, y_pool, y_nsa, w_out, g, b, layer, *, tm=512):
    T, D = x.shape
    return pl.pallas_call(
        _outproj_kernel,
        out_shape=jax.ShapeDtypeStruct((T, D), F32),
        grid=(T // tm,),
        in_specs=[
            pl.BlockSpec((tm, D), lambda i: (i, 0)),
            pl.BlockSpec((tm, D_RWKV), lambda i: (i, 0)),
            pl.BlockSpec((tm, D_POOL), lambda i: (i, 0)),
            pl.BlockSpec((tm, D_NSA), lambda i: (i, 0)),
            pl.BlockSpec((None, D_RWKV, D), lambda i: (layer, 0, 0)),
            pl.BlockSpec((None, D_POOL, D), lambda i: (layer, 0, 0)),
            pl.BlockSpec((None, D_NSA, D), lambda i: (layer, 0, 0)),
            pl.BlockSpec((None, 1, D), lambda i: (layer, 0, 0)),
            pl.BlockSpec((None, 1, D), lambda i: (layer, 0, 0)),
        ],
        out_specs=pl.BlockSpec((tm, D), lambda i: (i, 0)),
        compiler_params=_params("parallel"),
        name="out_proj_ln",
    )(x, y_rw, y_pool, y_nsa, w_out[0], w_out[1], w_out[2], g, b)


POOL_HALO = 16


def _pool_kernel(p_ref, halo_ref, w_ref, b_ref, sc_ref, o_ref, xs_ref, *, ts):
    s = pl.program_id(1)
    x = p_ref[0]
    halo = jnp.where(s > 0, halo_ref[0], 0.0)
    xs_ref[0:POOL_HALO, :] = halo
    xs_ref[POOL_HALO:POOL_HALO + ts, :] = x
    t1 = (s * ts + 1 + lax.broadcasted_iota(jnp.int32, (ts, 1), 0)).astype(F32)
    for gi, win in enumerate(POOL_WINDOWS):
        c0 = gi * POOL_GROUP
        acc = x[:, c0:c0 + POOL_GROUP]
        for j in range(1, win):
            acc = acc + xs_ref[POOL_HALO - j:POOL_HALO - j + ts, c0:c0 + POOL_GROUP]
        pooled = acc / jnp.minimum(t1, float(win)) - x[:, c0:c0 + POOL_GROUP]
        z = _dot(pooled.astype(BF16), w_ref[gi]) + b_ref[:, c0:c0 + POOL_GROUP]
        o_ref[0, :, c0:c0 + POOL_GROUP] = (z * sc_ref[:, c0:c0 + POOL_GROUP]).astype(o_ref.dtype)


def _pool_mix(p_all, pool_w, pool_b, pool_scale, layer, *, ts=512):
    B, S, _ = p_all.shape
    cb = P_POOL_OFF // D_POOL
    hb = ts // POOL_HALO
    return pl.pallas_call(
        functools.partial(_pool_kernel, ts=ts),
        out_shape=jax.ShapeDtypeStruct((B, S, D_POOL), BF16),
        grid=(B, S // ts),
        in_specs=[
            pl.BlockSpec((1, ts, D_POOL), lambda b, s: (b, s, cb)),
            pl.BlockSpec((1, POOL_HALO, D_POOL), lambda b, s: (b, jnp.maximum(s * hb - 1, 0), cb)),
            pl.BlockSpec((None, 4, POOL_GROUP, POOL_GROUP), lambda b, s: (layer, 0, 0, 0)),
            pl.BlockSpec((None, 1, D_POOL), lambda b, s: (layer, 0, 0)),
            pl.BlockSpec((None, 1, D_POOL), lambda b, s: (layer, 0, 0)),
        ],
        out_specs=pl.BlockSpec((1, ts, D_POOL), lambda b, s: (b, s, 0)),
        scratch_shapes=[pltpu.VMEM((ts + POOL_HALO, D_POOL), F32)],
        compiler_params=_params("parallel", "parallel"),
        name="pool_mix",
    )(p_all, p_all, pool_w, pool_b, pool_scale)


def _softplus(z):
    return jnp.maximum(z, 0.0) + jnp.log1p(jnp.exp(-jnp.abs(z)))


def _sigmoid(z):
    return 1.0 / (1.0 + jnp.exp(-z))


def _lanes_to_parity_major(x):
    n_pairs = x.shape[1] // LANES
    low = lax.broadcasted_iota(jnp.int32, (1, LANES), 1) < HEAD_DIM

    def chunk(e, to_low):
        pair, parity = e % n_pairs, e // n_pairs
        src = x[:, pair * LANES:(pair + 1) * LANES]
        return src if (parity == 0) == to_low else pltpu.roll(src, HEAD_DIM, 1)

    return jnp.concatenate([jnp.where(low, chunk(2 * d, True), chunk(2 * d + 1, False))
                            for d in range(n_pairs)], axis=1)


def _rw_prep_kernel(p_ref, prev_ref, mu_ref, w0_ref, w2_ref, a0_ref, a2_ref, g2_ref, kk_ref, ka_ref,
                    rk_ref, ones_ref,
                    r_o, w_o, k_o, v_o, kk_o, be_o, bo_o, g_o, *, ts):
    s = pl.program_id(1)
    x = p_ref[0]
    last = jnp.where(s > 0, prev_ref[0][SUBLANES - 1:SUBLANES, :], 0.0)
    row = lax.broadcasted_iota(jnp.int32, (ts, 1), 0)
    shifted = jnp.where(row == 0, last, pltpu.roll(x, 1, 0))
    xm = x + (shifted - x) * mu_ref[...]
    c = D_RWKV
    r = xm[:, 0:c]
    k = xm[:, c:2 * c]
    v = xm[:, 2 * c:3 * c]
    lora = xm[:, 3 * c:3 * c + LANES]
    wl = lora[:, :RW_DECAY_LORA]
    al = lora[:, RW_DECAY_LORA:]
    gl = xm[:, 3 * c + LANES:]
    w = -_softplus(-(w0_ref[...] + _dot(jnp.tanh(wl).astype(BF16), w2_ref[...]))) - 0.5
    decay = jnp.exp(-jnp.exp(w))
    a = _sigmoid(a0_ref[...] + _dot(al.astype(BF16), a2_ref[...]))
    g = _dot(_sigmoid(gl).astype(BF16), g2_ref[...])
    ones = ones_ref[...]
    kk = k * kk_ref[...]
    nrm = jnp.sqrt(_dot_split(kk * kk, ones))
    kk = kk / jnp.maximum(nrm, 1e-12)
    k_mod = k * (1.0 + (a - 1.0) * ka_ref[...])
    bonus = _dot_split(r * k_mod * rk_ref[...], ones) * v
    r_o[0] = r
    w_o[0] = decay
    k_o[0] = k_mod
    v_o[0] = v
    kk_o[0] = kk
    be_o[0] = kk * a
    bo_o[0] = _lanes_to_parity_major(bonus).astype(bo_o.dtype)
    g_o[0] = g.astype(g_o.dtype)


def _rw_prep(p_all, prm, layer, *, ts=256):
    B, S, _ = p_all.shape
    cb = P_RW_OFF // RW_COLS
    hb = ts // SUBLANES
    vec = lambda n: pl.BlockSpec((None, 1, n), lambda b, s: (layer, 0, 0))
    mat = lambda m, n: pl.BlockSpec((None, m, n), lambda b, s: (layer, 0, 0))
    out = jax.ShapeDtypeStruct((B, S, D_RWKV), F32)
    ospec = pl.BlockSpec((1, ts, D_RWKV), lambda b, s: (b, s, 0))
    return pl.pallas_call(
        functools.partial(_rw_prep_kernel, ts=ts),
        out_shape=[out] * 6 + [jax.ShapeDtypeStruct((B, S, D_RWKV), BF16)] * 2,
        grid=(B, S // ts),
        in_specs=[
            pl.BlockSpec((1, ts, RW_COLS), lambda b, s: (b, s, cb)),
            pl.BlockSpec((1, SUBLANES, RW_COLS), lambda b, s: (b, jnp.maximum(s * hb - 1, 0), cb)),
            vec(RW_COLS), vec(D_RWKV), mat(RW_DECAY_LORA, D_RWKV), vec(D_RWKV), mat(RW_A_LORA, D_RWKV),
            mat(RW_GATE_LORA, D_RWKV), vec(D_RWKV), vec(D_RWKV), vec(D_RWKV),
            pl.BlockSpec((D_RWKV, D_RWKV), lambda b, s: (0, 0)),
        ],
        out_specs=[ospec] * 8,
        compiler_params=_params("parallel", "parallel"),
        name="rw_prep",
    )(p_all, p_all, prm["mu"], prm["w0"], prm["w2"], prm["a0"], prm["a2"], prm["g2"], prm["k_k"],
      prm["k_a"], prm["r_k"], prm["ones"])


RW_PAIRS = D_RWKV // LANES


def _rw_scan_kernel(r_ref, w_ref, k_ref, v_ref, kk_ref, be_ref, bo_ref, g_ref, gng_ref, gnb_ref, ones_ref,
                    pair_ref, half_ref, o_ref, st_ref, vc_ref, sr_ref, y_ref, *, ts, nb):
    s = pl.program_id(1)

    @pl.when(s == 0)
    def _():
        st_ref[...] = jnp.zeros_like(st_ref)

    side, stack = 2, RW_PAIRS // 2
    rows, width = stack * HEAD_DIM, side * LANES
    lane = lax.broadcasted_iota(jnp.int32, (rows, width), 1)
    row = lax.broadcasted_iota(jnp.int32, (rows, width), 0)
    eye = jnp.where(lane % HEAD_DIM == row % HEAD_DIM, 1.0, 0.0)
    ones_blk = pair_ref[...]

    def spread(x8, j):
        blocks = []
        for st in range(stack):
            lanes = jnp.concatenate([x8[j:j + 1, (sd * stack + st) * LANES:(sd * stack + st + 1) * LANES]
                                     for sd in range(side)], axis=1)
            blocks.append(jnp.broadcast_to(lanes, (HEAD_DIM, width)))
        return jnp.concatenate(blocks, axis=0)

    def step(i, carry):
        base = pl.multiple_of(i * SUBLANES, SUBLANES)
        tiles = [[ref[b, pl.ds(base, SUBLANES), :] for ref in (kk_ref, w_ref, be_ref, k_ref, r_ref, v_ref)]
                 for b in range(nb)]
        for b in range(nb):
            lhs = jnp.concatenate([eye * spread(tiles[b][5], j) for j in range(SUBLANES)], axis=0)
            vc_ref[b] = _dot(lhs.astype(BF16), ones_blk)
        sts = [st_ref[b] for b in range(nb)]
        for j in range(SUBLANES):
            for b in range(nb):
                kk8, w8, be8, k8, r8, _ = tiles[b]
                st = sts[b]
                sa = _dot((st * spread(kk8, j)).astype(BF16), ones_blk)
                st = (st * spread(w8, j) - sa * spread(be8, j)
                      + vc_ref[b, j * rows:(j + 1) * rows, :] * spread(k8, j))
                sr = (st * spread(r8, j)).astype(BF16)
                for sd in range(side):
                    lo = (j * side + sd) * rows
                    sr_ref[b, lo:lo + rows, :] = sr[:, sd * LANES:(sd + 1) * LANES]
                sts[b] = st
        for b in range(nb):
            st_ref[b] = sts[b]
            yt = _dot_nt(half_ref[...], sr_ref[b])
            per = side * rows
            y_ref[b, pl.ds(base, SUBLANES), :] = jnp.concatenate(
                [jnp.concatenate([yt[0:1, j * per:(j + 1) * per], yt[1:2, j * per:(j + 1) * per]], axis=1)
                 for j in range(SUBLANES)], axis=0)
        return carry

    lax.fori_loop(0, ts // SUBLANES, step, 0)

    ones = ones_ref[...]
    for b in range(nb):
        y = y_ref[b]
        mean = _dot_split(y, ones) * (1.0 / HEAD_DIM)
        yc = y - mean
        var = _dot_split(yc * yc, ones) * (1.0 / HEAD_DIM)
        yn = yc * lax.rsqrt(var + RW_GN_EPS) * gng_ref[...] + gnb_ref[...]
        o_ref[b] = ((yn + bo_ref[b]) * g_ref[b]).astype(o_ref.dtype)


def _rw_scan(r, w, k, v, kk, be, bo, g, prm, layer, *, ts=128):
    B, S, _ = r.shape
    nb = 4 if B % 4 == 0 else (2 if B % 2 == 0 else 1)
    rows, width = RW_PAIRS // 2 * HEAD_DIM, 2 * LANES
    seq = pl.BlockSpec((nb, ts, D_RWKV), lambda b, s: (b, s, 0))
    vec = pl.BlockSpec((None, 1, D_RWKV), lambda b, s: (layer, 0, 0))
    return pl.pallas_call(
        functools.partial(_rw_scan_kernel, ts=ts, nb=nb),
        out_shape=jax.ShapeDtypeStruct((B, S, D_RWKV), BF16),
        grid=(B // nb, S // ts),
        in_specs=[seq] * 8 + [vec, vec, pl.BlockSpec((D_RWKV, D_RWKV), lambda b, s: (0, 0)),
                              pl.BlockSpec((width, width), lambda b, s: (0, 0)),
                              pl.BlockSpec((SUBLANES, LANES), lambda b, s: (0, 0))],
        out_specs=seq,
        scratch_shapes=[pltpu.VMEM((nb, rows, width), F32), pltpu.VMEM((nb, SUBLANES * rows, width), F32),
                        pltpu.VMEM((nb, SUBLANES * RW_PAIRS * HEAD_DIM, LANES), BF16),
                        pltpu.VMEM((nb, ts, D_RWKV), F32)],
        compiler_params=_params("parallel", "arbitrary"),
        name="rw_scan",
    )(r, w, k, v, kk, be, bo, g, prm["gn_g"], prm["gn_b"], prm["ones"], prm["ones"][:width, :width],
      prm["ones"][:SUBLANES * HEAD_DIM:HEAD_DIM, :LANES])


def _rope(x, cos, sin):
    half = ROPE_DIM // 2
    w = x.shape[-1]
    d = lax.broadcasted_iota(jnp.int32, x.shape, 1) % HEAD_DIM
    partner = jnp.where(d < half, pltpu.roll(x, w - half, 1), pltpu.roll(x, half, 1))
    return x * cos + partner * sin


def _nsa_prep_kernel(p_ref, cos_ref, sin_ref, gb_ref, q_o, kc_o, vc_o, ks_o, vs_o, kw_o, vw_o, gt_o):
    cos = cos_ref[...]
    sin = sin_ref[...]
    for hq in range(NSA_HEADS // 4):
        x = p_ref[0, :, hq * KV_SLOT:(hq + 1) * KV_SLOT]
        qr = _rope(x, cos, sin) * (HEAD_DIM ** -0.5)
        for j in range(4):
            q_o[0, 4 * hq + j] = qr[:, j * HEAD_DIM:(j + 1) * HEAD_DIM].astype(BF16)
    for i, ref in enumerate((kc_o, vc_o, ks_o, vs_o, kw_o, vw_o)):
        part = p_ref[0, :, D_NSA + i * KV_SLOT:D_NSA + (i + 1) * KV_SLOT]
        if i in (2, 4):
            part = _rope(part, cos, sin)
        if i in (3, 5):
            part = part.T
            for h in range(NSA_KV_HEADS):
                ref[0, h] = part[h * HEAD_DIM:(h + 1) * HEAD_DIM, :].astype(BF16)
        else:
            for h in range(NSA_KV_HEADS):
                ref[0, h] = part[:, h * HEAD_DIM:(h + 1) * HEAD_DIM].astype(BF16)
    gts = _sigmoid(p_ref[0, :, NSA_GATE_OFF:NSA_GATE_OFF + LANES] + gb_ref[...]).T
    for h in range(NSA_KV_HEADS):
        gt_o[0, h] = gts[h * 12:(h + 1) * 12, :]


def _nsa_prep(p_all, cos, sin, gate_b, layer, *, ts=256):
    B, S, _ = p_all.shape
    H = NSA_KV_HEADS
    k_shape = jax.ShapeDtypeStruct((B, H, S, HEAD_DIM), BF16)
    k_spec = pl.BlockSpec((1, H, ts, HEAD_DIM), lambda b, s: (b, 0, s, 0))
    vt_shape = jax.ShapeDtypeStruct((B, H, HEAD_DIM, S), BF16)
    vt_spec = pl.BlockSpec((1, H, HEAD_DIM, ts), lambda b, s: (b, 0, 0, s))
    return pl.pallas_call(
        _nsa_prep_kernel,
        out_shape=[jax.ShapeDtypeStruct((B, NSA_HEADS, S, HEAD_DIM), BF16),
                   k_shape, k_shape, k_shape, vt_shape, k_shape, vt_shape,
                   jax.ShapeDtypeStruct((B, H, 12, S), F32)],
        grid=(B, S // ts),
        in_specs=[
            pl.BlockSpec((1, ts, NSA_PAD), lambda b, s: (b, s, P_NSA_OFF // NSA_PAD)),
            pl.BlockSpec((ts, KV_SLOT), lambda b, s: (s, 0)),
            pl.BlockSpec((ts, KV_SLOT), lambda b, s: (s, 0)),
            pl.BlockSpec((None, 1, LANES), lambda b, s: (layer, 0, 0)),
        ],
        out_specs=[pl.BlockSpec((1, NSA_HEADS, ts, HEAD_DIM), lambda b, s: (b, 0, s, 0)),
                   k_spec, k_spec, k_spec, vt_spec, k_spec, vt_spec,
                   pl.BlockSpec((1, H, 12, ts), lambda b, s: (b, 0, 0, s))],
        compiler_params=_params("parallel", "parallel"),
        name="nsa_prep",
    )(p_all, cos, sin, gate_b)


def _gelu_tanh(x):
    return 0.5 * x * (1.0 + jnp.tanh(float(np.sqrt(2.0 / np.pi)) * (x + 0.044715 * (x * x * x))))


def _nsa_cmp_kernel(kc_ref, vc_ref, kw1_ref, kw2_ref, kpe_ref, vw1_ref, vw2_ref, vpe_ref, cos_ref, sin_ref,
                    k_o, v_o):
    def hidden(g, w1_ref, pe_ref):
        half = CMP_STRIDE * HEAD_DIM
        first = _dot(g, w1_ref[:half, :])
        second = _dot(g, w1_ref[half:, :])
        n = first.shape[0]
        bias = _dot(pe_ref[...], w1_ref[...])[0:1]
        return _gelu_tanh(first + pltpu.roll(second, n - 1, 0) + bias).astype(BF16)

    k = _dot(hidden(kc_ref[0, 0], kw1_ref, kpe_ref), kw2_ref[...])
    k_o[0, 0] = _rope(k, cos_ref[...], sin_ref[...])[:, :HEAD_DIM].astype(BF16)
    v_o[0, 0] = _dot_nt(vw2_ref[...], hidden(vc_ref[0, 0], vw1_ref, vpe_ref)).astype(BF16)


def _nsa_compress(kc, vc, prm, cos_c, sin_c, layer):
    B, H, S, _ = kc.shape
    ng = S // CMP_STRIDE
    gw = CMP_STRIDE * HEAD_DIM
    g_k = kc.reshape(B, H, ng, gw)
    g_v = vc.reshape(B, H, ng, gw)
    gspec = pl.BlockSpec((1, 1, ng, gw), lambda b, h: (b, h, 0, 0))
    w1 = pl.BlockSpec((None, 2 * gw, CMP_HIDDEN), lambda b, h: (layer, 0, 0))
    w2 = pl.BlockSpec((None, CMP_HIDDEN, LANES), lambda b, h: (layer, 0, 0))
    pe = pl.BlockSpec((None, SUBLANES, 2 * gw), lambda b, h: (layer, 0, 0))
    tab = pl.BlockSpec((ng, LANES), lambda b, h: (0, 0))
    return pl.pallas_call(
        _nsa_cmp_kernel,
        out_shape=[jax.ShapeDtypeStruct((B, H, ng, HEAD_DIM), BF16), jax.ShapeDtypeStruct((B, H, HEAD_DIM, ng), BF16)],
        grid=(B, H),
        in_specs=[gspec, gspec, w1, w2, pe, w1,
                  pl.BlockSpec((None, HEAD_DIM, CMP_HIDDEN), lambda b, h: (layer, 0, 0)), pe, tab, tab],
        out_specs=[pl.BlockSpec((1, 1, ng, HEAD_DIM), lambda b, h: (b, h, 0, 0)),
                   pl.BlockSpec((1, 1, HEAD_DIM, ng), lambda b, h: (b, h, 0, 0))],
        compiler_params=_params("parallel", "parallel"),
        name="nsa_compress",
    )(g_k, g_v, prm["k_w1"], prm["k_w2"], prm["k_pe"], prm["v_w1"], prm["v_w2"], prm["v_pe"], cos_c, sin_c)


def _nsa_attn_kernel(q_ref, kcmp_ref, vcmp_ref, ks_ref, vs_ref, kw_ref, vw_ref, gt_ref, ov_ref, o_ref,
                     m_ref, l_ref, acc_ref, sb_ref, *, tq):
    G = NSA_GQA
    tk = tq
    qi = pl.program_id(2)
    q0 = qi * tq
    q_all = q_ref[0].reshape(G * tq, HEAD_DIM)
    pos = q0 + lax.broadcasted_iota(jnp.int32, (1, tq), 1)
    tiny = jnp.finfo(F32).tiny
    heads = lambda x: jnp.tile(x, (1, G))
    split = lambda x: [x[:, g * tq:(g + 1) * tq] for g in range(G)]

    ncp = kcmp_ref.shape[2]
    n_idx = lax.broadcasted_iota(jnp.int32, (ncp, 1), 0)
    cbias = jnp.where((n_idx * CMP_STRIDE + (CMP_BLOCK - 1)) <= pos, 0.0, NEG)
    any_cmp = jnp.where(pos >= CMP_BLOCK - 1, 1.0, 0.0)
    s = _dot_nt(kcmp_ref[0, 0], q_all) + heads(cbias)
    e = jnp.exp(s - jnp.max(s, axis=0, keepdims=True))
    p = e * (heads(any_cmp) / jnp.maximum(jnp.sum(e, axis=0, keepdims=True), tiny))
    o_cmp = split(_dot(vcmp_ref[0, 0], p.astype(BF16)))
    ps = split(p)
    psum = (ps[0] + ps[1]) + (ps[2] + ps[3])
    p_hi = psum.astype(BF16)
    p_lo = (psum - p_hi.astype(F32)).astype(BF16)
    imp = _dot(ov_ref[...], p_hi) + _dot(ov_ref[...], p_lo)

    wk = min(WINDOW + tq, ks_ref.shape[2])
    w0 = pl.multiple_of(jnp.maximum(q0 + tq - wk, 0), tk)
    wpos = w0 + lax.broadcasted_iota(jnp.int32, (wk, 1), 0)
    wbias = jnp.where((wpos <= pos) & (wpos > pos - WINDOW), 0.0, NEG)
    s = _dot_nt(kw_ref[0, 0, pl.ds(w0, wk), :], q_all) + heads(wbias)
    e = jnp.exp(s - jnp.max(s, axis=0, keepdims=True))
    pv = _dot(vw_ref[0, 0, :, pl.ds(w0, wk)], e.astype(BF16))
    o_win = split(pv / jnp.maximum(jnp.sum(e, axis=0, keepdims=True), tiny))

    n_sel = ov_ref.shape[0]
    blk = lax.broadcasted_iota(jnp.int32, (n_sel, 1), 0)
    cur = pos // SEL_BLOCK
    forced = (blk == 0) | (blk == cur) | (blk == cur - 1)
    valid = blk <= cur
    score = jnp.where(valid, jnp.where(forced, FORCE_SCORE, imp), -jnp.inf)
    groups = [score[g * SUBLANES:(g + 1) * SUBLANES, :] for g in range(n_sel // SUBLANES)]
    ranks = [jnp.zeros((SUBLANES, tq), F32) for _ in groups]
    sub = lax.broadcasted_iota(jnp.int32, (SUBLANES, 1), 0)
    for i in range(n_sel):
        ci = jnp.broadcast_to(score[i:i + 1, :], (SUBLANES, tq))
        for g, sg in enumerate(groups):
            if g > i // SUBLANES:
                beats = ci >= sg
            elif g < i // SUBLANES:
                beats = ci > sg
            else:
                beats = (ci > sg) | ((ci == sg) & (sub > i % SUBLANES))
            ranks[g] = ranks[g] + jnp.where(beats, 1.0, 0.0)
    rank = jnp.concatenate(ranks, axis=0)
    chosen = (rank < float(SEL_TOPK)) & valid
    sel_bias = jnp.where(chosen, 0.0, NEG)
    for j in range(n_sel):
        sb_ref[j] = jnp.broadcast_to(sel_bias[j:j + 1, :], (SUBLANES, tq))

    kcol = lax.broadcasted_iota(jnp.int32, (tk, 1), 0)

    def scores(kt):
        k0 = pl.multiple_of(kt * tk, tk)
        return _dot_nt(ks_ref[0, 0, pl.ds(k0, tk), :], q_all)

    def sel_update(kt, s, extra=None):
        bpt = tk // SEL_BLOCK
        bias = jnp.concatenate([jnp.tile(sb_ref[kt * bpt + j], (SEL_BLOCK // SUBLANES, 1)) for j in range(bpt)],
                               axis=0)
        if extra is not None:
            bias = bias + extra
        k0 = pl.multiple_of(kt * tk, tk)
        vt = vs_ref[0, 0, :, pl.ds(k0, tk)]
        s = s + heads(bias)
        m_prev = m_ref[...]
        m_new = jnp.maximum(m_prev, jnp.max(s, axis=0, keepdims=True))
        p = jnp.exp(s - m_new)
        corr = jnp.exp(m_prev - m_new)
        l_ref[...] = corr * l_ref[...] + jnp.sum(p, axis=0, keepdims=True)
        acc_ref[...] = corr * acc_ref[...] + _dot(vt, p.astype(BF16))
        m_ref[...] = m_new

    m_ref[...] = jnp.full_like(m_ref, NEG)
    l_ref[...] = jnp.zeros_like(l_ref)
    acc_ref[...] = jnp.zeros_like(acc_ref)

    def sel_body(i, carry):
        s_a, s_b = scores(2 * i), scores(2 * i + 1)
        sel_update(2 * i, s_a)
        sel_update(2 * i + 1, s_b)
        return carry

    lax.fori_loop(0, qi // 2, sel_body, 0)

    @pl.when(qi % 2 == 1)
    def _():
        sel_update(qi - 1, scores(qi - 1))

    causal = jnp.where(q0 + kcol <= pos, 0.0, NEG)
    sel_update(qi, scores(qi), causal)
    o_sel = split(acc_ref[...] / jnp.maximum(l_ref[...], tiny))

    gt = gt_ref[0, 0]
    outs = [gt[g:g + 1] * o_cmp[g] + gt[G + g:G + g + 1] * o_sel[g] + gt[2 * G + g:2 * G + g + 1] * o_win[g]
            for g in range(G)]
    for half in range(G // 2):
        pair = jnp.concatenate(outs[2 * half:2 * half + 2], axis=0)
        o_ref[0, :, half * LANES:(half + 1) * LANES] = pair.T.astype(o_ref.dtype)


def _nsa_attention(q, k_cmp, v_cmp_t, ks, vs_t, kw, vw_t, gates_t, overlap_t, *, tq=256):
    B, _, S, _ = q.shape
    H, G = NSA_KV_HEADS, NSA_GQA
    tq = min(tq, S)
    assert WINDOW == 2 * tq or S <= tq
    ncp = k_cmp.shape[2]
    n_cmp = (S - CMP_BLOCK) // CMP_STRIDE + 1
    keys = pl.BlockSpec((1, 1, S, HEAD_DIM), lambda b, h, i: (b, h, 0, 0))
    vals = pl.BlockSpec((1, 1, HEAD_DIM, S), lambda b, h, i: (b, h, 0, 0))
    return pl.pallas_call(
        functools.partial(_nsa_attn_kernel, tq=tq),
        out_shape=jax.ShapeDtypeStruct((B, S, D_NSA), BF16),
        grid=(B, H, S // tq),
        in_specs=[
            pl.BlockSpec((1, G, tq, HEAD_DIM), lambda b, h, i: (b, h, i, 0)),
            pl.BlockSpec((1, 1, ncp, HEAD_DIM), lambda b, h, i: (b, h, 0, 0)),
            pl.BlockSpec((1, 1, HEAD_DIM, ncp), lambda b, h, i: (b, h, 0, 0)),
            keys, vals, keys, vals,
            pl.BlockSpec((1, 1, 12, tq), lambda b, h, i: (b, h, 0, i)),
            pl.BlockSpec(overlap_t.shape, lambda b, h, i: (0, 0)),
        ],
        out_specs=pl.BlockSpec((1, tq, G * HEAD_DIM), lambda b, h, i: (b, i, h)),
        scratch_shapes=[pltpu.VMEM((1, G * tq), F32), pltpu.VMEM((1, G * tq), F32),
                        pltpu.VMEM((HEAD_DIM, G * tq), F32), pltpu.VMEM((S // SEL_BLOCK, SUBLANES, tq), F32)],
        compiler_params=_params("parallel", "parallel", "arbitrary"),
        name="nsa_attention",
    )(q, k_cmp, v_cmp_t, ks, vs_t, kw, vw_t, gates_t, overlap_t)


def _rope_tables(pos, heads):
    half = ROPE_DIM // 2
    inv_freq = ROPE_THETA ** (-jnp.arange(half, dtype=F32) / half)
    ang = pos.astype(F32)[:, None] * inv_freq
    cos, sin = jnp.cos(ang), jnp.sin(ang)
    n = pos.shape[0]
    rest = HEAD_DIM - ROPE_DIM
    cos_h = jnp.concatenate([cos, cos, jnp.ones((n, rest), F32)], axis=1)
    sin_h = jnp.concatenate([-sin, sin, jnp.zeros((n, rest), F32)], axis=1)
    return jnp.tile(cos_h, (1, heads)), jnp.tile(sin_h, (1, heads))


def _overlap_matrix(S, ncp):
    n_cmp = (S - CMP_BLOCK) // CMP_STRIDE + 1
    n_sel = S // SEL_BLOCK
    cmp_start = np.arange(n_cmp) * CMP_STRIDE
    sel_start = np.arange(n_sel) * SEL_BLOCK
    ov = np.clip(np.minimum(cmp_start[:, None] + CMP_BLOCK, sel_start[None, :] + SEL_BLOCK)
                 - np.maximum(cmp_start[:, None], sel_start[None, :]), 0, None) / CMP_BLOCK
    full = np.zeros((n_sel, ncp), np.float32)
    full[:, :n_cmp] = ov.T
    return jnp.asarray(full, BF16)


RW_HEAD_ORDER = tuple(2 * (e % RW_PAIRS) + e // RW_PAIRS for e in range(RW_HEADS))


def _parity_major(a, axis):
    axis = axis % a.ndim
    shape = a.shape
    a = a.reshape(shape[:axis] + (RW_HEADS, HEAD_DIM) + shape[axis + 1:])
    a = jnp.take(a, jnp.asarray(RW_HEAD_ORDER), axis=axis)
    return a.reshape(shape)


def _block_ones(n):
    idx = np.arange(n) // HEAD_DIM
    return jnp.asarray((idx[:, None] == idx[None, :]).astype(np.float32), BF16)


def _w_in_layout(w_in):
    L = w_in.shape[0]
    w_in = w_in.astype(BF16)
    rw = w_in[:, :, :RW_COLS]
    pool = w_in[:, :, RW_COLS:RW_COLS + D_POOL]
    nsa = w_in[:, :, RW_COLS + D_POOL:]
    zeros = lambda n: jnp.zeros((L, D_MODEL, n), w_in.dtype)
    segs = [nsa[:, :, :D_NSA]]
    for i in range(6):
        segs += [nsa[:, :, D_NSA + i * NSA_KV:D_NSA + (i + 1) * NSA_KV], zeros(KV_SLOT - NSA_KV)]
    gates = nsa[:, :, D_NSA + 6 * NSA_KV:]
    gates = gates.reshape(L, D_MODEL, NSA_KV_HEADS, NSA_GQA, 3).transpose(0, 1, 2, 4, 3)
    segs += [gates.reshape(L, D_MODEL, 3 * NSA_HEADS),
             zeros(NSA_PAD - NSA_GATE_OFF - 3 * NSA_HEADS)]
    return jnp.concatenate([rw] + segs + [pool], axis=-1)


def _gate_bias_layout(gate_b):
    L = gate_b.shape[0]
    gb = gate_b.reshape(L, NSA_KV_HEADS, NSA_GQA, 3).transpose(0, 1, 3, 2).reshape(L, 1, 3 * NSA_HEADS)
    return jnp.pad(gb, ((0, 0), (0, 0), (0, LANES - 3 * NSA_HEADS)))


def kernel(x, ffn1_w_up, ffn1_w_down, ln1_g, ln1_b, w_in, rw_mu, rw_w0, rw_w2, rw_a0, rw_a2, rw_g2, rw_k_k,
           rw_k_a, rw_r_k, rw_gn_g, rw_gn_b, pool_w, pool_b, pool_scale, nsa_cmp_pe_k, nsa_cmp_pe_v,
           nsa_cmp_k_w1, nsa_cmp_k_w2, nsa_cmp_v_w1, nsa_cmp_v_w2, nsa_gate_b, w_out, ln2_g, ln2_b,
           ffn2_w_up, ffn2_w_down, ln3_g, ln3_b):
    prm = _prepare(x.shape[1], ffn1_w_up, ffn1_w_down, ln1_g, ln1_b, w_in, rw_mu, rw_w0, rw_w2, rw_a0, rw_a2,
                   rw_g2, rw_k_k, rw_k_a, rw_r_k, rw_gn_g, rw_gn_b, pool_w, pool_b, pool_scale, nsa_cmp_pe_k,
                   nsa_cmp_pe_v, nsa_cmp_k_w1, nsa_cmp_k_w2, nsa_cmp_v_w1, nsa_cmp_v_w2, nsa_gate_b, w_out,
                   ln2_g, ln2_b, ffn2_w_up, ffn2_w_down, ln3_g, ln3_b)
    B, S, D = x.shape
    h = x.reshape(B * S, D)
    for l in range(w_in.shape[0]):
        h = _layer(h, prm, l, B, S)
    return h.reshape(B, S, D)


def _prepare(S, ffn1_w_up, ffn1_w_down, ln1_g, ln1_b, w_in, rw_mu, rw_w0, rw_w2, rw_a0, rw_a2, rw_g2, rw_k_k,
             rw_k_a, rw_r_k, rw_gn_g, rw_gn_b, pool_w, pool_b, pool_scale, nsa_cmp_pe_k, nsa_cmp_pe_v,
             nsa_cmp_k_w1, nsa_cmp_k_w2, nsa_cmp_v_w1, nsa_cmp_v_w2, nsa_gate_b, w_out, ln2_g, ln2_b,
             ffn2_w_up, ffn2_w_down, ln3_g, ln3_b):
    L = w_in.shape[0]
    fpad = D_FF_PAD - D_FF

    def up(w):
        w = w.astype(BF16)
        a = jnp.pad(w[:, :, :D_FF], ((0, 0), (0, 0), (0, fpad)))
        b = jnp.pad(w[:, :, D_FF:], ((0, 0), (0, 0), (0, fpad)))
        return a, b

    def down(w):
        return jnp.pad(w.astype(BF16), ((0, 0), (0, fpad), (0, 0)))

    row = lambda v: v[:, None, :]
    f1a, f1b = up(ffn1_w_up)
    f2a, f2b = up(ffn2_w_up)
    w_out_b = w_out.astype(BF16)
    gw = CMP_BLOCK * HEAD_DIM
    pad_w2 = lambda w: jnp.pad(w, ((0, 0), (0, 0), (0, LANES - HEAD_DIM))).astype(BF16)
    pe_rows = lambda pe: jnp.broadcast_to(pe.reshape(L, 1, gw), (L, SUBLANES, gw)).astype(BF16)
    ncp = S // CMP_STRIDE
    cos_t, sin_t = _rope_tables(jnp.arange(S), 4)
    cos_c, sin_c = _rope_tables(jnp.arange(ncp) * CMP_STRIDE + (CMP_BLOCK - 1), 2)
    return dict(
        ffn1=(f1a, f1b, down(ffn1_w_down), row(ln1_g), row(ln1_b)),
        ffn2=(f2a, f2b, down(ffn2_w_down), row(ln3_g), row(ln3_b)),
        w_in=_w_in_layout(w_in),
        w_out=(_parity_major(w_out_b[:, :D_RWKV], 1), w_out_b[:, D_RWKV:D_RWKV + D_POOL],
               w_out_b[:, D_RWKV + D_POOL:]),
        ln2=(row(ln2_g), row(ln2_b)),
        rw=dict(mu=row(rw_mu), w0=row(rw_w0), w2=rw_w2.astype(BF16), a0=row(rw_a0), a2=rw_a2.astype(BF16),
                g2=_parity_major(rw_g2, -1).astype(BF16), k_k=row(rw_k_k), k_a=row(rw_k_a),
                r_k=rw_r_k.reshape(L, 1, D_RWKV), gn_g=row(_parity_major(rw_gn_g, -1)),
                gn_b=row(_parity_major(rw_gn_b, -1)), ones=_block_ones(D_RWKV)),
        cmp=dict(k_w1=nsa_cmp_k_w1.reshape(L, gw, CMP_HIDDEN).astype(BF16), k_w2=pad_w2(nsa_cmp_k_w2),
                 k_pe=pe_rows(nsa_cmp_pe_k),
                 v_w1=nsa_cmp_v_w1.reshape(L, gw, CMP_HIDDEN).astype(BF16), v_w2=nsa_cmp_v_w2.transpose(0, 2, 1).astype(BF16),
                 v_pe=pe_rows(nsa_cmp_pe_v)),
        gate_b=_gate_bias_layout(nsa_gate_b),
        pool=(pool_w.astype(BF16), row(pool_b), row(pool_scale)),
        rope=(cos_t, sin_t), rope_cmp=(cos_c, sin_c), overlap=_overlap_matrix(S, ncp))


def _mixers(p_all, prm, l):
    r, w, k, v, kk, be, bo, g = _rw_prep(p_all, prm["rw"], l)
    y_rw = _rw_scan(r, w, k, v, kk, be, bo, g, prm["rw"], l)
    y_pool = _pool_mix(p_all, *prm["pool"], l)
    q, kc, vc, ks, vs, kw, vw, gates = _nsa_prep(p_all, *prm["rope"], prm["gate_b"], l)
    k_cmp, v_cmp = _nsa_compress(kc, vc, prm["cmp"], *prm["rope_cmp"], l)
    y_nsa = _nsa_attention(q, k_cmp, v_cmp, ks, vs, kw, vw, gates, prm["overlap"])
    return y_rw, y_pool, y_nsa


def _layer(h, prm, l, B, S):
    T = B * S
    h = _ffn_ln(h, *prm["ffn1"], l)
    p_all = _in_proj(h, prm["w_in"], l).reshape(B, S, P_COLS)
    y_rw, y_pool, y_nsa = _mixers(p_all, prm, l)
    h = _out_proj_ln(h, y_rw.reshape(T, D_RWKV), y_pool.reshape(T, D_POOL), y_nsa.reshape(T, D_NSA),
                     prm["w_out"], *prm["ln2"], l)
    return _ffn_ln(h, *prm["ffn2"], l)
```

```python
import functools

import numpy as np
import jax
import jax.numpy as jnp
from jax import lax
from jax.experimental import pallas as pl
from jax.experimental.pallas import tpu as pltpu

F32 = jnp.float32
BF16 = jnp.bfloat16

D_MODEL = 2048
DEPTH = 4
HEAD_DIM = 64
D_RWKV = 768
D_POOL = 512
D_NSA = 768
RW_HEADS = 12
RW_DECAY_LORA = 64
RW_A_LORA = 64
RW_GATE_LORA = 128
RW_GN_EPS = 64e-5
RW_COLS = 3 * D_RWKV + RW_DECAY_LORA + RW_A_LORA + RW_GATE_LORA
POOL_WINDOWS = (2, 4, 8, 16)
POOL_GROUP = 128
NSA_HEADS = 12
NSA_KV_HEADS = 3
NSA_GQA = 4
NSA_KV = 192
NSA_COLS = D_NSA + 6 * NSA_KV + 3 * NSA_HEADS
CMP_BLOCK = 32
CMP_STRIDE = 16
CMP_HIDDEN = 256
SEL_BLOCK = 64
SEL_TOPK = 16
FORCE_SCORE = 1e9
WINDOW = 512
ROPE_THETA = 500000.0
ROPE_DIM = 16
D_FF = 5504
IN_COLS = RW_COLS + D_POOL + NSA_COLS
ALPHA = (2 * DEPTH) ** 0.25
LN_EPS = 1e-5

LANES = 128
SUBLANES = 8
VMEM_LIMIT = 56 * 1024 * 1024

KV_SLOT = 2 * LANES
NSA_PAD = 2560
P_RW_OFF = 0
P_NSA_OFF = RW_COLS
P_POOL_OFF = RW_COLS + NSA_PAD
P_COLS = RW_COLS + NSA_PAD + D_POOL
NSA_GATE_OFF = D_NSA + 6 * KV_SLOT
D_FF_PAD = 5632

NEG = -1e30


def _params(*sem):
    return pltpu.CompilerParams(dimension_semantics=sem, vmem_limit_bytes=VMEM_LIMIT)


def _layer_norm(z, g, b):
    mu = jnp.mean(z, axis=-1, keepdims=True)
    zc = z - mu
    var = jnp.mean(zc * zc, axis=-1, keepdims=True)
    return zc * lax.rsqrt(var + LN_EPS) * g + b


def _dot(a, b):
    return jnp.dot(a, b, preferred_element_type=F32)


def _dot_nt(a, b):
    return lax.dot_general(a, b, (((1,), (1,)), ((), ())), preferred_element_type=F32)


def _dot_split(x, w):
    hi = x.astype(BF16)
    lo = (x - hi.astype(F32)).astype(BF16)
    return _dot(hi, w) + _dot(lo, w)


FFN_LN_ROWS = 256


def _ffn_kernel(x_ref, wa_ref, wb_ref, wd_ref, g_ref, b_ref, o_ref, xb_ref):
    k = pl.program_id(1)

    @pl.when(k == 0)
    def _():
        xb_ref[...] = x_ref[...].astype(BF16)
        o_ref[...] = jnp.zeros_like(o_ref)

    xb = xb_ref[...]
    a = _dot(xb, wa_ref[...])
    b = _dot(xb, wb_ref[...])
    h = (a / (1.0 + jnp.exp(-a))) * b
    o_ref[...] += _dot(h.astype(BF16), wd_ref[...])

    @pl.when(k == pl.num_programs(1) - 1)
    def _():
        for c in range(o_ref.shape[0] // FFN_LN_ROWS):
            rows = slice(c * FFN_LN_ROWS, (c + 1) * FFN_LN_ROWS)
            z = ALPHA * x_ref[rows, :] + 0.5 * o_ref[rows, :]
            o_ref[rows, :] = _layer_norm(z, g_ref[...], b_ref[...])


def _ffn_ln(x, wa, wb, wd, g, b, layer, *, tm=1024, tf=256):
    T, D = x.shape
    fp = wa.shape[-1]
    tm = min(tm, T)
    return pl.pallas_call(
        _ffn_kernel,
        out_shape=jax.ShapeDtypeStruct((T, D), F32),
        grid=(T // tm, fp // tf),
        in_specs=[
            pl.BlockSpec((tm, D), lambda i, k: (i, 0)),
            pl.BlockSpec((None, D, tf), lambda i, k: (layer, 0, k)),
            pl.BlockSpec((None, D, tf), lambda i, k: (layer, 0, k)),
            pl.BlockSpec((None, tf, D), lambda i, k: (layer, k, 0)),
            pl.BlockSpec((None, 1, D), lambda i, k: (layer, 0, 0)),
            pl.BlockSpec((None, 1, D), lambda i, k: (layer, 0, 0)),
        ],
        out_specs=pl.BlockSpec((tm, D), lambda i, k: (i, 0)),
        scratch_shapes=[pltpu.VMEM((tm, D), BF16)],
        compiler_params=_params("parallel", "arbitrary"),
        name="ffn_ln",
    )(x, wa, wb, wd, g, b)


def _inproj_kernel(x_ref, w_ref, o_ref, xb_ref):
    @pl.when(pl.program_id(1) == 0)
    def _():
        xb_ref[...] = x_ref[...].astype(BF16)

    o_ref[...] = _dot(xb_ref[...], w_ref[...])


def _in_proj(x, w, layer, *, tm=1024, tn=512):
    T, D = x.shape
    n = w.shape[-1]
    tm = min(tm, T)
    return pl.pallas_call(
        _inproj_kernel,
        out_shape=jax.ShapeDtypeStruct((T, n), F32),
        grid=(T // tm, n // tn),
        in_specs=[
            pl.BlockSpec((tm, D), lambda i, j: (i, 0)),
            pl.BlockSpec((None, D, tn), lambda i, j: (layer, 0, j)),
        ],
        out_specs=pl.BlockSpec((tm, tn), lambda i, j: (i, j)),
        scratch_shapes=[pltpu.VMEM((tm, D), BF16)],
        compiler_params=_params("parallel", "arbitrary"),
        name="in_proj",
    )(x, w)


def _outproj_kernel(x_ref, yr_ref, yp_ref, yn_ref, wr_ref, wp_ref, wn_ref, g_ref, b_ref, o_ref):
    y = _dot(yr_ref[...], wr_ref[...]) + _dot(yp_ref[...], wp_ref[...]) + _dot(yn_ref[...], wn_ref[...])
    o_ref[...] = _layer_norm(ALPHA * x_ref[...] + y, g_ref[...], b_ref[...])


def _out_proj_ln(x, y_rw, y_pool, y_nsa, w_out, g, b, layer, *, tm=512):
    T, D = x.shape
    return pl.pallas_call(
        _outproj_kernel,
        out_shape=jax.ShapeDtypeStruct((T, D), F32),
        grid=(T // tm,),
        in_specs=[
            pl.BlockSpec((tm, D), lambda i: (i, 0)),
            pl.BlockSpec((tm, D_RWKV), lambda i: (i, 0)),
            pl.BlockSpec((tm, D_POOL), lambda i: (i, 0)),
            pl.BlockSpec((tm, D_NSA), lambda i: (i, 0)),
            pl.BlockSpec((None, D_RWKV, D), lambda i: (layer, 0, 0)),
            pl.BlockSpec((None, D_POOL, D), lambda i: (layer, 0, 0)),
            pl.BlockSpec((None, D_NSA, D), lambda i: (layer, 0, 0)),
            pl.BlockSpec((None, 1, D), lambda i: (layer, 0, 0)),
            pl.BlockSpec((None, 1, D), lambda i: (layer, 0, 0)),
        ],
        out_specs=pl.BlockSpec((tm, D), lambda i: (i, 0)),
        compiler_params=_params("parallel"),
        name="out_proj_ln",
    )(x, y_rw, y_pool, y_nsa, w_out[0], w_out[1], w_out[2], g, b)


POOL_HALO = 16


def _pool_kernel(p_ref, halo_ref, w_ref, b_ref, sc_ref, o_ref, xs_ref, *, ts):
    s = pl.program_id(1)
    x = p_ref[0]
    halo = jnp.where(s > 0, halo_ref[0], 0.0)
    xs_ref[0:POOL_HALO, :] = halo
    xs_ref[POOL_HALO:POOL_HALO + ts, :] = x
    t1 = (s * ts + 1 + lax.broadcasted_iota(jnp.int32, (ts, 1), 0)).astype(F32)
    for gi, win in enumerate(POOL_WINDOWS):
        c0 = gi * POOL_GROUP
        acc = x[:, c0:c0 + POOL_GROUP]
        for j in range(1, win):
            acc = acc + xs_ref[POOL_HALO - j:POOL_HALO - j + ts, c0:c0 + POOL_GROUP]
        pooled = acc / jnp.minimum(t1, float(win)) - x[:, c0:c0 + POOL_GROUP]
        z = _dot(pooled.astype(BF16), w_ref[gi]) + b_ref[:, c0:c0 + POOL_GROUP]
        o_ref[0, :, c0:c0 + POOL_GROUP] = (z * sc_ref[:, c0:c0 + POOL_GROUP]).astype(o_ref.dtype)


def _pool_mix(p_all, pool_w, pool_b, pool_scale, layer, *, ts=512):
    B, S, _ = p_all.shape
    cb = P_POOL_OFF // D_POOL
    hb = ts // POOL_HALO
    return pl.pallas_call(
        functools.partial(_pool_kernel, ts=ts),
        out_shape=jax.ShapeDtypeStruct((B, S, D_POOL), BF16),
        grid=(B, S // ts),
        in_specs=[
            pl.BlockSpec((1, ts, D_POOL), lambda b, s: (b, s, cb)),
            pl.BlockSpec((1, POOL_HALO, D_POOL), lambda b, s: (b, jnp.maximum(s * hb - 1, 0), cb)),
            pl.BlockSpec((None, 4, POOL_GROUP, POOL_GROUP), lambda b, s: (layer, 0, 0, 0)),
            pl.BlockSpec((None, 1, D_POOL), lambda b, s: (layer, 0, 0)),
            pl.BlockSpec((None, 1, D_POOL), lambda b, s: (layer, 0, 0)),
        ],
        out_specs=pl.BlockSpec((1, ts, D_POOL), lambda b, s: (b, s, 0)),
        scratch_shapes=[pltpu.VMEM((ts + POOL_HALO, D_POOL), F32)],
        compiler_params=_params("parallel", "parallel"),
        name="pool_mix",
    )(p_all, p_all, pool_w, pool_b, pool_scale)


def _softplus(z):
    return jnp.maximum(z, 0.0) + jnp.log1p(jnp.exp(-jnp.abs(z)))


def _sigmoid(z):
    return 1.0 / (1.0 + jnp.exp(-z))


def _lanes_to_parity_major(x):
    n_pairs = x.shape[1] // LANES
    low = lax.broadcasted_iota(jnp.int32, (1, LANES), 1) < HEAD_DIM

    def chunk(e, to_low):
        pair, parity = e % n_pairs, e // n_pairs
        src = x[:, pair * LANES:(pair + 1) * LANES]
        return src if (parity == 0) == to_low else pltpu.roll(src, HEAD_DIM, 1)

    return jnp.concatenate([jnp.where(low, chunk(2 * d, True), chunk(2 * d + 1, False))
                            for d in range(n_pairs)], axis=1)


def _rw_prep_kernel(p_ref, prev_ref, mu_ref, w0_ref, w2_ref, a0_ref, a2_ref, g2_ref, kk_ref, ka_ref,
                    rk_ref, ones_ref,
                    r_o, w_o, k_o, v_o, kk_o, be_o, bo_o, g_o, *, ts):
    s = pl.program_id(1)
    x = p_ref[0]
    last = jnp.where(s > 0, prev_ref[0][SUBLANES - 1:SUBLANES, :], 0.0)
    row = lax.broadcasted_iota(jnp.int32, (ts, 1), 0)
    shifted = jnp.where(row == 0, last, pltpu.roll(x, 1, 0))
    xm = x + (shifted - x) * mu_ref[...]
    c = D_RWKV
    r = xm[:, 0:c]
    k = xm[:, c:2 * c]
    v = xm[:, 2 * c:3 * c]
    lora = xm[:, 3 * c:3 * c + LANES]
    wl = lora[:, :RW_DECAY_LORA]
    al = lora[:, RW_DECAY_LORA:]
    gl = xm[:, 3 * c + LANES:]
    w = -_softplus(-(w0_ref[...] + _dot(jnp.tanh(wl).astype(BF16), w2_ref[...]))) - 0.5
    decay = jnp.exp(-jnp.exp(w))
    a = _sigmoid(a0_ref[...] + _dot(al.astype(BF16), a2_ref[...]))
    g = _dot(_sigmoid(gl).astype(BF16), g2_ref[...])
    ones = ones_ref[...]
    kk = k * kk_ref[...]
    nrm = jnp.sqrt(_dot_split(kk * kk, ones))
    kk = kk / jnp.maximum(nrm, 1e-12)
    k_mod = k * (1.0 + (a - 1.0) * ka_ref[...])
    bonus = _dot_split(r * k_mod * rk_ref[...], ones) * v
    r_o[0] = r
    w_o[0] = decay
    k_o[0] = k_mod
    v_o[0] = v
    kk_o[0] = kk
    be_o[0] = kk * a
    bo_o[0] = _lanes_to_parity_major(bonus).astype(bo_o.dtype)
    g_o[0] = g.astype(g_o.dtype)


def _rw_prep(p_all, prm, layer, *, ts=256):
    B, S, _ = p_all.shape
    cb = P_RW_OFF // RW_COLS
    hb = ts // SUBLANES
    vec = lambda n: pl.BlockSpec((None, 1, n), lambda b, s: (layer, 0, 0))
    mat = lambda m, n: pl.BlockSpec((None, m, n), lambda b, s: (layer, 0, 0))
    out = jax.ShapeDtypeStruct((B, S, D_RWKV), F32)
    ospec = pl.BlockSpec((1, ts, D_RWKV), lambda b, s: (b, s, 0))
    return pl.pallas_call(
        functools.partial(_rw_prep_kernel, ts=ts),
        out_shape=[out] * 6 + [jax.ShapeDtypeStruct((B, S, D_RWKV), BF16)] * 2,
        grid=(B, S // ts),
        in_specs=[
            pl.BlockSpec((1, ts, RW_COLS), lambda b, s: (b, s, cb)),
            pl.BlockSpec((1, SUBLANES, RW_COLS), lambda b, s: (b, jnp.maximum(s * hb - 1, 0), cb)),
            vec(RW_COLS), vec(D_RWKV), mat(RW_DECAY_LORA, D_RWKV), vec(D_RWKV), mat(RW_A_LORA, D_RWKV),
            mat(RW_GATE_LORA, D_RWKV), vec(D_RWKV), vec(D_RWKV), vec(D_RWKV),
            pl.BlockSpec((D_RWKV, D_RWKV), lambda b, s: (0, 0)),
        ],
        out_specs=[ospec] * 8,
        compiler_params=_params("parallel", "parallel"),
        name="rw_prep",
    )(p_all, p_all, prm["mu"], prm["w0"], prm["w2"], prm["a0"], prm["a2"], prm["g2"], prm["k_k"],
      prm["k_a"], prm["r_k"], prm["ones"])


RW_PAIRS = D_RWKV // LANES


def _rw_scan_kernel(r_ref, w_ref, k_ref, v_ref, kk_ref, be_ref, bo_ref, g_ref, gng_ref, gnb_ref, ones_ref,
                    pair_ref, half_ref, o_ref, st_ref, vc_ref, sr_ref, y_ref, *, ts, nb):
    s = pl.program_id(1)

    @pl.when(s == 0)
    def _():
        st_ref[...] = jnp.zeros_like(st_ref)

    side, stack = 2, RW_PAIRS // 2
    rows, width = stack * HEAD_DIM, side * LANES
    lane = lax.broadcasted_iota(jnp.int32, (rows, width), 1)
    row = lax.broadcasted_iota(jnp.int32, (rows, width), 0)
    eye = jnp.where(lane % HEAD_DIM == row % HEAD_DIM, 1.0, 0.0)
    ones_blk = pair_ref[...]

    def spread(x8, j):
        blocks = []
        for st in range(stack):
            lanes = jnp.concatenate([x8[j:j + 1, (sd * stack + st) * LANES:(sd * stack + st + 1) * LANES]
                                     for sd in range(side)], axis=1)
            blocks.append(jnp.broadcast_to(lanes, (HEAD_DIM, width)))
        return jnp.concatenate(blocks, axis=0)

    def step(i, carry):
        base = pl.multiple_of(i * SUBLANES, SUBLANES)
        tiles = [[ref[b, pl.ds(base, SUBLANES), :] for ref in (kk_ref, w_ref, be_ref, k_ref, r_ref, v_ref)]
                 for b in range(nb)]
        for b in range(nb):
            lhs = jnp.concatenate([eye * spread(tiles[b][5], j) for j in range(SUBLANES)], axis=0)
            vc_ref[b] = _dot(lhs.astype(BF16), ones_blk)
        sts = [st_ref[b] for b in range(nb)]
        for j in range(SUBLANES):
            for b in range(nb):
                kk8, w8, be8, k8, r8, _ = tiles[b]
                st = sts[b]
                sa = _dot((st * spread(kk8, j)).astype(BF16), ones_blk)
                st = (st * spread(w8, j) - sa * spread(be8, j)
                      + vc_ref[b, j * rows:(j + 1) * rows, :] * spread(k8, j))
                sr = (st * spread(r8, j)).astype(BF16)
                for sd in range(side):
                    lo = (j * side + sd) * rows
                    sr_ref[b, lo:lo + rows, :] = sr[:, sd * LANES:(sd + 1) * LANES]
                sts[b] = st
        for b in range(nb):
            st_ref[b] = sts[b]
            yt = _dot_nt(half_ref[...], sr_ref[b])
            per = side * rows
            y_ref[b, pl.ds(base, SUBLANES), :] = jnp.concatenate(
                [jnp.concatenate([yt[0:1, j * per:(j + 1) * per], yt[1:2, j * per:(j + 1) * per]], axis=1)
                 for j in range(SUBLANES)], axis=0)
        return carry

    lax.fori_loop(0, ts // SUBLANES, step, 0)

    ones = ones_ref[...]
    for b in range(nb):
        y = y_ref[b]
        mean = _dot_split(y, ones) * (1.0 / HEAD_DIM)
        yc = y - mean
        var = _dot_split(yc * yc, ones) * (1.0 / HEAD_DIM)
        yn = yc * lax.rsqrt(var + RW_GN_EPS) * gng_ref[...] + gnb_ref[...]
        o_ref[b] = ((yn + bo_ref[b]) * g_ref[b]).astype(o_ref.dtype)


def _rw_scan(r, w, k, v, kk, be, bo, g, prm, layer, *, ts=128):
    B, S, _ = r.shape
    nb = 4 if B % 4 == 0 else (2 if B % 2 == 0 else 1)
    rows, width = RW_PAIRS // 2 * HEAD_DIM, 2 * LANES
    seq = pl.BlockSpec((nb, ts, D_RWKV), lambda b, s: (b, s, 0))
    vec = pl.BlockSpec((None, 1, D_RWKV), lambda b, s: (layer, 0, 0))
    return pl.pallas_call(
        functools.partial(_rw_scan_kernel, ts=ts, nb=nb),
        out_shape=jax.ShapeDtypeStruct((B, S, D_RWKV), BF16),
        grid=(B // nb, S // ts),
        in_specs=[seq] * 8 + [vec, vec, pl.BlockSpec((D_RWKV, D_RWKV), lambda b, s: (0, 0)),
                              pl.BlockSpec((width, width), lambda b, s: (0, 0)),
                              pl.BlockSpec((SUBLANES, LANES), lambda b, s: (0, 0))],
        out_specs=seq,
        scratch_shapes=[pltpu.VMEM((nb, rows, width), F32), pltpu.VMEM((nb, SUBLANES * rows, width), F32),
                        pltpu.VMEM((nb, SUBLANES * RW_PAIRS * HEAD_DIM, LANES), BF16),
                        pltpu.VMEM((nb, ts, D_RWKV), F32)],
        compiler_params=_params("parallel", "arbitrary"),
        name="rw_scan",
    )(r, w, k, v, kk, be, bo, g, prm["gn_g"], prm["gn_b"], prm["ones"], prm["ones"][:width, :width],
      prm["ones"][:SUBLANES * HEAD_DIM:HEAD_DIM, :LANES])


def _rope(x, cos, sin):
    half = ROPE_DIM // 2
    w = x.shape[-1]
    d = lax.broadcasted_iota(jnp.int32, x.shape, 1) % HEAD_DIM
    partner = jnp.where(d < half, pltpu.roll(x, w - half, 1), pltpu.roll(x, half, 1))
    return x * cos + partner * sin


def _nsa_prep_kernel(p_ref, cos_ref, sin_ref, gb_ref, q_o, kc_o, vc_o, ks_o, vs_o, kw_o, vw_o, gt_o):
    cos = cos_ref[...]
    sin = sin_ref[...]
    for hq in range(NSA_HEADS // 4):
        x = p_ref[0, :, hq * KV_SLOT:(hq + 1) * KV_SLOT]
        qr = _rope(x, cos, sin) * (HEAD_DIM ** -0.5)
        for j in range(4):
            q_o[0, 4 * hq + j] = qr[:, j * HEAD_DIM:(j + 1) * HEAD_DIM].astype(BF16)
    for i, ref in enumerate((kc_o, vc_o, ks_o, vs_o, kw_o, vw_o)):
        part = p_ref[0, :, D_NSA + i * KV_SLOT:D_NSA + (i + 1) * KV_SLOT]
        if i in (2, 4):
            part = _rope(part, cos, sin)
        if i in (3, 5):
            part = part.T
            for h in range(NSA_KV_HEADS):
                ref[0, h] = part[h * HEAD_DIM:(h + 1) * HEAD_DIM, :].astype(BF16)
        else:
            for h in range(NSA_KV_HEADS):
                ref[0, h] = part[:, h * HEAD_DIM:(h + 1) * HEAD_DIM].astype(BF16)
    gts = _sigmoid(p_ref[0, :, NSA_GATE_OFF:NSA_GATE_OFF + LANES] + gb_ref[...]).T
    for h in range(NSA_KV_HEADS):
        gt_o[0, h] = gts[h * 12:(h + 1) * 12, :]


def _nsa_prep(p_all, cos, sin, gate_b, layer, *, ts=256):
    B, S, _ = p_all.shape
    H = NSA_KV_HEADS
    k_shape = jax.ShapeDtypeStruct((B, H, S, HEAD_DIM), BF16)
    k_spec = pl.BlockSpec((1, H, ts, HEAD_DIM), lambda b, s: (b, 0, s, 0))
    vt_shape = jax.ShapeDtypeStruct((B, H, HEAD_DIM, S), BF16)
    vt_spec = pl.BlockSpec((1, H, HEAD_DIM, ts), lambda b, s: (b, 0, 0, s))
    return pl.pallas_call(
        _nsa_prep_kernel,
        out_shape=[jax.ShapeDtypeStruct((B, NSA_HEADS, S, HEAD_DIM), BF16),
                   k_shape, k_shape, k_shape, vt_shape, k_shape, vt_shape,
                   jax.ShapeDtypeStruct((B, H, 12, S), F32)],
        grid=(B, S // ts),
        in_specs=[
            pl.BlockSpec((1, ts, NSA_PAD), lambda b, s: (b, s, P_NSA_OFF // NSA_PAD)),
            pl.BlockSpec((ts, KV_SLOT), lambda b, s: (s, 0)),
            pl.BlockSpec((ts, KV_SLOT), lambda b, s: (s, 0)),
            pl.BlockSpec((None, 1, LANES), lambda b, s: (layer, 0, 0)),
        ],
        out_specs=[pl.BlockSpec((1, NSA_HEADS, ts, HEAD_DIM), lambda b, s: (b, 0, s, 0)),
                   k_spec, k_spec, k_spec, vt_spec, k_spec, vt_spec,
                   pl.BlockSpec((1, H, 12, ts), lambda b, s: (b, 0, 0, s))],
        compiler_params=_params("parallel", "parallel"),
        name="nsa_prep",
    )(p_all, cos, sin, gate_b)


def _gelu_tanh(x):
    return 0.5 * x * (1.0 + jnp.tanh(float(np.sqrt(2.0 / np.pi)) * (x + 0.044715 * (x * x * x))))


def _nsa_cmp_kernel(kc_ref, vc_ref, kw1_ref, kw2_ref, kpe_ref, vw1_ref, vw2_ref, vpe_ref, cos_ref, sin_ref,
                    k_o, v_o):
    def hidden(g, w1_ref, pe_ref):
        half = CMP_STRIDE * HEAD_DIM
        first = _dot(g, w1_ref[:half, :])
        second = _dot(g, w1_ref[half:, :])
        n = first.shape[0]
        bias = _dot(pe_ref[...], w1_ref[...])[0:1]
        return _gelu_tanh(first + pltpu.roll(second, n - 1, 0) + bias).astype(BF16)

    k = _dot(hidden(kc_ref[0, 0], kw1_ref, kpe_ref), kw2_ref[...])
    k_o[0, 0] = _rope(k, cos_ref[...], sin_ref[...])[:, :HEAD_DIM].astype(BF16)
    v_o[0, 0] = _dot_nt(vw2_ref[...], hidden(vc_ref[0, 0], vw1_ref, vpe_ref)).astype(BF16)


def _nsa_compress(kc, vc, prm, cos_c, sin_c, layer):
    B, H, S, _ = kc.shape
    ng = S // CMP_STRIDE
    gw = CMP_STRIDE * HEAD_DIM
    g_k = kc.reshape(B, H, ng, gw)
    g_v = vc.reshape(B, H, ng, gw)
    gspec = pl.BlockSpec((1, 1, ng, gw), lambda b, h: (b, h, 0, 0))
    w1 = pl.BlockSpec((None, 2 * gw, CMP_HIDDEN), lambda b, h: (layer, 0, 0))
    w2 = pl.BlockSpec((None, CMP_HIDDEN, LANES), lambda b, h: (layer, 0, 0))
    pe = pl.BlockSpec((None, SUBLANES, 2 * gw), lambda b, h: (layer, 0, 0))
    tab = pl.BlockSpec((ng, LANES), lambda b, h: (0, 0))
    return pl.pallas_call(
        _nsa_cmp_kernel,
        out_shape=[jax.ShapeDtypeStruct((B, H, ng, HEAD_DIM), BF16), jax.ShapeDtypeStruct((B, H, HEAD_DIM, ng), BF16)],
        grid=(B, H),
        in_specs=[gspec, gspec, w1, w2, pe, w1,
                  pl.BlockSpec((None, HEAD_DIM, CMP_HIDDEN), lambda b, h: (layer, 0, 0)), pe, tab, tab],
        out_specs=[pl.BlockSpec((1, 1, ng, HEAD_DIM), lambda b, h: (b, h, 0, 0)),
                   pl.BlockSpec((1, 1, HEAD_DIM, ng), lambda b, h: (b, h, 0, 0))],
        compiler_params=_params("parallel", "parallel"),
        name="nsa_compress",
    )(g_k, g_v, prm["k_w1"], prm["k_w2"], prm["k_pe"], prm["v_w1"], prm["v_w2"], prm["v_pe"], cos_c, sin_c)


def _nsa_attn_kernel(q_ref, kcmp_ref, vcmp_ref, ks_ref, vs_ref, kw_ref, vw_ref, gt_ref, ov_ref, o_ref,
                     m_ref, l_ref, acc_ref, sb_ref, *, tq):
    G = NSA_GQA
    tk = tq
    qi = pl.program_id(2)
    q0 = qi * tq
    q_all = q_ref[0].reshape(G * tq, HEAD_DIM)
    pos = q0 + lax.broadcasted_iota(jnp.int32, (1, tq), 1)
    tiny = jnp.finfo(F32).tiny
    heads = lambda x: jnp.tile(x, (1, G))
    split = lambda x: [x[:, g * tq:(g + 1) * tq] for g in range(G)]

    ncp = kcmp_ref.shape[2]
    n_idx = lax.broadcasted_iota(jnp.int32, (ncp, 1), 0)
    cbias = jnp.where((n_idx * CMP_STRIDE + (CMP_BLOCK - 1)) <= pos, 0.0, NEG)
    any_cmp = jnp.where(pos >= CMP_BLOCK - 1, 1.0, 0.0)
    s = _dot_nt(kcmp_ref[0, 0], q_all) + heads(cbias)
    e = jnp.exp(s - jnp.max(s, axis=0, keepdims=True))
    p = e * (heads(any_cmp) / jnp.maximum(jnp.sum(e, axis=0, keepdims=True), tiny))
    o_cmp = split(_dot(vcmp_ref[0, 0], p.astype(BF16)))
    ps = split(p)
    psum = (ps[0] + ps[1]) + (ps[2] + ps[3])
    p_hi = psum.astype(BF16)
    p_lo = (psum - p_hi.astype(F32)).astype(BF16)
    imp = _dot(ov_ref[...], p_hi) + _dot(ov_ref[...], p_lo)

    wk = min(WINDOW + tq, ks_ref.shape[2])
    w0 = pl.multiple_of(jnp.maximum(q0 + tq - wk, 0), tk)
    wpos = w0 + lax.broadcasted_iota(jnp.int32, (wk, 1), 0)
    wbias = jnp.where((wpos <= pos) & (wpos > pos - WINDOW), 0.0, NEG)
    s = _dot_nt(kw_ref[0, 0, pl.ds(w0, wk), :], q_all) + heads(wbias)
    e = jnp.exp(s - jnp.max(s, axis=0, keepdims=True))
    pv = _dot(vw_ref[0, 0, :, pl.ds(w0, wk)], e.astype(BF16))
    o_win = split(pv / jnp.maximum(jnp.sum(e, axis=0, keepdims=True), tiny))

    n_sel = ov_ref.shape[0]
    blk = lax.broadcasted_iota(jnp.int32, (n_sel, 1), 0)
    cur = pos // SEL_BLOCK
    forced = (blk == 0) | (blk == cur) | (blk == cur - 1)
    valid = blk <= cur
    score = jnp.where(valid, jnp.where(forced, FORCE_SCORE, imp), -jnp.inf)
    groups = [score[g * SUBLANES:(g + 1) * SUBLANES, :] for g in range(n_sel // SUBLANES)]
    ranks = [jnp.zeros((SUBLANES, tq), F32) for _ in groups]
    sub = lax.broadcasted_iota(jnp.int32, (SUBLANES, 1), 0)
    for i in range(n_sel):
        ci = jnp.broadcast_to(score[i:i + 1, :], (SUBLANES, tq))
        for g, sg in enumerate(groups):
            if g > i // SUBLANES:
                beats = ci >= sg
            elif g < i // SUBLANES:
                beats = ci > sg
            else:
                beats = (ci > sg) | ((ci == sg) & (sub > i % SUBLANES))
            ranks[g] = ranks[g] + jnp.where(beats, 1.0, 0.0)
    rank = jnp.concatenate(ranks, axis=0)
    chosen = (rank < float(SEL_TOPK)) & valid
    sel_bias = jnp.where(chosen, 0.0, NEG)
    for j in range(n_sel):
        sb_ref[j] = jnp.broadcast_to(sel_bias[j:j + 1, :], (SUBLANES, tq))

    kcol = lax.broadcasted_iota(jnp.int32, (tk, 1), 0)

    def sel_update(kt, n, extra=None):
        bpt = n // SEL_BLOCK
        k0 = pl.multiple_of(kt * tk, tk)
        s = _dot_nt(ks_ref[0, 0, pl.ds(k0, n), :], q_all)
        b0 = kt * (tk // SEL_BLOCK)
        bias = jnp.concatenate([jnp.tile(sb_ref[b0 + j], (SEL_BLOCK // SUBLANES, 1)) for j in range(bpt)],
                               axis=0)
        if extra is not None:
            bias = bias + extra
        vt = vs_ref[0, 0, :, pl.ds(k0, n)]
        s = s + heads(bias)
        m_prev = m_ref[...]
        m_new = jnp.maximum(m_prev, jnp.max(s, axis=0, keepdims=True))
        p = jnp.exp(s - m_new)
        corr = jnp.exp(m_prev - m_new)
        l_ref[...] = corr * l_ref[...] + jnp.sum(p, axis=0, keepdims=True)
        acc_ref[...] = corr * acc_ref[...] + _dot(vt, p.astype(BF16))
        m_ref[...] = m_new

    m_ref[...] = jnp.full_like(m_ref, NEG)
    l_ref[...] = jnp.zeros_like(l_ref)
    acc_ref[...] = jnp.zeros_like(acc_ref)

    def sel_body(i, carry):
        sel_update(4 * i, 4 * tk)
        return carry

    lax.fori_loop(0, qi // 4, sel_body, 0)

    @pl.when(qi % 4 >= 2)
    def _():
        sel_update((qi // 4) * 4, 2 * tk)

    @pl.when(qi % 2 == 1)
    def _():
        sel_update(qi - 1, tk)

    causal = jnp.where(q0 + kcol <= pos, 0.0, NEG)
    sel_update(qi, tk, causal)
    o_sel = split(acc_ref[...] / jnp.maximum(l_ref[...], tiny))

    gt = gt_ref[0, 0]
    outs = [gt[g:g + 1] * o_cmp[g] + gt[G + g:G + g + 1] * o_sel[g] + gt[2 * G + g:2 * G + g + 1] * o_win[g]
            for g in range(G)]
    for half in range(G // 2):
        pair = jnp.concatenate(outs[2 * half:2 * half + 2], axis=0)
        o_ref[0, :, half * LANES:(half + 1) * LANES] = pair.T.astype(o_ref.dtype)


def _nsa_attention(q, k_cmp, v_cmp_t, ks, vs_t, kw, vw_t, gates_t, overlap_t, *, tq=256):
    B, _, S, _ = q.shape
    H, G = NSA_KV_HEADS, NSA_GQA
    tq = min(tq, S)
    assert WINDOW == 2 * tq or S <= tq
    ncp = k_cmp.shape[2]
    n_cmp = (S - CMP_BLOCK) // CMP_STRIDE + 1
    keys = pl.BlockSpec((1, 1, S, HEAD_DIM), lambda b, h, i: (b, h, 0, 0))
    vals = pl.BlockSpec((1, 1, HEAD_DIM, S), lambda b, h, i: (b, h, 0, 0))
    return pl.pallas_call(
        functools.partial(_nsa_attn_kernel, tq=tq),
        out_shape=jax.ShapeDtypeStruct((B, S, D_NSA), BF16),
        grid=(B, H, S // tq),
        in_specs=[
            pl.BlockSpec((1, G, tq, HEAD_DIM), lambda b, h, i: (b, h, i, 0)),
            pl.BlockSpec((1, 1, ncp, HEAD_DIM), lambda b, h, i: (b, h, 0, 0)),
            pl.BlockSpec((1, 1, HEAD_DIM, ncp), lambda b, h, i: (b, h, 0, 0)),
            keys, vals, keys, vals,
            pl.BlockSpec((1, 1, 12, tq), lambda b, h, i: (b, h, 0, i)),
            pl.BlockSpec(overlap_t.shape, lambda b, h, i: (0, 0)),
        ],
        out_specs=pl.BlockSpec((1, tq, G * HEAD_DIM), lambda b, h, i: (b, i, h)),
        scratch_shapes=[pltpu.VMEM((1, G * tq), F32), pltpu.VMEM((1, G * tq), F32),
                        pltpu.VMEM((HEAD_DIM, G * tq), F32), pltpu.VMEM((S // SEL_BLOCK, SUBLANES, tq), F32)],
        compiler_params=_params("parallel", "parallel", "arbitrary"),
        name="nsa_attention",
    )(q, k_cmp, v_cmp_t, ks, vs_t, kw, vw_t, gates_t, overlap_t)


def _rope_tables(pos, heads):
    half = ROPE_DIM // 2
    inv_freq = ROPE_THETA ** (-jnp.arange(half, dtype=F32) / half)
    ang = pos.astype(F32)[:, None] * inv_freq
    cos, sin = jnp.cos(ang), jnp.sin(ang)
    n = pos.shape[0]
    rest = HEAD_DIM - ROPE_DIM
    cos_h = jnp.concatenate([cos, cos, jnp.ones((n, rest), F32)], axis=1)
    sin_h = jnp.concatenate([-sin, sin, jnp.zeros((n, rest), F32)], axis=1)
    return jnp.tile(cos_h, (1, heads)), jnp.tile(sin_h, (1, heads))


def _overlap_matrix(S, ncp):
    n_cmp = (S - CMP_BLOCK) // CMP_STRIDE + 1
    n_sel = S // SEL_BLOCK
    cmp_start = np.arange(n_cmp) * CMP_STRIDE
    sel_start = np.arange(n_sel) * SEL_BLOCK
    ov = np.clip(np.minimum(cmp_start[:, None] + CMP_BLOCK, sel_start[None, :] + SEL_BLOCK)
                 - np.maximum(cmp_start[:, None], sel_start[None, :]), 0, None) / CMP_BLOCK
    full = np.zeros((n_sel, ncp), np.float32)
    full[:, :n_cmp] = ov.T
    return jnp.asarray(full, BF16)


RW_HEAD_ORDER = tuple(2 * (e % RW_PAIRS) + e // RW_PAIRS for e in range(RW_HEADS))


def _parity_major(a, axis):
    axis = axis % a.ndim
    shape = a.shape
    a = a.reshape(shape[:axis] + (RW_HEADS, HEAD_DIM) + shape[axis + 1:])
    a = jnp.take(a, jnp.asarray(RW_HEAD_ORDER), axis=axis)
    return a.reshape(shape)


def _block_ones(n):
    idx = np.arange(n) // HEAD_DIM
    return jnp.asarray((idx[:, None] == idx[None, :]).astype(np.float32), BF16)


def _w_in_layout(w_in):
    L = w_in.shape[0]
    rw = w_in[:, :, :RW_COLS]
    pool = w_in[:, :, RW_COLS:RW_COLS + D_POOL]
    nsa = w_in[:, :, RW_COLS + D_POOL:]
    zeros = lambda n: jnp.zeros((L, D_MODEL, n), w_in.dtype)
    segs = [nsa[:, :, :D_NSA]]
    for i in range(6):
        segs += [nsa[:, :, D_NSA + i * NSA_KV:D_NSA + (i + 1) * NSA_KV], zeros(KV_SLOT - NSA_KV)]
    gates = nsa[:, :, D_NSA + 6 * NSA_KV:]
    gates = gates.reshape(L, D_MODEL, NSA_KV_HEADS, NSA_GQA, 3).transpose(0, 1, 2, 4, 3)
    segs += [gates.reshape(L, D_MODEL, 3 * NSA_HEADS),
             zeros(NSA_PAD - NSA_GATE_OFF - 3 * NSA_HEADS)]
    return jnp.concatenate([rw] + segs + [pool], axis=-1).astype(BF16)


def _gate_bias_layout(gate_b):
    L = gate_b.shape[0]
    gb = gate_b.reshape(L, NSA_KV_HEADS, NSA_GQA, 3).transpose(0, 1, 3, 2).reshape(L, 1, 3 * NSA_HEADS)
    return jnp.pad(gb, ((0, 0), (0, 0), (0, LANES - 3 * NSA_HEADS)))


def kernel(x, ffn1_w_up, ffn1_w_down, ln1_g, ln1_b, w_in, rw_mu, rw_w0, rw_w2, rw_a0, rw_a2, rw_g2, rw_k_k,
           rw_k_a, rw_r_k, rw_gn_g, rw_gn_b, pool_w, pool_b, pool_scale, nsa_cmp_pe_k, nsa_cmp_pe_v,
           nsa_cmp_k_w1, nsa_cmp_k_w2, nsa_cmp_v_w1, nsa_cmp_v_w2, nsa_gate_b, w_out, ln2_g, ln2_b,
           ffn2_w_up, ffn2_w_down, ln3_g, ln3_b):
    prm = _prepare(x.shape[1], ffn1_w_up, ffn1_w_down, ln1_g, ln1_b, w_in, rw_mu, rw_w0, rw_w2, rw_a0, rw_a2,
                   rw_g2, rw_k_k, rw_k_a, rw_r_k, rw_gn_g, rw_gn_b, pool_w, pool_b, pool_scale, nsa_cmp_pe_k,
                   nsa_cmp_pe_v, nsa_cmp_k_w1, nsa_cmp_k_w2, nsa_cmp_v_w1, nsa_cmp_v_w2, nsa_gate_b, w_out,
                   ln2_g, ln2_b, ffn2_w_up, ffn2_w_down, ln3_g, ln3_b)
    B, S, D = x.shape
    h = x.reshape(B * S, D)
    for l in range(w_in.shape[0]):
        h = _layer(h, prm, l, B, S)
    return h.reshape(B, S, D)


def _prepare(S, ffn1_w_up, ffn1_w_down, ln1_g, ln1_b, w_in, rw_mu, rw_w0, rw_w2, rw_a0, rw_a2, rw_g2, rw_k_k,
             rw_k_a, rw_r_k, rw_gn_g, rw_gn_b, pool_w, pool_b, pool_scale, nsa_cmp_pe_k, nsa_cmp_pe_v,
             nsa_cmp_k_w1, nsa_cmp_k_w2, nsa_cmp_v_w1, nsa_cmp_v_w2, nsa_gate_b, w_out, ln2_g, ln2_b,
             ffn2_w_up, ffn2_w_down, ln3_g, ln3_b):
    L = w_in.shape[0]
    fpad = D_FF_PAD - D_FF

    def up(w):
        a = jnp.pad(w[:, :, :D_FF], ((0, 0), (0, 0), (0, fpad))).astype(BF16)
        b = jnp.pad(w[:, :, D_FF:], ((0, 0), (0, 0), (0, fpad))).astype(BF16)
        return a, b

    def down(w):
        return jnp.pad(w, ((0, 0), (0, fpad), (0, 0))).astype(BF16)

    row = lambda v: v[:, None, :]
    f1a, f1b = up(ffn1_w_up)
    f2a, f2b = up(ffn2_w_up)
    w_out_b = w_out.astype(BF16)
    gw = CMP_BLOCK * HEAD_DIM
    pad_w2 = lambda w: jnp.pad(w, ((0, 0), (0, 0), (0, LANES - HEAD_DIM))).astype(BF16)
    pe_rows = lambda pe: jnp.broadcast_to(pe.reshape(L, 1, gw), (L, SUBLANES, gw)).astype(BF16)
    ncp = S // CMP_STRIDE
    cos_t, sin_t = _rope_tables(jnp.arange(S), 4)
    cos_c, sin_c = _rope_tables(jnp.arange(ncp) * CMP_STRIDE + (CMP_BLOCK - 1), 2)
    return dict(
        ffn1=(f1a, f1b, down(ffn1_w_down), row(ln1_g), row(ln1_b)),
        ffn2=(f2a, f2b, down(ffn2_w_down), row(ln3_g), row(ln3_b)),
        w_in=_w_in_layout(w_in),
        w_out=(_parity_major(w_out_b[:, :D_RWKV], 1), w_out_b[:, D_RWKV:D_RWKV + D_POOL],
               w_out_b[:, D_RWKV + D_POOL:]),
        ln2=(row(ln2_g), row(ln2_b)),
        rw=dict(mu=row(rw_mu), w0=row(rw_w0), w2=rw_w2.astype(BF16), a0=row(rw_a0), a2=rw_a2.astype(BF16),
                g2=_parity_major(rw_g2, -1).astype(BF16), k_k=row(rw_k_k), k_a=row(rw_k_a),
                r_k=rw_r_k.reshape(L, 1, D_RWKV), gn_g=row(_parity_major(rw_gn_g, -1)),
                gn_b=row(_parity_major(rw_gn_b, -1)), ones=_block_ones(D_RWKV)),
        cmp=dict(k_w1=nsa_cmp_k_w1.reshape(L, gw, CMP_HIDDEN).astype(BF16), k_w2=pad_w2(nsa_cmp_k_w2),
                 k_pe=pe_rows(nsa_cmp_pe_k),
                 v_w1=nsa_cmp_v_w1.reshape(L, gw, CMP_HIDDEN).astype(BF16), v_w2=nsa_cmp_v_w2.transpose(0, 2, 1).astype(BF16),
                 v_pe=pe_rows(nsa_cmp_pe_v)),
        gate_b=_gate_bias_layout(nsa_gate_b),
        pool=(pool_w.astype(BF16), row(pool_b), row(pool_scale)),
        rope=(cos_t, sin_t), rope_cmp=(cos_c, sin_c), overlap=_overlap_matrix(S, ncp))


def _mixers(p_all, prm, l):
    r, w, k, v, kk, be, bo, g = _rw_prep(p_all, prm["rw"], l)
    y_rw = _rw_scan(r, w, k, v, kk, be, bo, g, prm["rw"], l)
    y_pool = _pool_mix(p_all, *prm["pool"], l)
    q, kc, vc, ks, vs, kw, vw, gates = _nsa_prep(p_all, *prm["rope"], prm["gate_b"], l)
    k_cmp, v_cmp = _nsa_compress(kc, vc, prm["cmp"], *prm["rope_cmp"], l)
    y_nsa = _nsa_attention(q, k_cmp, v_cmp, ks, vs, kw, vw, gates, prm["overlap"])
    return y_rw, y_pool, y_nsa


def _layer(h, prm, l, B, S):
    T = B * S
    h = _ffn_ln(h, *prm["ffn1"], l)
    p_all = _in_proj(h, prm["w_in"], l).reshape(B, S, P_COLS)
    y_rw, y_pool, y_nsa = _mixers(p_all, prm, l)
    h = _out_proj_ln(h, y_rw.reshape(T, D_RWKV), y_pool.reshape(T, D_POOL), y_nsa.reshape(T, D_NSA),
                     prm["w_out"], *prm["ln2"], l)
    return _ffn_ln(h, *prm["ffn2"], l)
```

```python
import functools

import numpy as np
import jax
import jax.numpy as jnp
from jax import lax
from jax.experimental import pallas as pl
from jax.experimental.pallas import tpu as pltpu

F32 = jnp.float32
BF16 = jnp.bfloat16

D_MODEL = 2048
DEPTH = 4
HEAD_DIM = 64
D_RWKV = 768
D_POOL = 512
D_NSA = 768
RW_HEADS = 12
RW_DECAY_LORA = 64
RW_A_LORA = 64
RW_GATE_LORA = 128
RW_GN_EPS = 64e-5
RW_COLS = 3 * D_RWKV + RW_DECAY_LORA + RW_A_LORA + RW_GATE_LORA
POOL_WINDOWS = (2, 4, 8, 16)
POOL_GROUP = 128
NSA_HEADS = 12
NSA_KV_HEADS = 3
NSA_GQA = 4
NSA_KV = 192
NSA_COLS = D_NSA + 6 * NSA_KV + 3 * NSA_HEADS
CMP_BLOCK = 32
CMP_STRIDE = 16
CMP_HIDDEN = 256
SEL_BLOCK = 64
SEL_TOPK = 16
FORCE_SCORE = 1e9
WINDOW = 512
ROPE_THETA = 500000.0
ROPE_DIM = 16
D_FF = 5504
IN_COLS = RW_COLS + D_POOL + NSA_COLS
ALPHA = (2 * DEPTH) ** 0.25
LN_EPS = 1e-5

LANES = 128
SUBLANES = 8
VMEM_LIMIT = 56 * 1024 * 1024

KV_SLOT = 2 * LANES
NSA_PAD = 2560
P_RW_OFF = 0
P_NSA_OFF = RW_COLS
P_POOL_OFF = RW_COLS + NSA_PAD
P_COLS = RW_COLS + NSA_PAD + D_POOL
NSA_GATE_OFF = D_NSA + 6 * KV_SLOT
D_FF_PAD = 5632

NEG = -1e30


def _params(*sem):
    return pltpu.CompilerParams(dimension_semantics=sem, vmem_limit_bytes=VMEM_LIMIT)


def _layer_norm(z, g, b):
    mu = jnp.mean(z, axis=-1, keepdims=True)
    zc = z - mu
    var = jnp.mean(zc * zc, axis=-1, keepdims=True)
    return zc * lax.rsqrt(var + LN_EPS) * g + b


def _dot(a, b):
    return jnp.dot(a, b, preferred_element_type=F32)


def _dot_nt(a, b):
    return lax.dot_general(a, b, (((1,), (1,)), ((), ())), preferred_element_type=F32)


def _dot_split(x, w):
    hi = x.astype(BF16)
    lo = (x - hi.astype(F32)).astype(BF16)
    return _dot(hi, w) + _dot(lo, w)


FFN_LN_ROWS = 256


def _ffn_kernel(x_ref, wa_ref, wb_ref, wd_ref, g_ref, b_ref, o_ref, xb_ref):
    k = pl.program_id(1)

    @pl.when(k == 0)
    def _():
        xb_ref[...] = x_ref[...].astype(BF16)
        o_ref[...] = jnp.zeros_like(o_ref)

    xb = xb_ref[...]
    a = _dot(xb, wa_ref[...])
    b = _dot(xb, wb_ref[...])
    h = (a / (1.0 + jnp.exp(-a))) * b
    o_ref[...] += _dot(h.astype(BF16), wd_ref[...])

    @pl.when(k == pl.num_programs(1) - 1)
    def _():
        for c in range(o_ref.shape[0] // FFN_LN_ROWS):
            rows = slice(c * FFN_LN_ROWS, (c + 1) * FFN_LN_ROWS)
            z = ALPHA * x_ref[rows, :] + 0.5 * o_ref[rows, :]
            o_ref[rows, :] = _layer_norm(z, g_ref[...], b_ref[...])


def _ffn_ln(x, wa, wb, wd, g, b, layer, *, tm=1024, tf=256):
    T, D = x.shape
    fp = wa.shape[-1]
    tm = min(tm, T)
    return pl.pallas_call(
        _ffn_kernel,
        out_shape=jax.ShapeDtypeStruct((T, D), F32),
        grid=(T // tm, fp // tf),
        in_specs=[
            pl.BlockSpec((tm, D), lambda i, k: (i, 0)),
            pl.BlockSpec((None, D, tf), lambda i, k: (layer, 0, k)),
            pl.BlockSpec((None, D, tf), lambda i, k: (layer, 0, k)),
            pl.BlockSpec((None, tf, D), lambda i, k: (layer, k, 0)),
            pl.BlockSpec((None, 1, D), lambda i, k: (layer, 0, 0)),
            pl.BlockSpec((None, 1, D), lambda i, k: (layer, 0, 0)),
        ],
        out_specs=pl.BlockSpec((tm, D), lambda i, k: (i, 0)),
        scratch_shapes=[pltpu.VMEM((tm, D), BF16)],
        compiler_params=_params("parallel", "arbitrary"),
        name="ffn_ln",
    )(x, wa, wb, wd, g, b)


def _inproj_kernel(x_ref, w_ref, o_ref, xb_ref):
    @pl.when(pl.program_id(1) == 0)
    def _():
        xb_ref[...] = x_ref[...].astype(BF16)

    o_ref[...] = _dot(xb_ref[...], w_ref[...])


def _in_proj(x, w, layer, *, tm=1024, tn=P_COLS // 4):
    T, D = x.shape
    n = w.shape[-1]
    tm = min(tm, T)
    return pl.pallas_call(
        _inproj_kernel,
        out_shape=jax.ShapeDtypeStruct((T, n), F32),
        grid=(T // tm, n // tn),
        in_specs=[
            pl.BlockSpec((tm, D), lambda i, j: (i, 0)),
            pl.BlockSpec((None, D, tn), lambda i, j: (layer, 0, j)),
        ],
        out_specs=pl.BlockSpec((tm, tn), lambda i, j: (i, j)),
        scratch_shapes=[pltpu.VMEM((tm, D), BF16)],
        compiler_params=_params("parallel", "arbitrary"),
        name="in_proj",
    )(x, w)


def _outproj_kernel(x_ref, yr_ref, yp_ref, yn_ref, wr_ref, wp_ref, wn_ref, g_ref, b_ref, o_ref):
    y = _dot(yr_ref[...], wr_ref[...]) + _dot(yp_ref[...], wp_ref[...]) + _dot(yn_ref[...], wn_ref[...])
    o_ref[...] = _layer_norm(ALPHA * x_ref[...] + y, g_ref[...], b_ref[...])


def _out_proj_ln(x, y_rw, y_pool, y_nsa, w_out, g, b, layer, *, tm=512):
    T, D = x.shape
    return pl.pallas_call(
        _outproj_kernel,
        out_shape=jax.ShapeDtypeStruct((T, D), F32),
        grid=(T // tm,),
        in_specs=[
            pl.BlockSpec((tm, D), lambda i: (i, 0)),
            pl.BlockSpec((tm, D_RWKV), lambda i: (i, 0)),
            pl.BlockSpec((tm, D_POOL), lambda i: (i, 0)),
            pl.BlockSpec((tm, D_NSA), lambda i: (i, 0)),
            pl.BlockSpec((None, D_RWKV, D), lambda i: (layer, 0, 0)),
            pl.BlockSpec((None, D_POOL, D), lambda i: (layer, 0, 0)),
            pl.BlockSpec((None, D_NSA, D), lambda i: (layer, 0, 0)),
            pl.BlockSpec((None, 1, D), lambda i: (layer, 0, 0)),
            pl.BlockSpec((None, 1, D), lambda i: (layer, 0, 0)),
        ],
        out_specs=pl.BlockSpec((tm, D), lambda i: (i, 0)),
        compiler_params=_params("parallel"),
        name="out_proj_ln",
    )(x, y_rw, y_pool, y_nsa, w_out[0], w_out[1], w_out[2], g, b)


POOL_HALO = 16


def _pool_kernel(p_ref, halo_ref, w_ref, b_ref, sc_ref, o_ref, xs_ref, *, ts):
    s = pl.program_id(1)
    x = p_ref[0]
    halo = jnp.where(s > 0, halo_ref[0], 0.0)
    xs_ref[0:POOL_HALO, :] = halo
    xs_ref[POOL_HALO:POOL_HALO + ts, :] = x
    t1 = (s * ts + 1 + lax.broadcasted_iota(jnp.int32, (ts, 1), 0)).astype(F32)
    for gi, win in enumerate(POOL_WINDOWS):
        c0 = gi * POOL_GROUP
        acc = x[:, c0:c0 + POOL_GROUP]
        for j in range(1, win):
            acc = acc + xs_ref[POOL_HALO - j:POOL_HALO - j + ts, c0:c0 + POOL_GROUP]
        pooled = acc / jnp.minimum(t1, float(win)) - x[:, c0:c0 + POOL_GROUP]
        z = _dot(pooled.astype(BF16), w_ref[gi]) + b_ref[:, c0:c0 + POOL_GROUP]
        o_ref[0, :, c0:c0 + POOL_GROUP] = (z * sc_ref[:, c0:c0 + POOL_GROUP]).astype(o_ref.dtype)


def _pool_mix(p_all, pool_w, pool_b, pool_scale, layer, *, ts=512):
    B, S, _ = p_all.shape
    cb = P_POOL_OFF // D_POOL
    hb = ts // POOL_HALO
    return pl.pallas_call(
        functools.partial(_pool_kernel, ts=ts),
        out_shape=jax.ShapeDtypeStruct((B, S, D_POOL), BF16),
        grid=(B, S // ts),
        in_specs=[
            pl.BlockSpec((1, ts, D_POOL), lambda b, s: (b, s, cb)),
            pl.BlockSpec((1, POOL_HALO, D_POOL), lambda b, s: (b, jnp.maximum(s * hb - 1, 0), cb)),
            pl.BlockSpec((None, 4, POOL_GROUP, POOL_GROUP), lambda b, s: (layer, 0, 0, 0)),
            pl.BlockSpec((None, 1, D_POOL), lambda b, s: (layer, 0, 0)),
            pl.BlockSpec((None, 1, D_POOL), lambda b, s: (layer, 0, 0)),
        ],
        out_specs=pl.BlockSpec((1, ts, D_POOL), lambda b, s: (b, s, 0)),
        scratch_shapes=[pltpu.VMEM((ts + POOL_HALO, D_POOL), F32)],
        compiler_params=_params("parallel", "parallel"),
        name="pool_mix",
    )(p_all, p_all, pool_w, pool_b, pool_scale)


def _softplus(z):
    return jnp.maximum(z, 0.0) + jnp.log1p(jnp.exp(-jnp.abs(z)))


def _sigmoid(z):
    return 1.0 / (1.0 + jnp.exp(-z))


def _lanes_to_parity_major(x):
    n_pairs = x.shape[1] // LANES
    low = lax.broadcasted_iota(jnp.int32, (1, LANES), 1) < HEAD_DIM

    def chunk(e, to_low):
        pair, parity = e % n_pairs, e // n_pairs
        src = x[:, pair * LANES:(pair + 1) * LANES]
        return src if (parity == 0) == to_low else pltpu.roll(src, HEAD_DIM, 1)

    return jnp.concatenate([jnp.where(low, chunk(2 * d, True), chunk(2 * d + 1, False))
                            for d in range(n_pairs)], axis=1)


def _rw_prep_kernel(p_ref, prev_ref, mu_ref, w0_ref, w2_ref, a0_ref, a2_ref, g2_ref, kk_ref, ka_ref,
                    rk_ref, ones_ref,
                    r_o, w_o, k_o, v_o, kk_o, be_o, bo_o, g_o, *, ts):
    s = pl.program_id(1)
    x = p_ref[0]
    last = jnp.where(s > 0, prev_ref[0][SUBLANES - 1:SUBLANES, :], 0.0)
    row = lax.broadcasted_iota(jnp.int32, (ts, 1), 0)
    shifted = jnp.where(row == 0, last, pltpu.roll(x, 1, 0))
    xm = x + (shifted - x) * mu_ref[...]
    c = D_RWKV
    r = xm[:, 0:c]
    k = xm[:, c:2 * c]
    v = xm[:, 2 * c:3 * c]
    lora = xm[:, 3 * c:3 * c + LANES]
    wl = lora[:, :RW_DECAY_LORA]
    al = lora[:, RW_DECAY_LORA:]
    gl = xm[:, 3 * c + LANES:]
    w = -_softplus(-(w0_ref[...] + _dot(jnp.tanh(wl).astype(BF16), w2_ref[...]))) - 0.5
    decay = jnp.exp(-jnp.exp(w))
    a = _sigmoid(a0_ref[...] + _dot(al.astype(BF16), a2_ref[...]))
    g = _dot(_sigmoid(gl).astype(BF16), g2_ref[...])
    ones = ones_ref[...]
    kk = k * kk_ref[...]
    nrm = jnp.sqrt(_dot_split(kk * kk, ones))
    kk = kk / jnp.maximum(nrm, 1e-12)
    k_mod = k * (1.0 + (a - 1.0) * ka_ref[...])
    bonus = _dot_split(r * k_mod * rk_ref[...], ones) * v
    r_o[0] = r
    w_o[0] = decay
    k_o[0] = k_mod
    v_o[0] = v
    kk_o[0] = kk
    be_o[0] = kk * a
    bo_o[0] = _lanes_to_parity_major(bonus).astype(bo_o.dtype)
    g_o[0] = g.astype(g_o.dtype)


def _rw_prep(p_all, prm, layer, *, ts=512):
    B, S, _ = p_all.shape
    cb = P_RW_OFF // RW_COLS
    hb = ts // SUBLANES
    vec = lambda n: pl.BlockSpec((None, 1, n), lambda b, s: (layer, 0, 0))
    mat = lambda m, n: pl.BlockSpec((None, m, n), lambda b, s: (layer, 0, 0))
    out = jax.ShapeDtypeStruct((B, S, D_RWKV), F32)
    ospec = pl.BlockSpec((1, ts, D_RWKV), lambda b, s: (b, s, 0))
    return pl.pallas_call(
        functools.partial(_rw_prep_kernel, ts=ts),
        out_shape=[out] * 6 + [jax.ShapeDtypeStruct((B, S, D_RWKV), BF16)] * 2,
        grid=(B, S // ts),
        in_specs=[
            pl.BlockSpec((1, ts, RW_COLS), lambda b, s: (b, s, cb)),
            pl.BlockSpec((1, SUBLANES, RW_COLS), lambda b, s: (b, jnp.maximum(s * hb - 1, 0), cb)),
            vec(RW_COLS), vec(D_RWKV), mat(RW_DECAY_LORA, D_RWKV), vec(D_RWKV), mat(RW_A_LORA, D_RWKV),
            mat(RW_GATE_LORA, D_RWKV), vec(D_RWKV), vec(D_RWKV), vec(D_RWKV),
            pl.BlockSpec((D_RWKV, D_RWKV), lambda b, s: (0, 0)),
        ],
        out_specs=[ospec] * 8,
        compiler_params=_params("parallel", "parallel"),
        name="rw_prep",
    )(p_all, p_all, prm["mu"], prm["w0"], prm["w2"], prm["a0"], prm["a2"], prm["g2"], prm["k_k"],
      prm["k_a"], prm["r_k"], prm["ones"])


RW_PAIRS = D_RWKV // LANES


def _rw_scan_kernel(r_ref, w_ref, k_ref, v_ref, kk_ref, be_ref, bo_ref, g_ref, gng_ref, gnb_ref, ones_ref,
                    pair_ref, half_ref, o_ref, st_ref, vc_ref, sr_ref, y_ref, *, ts, nb):
    s = pl.program_id(1)

    @pl.when(s == 0)
    def _():
        st_ref[...] = jnp.zeros_like(st_ref)

    side, stack = 2, RW_PAIRS // 2
    rows, width = stack * HEAD_DIM, side * LANES
    lane = lax.broadcasted_iota(jnp.int32, (rows, width), 1)
    row = lax.broadcasted_iota(jnp.int32, (rows, width), 0)
    eye = jnp.where(lane % HEAD_DIM == row % HEAD_DIM, 1.0, 0.0)
    ones_blk = pair_ref[...]

    def spread(x8, j):
        blocks = []
        for st in range(stack):
            lanes = jnp.concatenate([x8[j:j + 1, (sd * stack + st) * LANES:(sd * stack + st + 1) * LANES]
                                     for sd in range(side)], axis=1)
            blocks.append(jnp.broadcast_to(lanes, (HEAD_DIM, width)))
        return jnp.concatenate(blocks, axis=0)

    def step(i, carry):
        base = pl.multiple_of(i * SUBLANES, SUBLANES)
        tiles = [[ref[b, pl.ds(base, SUBLANES), :] for ref in (kk_ref, w_ref, be_ref, k_ref, r_ref, v_ref)]
                 for b in range(nb)]
        for b in range(nb):
            lhs = jnp.concatenate([eye * spread(tiles[b][5], j) for j in range(SUBLANES)], axis=0)
            vc_ref[b] = _dot(lhs.astype(BF16), ones_blk)
        sts = [st_ref[b] for b in range(nb)]
        for j in range(SUBLANES):
            for b in range(nb):
                kk8, w8, be8, k8, r8, _ = tiles[b]
                st = sts[b]
                sa = _dot((st * spread(kk8, j)).astype(BF16), ones_blk)
                st = (st * spread(w8, j) - sa * spread(be8, j)
                      + vc_ref[b, j * rows:(j + 1) * rows, :] * spread(k8, j))
                sr = (st * spread(r8, j)).astype(BF16)
                for sd in range(side):
                    lo = (j * side + sd) * rows
                    sr_ref[b, lo:lo + rows, :] = sr[:, sd * LANES:(sd + 1) * LANES]
                sts[b] = st
        for b in range(nb):
            st_ref[b] = sts[b]
            yt = _dot_nt(half_ref[...], sr_ref[b])
            per = side * rows
            y_ref[b, pl.ds(base, SUBLANES), :] = jnp.concatenate(
                [jnp.concatenate([yt[0:1, j * per:(j + 1) * per], yt[1:2, j * per:(j + 1) * per]], axis=1)
                 for j in range(SUBLANES)], axis=0)
        return carry

    lax.fori_loop(0, ts // SUBLANES, step, 0)

    ones = ones_ref[...]
    for b in range(nb):
        y = y_ref[b]
        mean = _dot_split(y, ones) * (1.0 / HEAD_DIM)
        yc = y - mean
        var = _dot_split(yc * yc, ones) * (1.0 / HEAD_DIM)
        yn = yc * lax.rsqrt(var + RW_GN_EPS) * gng_ref[...] + gnb_ref[...]
        o_ref[b] = ((yn + bo_ref[b]) * g_ref[b]).astype(o_ref.dtype)


def _rw_scan(r, w, k, v, kk, be, bo, g, prm, layer, *, ts=128):
    B, S, _ = r.shape
    nb = 4 if B % 4 == 0 else (2 if B % 2 == 0 else 1)
    rows, width = RW_PAIRS // 2 * HEAD_DIM, 2 * LANES
    seq = pl.BlockSpec((nb, ts, D_RWKV), lambda b, s: (b, s, 0))
    vec = pl.BlockSpec((None, 1, D_RWKV), lambda b, s: (layer, 0, 0))
    return pl.pallas_call(
        functools.partial(_rw_scan_kernel, ts=ts, nb=nb),
        out_shape=jax.ShapeDtypeStruct((B, S, D_RWKV), BF16),
        grid=(B // nb, S // ts),
        in_specs=[seq] * 8 + [vec, vec, pl.BlockSpec((D_RWKV, D_RWKV), lambda b, s: (0, 0)),
                              pl.BlockSpec((width, width), lambda b, s: (0, 0)),
                              pl.BlockSpec((SUBLANES, LANES), lambda b, s: (0, 0))],
        out_specs=seq,
        scratch_shapes=[pltpu.VMEM((nb, rows, width), F32), pltpu.VMEM((nb, SUBLANES * rows, width), F32),
                        pltpu.VMEM((nb, SUBLANES * RW_PAIRS * HEAD_DIM, LANES), BF16),
                        pltpu.VMEM((nb, ts, D_RWKV), F32)],
        compiler_params=_params("parallel", "arbitrary"),
        name="rw_scan",
    )(r, w, k, v, kk, be, bo, g, prm["gn_g"], prm["gn_b"], prm["ones"], prm["ones"][:width, :width],
      prm["ones"][:SUBLANES * HEAD_DIM:HEAD_DIM, :LANES])


def _rope(x, cos, sin):
    half = ROPE_DIM // 2
    w = x.shape[-1]
    d = lax.broadcasted_iota(jnp.int32, x.shape, 1) % HEAD_DIM
    partner = jnp.where(d < half, pltpu.roll(x, w - half, 1), pltpu.roll(x, half, 1))
    return x * cos + partner * sin


def _nsa_prep_kernel(p_ref, cos_ref, sin_ref, gb_ref, q_o, kc_o, vc_o, ks_o, vs_o, kw_o, vw_o, gt_o):
    cos = cos_ref[...]
    sin = sin_ref[...]
    for hq in range(NSA_HEADS // 4):
        x = p_ref[0, :, hq * KV_SLOT:(hq + 1) * KV_SLOT]
        qr = _rope(x, cos, sin) * (HEAD_DIM ** -0.5)
        for j in range(4):
            q_o[0, 4 * hq + j] = qr[:, j * HEAD_DIM:(j + 1) * HEAD_DIM].astype(BF16)
    for i, ref in enumerate((kc_o, vc_o, ks_o, vs_o, kw_o, vw_o)):
        part = p_ref[0, :, D_NSA + i * KV_SLOT:D_NSA + (i + 1) * KV_SLOT]
        if i in (2, 4):
            part = _rope(part, cos, sin)
        if i in (3, 5):
            part = part.T
            for h in range(NSA_KV_HEADS):
                ref[0, h] = part[h * HEAD_DIM:(h + 1) * HEAD_DIM, :].astype(BF16)
        else:
            for h in range(NSA_KV_HEADS):
                ref[0, h] = part[:, h * HEAD_DIM:(h + 1) * HEAD_DIM].astype(BF16)
    gts = _sigmoid(p_ref[0, :, NSA_GATE_OFF:NSA_GATE_OFF + LANES] + gb_ref[...]).T
    for h in range(NSA_KV_HEADS):
        gt_o[0, h] = gts[h * 12:(h + 1) * 12, :]


def _nsa_prep(p_all, cos, sin, gate_b, layer, *, ts=512):
    B, S, _ = p_all.shape
    H = NSA_KV_HEADS
    k_shape = jax.ShapeDtypeStruct((B, H, S, HEAD_DIM), BF16)
    k_spec = pl.BlockSpec((1, H, ts, HEAD_DIM), lambda b, s: (b, 0, s, 0))
    vt_shape = jax.ShapeDtypeStruct((B, H, HEAD_DIM, S), BF16)
    vt_spec = pl.BlockSpec((1, H, HEAD_DIM, ts), lambda b, s: (b, 0, 0, s))
    return pl.pallas_call(
        _nsa_prep_kernel,
        out_shape=[jax.ShapeDtypeStruct((B, NSA_HEADS, S, HEAD_DIM), BF16),
                   k_shape, k_shape, k_shape, vt_shape, k_shape, vt_shape,
                   jax.ShapeDtypeStruct((B, H, 12, S), F32)],
        grid=(B, S // ts),
        in_specs=[
            pl.BlockSpec((1, ts, NSA_PAD), lambda b, s: (b, s, P_NSA_OFF // NSA_PAD)),
            pl.BlockSpec((ts, KV_SLOT), lambda b, s: (s, 0)),
            pl.BlockSpec((ts, KV_SLOT), lambda b, s: (s, 0)),
            pl.BlockSpec((None, 1, LANES), lambda b, s: (layer, 0, 0)),
        ],
        out_specs=[pl.BlockSpec((1, NSA_HEADS, ts, HEAD_DIM), lambda b, s: (b, 0, s, 0)),
                   k_spec, k_spec, k_spec, vt_spec, k_spec, vt_spec,
                   pl.BlockSpec((1, H, 12, ts), lambda b, s: (b, 0, 0, s))],
        compiler_params=_params("parallel", "parallel"),
        name="nsa_prep",
    )(p_all, cos, sin, gate_b)


def _gelu_tanh(x):
    return 0.5 * x * (1.0 + jnp.tanh(float(np.sqrt(2.0 / np.pi)) * (x + 0.044715 * (x * x * x))))


def _nsa_cmp_kernel(kc_ref, vc_ref, kw1_ref, kw2_ref, kpe_ref, vw1_ref, vw2_ref, vpe_ref, cos_ref, sin_ref,
                    k_o, v_o):
    def hidden(g, w1_ref, pe_ref):
        half = CMP_STRIDE * HEAD_DIM
        first = _dot(g, w1_ref[:half, :])
        second = _dot(g, w1_ref[half:, :])
        n = first.shape[0]
        bias = _dot(pe_ref[...], w1_ref[...])[0:1]
        return _gelu_tanh(first + pltpu.roll(second, n - 1, 0) + bias).astype(BF16)

    k = _dot(hidden(kc_ref[0, 0], kw1_ref, kpe_ref), kw2_ref[...])
    k_o[0, 0] = _rope(k, cos_ref[...], sin_ref[...])[:, :HEAD_DIM].astype(BF16)
    v_o[0, 0] = _dot_nt(vw2_ref[...], hidden(vc_ref[0, 0], vw1_ref, vpe_ref)).astype(BF16)


def _nsa_compress(kc, vc, prm, cos_c, sin_c, layer):
    B, H, S, _ = kc.shape
    ng = S // CMP_STRIDE
    gw = CMP_STRIDE * HEAD_DIM
    g_k = kc.reshape(B, H, ng, gw)
    g_v = vc.reshape(B, H, ng, gw)
    gspec = pl.BlockSpec((1, 1, ng, gw), lambda b, h: (b, h, 0, 0))
    w1 = pl.BlockSpec((None, 2 * gw, CMP_HIDDEN), lambda b, h: (layer, 0, 0))
    w2 = pl.BlockSpec((None, CMP_HIDDEN, LANES), lambda b, h: (layer, 0, 0))
    pe = pl.BlockSpec((None, SUBLANES, 2 * gw), lambda b, h: (layer, 0, 0))
    tab = pl.BlockSpec((ng, LANES), lambda b, h: (0, 0))
    return pl.pallas_call(
        _nsa_cmp_kernel,
        out_shape=[jax.ShapeDtypeStruct((B, H, ng, HEAD_DIM), BF16), jax.ShapeDtypeStruct((B, H, HEAD_DIM, ng), BF16)],
        grid=(B, H),
        in_specs=[gspec, gspec, w1, w2, pe, w1,
                  pl.BlockSpec((None, HEAD_DIM, CMP_HIDDEN), lambda b, h: (layer, 0, 0)), pe, tab, tab],
        out_specs=[pl.BlockSpec((1, 1, ng, HEAD_DIM), lambda b, h: (b, h, 0, 0)),
                   pl.BlockSpec((1, 1, HEAD_DIM, ng), lambda b, h: (b, h, 0, 0))],
        compiler_params=_params("parallel", "parallel"),
        name="nsa_compress",
    )(g_k, g_v, prm["k_w1"], prm["k_w2"], prm["k_pe"], prm["v_w1"], prm["v_w2"], prm["v_pe"], cos_c, sin_c)


def _nsa_attn_kernel(q_ref, kcmp_ref, vcmp_ref, ks_ref, vs_ref, kw_ref, vw_ref, gt_ref, ov_ref, o_ref,
                     m_ref, l_ref, acc_ref, sb_ref, *, tq):
    G = NSA_GQA
    tk = tq
    qi = pl.program_id(2)
    q0 = qi * tq
    q_all = q_ref[0].reshape(G * tq, HEAD_DIM)
    pos = q0 + lax.broadcasted_iota(jnp.int32, (1, tq), 1)
    tiny = jnp.finfo(F32).tiny
    heads = lambda x: jnp.tile(x, (1, G))
    split = lambda x: [x[:, g * tq:(g + 1) * tq] for g in range(G)]

    ncp = kcmp_ref.shape[2]
    n_idx = lax.broadcasted_iota(jnp.int32, (ncp, 1), 0)
    cbias = jnp.where((n_idx * CMP_STRIDE + (CMP_BLOCK - 1)) <= pos, 0.0, NEG)
    any_cmp = jnp.where(pos >= CMP_BLOCK - 1, 1.0, 0.0)
    s = _dot_nt(kcmp_ref[0, 0], q_all) + heads(cbias)
    e = jnp.exp(s - jnp.max(s, axis=0, keepdims=True))
    p = e * (heads(any_cmp) / jnp.maximum(jnp.sum(e, axis=0, keepdims=True), tiny))
    o_cmp = split(_dot(vcmp_ref[0, 0], p.astype(BF16)))
    ps = split(p)
    psum = (ps[0] + ps[1]) + (ps[2] + ps[3])
    p_hi = psum.astype(BF16)
    p_lo = (psum - p_hi.astype(F32)).astype(BF16)
    imp = _dot(ov_ref[...], p_hi) + _dot(ov_ref[...], p_lo)

    wk = min(WINDOW + tq, ks_ref.shape[2])
    w0 = pl.multiple_of(jnp.maximum(q0 + tq - wk, 0), tk)
    wpos = w0 + lax.broadcasted_iota(jnp.int32, (wk, 1), 0)
    wbias = jnp.where((wpos <= pos) & (wpos > pos - WINDOW), 0.0, NEG)
    s = _dot_nt(kw_ref[0, 0, pl.ds(w0, wk), :], q_all) + heads(wbias)
    e = jnp.exp(s - jnp.max(s, axis=0, keepdims=True))
    pv = _dot(vw_ref[0, 0, :, pl.ds(w0, wk)], e.astype(BF16))
    o_win = split(pv / jnp.maximum(jnp.sum(e, axis=0, keepdims=True), tiny))

    n_sel = ov_ref.shape[0]
    blk = lax.broadcasted_iota(jnp.int32, (n_sel, 1), 0)
    cur = pos // SEL_BLOCK
    forced = (blk == 0) | (blk == cur) | (blk == cur - 1)
    valid = blk <= cur
    score = jnp.where(valid, jnp.where(forced, FORCE_SCORE, imp), -jnp.inf)
    groups = [score[g * SUBLANES:(g + 1) * SUBLANES, :] for g in range(n_sel // SUBLANES)]
    ranks = [jnp.zeros((SUBLANES, tq), F32) for _ in groups]
    sub = lax.broadcasted_iota(jnp.int32, (SUBLANES, 1), 0)
    for i in range(n_sel):
        ci = jnp.broadcast_to(score[i:i + 1, :], (SUBLANES, tq))
        for g, sg in enumerate(groups):
            if g > i // SUBLANES:
                beats = ci >= sg
            elif g < i // SUBLANES:
                beats = ci > sg
            else:
                beats = (ci > sg) | ((ci == sg) & (sub > i % SUBLANES))
            ranks[g] = ranks[g] + jnp.where(beats, 1.0, 0.0)
    rank = jnp.concatenate(ranks, axis=0)
    chosen = (rank < float(SEL_TOPK)) & valid
    sel_bias = jnp.where(chosen, 0.0, NEG)
    for j in range(n_sel):
        sb_ref[j] = jnp.broadcast_to(sel_bias[j:j + 1, :], (SUBLANES, tq))

    kcol = lax.broadcasted_iota(jnp.int32, (tk, 1), 0)

    def sel_update(kt, n, extra=None):
        bpt = n // SEL_BLOCK
        k0 = pl.multiple_of(kt * tk, tk)
        s = _dot_nt(ks_ref[0, 0, pl.ds(k0, n), :], q_all)
        b0 = kt * (tk // SEL_BLOCK)
        bias = jnp.concatenate([jnp.tile(sb_ref[b0 + j], (SEL_BLOCK // SUBLANES, 1)) for j in range(bpt)],
                               axis=0)
        if extra is not None:
            bias = bias + extra
        vt = vs_ref[0, 0, :, pl.ds(k0, n)]
        s = s + heads(bias)
        m_prev = m_ref[...]
        m_new = jnp.maximum(m_prev, jnp.max(s, axis=0, keepdims=True))
        p = jnp.exp(s - m_new)
        corr = jnp.exp(m_prev - m_new)
        l_ref[...] = corr * l_ref[...] + jnp.sum(p, axis=0, keepdims=True)
        acc_ref[...] = corr * acc_ref[...] + _dot(vt, p.astype(BF16))
        m_ref[...] = m_new

    m_ref[...] = jnp.full_like(m_ref, NEG)
    l_ref[...] = jnp.zeros_like(l_ref)
    acc_ref[...] = jnp.zeros_like(acc_ref)

    def sel_body(i, carry):
        sel_update(4 * i, 4 * tk)
        return carry

    lax.fori_loop(0, qi // 4, sel_body, 0)

    @pl.when(qi % 4 >= 2)
    def _():
        sel_update((qi // 4) * 4, 2 * tk)

    @pl.when(qi % 2 == 1)
    def _():
        sel_update(qi - 1, tk)

    causal = jnp.where(q0 + kcol <= pos, 0.0, NEG)
    sel_update(qi, tk, causal)
    o_sel = split(acc_ref[...] / jnp.maximum(l_ref[...], tiny))

    gt = gt_ref[0, 0]
    outs = [gt[g:g + 1] * o_cmp[g] + gt[G + g:G + g + 1] * o_sel[g] + gt[2 * G + g:2 * G + g + 1] * o_win[g]
            for g in range(G)]
    for half in range(G // 2):
        pair = jnp.concatenate(outs[2 * half:2 * half + 2], axis=0)
        o_ref[0, :, half * LANES:(half + 1) * LANES] = pair.T.astype(o_ref.dtype)


def _nsa_attention(q, k_cmp, v_cmp_t, ks, vs_t, kw, vw_t, gates_t, overlap_t, *, tq=256):
    B, _, S, _ = q.shape
    H, G = NSA_KV_HEADS, NSA_GQA
    tq = min(tq, S)
    assert WINDOW == 2 * tq or S <= tq
    ncp = k_cmp.shape[2]
    n_cmp = (S - CMP_BLOCK) // CMP_STRIDE + 1
    keys = pl.BlockSpec((1, 1, S, HEAD_DIM), lambda b, h, i: (b, h, 0, 0))
    vals = pl.BlockSpec((1, 1, HEAD_DIM, S), lambda b, h, i: (b, h, 0, 0))
    return pl.pallas_call(
        functools.partial(_nsa_attn_kernel, tq=tq),
        out_shape=jax.ShapeDtypeStruct((B, S, D_NSA), BF16),
        grid=(B, H, S // tq),
        in_specs=[
            pl.BlockSpec((1, G, tq, HEAD_DIM), lambda b, h, i: (b, h, i, 0)),
            pl.BlockSpec((1, 1, ncp, HEAD_DIM), lambda b, h, i: (b, h, 0, 0)),
            pl.BlockSpec((1, 1, HEAD_DIM, ncp), lambda b, h, i: (b, h, 0, 0)),
            keys, vals, keys, vals,
            pl.BlockSpec((1, 1, 12, tq), lambda b, h, i: (b, h, 0, i)),
            pl.BlockSpec(overlap_t.shape, lambda b, h, i: (0, 0)),
        ],
        out_specs=pl.BlockSpec((1, tq, G * HEAD_DIM), lambda b, h, i: (b, i, h)),
        scratch_shapes=[pltpu.VMEM((1, G * tq), F32), pltpu.VMEM((1, G * tq), F32),
                        pltpu.VMEM((HEAD_DIM, G * tq), F32), pltpu.VMEM((S // SEL_BLOCK, SUBLANES, tq), F32)],
        compiler_params=_params("parallel", "parallel", "arbitrary"),
        name="nsa_attention",
    )(q, k_cmp, v_cmp_t, ks, vs_t, kw, vw_t, gates_t, overlap_t)


def _rope_tables(pos, heads):
    half = ROPE_DIM // 2
    inv_freq = ROPE_THETA ** (-jnp.arange(half, dtype=F32) / half)
    ang = pos.astype(F32)[:, None] * inv_freq
    cos, sin = jnp.cos(ang), jnp.sin(ang)
    n = pos.shape[0]
    rest = HEAD_DIM - ROPE_DIM
    cos_h = jnp.concatenate([cos, cos, jnp.ones((n, rest), F32)], axis=1)
    sin_h = jnp.concatenate([-sin, sin, jnp.zeros((n, rest), F32)], axis=1)
    return jnp.tile(cos_h, (1, heads)), jnp.tile(sin_h, (1, heads))


def _overlap_matrix(S, ncp):
    n_cmp = (S - CMP_BLOCK) // CMP_STRIDE + 1
    n_sel = S // SEL_BLOCK
    cmp_start = np.arange(n_cmp) * CMP_STRIDE
    sel_start = np.arange(n_sel) * SEL_BLOCK
    ov = np.clip(np.minimum(cmp_start[:, None] + CMP_BLOCK, sel_start[None, :] + SEL_BLOCK)
                 - np.maximum(cmp_start[:, None], sel_start[None, :]), 0, None) / CMP_BLOCK
    full = np.zeros((n_sel, ncp), np.float32)
    full[:, :n_cmp] = ov.T
    return jnp.asarray(full, BF16)


RW_HEAD_ORDER = tuple(2 * (e % RW_PAIRS) + e // RW_PAIRS for e in range(RW_HEADS))


def _parity_major(a, axis):
    axis = axis % a.ndim
    shape = a.shape
    a = a.reshape(shape[:axis] + (RW_HEADS, HEAD_DIM) + shape[axis + 1:])
    a = jnp.take(a, jnp.asarray(RW_HEAD_ORDER), axis=axis)
    return a.reshape(shape)


def _block_ones(n):
    idx = np.arange(n) // HEAD_DIM
    return jnp.asarray((idx[:, None] == idx[None, :]).astype(np.float32), BF16)


def _w_in_layout(w_in):
    L = w_in.shape[0]
    rw = w_in[:, :, :RW_COLS]
    pool = w_in[:, :, RW_COLS:RW_COLS + D_POOL]
    nsa = w_in[:, :, RW_COLS + D_POOL:]
    zeros = lambda n: jnp.zeros((L, D_MODEL, n), w_in.dtype)
    segs = [nsa[:, :, :D_NSA]]
    for i in range(6):
        segs += [nsa[:, :, D_NSA + i * NSA_KV:D_NSA + (i + 1) * NSA_KV], zeros(KV_SLOT - NSA_KV)]
    gates = nsa[:, :, D_NSA + 6 * NSA_KV:]
    gates = gates.reshape(L, D_MODEL, NSA_KV_HEADS, NSA_GQA, 3).transpose(0, 1, 2, 4, 3)
    segs += [gates.reshape(L, D_MODEL, 3 * NSA_HEADS),
             zeros(NSA_PAD - NSA_GATE_OFF - 3 * NSA_HEADS)]
    return jnp.concatenate([rw] + segs + [pool], axis=-1).astype(BF16)


def _gate_bias_layout(gate_b):
    L = gate_b.shape[0]
    gb = gate_b.reshape(L, NSA_KV_HEADS, NSA_GQA, 3).transpose(0, 1, 3, 2).reshape(L, 1, 3 * NSA_HEADS)
    return jnp.pad(gb, ((0, 0), (0, 0), (0, LANES - 3 * NSA_HEADS)))


def kernel(x, ffn1_w_up, ffn1_w_down, ln1_g, ln1_b, w_in, rw_mu, rw_w0, rw_w2, rw_a0, rw_a2, rw_g2, rw_k_k,
           rw_k_a, rw_r_k, rw_gn_g, rw_gn_b, pool_w, pool_b, pool_scale, nsa_cmp_pe_k, nsa_cmp_pe_v,
           nsa_cmp_k_w1, nsa_cmp_k_w2, nsa_cmp_v_w1, nsa_cmp_v_w2, nsa_gate_b, w_out, ln2_g, ln2_b,
           ffn2_w_up, ffn2_w_down, ln3_g, ln3_b):
    prm = _prepare(x.shape[1], ffn1_w_up, ffn1_w_down, ln1_g, ln1_b, w_in, rw_mu, rw_w0, rw_w2, rw_a0, rw_a2,
                   rw_g2, rw_k_k, rw_k_a, rw_r_k, rw_gn_g, rw_gn_b, pool_w, pool_b, pool_scale, nsa_cmp_pe_k,
                   nsa_cmp_pe_v, nsa_cmp_k_w1, nsa_cmp_k_w2, nsa_cmp_v_w1, nsa_cmp_v_w2, nsa_gate_b, w_out,
                   ln2_g, ln2_b, ffn2_w_up, ffn2_w_down, ln3_g, ln3_b)
    B, S, D = x.shape
    h = x.reshape(B * S, D)
    for l in range(w_in.shape[0]):
        h = _layer(h, prm, l, B, S)
    return h.reshape(B, S, D)


def _prepare(S, ffn1_w_up, ffn1_w_down, ln1_g, ln1_b, w_in, rw_mu, rw_w0, rw_w2, rw_a0, rw_a2, rw_g2, rw_k_k,
             rw_k_a, rw_r_k, rw_gn_g, rw_gn_b, pool_w, pool_b, pool_scale, nsa_cmp_pe_k, nsa_cmp_pe_v,
             nsa_cmp_k_w1, nsa_cmp_k_w2, nsa_cmp_v_w1, nsa_cmp_v_w2, nsa_gate_b, w_out, ln2_g, ln2_b,
             ffn2_w_up, ffn2_w_down, ln3_g, ln3_b):
    L = w_in.shape[0]
    fpad = D_FF_PAD - D_FF

    def up(w):
        a = jnp.pad(w[:, :, :D_FF], ((0, 0), (0, 0), (0, fpad))).astype(BF16)
        b = jnp.pad(w[:, :, D_FF:], ((0, 0), (0, 0), (0, fpad))).astype(BF16)
        return a, b

    def down(w):
        return jnp.pad(w, ((0, 0), (0, fpad), (0, 0))).astype(BF16)

    row = lambda v: v[:, None, :]
    f1a, f1b = up(ffn1_w_up)
    f2a, f2b = up(ffn2_w_up)
    w_out_b = w_out.astype(BF16)
    gw = CMP_BLOCK * HEAD_DIM
    pad_w2 = lambda w: jnp.pad(w, ((0, 0), (0, 0), (0, LANES - HEAD_DIM))).astype(BF16)
    pe_rows = lambda pe: jnp.broadcast_to(pe.reshape(L, 1, gw), (L, SUBLANES, gw)).astype(BF16)
    ncp = S // CMP_STRIDE
    cos_t, sin_t = _rope_tables(jnp.arange(S), 4)
    cos_c, sin_c = _rope_tables(jnp.arange(ncp) * CMP_STRIDE + (CMP_BLOCK - 1), 2)
    return dict(
        ffn1=(f1a, f1b, down(ffn1_w_down), row(ln1_g), row(ln1_b)),
        ffn2=(f2a, f2b, down(ffn2_w_down), row(ln3_g), row(ln3_b)),
        w_in=_w_in_layout(w_in),
        w_out=(_parity_major(w_out_b[:, :D_RWKV], 1), w_out_b[:, D_RWKV:D_RWKV + D_POOL],
               w_out_b[:, D_RWKV + D_POOL:]),
        ln2=(row(ln2_g), row(ln2_b)),
        rw=dict(mu=row(rw_mu), w0=row(rw_w0), w2=rw_w2.astype(BF16), a0=row(rw_a0), a2=rw_a2.astype(BF16),
                g2=_parity_major(rw_g2, -1).astype(BF16), k_k=row(rw_k_k), k_a=row(rw_k_a),
                r_k=rw_r_k.reshape(L, 1, D_RWKV), gn_g=row(_parity_major(rw_gn_g, -1)),
                gn_b=row(_parity_major(rw_gn_b, -1)), ones=_block_ones(D_RWKV)),
        cmp=dict(k_w1=nsa_cmp_k_w1.reshape(L, gw, CMP_HIDDEN).astype(BF16), k_w2=pad_w2(nsa_cmp_k_w2),
                 k_pe=pe_rows(nsa_cmp_pe_k),
                 v_w1=nsa_cmp_v_w1.reshape(L, gw, CMP_HIDDEN).astype(BF16), v_w2=nsa_cmp_v_w2.transpose(0, 2, 1).astype(BF16),
                 v_pe=pe_rows(nsa_cmp_pe_v)),
        gate_b=_gate_bias_layout(nsa_gate_b),
        pool=(pool_w.astype(BF16), row(pool_b), row(pool_scale)),
        rope=(cos_t, sin_t), rope_cmp=(cos_c, sin_c), overlap=_overlap_matrix(S, ncp))


def _mixers(p_all, prm, l):
    r, w, k, v, kk, be, bo, g = _rw_prep(p_all, prm["rw"], l)
    y_rw = _rw_scan(r, w, k, v, kk, be, bo, g, prm["rw"], l)
    y_pool = _pool_mix(p_all, *prm["pool"], l)
    q, kc, vc, ks, vs, kw, vw, gates = _nsa_prep(p_all, *prm["rope"], prm["gate_b"], l)
    k_cmp, v_cmp = _nsa_compress(kc, vc, prm["cmp"], *prm["rope_cmp"], l)
    y_nsa = _nsa_attention(q, k_cmp, v_cmp, ks, vs, kw, vw, gates, prm["overlap"])
    return y_rw, y_pool, y_nsa


def _layer(h, prm, l, B, S):
    T = B * S
    h = _ffn_ln(h, *prm["ffn1"], l)
    p_all = _in_proj(h, prm["w_in"], l).reshape(B, S, P_COLS)
    y_rw, y_pool, y_nsa = _mixers(p_all, prm, l)
    h = _out_proj_ln(h, y_rw.reshape(T, D_RWKV), y_pool.reshape(T, D_POOL), y_nsa.reshape(T, D_NSA),
                     prm["w_out"], *prm["ln2"], l)
    return _ffn_ln(h, *prm["ffn2"], l)
```

```python
import functools

import numpy as np
import jax
import jax.numpy as jnp
from jax import lax
from jax.experimental import pallas as pl
from jax.experimental.pallas import tpu as pltpu

F32 = jnp.float32
BF16 = jnp.bfloat16

D_MODEL = 2048
DEPTH = 4
HEAD_DIM = 64
D_RWKV = 768
D_POOL = 512
D_NSA = 768
RW_HEADS = 12
RW_DECAY_LORA = 64
RW_A_LORA = 64
RW_GATE_LORA = 128
RW_GN_EPS = 64e-5
RW_COLS = 3 * D_RWKV + RW_DECAY_LORA + RW_A_LORA + RW_GATE_LORA
POOL_WINDOWS = (2, 4, 8, 16)
POOL_GROUP = 128
NSA_HEADS = 12
NSA_KV_HEADS = 3
NSA_GQA = 4
NSA_KV = 192
NSA_COLS = D_NSA + 6 * NSA_KV + 3 * NSA_HEADS
CMP_BLOCK = 32
CMP_STRIDE = 16
CMP_HIDDEN = 256
SEL_BLOCK = 64
SEL_TOPK = 16
FORCE_SCORE = 1e9
WINDOW = 512
ROPE_THETA = 500000.0
ROPE_DIM = 16
D_FF = 5504
IN_COLS = RW_COLS + D_POOL + NSA_COLS
ALPHA = (2 * DEPTH) ** 0.25
LN_EPS = 1e-5

LANES = 128
SUBLANES = 8
VMEM_LIMIT = 56 * 1024 * 1024

KV_SLOT = 2 * LANES
NSA_PAD = 2560
P_RW_OFF = 0
P_NSA_OFF = RW_COLS
P_POOL_OFF = RW_COLS + NSA_PAD
P_COLS = RW_COLS + NSA_PAD + D_POOL
NSA_GATE_OFF = D_NSA + 6 * KV_SLOT
D_FF_PAD = 5632

NEG = -1e30


def _params(*sem):
    return pltpu.CompilerParams(dimension_semantics=sem, vmem_limit_bytes=VMEM_LIMIT)


def _layer_norm(z, g, b):
    mu = jnp.mean(z, axis=-1, keepdims=True)
    zc = z - mu
    var = jnp.mean(zc * zc, axis=-1, keepdims=True)
    return zc * lax.rsqrt(var + LN_EPS) * g + b


def _dot(a, b):
    return jnp.dot(a, b, preferred_element_type=F32)


def _dot_nt(a, b):
    return lax.dot_general(a, b, (((1,), (1,)), ((), ())), preferred_element_type=F32)


def _dot_split(x, w):
    hi = x.astype(BF16)
    lo = (x - hi.astype(F32)).astype(BF16)
    return _dot(hi, w) + _dot(lo, w)


FFN_LN_ROWS = 256
FFN_MM_ROWS = 512


def _ffn_kernel(x_ref, wa_ref, wb_ref, wd_ref, g_ref, b_ref, o_ref, xb_ref):
    k = pl.program_id(1)

    @pl.when(k == 0)
    def _():
        xb_ref[...] = x_ref[...].astype(BF16)
        o_ref[...] = jnp.zeros_like(o_ref)

    for c in range(o_ref.shape[0] // FFN_MM_ROWS):
        rows = slice(c * FFN_MM_ROWS, (c + 1) * FFN_MM_ROWS)
        xb = xb_ref[rows, :]
        a = _dot(xb, wa_ref[...])
        b = _dot(xb, wb_ref[...])
        h = (a / (1.0 + jnp.exp(-a))) * b
        o_ref[rows, :] += _dot(h.astype(BF16), wd_ref[...])

    @pl.when(k == pl.num_programs(1) - 1)
    def _():
        for c in range(o_ref.shape[0] // FFN_LN_ROWS):
            rows = slice(c * FFN_LN_ROWS, (c + 1) * FFN_LN_ROWS)
            z = ALPHA * x_ref[rows, :] + 0.5 * o_ref[rows, :]
            o_ref[rows, :] = _layer_norm(z, g_ref[...], b_ref[...])


def _ffn_ln(x, wa, wb, wd, g, b, layer, *, tm=1024, tf=512):
    T, D = x.shape
    fp = wa.shape[-1]
    tm = min(tm, T)
    return pl.pallas_call(
        _ffn_kernel,
        out_shape=jax.ShapeDtypeStruct((T, D), F32),
        grid=(T // tm, fp // tf),
        in_specs=[
            pl.BlockSpec((tm, D), lambda i, k: (i, 0)),
            pl.BlockSpec((None, D, tf), lambda i, k: (layer, 0, k)),
            pl.BlockSpec((None, D, tf), lambda i, k: (layer, 0, k)),
            pl.BlockSpec((None, tf, D), lambda i, k: (layer, k, 0)),
            pl.BlockSpec((None, 1, D), lambda i, k: (layer, 0, 0)),
            pl.BlockSpec((None, 1, D), lambda i, k: (layer, 0, 0)),
        ],
        out_specs=pl.BlockSpec((tm, D), lambda i, k: (i, 0)),
        scratch_shapes=[pltpu.VMEM((tm, D), BF16)],
        compiler_params=_params("parallel", "arbitrary"),
        name="ffn_ln",
    )(x, wa, wb, wd, g, b)


def _inproj_kernel(x_ref, w_ref, o_ref, xb_ref):
    @pl.when(pl.program_id(1) == 0)
    def _():
        xb_ref[...] = x_ref[...].astype(BF16)

    o_ref[...] = _dot(xb_ref[...], w_ref[...])


def _in_proj(x, w, layer, *, tm=1024, tn=P_COLS // 4):
    T, D = x.shape
    n = w.shape[-1]
    tm = min(tm, T)
    return pl.pallas_call(
        _inproj_kernel,
        out_shape=jax.ShapeDtypeStruct((T, n), F32),
        grid=(T // tm, n // tn),
        in_specs=[
            pl.BlockSpec((tm, D), lambda i, j: (i, 0)),
            pl.BlockSpec((None, D, tn), lambda i, j: (layer, 0, j)),
        ],
        out_specs=pl.BlockSpec((tm, tn), lambda i, j: (i, j)),
        scratch_shapes=[pltpu.VMEM((tm, D), BF16)],
        compiler_params=_params("parallel", "arbitrary"),
        name="in_proj",
    )(x, w)


def _outproj_kernel(x_ref, yr_ref, yp_ref, yn_ref, wr_ref, wp_ref, wn_ref, g_ref, b_ref, o_ref):
    y = _dot(yr_ref[...], wr_ref[...]) + _dot(yp_ref[...], wp_ref[...]) + _dot(yn_ref[...], wn_ref[...])
    o_ref[...] = _layer_norm(ALPHA * x_ref[...] + y, g_ref[...], b_ref[...])


def _out_proj_ln(x, y_rw, y_pool, y_nsa, w_out, g, b, layer, *, tm=512):
    T, D = x.shape
    return pl.pallas_call(
        _outproj_kernel,
        out_shape=jax.ShapeDtypeStruct((T, D), F32),
        grid=(T // tm,),
        in_specs=[
            pl.BlockSpec((tm, D), lambda i: (i, 0)),
            pl.BlockSpec((tm, D_RWKV), lambda i: (i, 0)),
            pl.BlockSpec((tm, D_POOL), lambda i: (i, 0)),
            pl.BlockSpec((tm, D_NSA), lambda i: (i, 0)),
            pl.BlockSpec((None, D_RWKV, D), lambda i: (layer, 0, 0)),
            pl.BlockSpec((None, D_POOL, D), lambda i: (layer, 0, 0)),
            pl.BlockSpec((None, D_NSA, D), lambda i: (layer, 0, 0)),
            pl.BlockSpec((None, 1, D), lambda i: (layer, 0, 0)),
            pl.BlockSpec((None, 1, D), lambda i: (layer, 0, 0)),
        ],
        out_specs=pl.BlockSpec((tm, D), lambda i: (i, 0)),
        compiler_params=_params("parallel"),
        name="out_proj_ln",
    )(x, y_rw, y_pool, y_nsa, w_out[0], w_out[1], w_out[2], g, b)


POOL_HALO = 16


def _pool_kernel(p_ref, halo_ref, w_ref, b_ref, sc_ref, o_ref, xs_ref, *, ts):
    s = pl.program_id(1)
    x = p_ref[0]
    halo = jnp.where(s > 0, halo_ref[0], 0.0)
    xs_ref[0:POOL_HALO, :] = halo
    xs_ref[POOL_HALO:POOL_HALO + ts, :] = x
    t1 = (s * ts + 1 + lax.broadcasted_iota(jnp.int32, (ts, 1), 0)).astype(F32)
    for gi, win in enumerate(POOL_WINDOWS):
        c0 = gi * POOL_GROUP
        acc = x[:, c0:c0 + POOL_GROUP]
        for j in range(1, win):
            acc = acc + xs_ref[POOL_HALO - j:POOL_HALO - j + ts, c0:c0 + POOL_GROUP]
        pooled = acc / jnp.minimum(t1, float(win)) - x[:, c0:c0 + POOL_GROUP]
        z = _dot(pooled.astype(BF16), w_ref[gi]) + b_ref[:, c0:c0 + POOL_GROUP]
        o_ref[0, :, c0:c0 + POOL_GROUP] = (z * sc_ref[:, c0:c0 + POOL_GROUP]).astype(o_ref.dtype)


def _pool_mix(p_all, pool_w, pool_b, pool_scale, layer, *, ts=512):
    B, S, _ = p_all.shape
    cb = P_POOL_OFF // D_POOL
    hb = ts // POOL_HALO
    return pl.pallas_call(
        functools.partial(_pool_kernel, ts=ts),
        out_shape=jax.ShapeDtypeStruct((B, S, D_POOL), BF16),
        grid=(B, S // ts),
        in_specs=[
            pl.BlockSpec((1, ts, D_POOL), lambda b, s: (b, s, cb)),
            pl.BlockSpec((1, POOL_HALO, D_POOL), lambda b, s: (b, jnp.maximum(s * hb - 1, 0), cb)),
            pl.BlockSpec((None, 4, POOL_GROUP, POOL_GROUP), lambda b, s: (layer, 0, 0, 0)),
            pl.BlockSpec((None, 1, D_POOL), lambda b, s: (layer, 0, 0)),
            pl.BlockSpec((None, 1, D_POOL), lambda b, s: (layer, 0, 0)),
        ],
        out_specs=pl.BlockSpec((1, ts, D_POOL), lambda b, s: (b, s, 0)),
        scratch_shapes=[pltpu.VMEM((ts + POOL_HALO, D_POOL), F32)],
        compiler_params=_params("parallel", "parallel"),
        name="pool_mix",
    )(p_all, p_all, pool_w, pool_b, pool_scale)


def _softplus(z):
    return jnp.maximum(z, 0.0) + jnp.log1p(jnp.exp(-jnp.abs(z)))


def _sigmoid(z):
    return 1.0 / (1.0 + jnp.exp(-z))


def _lanes_to_parity_major(x):
    n_pairs = x.shape[1] // LANES
    low = lax.broadcasted_iota(jnp.int32, (1, LANES), 1) < HEAD_DIM

    def chunk(e, to_low):
        pair, parity = e % n_pairs, e // n_pairs
        src = x[:, pair * LANES:(pair + 1) * LANES]
        return src if (parity == 0) == to_low else pltpu.roll(src, HEAD_DIM, 1)

    return jnp.concatenate([jnp.where(low, chunk(2 * d, True), chunk(2 * d + 1, False))
                            for d in range(n_pairs)], axis=1)


def _rw_prep_kernel(p_ref, prev_ref, mu_ref, w0_ref, w2_ref, a0_ref, a2_ref, g2_ref, kk_ref, ka_ref,
                    rk_ref, ones_ref,
                    r_o, w_o, k_o, v_o, kk_o, be_o, bo_o, g_o, *, ts):
    s = pl.program_id(1)
    x = p_ref[0]
    last = jnp.where(s > 0, prev_ref[0][SUBLANES - 1:SUBLANES, :], 0.0)
    row = lax.broadcasted_iota(jnp.int32, (ts, 1), 0)
    shifted = jnp.where(row == 0, last, pltpu.roll(x, 1, 0))
    xm = x + (shifted - x) * mu_ref[...]
    c = D_RWKV
    r = xm[:, 0:c]
    k = xm[:, c:2 * c]
    v = xm[:, 2 * c:3 * c]
    lora = xm[:, 3 * c:3 * c + LANES]
    wl = lora[:, :RW_DECAY_LORA]
    al = lora[:, RW_DECAY_LORA:]
    gl = xm[:, 3 * c + LANES:]
    w = -_softplus(-(w0_ref[...] + _dot(jnp.tanh(wl).astype(BF16), w2_ref[...]))) - 0.5
    decay = jnp.exp(-jnp.exp(w))
    a = _sigmoid(a0_ref[...] + _dot(al.astype(BF16), a2_ref[...]))
    g = _dot(_sigmoid(gl).astype(BF16), g2_ref[...])
    ones = ones_ref[...]
    kk = k * kk_ref[...]
    nrm = jnp.sqrt(_dot_split(kk * kk, ones))
    kk = kk / jnp.maximum(nrm, 1e-12)
    k_mod = k * (1.0 + (a - 1.0) * ka_ref[...])
    bonus = _dot_split(r * k_mod * rk_ref[...], ones) * v
    r_o[0] = r
    w_o[0] = decay
    k_o[0] = k_mod
    v_o[0] = v
    kk_o[0] = kk
    be_o[0] = kk * a
    bo_o[0] = _lanes_to_parity_major(bonus).astype(bo_o.dtype)
    g_o[0] = g.astype(g_o.dtype)


def _rw_prep(p_all, prm, layer, *, ts=512):
    B, S, _ = p_all.shape
    cb = P_RW_OFF // RW_COLS
    hb = ts // SUBLANES
    vec = lambda n: pl.BlockSpec((None, 1, n), lambda b, s: (layer, 0, 0))
    mat = lambda m, n: pl.BlockSpec((None, m, n), lambda b, s: (layer, 0, 0))
    out = jax.ShapeDtypeStruct((B, S, D_RWKV), F32)
    ospec = pl.BlockSpec((1, ts, D_RWKV), lambda b, s: (b, s, 0))
    return pl.pallas_call(
        functools.partial(_rw_prep_kernel, ts=ts),
        out_shape=[out] * 6 + [jax.ShapeDtypeStruct((B, S, D_RWKV), BF16)] * 2,
        grid=(B, S // ts),
        in_specs=[
            pl.BlockSpec((1, ts, RW_COLS), lambda b, s: (b, s, cb)),
            pl.BlockSpec((1, SUBLANES, RW_COLS), lambda b, s: (b, jnp.maximum(s * hb - 1, 0), cb)),
            vec(RW_COLS), vec(D_RWKV), mat(RW_DECAY_LORA, D_RWKV), vec(D_RWKV), mat(RW_A_LORA, D_RWKV),
            mat(RW_GATE_LORA, D_RWKV), vec(D_RWKV), vec(D_RWKV), vec(D_RWKV),
            pl.BlockSpec((D_RWKV, D_RWKV), lambda b, s: (0, 0)),
        ],
        out_specs=[ospec] * 8,
        compiler_params=_params("parallel", "parallel"),
        name="rw_prep",
    )(p_all, p_all, prm["mu"], prm["w0"], prm["w2"], prm["a0"], prm["a2"], prm["g2"], prm["k_k"],
      prm["k_a"], prm["r_k"], prm["ones"])


RW_PAIRS = D_RWKV // LANES


def _rw_scan_kernel(r_ref, w_ref, k_ref, v_ref, kk_ref, be_ref, bo_ref, g_ref, gng_ref, gnb_ref, ones_ref,
                    pair_ref, half_ref, o_ref, st_ref, vc_ref, sr_ref, y_ref, *, ts, nb):
    s = pl.program_id(1)

    @pl.when(s == 0)
    def _():
        st_ref[...] = jnp.zeros_like(st_ref)

    side, stack = 2, RW_PAIRS // 2
    rows, width = stack * HEAD_DIM, side * LANES
    lane = lax.broadcasted_iota(jnp.int32, (rows, width), 1)
    row = lax.broadcasted_iota(jnp.int32, (rows, width), 0)
    eye = jnp.where(lane % HEAD_DIM == row % HEAD_DIM, 1.0, 0.0)
    ones_blk = pair_ref[...]

    def spread(x8, j):
        blocks = []
        for st in range(stack):
            lanes = jnp.concatenate([x8[j:j + 1, (sd * stack + st) * LANES:(sd * stack + st + 1) * LANES]
                                     for sd in range(side)], axis=1)
            blocks.append(jnp.broadcast_to(lanes, (HEAD_DIM, width)))
        return jnp.concatenate(blocks, axis=0)

    def step(i, carry):
        base = pl.multiple_of(i * SUBLANES, SUBLANES)
        tiles = [[ref[b, pl.ds(base, SUBLANES), :] for ref in (kk_ref, w_ref, be_ref, k_ref, r_ref, v_ref)]
                 for b in range(nb)]
        for b in range(nb):
            lhs = jnp.concatenate([eye * spread(tiles[b][5], j) for j in range(SUBLANES)], axis=0)
            vc_ref[b] = _dot(lhs.astype(BF16), ones_blk)
        sts = [st_ref[b] for b in range(nb)]
        for j in range(SUBLANES):
            for b in range(nb):
                kk8, w8, be8, k8, r8, _ = tiles[b]
                st = sts[b]
                sa = _dot((st * spread(kk8, j)).astype(BF16), ones_blk)
                st = (st * spread(w8, j) - sa * spread(be8, j)
                      + vc_ref[b, j * rows:(j + 1) * rows, :] * spread(k8, j))
                sr = (st * spread(r8, j)).astype(BF16)
                for sd in range(side):
                    lo = (j * side + sd) * rows
                    sr_ref[b, lo:lo + rows, :] = sr[:, sd * LANES:(sd + 1) * LANES]
                sts[b] = st
        for b in range(nb):
            st_ref[b] = sts[b]
            yt = _dot_nt(half_ref[...], sr_ref[b])
            per = side * rows
            y_ref[b, pl.ds(base, SUBLANES), :] = jnp.concatenate(
                [jnp.concatenate([yt[0:1, j * per:(j + 1) * per], yt[1:2, j * per:(j + 1) * per]], axis=1)
                 for j in range(SUBLANES)], axis=0)
        return carry

    lax.fori_loop(0, ts // SUBLANES, step, 0)

    ones = ones_ref[...]
    for b in range(nb):
        y = y_ref[b]
        mean = _dot_split(y, ones) * (1.0 / HEAD_DIM)
        yc = y - mean
        var = _dot_split(yc * yc, ones) * (1.0 / HEAD_DIM)
        yn = yc * lax.rsqrt(var + RW_GN_EPS) * gng_ref[...] + gnb_ref[...]
        o_ref[b] = ((yn + bo_ref[b]) * g_ref[b]).astype(o_ref.dtype)


def _rw_scan(r, w, k, v, kk, be, bo, g, prm, layer, *, ts=128):
    B, S, _ = r.shape
    nb = 4 if B % 4 == 0 else (2 if B % 2 == 0 else 1)
    rows, width = RW_PAIRS // 2 * HEAD_DIM, 2 * LANES
    seq = pl.BlockSpec((nb, ts, D_RWKV), lambda b, s: (b, s, 0))
    vec = pl.BlockSpec((None, 1, D_RWKV), lambda b, s: (layer, 0, 0))
    return pl.pallas_call(
        functools.partial(_rw_scan_kernel, ts=ts, nb=nb),
        out_shape=jax.ShapeDtypeStruct((B, S, D_RWKV), BF16),
        grid=(B // nb, S // ts),
        in_specs=[seq] * 8 + [vec, vec, pl.BlockSpec((D_RWKV, D_RWKV), lambda b, s: (0, 0)),
                              pl.BlockSpec((width, width), lambda b, s: (0, 0)),
                              pl.BlockSpec((SUBLANES, LANES), lambda b, s: (0, 0))],
        out_specs=seq,
        scratch_shapes=[pltpu.VMEM((nb, rows, width), F32), pltpu.VMEM((nb, SUBLANES * rows, width), F32),
                        pltpu.VMEM((nb, SUBLANES * RW_PAIRS * HEAD_DIM, LANES), BF16),
                        pltpu.VMEM((nb, ts, D_RWKV), F32)],
        compiler_params=_params("parallel", "arbitrary"),
        name="rw_scan",
    )(r, w, k, v, kk, be, bo, g, prm["gn_g"], prm["gn_b"], prm["ones"], prm["ones"][:width, :width],
      prm["ones"][:SUBLANES * HEAD_DIM:HEAD_DIM, :LANES])


def _rope(x, cos, sin):
    half = ROPE_DIM // 2
    w = x.shape[-1]
    d = lax.broadcasted_iota(jnp.int32, x.shape, 1) % HEAD_DIM
    partner = jnp.where(d < half, pltpu.roll(x, w - half, 1), pltpu.roll(x, half, 1))
    return x * cos + partner * sin


def _nsa_prep_kernel(p_ref, cos_ref, sin_ref, gb_ref, q_o, kc_o, vc_o, ks_o, vs_o, kw_o, vw_o, gt_o):
    cos = cos_ref[...]
    sin = sin_ref[...]
    for hq in range(NSA_HEADS // 4):
        x = p_ref[0, :, hq * KV_SLOT:(hq + 1) * KV_SLOT]
        qr = _rope(x, cos, sin) * (HEAD_DIM ** -0.5)
        for j in range(4):
            q_o[0, 4 * hq + j] = qr[:, j * HEAD_DIM:(j + 1) * HEAD_DIM].astype(BF16)
    for i, ref in enumerate((kc_o, vc_o, ks_o, vs_o, kw_o, vw_o)):
        part = p_ref[0, :, D_NSA + i * KV_SLOT:D_NSA + (i + 1) * KV_SLOT]
        if i in (2, 4):
            part = _rope(part, cos, sin)
        if i in (3, 5):
            part = part.T
            for h in range(NSA_KV_HEADS):
                ref[0, h] = part[h * HEAD_DIM:(h + 1) * HEAD_DIM, :].astype(BF16)
        else:
            for h in range(NSA_KV_HEADS):
                ref[0, h] = part[:, h * HEAD_DIM:(h + 1) * HEAD_DIM].astype(BF16)
    gts = _sigmoid(p_ref[0, :, NSA_GATE_OFF:NSA_GATE_OFF + LANES] + gb_ref[...]).T
    for h in range(NSA_KV_HEADS):
        gt_o[0, h] = gts[h * 12:(h + 1) * 12, :]


def _nsa_prep(p_all, cos, sin, gate_b, layer, *, ts=512):
    B, S, _ = p_all.shape
    H = NSA_KV_HEADS
    k_shape = jax.ShapeDtypeStruct((B, H, S, HEAD_DIM), BF16)
    k_spec = pl.BlockSpec((1, H, ts, HEAD_DIM), lambda b, s: (b, 0, s, 0))
    vt_shape = jax.ShapeDtypeStruct((B, H, HEAD_DIM, S), BF16)
    vt_spec = pl.BlockSpec((1, H, HEAD_DIM, ts), lambda b, s: (b, 0, 0, s))
    return pl.pallas_call(
        _nsa_prep_kernel,
        out_shape=[jax.ShapeDtypeStruct((B, NSA_HEADS, S, HEAD_DIM), BF16),
                   k_shape, k_shape, k_shape, vt_shape, k_shape, vt_shape,
                   jax.ShapeDtypeStruct((B, H, 12, S), F32)],
        grid=(B, S // ts),
        in_specs=[
            pl.BlockSpec((1, ts, NSA_PAD), lambda b, s: (b, s, P_NSA_OFF // NSA_PAD)),
            pl.BlockSpec((ts, KV_SLOT), lambda b, s: (s, 0)),
            pl.BlockSpec((ts, KV_SLOT), lambda b, s: (s, 0)),
            pl.BlockSpec((None, 1, LANES), lambda b, s: (layer, 0, 0)),
        ],
        out_specs=[pl.BlockSpec((1, NSA_HEADS, ts, HEAD_DIM), lambda b, s: (b, 0, s, 0)),
                   k_spec, k_spec, k_spec, vt_spec, k_spec, vt_spec,
                   pl.BlockSpec((1, H, 12, ts), lambda b, s: (b, 0, 0, s))],
        compiler_params=_params("parallel", "parallel"),
        name="nsa_prep",
    )(p_all, cos, sin, gate_b)


def _gelu_tanh(x):
    return 0.5 * x * (1.0 + jnp.tanh(float(np.sqrt(2.0 / np.pi)) * (x + 0.044715 * (x * x * x))))


def _nsa_cmp_kernel(kc_ref, vc_ref, kw1_ref, kw2_ref, kpe_ref, vw1_ref, vw2_ref, vpe_ref, cos_ref, sin_ref,
                    k_o, v_o):
    def hidden(g, w1_ref, pe_ref):
        half = CMP_STRIDE * HEAD_DIM
        first = _dot(g, w1_ref[:half, :])
        second = _dot(g, w1_ref[half:, :])
        n = first.shape[0]
        bias = _dot(pe_ref[...], w1_ref[...])[0:1]
        return _gelu_tanh(first + pltpu.roll(second, n - 1, 0) + bias).astype(BF16)

    k = _dot(hidden(kc_ref[0, 0], kw1_ref, kpe_ref), kw2_ref[...])
    k_o[0, 0] = _rope(k, cos_ref[...], sin_ref[...])[:, :HEAD_DIM].astype(BF16)
    v_o[0, 0] = _dot_nt(vw2_ref[...], hidden(vc_ref[0, 0], vw1_ref, vpe_ref)).astype(BF16)


def _nsa_compress(kc, vc, prm, cos_c, sin_c, layer):
    B, H, S, _ = kc.shape
    ng = S // CMP_STRIDE
    gw = CMP_STRIDE * HEAD_DIM
    g_k = kc.reshape(B, H, ng, gw)
    g_v = vc.reshape(B, H, ng, gw)
    gspec = pl.BlockSpec((1, 1, ng, gw), lambda b, h: (b, h, 0, 0))
    w1 = pl.BlockSpec((None, 2 * gw, CMP_HIDDEN), lambda b, h: (layer, 0, 0))
    w2 = pl.BlockSpec((None, CMP_HIDDEN, LANES), lambda b, h: (layer, 0, 0))
    pe = pl.BlockSpec((None, SUBLANES, 2 * gw), lambda b, h: (layer, 0, 0))
    tab = pl.BlockSpec((ng, LANES), lambda b, h: (0, 0))
    return pl.pallas_call(
        _nsa_cmp_kernel,
        out_shape=[jax.ShapeDtypeStruct((B, H, ng, HEAD_DIM), BF16), jax.ShapeDtypeStruct((B, H, HEAD_DIM, ng), BF16)],
        grid=(B, H),
        in_specs=[gspec, gspec, w1, w2, pe, w1,
                  pl.BlockSpec((None, HEAD_DIM, CMP_HIDDEN), lambda b, h: (layer, 0, 0)), pe, tab, tab],
        out_specs=[pl.BlockSpec((1, 1, ng, HEAD_DIM), lambda b, h: (b, h, 0, 0)),
                   pl.BlockSpec((1, 1, HEAD_DIM, ng), lambda b, h: (b, h, 0, 0))],
        compiler_params=_params("parallel", "parallel"),
        name="nsa_compress",
    )(g_k, g_v, prm["k_w1"], prm["k_w2"], prm["k_pe"], prm["v_w1"], prm["v_w2"], prm["v_pe"], cos_c, sin_c)


def _nsa_attn_kernel(q_ref, kcmp_ref, vcmp_ref, ks_ref, vs_ref, kw_ref, vw_ref, gt_ref, ov_ref, o_ref,
                     m_ref, l_ref, acc_ref, sb_ref, *, tq):
    G = NSA_GQA
    tk = tq
    qi = pl.program_id(2)
    q0 = qi * tq
    q_all = q_ref[0].reshape(G * tq, HEAD_DIM)
    pos = q0 + lax.broadcasted_iota(jnp.int32, (1, tq), 1)
    tiny = jnp.finfo(F32).tiny
    heads = lambda x: jnp.tile(x, (1, G))
    split = lambda x: [x[:, g * tq:(g + 1) * tq] for g in range(G)]

    ncp = kcmp_ref.shape[2]
    n_idx = lax.broadcasted_iota(jnp.int32, (ncp, 1), 0)
    cbias = jnp.where((n_idx * CMP_STRIDE + (CMP_BLOCK - 1)) <= pos, 0.0, NEG)
    any_cmp = jnp.where(pos >= CMP_BLOCK - 1, 1.0, 0.0)
    s = _dot_nt(kcmp_ref[0, 0], q_all) + heads(cbias)
    e = jnp.exp(s - jnp.max(s, axis=0, keepdims=True))
    p = e * (heads(any_cmp) / jnp.maximum(jnp.sum(e, axis=0, keepdims=True), tiny))
    o_cmp = split(_dot(vcmp_ref[0, 0], p.astype(BF16)))
    ps = split(p)
    psum = (ps[0] + ps[1]) + (ps[2] + ps[3])
    p_hi = psum.astype(BF16)
    p_lo = (psum - p_hi.astype(F32)).astype(BF16)
    imp = _dot(ov_ref[...], p_hi) + _dot(ov_ref[...], p_lo)

    wk = min(WINDOW + tq, ks_ref.shape[2])
    w0 = pl.multiple_of(jnp.maximum(q0 + tq - wk, 0), tk)
    wpos = w0 + lax.broadcasted_iota(jnp.int32, (wk, 1), 0)
    wbias = jnp.where((wpos <= pos) & (wpos > pos - WINDOW), 0.0, NEG)
    s = _dot_nt(kw_ref[0, 0, pl.ds(w0, wk), :], q_all) + heads(wbias)
    e = jnp.exp(s - jnp.max(s, axis=0, keepdims=True))
    pv = _dot(vw_ref[0, 0, :, pl.ds(w0, wk)], e.astype(BF16))
    o_win = split(pv / jnp.maximum(jnp.sum(e, axis=0, keepdims=True), tiny))

    n_sel = ov_ref.shape[0]
    blk = lax.broadcasted_iota(jnp.int32, (n_sel, 1), 0)
    cur = pos // SEL_BLOCK
    forced = (blk == 0) | (blk == cur) | (blk == cur - 1)
    valid = blk <= cur
    score = jnp.where(valid, jnp.where(forced, FORCE_SCORE, imp), -jnp.inf)
    groups = [score[g * SUBLANES:(g + 1) * SUBLANES, :] for g in range(n_sel // SUBLANES)]
    ranks = [jnp.zeros((SUBLANES, tq), F32) for _ in groups]
    sub = lax.broadcasted_iota(jnp.int32, (SUBLANES, 1), 0)
    for i in range(n_sel):
        ci = jnp.broadcast_to(score[i:i + 1, :], (SUBLANES, tq))
        for g, sg in enumerate(groups):
            if g > i // SUBLANES:
                beats = ci >= sg
            elif g < i // SUBLANES:
                beats = ci > sg
            else:
                beats = (ci > sg) | ((ci == sg) & (sub > i % SUBLANES))
            ranks[g] = ranks[g] + jnp.where(beats, 1.0, 0.0)
    rank = jnp.concatenate(ranks, axis=0)
    chosen = (rank < float(SEL_TOPK)) & valid
    sel_bias = jnp.where(chosen, 0.0, NEG)
    for j in range(n_sel):
        sb_ref[j] = jnp.broadcast_to(sel_bias[j:j + 1, :], (SUBLANES, tq))

    kcol = lax.broadcasted_iota(jnp.int32, (tk, 1), 0)

    def sel_update(kt, n, extra=None):
        bpt = n // SEL_BLOCK
        k0 = pl.multiple_of(kt * tk, tk)
        s = _dot_nt(ks_ref[0, 0, pl.ds(k0, n), :], q_all)
        b0 = kt * (tk // SEL_BLOCK)
        bias = jnp.concatenate([jnp.tile(sb_ref[b0 + j], (SEL_BLOCK // SUBLANES, 1)) for j in range(bpt)],
                               axis=0)
        if extra is not None:
            bias = bias + extra
        vt = vs_ref[0, 0, :, pl.ds(k0, n)]
        s = s + heads(bias)
        m_prev = m_ref[...]
        m_new = jnp.maximum(m_prev, jnp.max(s, axis=0, keepdims=True))
        p = jnp.exp(s - m_new)
        corr = jnp.exp(m_prev - m_new)
        l_ref[...] = corr * l_ref[...] + jnp.sum(p, axis=0, keepdims=True)
        acc_ref[...] = corr * acc_ref[...] + _dot(vt, p.astype(BF16))
        m_ref[...] = m_new

    m_ref[...] = jnp.full_like(m_ref, NEG)
    l_ref[...] = jnp.zeros_like(l_ref)
    acc_ref[...] = jnp.zeros_like(acc_ref)

    def sel_body(i, carry):
        sel_update(4 * i, 4 * tk)
        return carry

    lax.fori_loop(0, qi // 4, sel_body, 0)

    @pl.when(qi % 4 >= 2)
    def _():
        sel_update((qi // 4) * 4, 2 * tk)

    @pl.when(qi % 2 == 1)
    def _():
        sel_update(qi - 1, tk)

    causal = jnp.where(q0 + kcol <= pos, 0.0, NEG)
    sel_update(qi, tk, causal)
    o_sel = split(acc_ref[...] / jnp.maximum(l_ref[...], tiny))

    gt = gt_ref[0, 0]
    outs = [gt[g:g + 1] * o_cmp[g] + gt[G + g:G + g + 1] * o_sel[g] + gt[2 * G + g:2 * G + g + 1] * o_win[g]
            for g in range(G)]
    for half in range(G // 2):
        pair = jnp.concatenate(outs[2 * half:2 * half + 2], axis=0)
        o_ref[0, :, half * LANES:(half + 1) * LANES] = pair.T.astype(o_ref.dtype)


def _nsa_attention(q, k_cmp, v_cmp_t, ks, vs_t, kw, vw_t, gates_t, overlap_t, *, tq=256):
    B, _, S, _ = q.shape
    H, G = NSA_KV_HEADS, NSA_GQA
    tq = min(tq, S)
    assert WINDOW == 2 * tq or S <= tq
    ncp = k_cmp.shape[2]
    n_cmp = (S - CMP_BLOCK) // CMP_STRIDE + 1
    keys = pl.BlockSpec((1, 1, S, HEAD_DIM), lambda b, h, i: (b, h, 0, 0))
    vals = pl.BlockSpec((1, 1, HEAD_DIM, S), lambda b, h, i: (b, h, 0, 0))
    return pl.pallas_call(
        functools.partial(_nsa_attn_kernel, tq=tq),
        out_shape=jax.ShapeDtypeStruct((B, S, D_NSA), BF16),
        grid=(B, H, S // tq),
        in_specs=[
            pl.BlockSpec((1, G, tq, HEAD_DIM), lambda b, h, i: (b, h, i, 0)),
            pl.BlockSpec((1, 1, ncp, HEAD_DIM), lambda b, h, i: (b, h, 0, 0)),
            pl.BlockSpec((1, 1, HEAD_DIM, ncp), lambda b, h, i: (b, h, 0, 0)),
            keys, vals, keys, vals,
            pl.BlockSpec((1, 1, 12, tq), lambda b, h, i: (b, h, 0, i)),
            pl.BlockSpec(overlap_t.shape, lambda b, h, i: (0, 0)),
        ],
        out_specs=pl.BlockSpec((1, tq, G * HEAD_DIM), lambda b, h, i: (b, i, h)),
        scratch_shapes=[pltpu.VMEM((1, G * tq), F32), pltpu.VMEM((1, G * tq), F32),
                        pltpu.VMEM((HEAD_DIM, G * tq), F32), pltpu.VMEM((S // SEL_BLOCK, SUBLANES, tq), F32)],
        compiler_params=_params("parallel", "parallel", "arbitrary"),
        name="nsa_attention",
    )(q, k_cmp, v_cmp_t, ks, vs_t, kw, vw_t, gates_t, overlap_t)


def _rope_tables(pos, heads):
    half = ROPE_DIM // 2
    inv_freq = ROPE_THETA ** (-jnp.arange(half, dtype=F32) / half)
    ang = pos.astype(F32)[:, None] * inv_freq
    cos, sin = jnp.cos(ang), jnp.sin(ang)
    n = pos.shape[0]
    rest = HEAD_DIM - ROPE_DIM
    cos_h = jnp.concatenate([cos, cos, jnp.ones((n, rest), F32)], axis=1)
    sin_h = jnp.concatenate([-sin, sin, jnp.zeros((n, rest), F32)], axis=1)
    return jnp.tile(cos_h, (1, heads)), jnp.tile(sin_h, (1, heads))


def _overlap_matrix(S, ncp):
    n_cmp = (S - CMP_BLOCK) // CMP_STRIDE + 1
    n_sel = S // SEL_BLOCK
    cmp_start = np.arange(n_cmp) * CMP_STRIDE
    sel_start = np.arange(n_sel) * SEL_BLOCK
    ov = np.clip(np.minimum(cmp_start[:, None] + CMP_BLOCK, sel_start[None, :] + SEL_BLOCK)
                 - np.maximum(cmp_start[:, None], sel_start[None, :]), 0, None) / CMP_BLOCK
    full = np.zeros((n_sel, ncp), np.float32)
    full[:, :n_cmp] = ov.T
    return jnp.asarray(full, BF16)


RW_HEAD_ORDER = tuple(2 * (e % RW_PAIRS) + e // RW_PAIRS for e in range(RW_HEADS))


def _parity_major(a, axis):
    axis = axis % a.ndim
    shape = a.shape
    a = a.reshape(shape[:axis] + (RW_HEADS, HEAD_DIM) + shape[axis + 1:])
    a = jnp.take(a, jnp.asarray(RW_HEAD_ORDER), axis=axis)
    return a.reshape(shape)


def _block_ones(n):
    idx = np.arange(n) // HEAD_DIM
    return jnp.asarray((idx[:, None] == idx[None, :]).astype(np.float32), BF16)


def _w_in_layout(w_in):
    L = w_in.shape[0]
    rw = w_in[:, :, :RW_COLS]
    pool = w_in[:, :, RW_COLS:RW_COLS + D_POOL]
    nsa = w_in[:, :, RW_COLS + D_POOL:]
    zeros = lambda n: jnp.zeros((L, D_MODEL, n), w_in.dtype)
    segs = [nsa[:, :, :D_NSA]]
    for i in range(6):
        segs += [nsa[:, :, D_NSA + i * NSA_KV:D_NSA + (i + 1) * NSA_KV], zeros(KV_SLOT - NSA_KV)]
    gates = nsa[:, :, D_NSA + 6 * NSA_KV:]
    gates = gates.reshape(L, D_MODEL, NSA_KV_HEADS, NSA_GQA, 3).transpose(0, 1, 2, 4, 3)
    segs += [gates.reshape(L, D_MODEL, 3 * NSA_HEADS),
             zeros(NSA_PAD - NSA_GATE_OFF - 3 * NSA_HEADS)]
    return jnp.concatenate([rw] + segs + [pool], axis=-1).astype(BF16)


def _gate_bias_layout(gate_b):
    L = gate_b.shape[0]
    gb = gate_b.reshape(L, NSA_KV_HEADS, NSA_GQA, 3).transpose(0, 1, 3, 2).reshape(L, 1, 3 * NSA_HEADS)
    return jnp.pad(gb, ((0, 0), (0, 0), (0, LANES - 3 * NSA_HEADS)))


def kernel(x, ffn1_w_up, ffn1_w_down, ln1_g, ln1_b, w_in, rw_mu, rw_w0, rw_w2, rw_a0, rw_a2, rw_g2, rw_k_k,
           rw_k_a, rw_r_k, rw_gn_g, rw_gn_b, pool_w, pool_b, pool_scale, nsa_cmp_pe_k, nsa_cmp_pe_v,
           nsa_cmp_k_w1, nsa_cmp_k_w2, nsa_cmp_v_w1, nsa_cmp_v_w2, nsa_gate_b, w_out, ln2_g, ln2_b,
           ffn2_w_up, ffn2_w_down, ln3_g, ln3_b):
    prm = _prepare(x.shape[1], ffn1_w_up, ffn1_w_down, ln1_g, ln1_b, w_in, rw_mu, rw_w0, rw_w2, rw_a0, rw_a2,
                   rw_g2, rw_k_k, rw_k_a, rw_r_k, rw_gn_g, rw_gn_b, pool_w, pool_b, pool_scale, nsa_cmp_pe_k,
                   nsa_cmp_pe_v, nsa_cmp_k_w1, nsa_cmp_k_w2, nsa_cmp_v_w1, nsa_cmp_v_w2, nsa_gate_b, w_out,
                   ln2_g, ln2_b, ffn2_w_up, ffn2_w_down, ln3_g, ln3_b)
    B, S, D = x.shape
    h = x.reshape(B * S, D)
    for l in range(w_in.shape[0]):
        h = _layer(h, prm, l, B, S)
    return h.reshape(B, S, D)


def _prepare(S, ffn1_w_up, ffn1_w_down, ln1_g, ln1_b, w_in, rw_mu, rw_w0, rw_w2, rw_a0, rw_a2, rw_g2, rw_k_k,
             rw_k_a, rw_r_k, rw_gn_g, rw_gn_b, pool_w, pool_b, pool_scale, nsa_cmp_pe_k, nsa_cmp_pe_v,
             nsa_cmp_k_w1, nsa_cmp_k_w2, nsa_cmp_v_w1, nsa_cmp_v_w2, nsa_gate_b, w_out, ln2_g, ln2_b,
             ffn2_w_up, ffn2_w_down, ln3_g, ln3_b):
    L = w_in.shape[0]
    fpad = D_FF_PAD - D_FF

    def up(w):
        a = jnp.pad(w[:, :, :D_FF], ((0, 0), (0, 0), (0, fpad))).astype(BF16)
        b = jnp.pad(w[:, :, D_FF:], ((0, 0), (0, 0), (0, fpad))).astype(BF16)
        return a, b

    def down(w):
        return jnp.pad(w, ((0, 0), (0, fpad), (0, 0))).astype(BF16)

    row = lambda v: v[:, None, :]
    f1a, f1b = up(ffn1_w_up)
    f2a, f2b = up(ffn2_w_up)
    w_out_b = w_out.astype(BF16)
    gw = CMP_BLOCK * HEAD_DIM
    pad_w2 = lambda w: jnp.pad(w, ((0, 0), (0, 0), (0, LANES - HEAD_DIM))).astype(BF16)
    pe_rows = lambda pe: jnp.broadcast_to(pe.reshape(L, 1, gw), (L, SUBLANES, gw)).astype(BF16)
    ncp = S // CMP_STRIDE
    cos_t, sin_t = _rope_tables(jnp.arange(S), 4)
    cos_c, sin_c = _rope_tables(jnp.arange(ncp) * CMP_STRIDE + (CMP_BLOCK - 1), 2)
    return dict(
        ffn1=(f1a, f1b, down(ffn1_w_down), row(ln1_g), row(ln1_b)),
        ffn2=(f2a, f2b, down(ffn2_w_down), row(ln3_g), row(ln3_b)),
        w_in=_w_in_layout(w_in),
        w_out=(_parity_major(w_out_b[:, :D_RWKV], 1), w_out_b[:, D_RWKV:D_RWKV + D_POOL],
               w_out_b[:, D_RWKV + D_POOL:]),
        ln2=(row(ln2_g), row(ln2_b)),
        rw=dict(mu=row(rw_mu), w0=row(rw_w0), w2=rw_w2.astype(BF16), a0=row(rw_a0), a2=rw_a2.astype(BF16),
                g2=_parity_major(rw_g2, -1).astype(BF16), k_k=row(rw_k_k), k_a=row(rw_k_a),
                r_k=rw_r_k.reshape(L, 1, D_RWKV), gn_g=row(_parity_major(rw_gn_g, -1)),
                gn_b=row(_parity_major(rw_gn_b, -1)), ones=_block_ones(D_RWKV)),
        cmp=dict(k_w1=nsa_cmp_k_w1.reshape(L, gw, CMP_HIDDEN).astype(BF16), k_w2=pad_w2(nsa_cmp_k_w2),
                 k_pe=pe_rows(nsa_cmp_pe_k),
                 v_w1=nsa_cmp_v_w1.reshape(L, gw, CMP_HIDDEN).astype(BF16), v_w2=nsa_cmp_v_w2.transpose(0, 2, 1).astype(BF16),
                 v_pe=pe_rows(nsa_cmp_pe_v)),
        gate_b=_gate_bias_layout(nsa_gate_b),
        pool=(pool_w.astype(BF16), row(pool_b), row(pool_scale)),
        rope=(cos_t, sin_t), rope_cmp=(cos_c, sin_c), overlap=_overlap_matrix(S, ncp))


def _mixers(p_all, prm, l):
    r, w, k, v, kk, be, bo, g = _rw_prep(p_all, prm["rw"], l)
    y_rw = _rw_scan(r, w, k, v, kk, be, bo, g, prm["rw"], l)
    y_pool = _pool_mix(p_all, *prm["pool"], l)
    q, kc, vc, ks, vs, kw, vw, gates = _nsa_prep(p_all, *prm["rope"], prm["gate_b"], l)
    k_cmp, v_cmp = _nsa_compress(kc, vc, prm["cmp"], *prm["rope_cmp"], l)
    y_nsa = _nsa_attention(q, k_cmp, v_cmp, ks, vs, kw, vw, gates, prm["overlap"])
    return y_rw, y_pool, y_nsa


def _layer(h, prm, l, B, S):
    T = B * S
    h = _ffn_ln(h, *prm["ffn1"], l)
    p_all = _in_proj(h, prm["w_in"], l).reshape(B, S, P_COLS)
    y_rw, y_pool, y_nsa = _mixers(p_all, prm, l)
    h = _out_proj_ln(h, y_rw.reshape(T, D_RWKV), y_pool.reshape(T, D_POOL), y_nsa.reshape(T, D_NSA),
                     prm["w_out"], *prm["ln2"], l)
    return _ffn_ln(h, *prm["ffn2"], l)
```

```python
import functools

import numpy as np
import jax
import jax.numpy as jnp
from jax import lax
from jax.experimental import pallas as pl
from jax.experimental.pallas import tpu as pltpu

F32 = jnp.float32
BF16 = jnp.bfloat16

D_MODEL = 2048
DEPTH = 4
HEAD_DIM = 64
D_RWKV = 768
D_POOL = 512
D_NSA = 768
RW_HEADS = 12
RW_DECAY_LORA = 64
RW_A_LORA = 64
RW_GATE_LORA = 128
RW_GN_EPS = 64e-5
RW_COLS = 3 * D_RWKV + RW_DECAY_LORA + RW_A_LORA + RW_GATE_LORA
POOL_WINDOWS = (2, 4, 8, 16)
POOL_GROUP = 128
NSA_HEADS = 12
NSA_KV_HEADS = 3
NSA_GQA = 4
NSA_KV = 192
NSA_COLS = D_NSA + 6 * NSA_KV + 3 * NSA_HEADS
CMP_BLOCK = 32
CMP_STRIDE = 16
CMP_HIDDEN = 256
SEL_BLOCK = 64
SEL_TOPK = 16
FORCE_SCORE = 1e9
WINDOW = 512
ROPE_THETA = 500000.0
ROPE_DIM = 16
D_FF = 5504
IN_COLS = RW_COLS + D_POOL + NSA_COLS
ALPHA = (2 * DEPTH) ** 0.25
LN_EPS = 1e-5

LANES = 128
SUBLANES = 8
VMEM_LIMIT = 56 * 1024 * 1024

KV_SLOT = 2 * LANES
NSA_PAD = 2560
P_RW_OFF = 0
P_NSA_OFF = RW_COLS
P_POOL_OFF = RW_COLS + NSA_PAD
P_COLS = RW_COLS + NSA_PAD + D_POOL
NSA_GATE_OFF = D_NSA + 6 * KV_SLOT
D_FF_PAD = 5632

NEG = -1e30


def _params(*sem):
    return pltpu.CompilerParams(dimension_semantics=sem, vmem_limit_bytes=VMEM_LIMIT)


def _layer_norm(z, g, b):
    mu = jnp.mean(z, axis=-1, keepdims=True)
    zc = z - mu
    var = jnp.mean(zc * zc, axis=-1, keepdims=True)
    return zc * lax.rsqrt(var + LN_EPS) * g + b


def _dot(a, b):
    return jnp.dot(a, b, preferred_element_type=F32)


def _dot_nt(a, b):
    return lax.dot_general(a, b, (((1,), (1,)), ((), ())), preferred_element_type=F32)


def _dot_split(x, w):
    hi = x.astype(BF16)
    lo = (x - hi.astype(F32)).astype(BF16)
    return _dot(hi, w) + _dot(lo, w)


FFN_LN_ROWS = 256
FFN_MM_ROWS = 512


def _ffn_kernel(x_ref, wa_ref, wb_ref, wd_ref, g_ref, b_ref, o_ref, xb_ref):
    k = pl.program_id(1)

    @pl.when(k == 0)
    def _():
        xb_ref[...] = x_ref[...].astype(BF16)
        o_ref[...] = jnp.zeros_like(o_ref)

    for c in range(o_ref.shape[0] // FFN_MM_ROWS):
        rows = slice(c * FFN_MM_ROWS, (c + 1) * FFN_MM_ROWS)
        xb = xb_ref[rows, :]
        a = _dot(xb, wa_ref[...])
        b = _dot(xb, wb_ref[...])
        h = (a / (1.0 + jnp.exp(-a))) * b
        o_ref[rows, :] += _dot(h.astype(BF16), wd_ref[...])

    @pl.when(k == pl.num_programs(1) - 1)
    def _():
        for c in range(o_ref.shape[0] // FFN_LN_ROWS):
            rows = slice(c * FFN_LN_ROWS, (c + 1) * FFN_LN_ROWS)
            z = ALPHA * x_ref[rows, :] + 0.5 * o_ref[rows, :]
            o_ref[rows, :] = _layer_norm(z, g_ref[...], b_ref[...])


def _ffn_ln(x, wa, wb, wd, g, b, layer, *, tm=1024, tf=512):
    T, D = x.shape
    fp = wa.shape[-1]
    tm = min(tm, T)
    return pl.pallas_call(
        _ffn_kernel,
        out_shape=jax.ShapeDtypeStruct((T, D), F32),
        grid=(T // tm, fp // tf),
        in_specs=[
            pl.BlockSpec((tm, D), lambda i, k: (i, 0)),
            pl.BlockSpec((None, D, tf), lambda i, k: (layer, 0, k)),
            pl.BlockSpec((None, D, tf), lambda i, k: (layer, 0, k)),
            pl.BlockSpec((None, tf, D), lambda i, k: (layer, k, 0)),
            pl.BlockSpec((None, 1, D), lambda i, k: (layer, 0, 0)),
            pl.BlockSpec((None, 1, D), lambda i, k: (layer, 0, 0)),
        ],
        out_specs=pl.BlockSpec((tm, D), lambda i, k: (i, 0)),
        scratch_shapes=[pltpu.VMEM((tm, D), BF16)],
        compiler_params=_params("parallel", "arbitrary"),
        name="ffn_ln",
    )(x, wa, wb, wd, g, b)


def _inproj_kernel(x_ref, w_ref, o_ref, xb_ref):
    @pl.when(pl.program_id(1) == 0)
    def _():
        xb_ref[...] = x_ref[...].astype(BF16)

    o_ref[...] = _dot(xb_ref[...], w_ref[...])


def _in_proj(x, w, layer, *, tm=1024, tn=P_COLS // 4):
    T, D = x.shape
    n = w.shape[-1]
    tm = min(tm, T)
    return pl.pallas_call(
        _inproj_kernel,
        out_shape=jax.ShapeDtypeStruct((T, n), F32),
        grid=(T // tm, n // tn),
        in_specs=[
            pl.BlockSpec((tm, D), lambda i, j: (i, 0)),
            pl.BlockSpec((None, D, tn), lambda i, j: (layer, 0, j)),
        ],
        out_specs=pl.BlockSpec((tm, tn), lambda i, j: (i, j)),
        scratch_shapes=[pltpu.VMEM((tm, D), BF16)],
        compiler_params=_params("parallel", "arbitrary"),
        name="in_proj",
    )(x, w)


def _outproj_kernel(x_ref, yr_ref, yp_ref, yn_ref, wr_ref, wp_ref, wn_ref, g_ref, b_ref, o_ref):
    y = _dot(yr_ref[...], wr_ref[...]) + _dot(yp_ref[...], wp_ref[...]) + _dot(yn_ref[...], wn_ref[...])
    o_ref[...] = _layer_norm(ALPHA * x_ref[...] + y, g_ref[...], b_ref[...])


def _out_proj_ln(x, y_rw, y_pool, y_nsa, w_out, g, b, layer, *, tm=512):
    T, D = x.shape
    return pl.pallas_call(
        _outproj_kernel,
        out_shape=jax.ShapeDtypeStruct((T, D), F32),
        grid=(T // tm,),
        in_specs=[
            pl.BlockSpec((tm, D), lambda i: (i, 0)),
            pl.BlockSpec((tm, D_RWKV), lambda i: (i, 0)),
            pl.BlockSpec((tm, D_POOL), lambda i: (i, 0)),
            pl.BlockSpec((tm, D_NSA), lambda i: (i, 0)),
            pl.BlockSpec((None, D_RWKV, D), lambda i: (layer, 0, 0)),
            pl.BlockSpec((None, D_POOL, D), lambda i: (layer, 0, 0)),
            pl.BlockSpec((None, D_NSA, D), lambda i: (layer, 0, 0)),
            pl.BlockSpec((None, 1, D), lambda i: (layer, 0, 0)),
            pl.BlockSpec((None, 1, D), lambda i: (layer, 0, 0)),
        ],
        out_specs=pl.BlockSpec((tm, D), lambda i: (i, 0)),
        compiler_params=_params("parallel"),
        name="out_proj_ln",
    )(x, y_rw, y_pool, y_nsa, w_out[0], w_out[1], w_out[2], g, b)


POOL_HALO = 16


def _pool_kernel(p_ref, halo_ref, w_ref, b_ref, sc_ref, o_ref, xs_ref, *, ts):
    s = pl.program_id(1)
    x = p_ref[0]
    halo = jnp.where(s > 0, halo_ref[0], 0.0)
    xs_ref[0:POOL_HALO, :] = halo
    xs_ref[POOL_HALO:POOL_HALO + ts, :] = x
    t1 = (s * ts + 1 + lax.broadcasted_iota(jnp.int32, (ts, 1), 0)).astype(F32)
    for gi, win in enumerate(POOL_WINDOWS):
        c0 = gi * POOL_GROUP
        acc = x[:, c0:c0 + POOL_GROUP]
        for j in range(1, win):
            acc = acc + xs_ref[POOL_HALO - j:POOL_HALO - j + ts, c0:c0 + POOL_GROUP]
        pooled = acc / jnp.minimum(t1, float(win)) - x[:, c0:c0 + POOL_GROUP]
        z = _dot(pooled.astype(BF16), w_ref[gi]) + b_ref[:, c0:c0 + POOL_GROUP]
        o_ref[0, :, c0:c0 + POOL_GROUP] = (z * sc_ref[:, c0:c0 + POOL_GROUP]).astype(o_ref.dtype)


def _pool_mix(p_all, pool_w, pool_b, pool_scale, layer, *, ts=512):
    B, S, _ = p_all.shape
    cb = P_POOL_OFF // D_POOL
    hb = ts // POOL_HALO
    return pl.pallas_call(
        functools.partial(_pool_kernel, ts=ts),
        out_shape=jax.ShapeDtypeStruct((B, S, D_POOL), BF16),
        grid=(B, S // ts),
        in_specs=[
            pl.BlockSpec((1, ts, D_POOL), lambda b, s: (b, s, cb)),
            pl.BlockSpec((1, POOL_HALO, D_POOL), lambda b, s: (b, jnp.maximum(s * hb - 1, 0), cb)),
            pl.BlockSpec((None, 4, POOL_GROUP, POOL_GROUP), lambda b, s: (layer, 0, 0, 0)),
            pl.BlockSpec((None, 1, D_POOL), lambda b, s: (layer, 0, 0)),
            pl.BlockSpec((None, 1, D_POOL), lambda b, s: (layer, 0, 0)),
        ],
        out_specs=pl.BlockSpec((1, ts, D_POOL), lambda b, s: (b, s, 0)),
        scratch_shapes=[pltpu.VMEM((ts + POOL_HALO, D_POOL), F32)],
        compiler_params=_params("parallel", "parallel"),
        name="pool_mix",
    )(p_all, p_all, pool_w, pool_b, pool_scale)


def _softplus(z):
    return jnp.maximum(z, 0.0) + jnp.log1p(jnp.exp(-jnp.abs(z)))


def _sigmoid(z):
    return 1.0 / (1.0 + jnp.exp(-z))


def _lanes_to_parity_major(x):
    n_pairs = x.shape[1] // LANES
    low = lax.broadcasted_iota(jnp.int32, (1, LANES), 1) < HEAD_DIM

    def chunk(e, to_low):
        pair, parity = e % n_pairs, e // n_pairs
        src = x[:, pair * LANES:(pair + 1) * LANES]
        return src if (parity == 0) == to_low else pltpu.roll(src, HEAD_DIM, 1)

    return jnp.concatenate([jnp.where(low, chunk(2 * d, True), chunk(2 * d + 1, False))
                            for d in range(n_pairs)], axis=1)


def _rw_prep_kernel(p_ref, prev_ref, mu_ref, w0_ref, w2_ref, a0_ref, a2_ref, g2_ref, kk_ref, ka_ref,
                    rk_ref, ones_ref,
                    r_o, w_o, k_o, v_o, kk_o, be_o, bo_o, g_o, *, ts):
    s = pl.program_id(1)
    x = p_ref[0]
    last = jnp.where(s > 0, prev_ref[0][SUBLANES - 1:SUBLANES, :], 0.0)
    row = lax.broadcasted_iota(jnp.int32, (ts, 1), 0)
    shifted = jnp.where(row == 0, last, pltpu.roll(x, 1, 0))
    xm = x + (shifted - x) * mu_ref[...]
    c = D_RWKV
    r = xm[:, 0:c]
    k = xm[:, c:2 * c]
    v = xm[:, 2 * c:3 * c]
    lora = xm[:, 3 * c:3 * c + LANES]
    wl = lora[:, :RW_DECAY_LORA]
    al = lora[:, RW_DECAY_LORA:]
    gl = xm[:, 3 * c + LANES:]
    w = -_softplus(-(w0_ref[...] + _dot(jnp.tanh(wl).astype(BF16), w2_ref[...]))) - 0.5
    decay = jnp.exp(-jnp.exp(w))
    a = _sigmoid(a0_ref[...] + _dot(al.astype(BF16), a2_ref[...]))
    g = _dot(_sigmoid(gl).astype(BF16), g2_ref[...])
    ones = ones_ref[...]
    kk = k * kk_ref[...]
    nrm = jnp.sqrt(_dot_split(kk * kk, ones))
    kk = kk / jnp.maximum(nrm, 1e-12)
    k_mod = k * (1.0 + (a - 1.0) * ka_ref[...])
    bonus = _dot_split(r * k_mod * rk_ref[...], ones) * v
    r_o[0] = r
    w_o[0] = decay
    k_o[0] = k_mod
    v_o[0] = v
    kk_o[0] = kk
    be_o[0] = kk * a
    bo_o[0] = _lanes_to_parity_major(bonus).astype(bo_o.dtype)
    g_o[0] = g.astype(g_o.dtype)


def _rw_prep(p_all, prm, layer, *, ts=512):
    B, S, _ = p_all.shape
    cb = P_RW_OFF // RW_COLS
    hb = ts // SUBLANES
    vec = lambda n: pl.BlockSpec((None, 1, n), lambda b, s: (layer, 0, 0))
    mat = lambda m, n: pl.BlockSpec((None, m, n), lambda b, s: (layer, 0, 0))
    out = jax.ShapeDtypeStruct((B, S, D_RWKV), F32)
    ospec = pl.BlockSpec((1, ts, D_RWKV), lambda b, s: (b, s, 0))
    return pl.pallas_call(
        functools.partial(_rw_prep_kernel, ts=ts),
        out_shape=[out] * 6 + [jax.ShapeDtypeStruct((B, S, D_RWKV), BF16)] * 2,
        grid=(B, S // ts),
        in_specs=[
            pl.BlockSpec((1, ts, RW_COLS), lambda b, s: (b, s, cb)),
            pl.BlockSpec((1, SUBLANES, RW_COLS), lambda b, s: (b, jnp.maximum(s * hb - 1, 0), cb)),
            vec(RW_COLS), vec(D_RWKV), mat(RW_DECAY_LORA, D_RWKV), vec(D_RWKV), mat(RW_A_LORA, D_RWKV),
            mat(RW_GATE_LORA, D_RWKV), vec(D_RWKV), vec(D_RWKV), vec(D_RWKV),
            pl.BlockSpec((D_RWKV, D_RWKV), lambda b, s: (0, 0)),
        ],
        out_specs=[ospec] * 8,
        compiler_params=_params("parallel", "parallel"),
        name="rw_prep",
    )(p_all, p_all, prm["mu"], prm["w0"], prm["w2"], prm["a0"], prm["a2"], prm["g2"], prm["k_k"],
      prm["k_a"], prm["r_k"], prm["ones"])


RW_PAIRS = D_RWKV // LANES
RW_STEPS = 8


def _rw_scan_kernel(r_ref, w_ref, k_ref, v_ref, kk_ref, be_ref, bo_ref, g_ref, gng_ref, gnb_ref, ones_ref,
                    pair_ref, half_ref, o_ref, st_ref, vc_ref, sr_ref, y_ref, *, ts, nb):
    s = pl.program_id(1)

    @pl.when(s == 0)
    def _():
        st_ref[...] = jnp.zeros_like(st_ref)

    side, stack = 2, RW_PAIRS // 2
    rows, width = stack * HEAD_DIM, side * LANES
    lane = lax.broadcasted_iota(jnp.int32, (rows, width), 1)
    row = lax.broadcasted_iota(jnp.int32, (rows, width), 0)
    eye = jnp.where(lane % HEAD_DIM == row % HEAD_DIM, 1.0, 0.0)
    ones_blk = pair_ref[...]

    def spread(x8, j):
        blocks = []
        for st in range(stack):
            lanes = jnp.concatenate([x8[j:j + 1, (sd * stack + st) * LANES:(sd * stack + st + 1) * LANES]
                                     for sd in range(side)], axis=1)
            blocks.append(jnp.broadcast_to(lanes, (HEAD_DIM, width)))
        return jnp.concatenate(blocks, axis=0)

    def step(i, carry):
        base = pl.multiple_of(i * RW_STEPS, RW_STEPS)
        tiles = [[ref[b, pl.ds(base, RW_STEPS), :] for ref in (kk_ref, w_ref, be_ref, k_ref, r_ref, v_ref)]
                 for b in range(nb)]
        for b in range(nb):
            lhs = jnp.concatenate([eye * spread(tiles[b][5], j) for j in range(RW_STEPS)], axis=0)
            vc_ref[b] = _dot(lhs.astype(BF16), ones_blk)
        sts = [st_ref[b] for b in range(nb)]
        for j in range(RW_STEPS):
            for b in range(nb):
                kk8, w8, be8, k8, r8, _ = tiles[b]
                st = sts[b]
                sa = _dot((st * spread(kk8, j)).astype(BF16), ones_blk)
                st = (st * spread(w8, j) - sa * spread(be8, j)
                      + vc_ref[b, j * rows:(j + 1) * rows, :] * spread(k8, j))
                sr = (st * spread(r8, j)).astype(BF16)
                for sd in range(side):
                    lo = (j * side + sd) * rows
                    sr_ref[b, lo:lo + rows, :] = sr[:, sd * LANES:(sd + 1) * LANES]
                sts[b] = st
        for b in range(nb):
            st_ref[b] = sts[b]
            yt = _dot_nt(half_ref[...], sr_ref[b])
            per = side * rows
            y_ref[b, pl.ds(base, RW_STEPS), :] = jnp.concatenate(
                [jnp.concatenate([yt[0:1, j * per:(j + 1) * per], yt[1:2, j * per:(j + 1) * per]], axis=1)
                 for j in range(RW_STEPS)], axis=0)
        return carry

    lax.fori_loop(0, ts // RW_STEPS, step, 0)

    ones = ones_ref[...]
    for b in range(nb):
        y = y_ref[b]
        mean = _dot_split(y, ones) * (1.0 / HEAD_DIM)
        yc = y - mean
        var = _dot_split(yc * yc, ones) * (1.0 / HEAD_DIM)
        yn = yc * lax.rsqrt(var + RW_GN_EPS) * gng_ref[...] + gnb_ref[...]
        o_ref[b] = ((yn + bo_ref[b]) * g_ref[b]).astype(o_ref.dtype)


def _rw_scan(r, w, k, v, kk, be, bo, g, prm, layer, *, ts=128):
    B, S, _ = r.shape
    nb = 4 if B % 4 == 0 else (2 if B % 2 == 0 else 1)
    rows, width = RW_PAIRS // 2 * HEAD_DIM, 2 * LANES
    seq = pl.BlockSpec((nb, ts, D_RWKV), lambda b, s: (b, s, 0))
    vec = pl.BlockSpec((None, 1, D_RWKV), lambda b, s: (layer, 0, 0))
    return pl.pallas_call(
        functools.partial(_rw_scan_kernel, ts=ts, nb=nb),
        out_shape=jax.ShapeDtypeStruct((B, S, D_RWKV), BF16),
        grid=(B // nb, S // ts),
        in_specs=[seq] * 8 + [vec, vec, pl.BlockSpec((D_RWKV, D_RWKV), lambda b, s: (0, 0)),
                              pl.BlockSpec((width, width), lambda b, s: (0, 0)),
                              pl.BlockSpec((SUBLANES, LANES), lambda b, s: (0, 0))],
        out_specs=seq,
        scratch_shapes=[pltpu.VMEM((nb, rows, width), F32), pltpu.VMEM((nb, RW_STEPS * rows, width), F32),
                        pltpu.VMEM((nb, RW_STEPS * RW_PAIRS * HEAD_DIM, LANES), BF16),
                        pltpu.VMEM((nb, ts, D_RWKV), F32)],
        compiler_params=_params("parallel", "arbitrary"),
        name="rw_scan",
    )(r, w, k, v, kk, be, bo, g, prm["gn_g"], prm["gn_b"], prm["ones"], prm["ones"][:width, :width],
      prm["ones"][:SUBLANES * HEAD_DIM:HEAD_DIM, :LANES])


def _rope(x, cos, sin):
    half = ROPE_DIM // 2
    w = x.shape[-1]
    d = lax.broadcasted_iota(jnp.int32, x.shape, 1) % HEAD_DIM
    partner = jnp.where(d < half, pltpu.roll(x, w - half, 1), pltpu.roll(x, half, 1))
    return x * cos + partner * sin


def _nsa_prep_kernel(p_ref, cos_ref, sin_ref, gb_ref, q_o, kc_o, vc_o, ks_o, vs_o, kw_o, vw_o, gt_o):
    cos = cos_ref[...]
    sin = sin_ref[...]
    for hq in range(NSA_HEADS // 4):
        x = p_ref[0, :, hq * KV_SLOT:(hq + 1) * KV_SLOT]
        qr = _rope(x, cos, sin) * (HEAD_DIM ** -0.5)
        for j in range(4):
            q_o[0, 4 * hq + j] = qr[:, j * HEAD_DIM:(j + 1) * HEAD_DIM].astype(BF16)
    for i, ref in enumerate((kc_o, vc_o, ks_o, vs_o, kw_o, vw_o)):
        part = p_ref[0, :, D_NSA + i * KV_SLOT:D_NSA + (i + 1) * KV_SLOT]
        if i in (2, 4):
            part = _rope(part, cos, sin)
        if i in (3, 5):
            part = part.T
            for h in range(NSA_KV_HEADS):
                ref[0, h] = part[h * HEAD_DIM:(h + 1) * HEAD_DIM, :].astype(BF16)
        else:
            for h in range(NSA_KV_HEADS):
                ref[0, h] = part[:, h * HEAD_DIM:(h + 1) * HEAD_DIM].astype(BF16)
    gts = _sigmoid(p_ref[0, :, NSA_GATE_OFF:NSA_GATE_OFF + LANES] + gb_ref[...]).T
    for h in range(NSA_KV_HEADS):
        gt_o[0, h] = gts[h * 12:(h + 1) * 12, :]


def _nsa_prep(p_all, cos, sin, gate_b, layer, *, ts=512):
    B, S, _ = p_all.shape
    H = NSA_KV_HEADS
    k_shape = jax.ShapeDtypeStruct((B, H, S, HEAD_DIM), BF16)
    k_spec = pl.BlockSpec((1, H, ts, HEAD_DIM), lambda b, s: (b, 0, s, 0))
    vt_shape = jax.ShapeDtypeStruct((B, H, HEAD_DIM, S), BF16)
    vt_spec = pl.BlockSpec((1, H, HEAD_DIM, ts), lambda b, s: (b, 0, 0, s))
    return pl.pallas_call(
        _nsa_prep_kernel,
        out_shape=[jax.ShapeDtypeStruct((B, NSA_HEADS, S, HEAD_DIM), BF16),
                   k_shape, k_shape, k_shape, vt_shape, k_shape, vt_shape,
                   jax.ShapeDtypeStruct((B, H, 12, S), F32)],
        grid=(B, S // ts),
        in_specs=[
            pl.BlockSpec((1, ts, NSA_PAD), lambda b, s: (b, s, P_NSA_OFF // NSA_PAD)),
            pl.BlockSpec((ts, KV_SLOT), lambda b, s: (s, 0)),
            pl.BlockSpec((ts, KV_SLOT), lambda b, s: (s, 0)),
            pl.BlockSpec((None, 1, LANES), lambda b, s: (layer, 0, 0)),
        ],
        out_specs=[pl.BlockSpec((1, NSA_HEADS, ts, HEAD_DIM), lambda b, s: (b, 0, s, 0)),
                   k_spec, k_spec, k_spec, vt_spec, k_spec, vt_spec,
                   pl.BlockSpec((1, H, 12, ts), lambda b, s: (b, 0, 0, s))],
        compiler_params=_params("parallel", "parallel"),
        name="nsa_prep",
    )(p_all, cos, sin, gate_b)


def _gelu_tanh(x):
    return 0.5 * x * (1.0 + jnp.tanh(float(np.sqrt(2.0 / np.pi)) * (x + 0.044715 * (x * x * x))))


def _nsa_cmp_kernel(kc_ref, vc_ref, kw1_ref, kw2_ref, kpe_ref, vw1_ref, vw2_ref, vpe_ref, cos_ref, sin_ref,
                    k_o, v_o):
    def hidden(g, w1_ref, pe_ref):
        half = CMP_STRIDE * HEAD_DIM
        first = _dot(g, w1_ref[:half, :])
        second = _dot(g, w1_ref[half:, :])
        n = first.shape[0]
        bias = _dot(pe_ref[...], w1_ref[...])[0:1]
        return _gelu_tanh(first + pltpu.roll(second, n - 1, 0) + bias).astype(BF16)

    k = _dot(hidden(kc_ref[0, 0], kw1_ref, kpe_ref), kw2_ref[...])
    k_o[0, 0] = _rope(k, cos_ref[...], sin_ref[...])[:, :HEAD_DIM].astype(BF16)
    v_o[0, 0] = _dot_nt(vw2_ref[...], hidden(vc_ref[0, 0], vw1_ref, vpe_ref)).astype(BF16)


def _nsa_compress(kc, vc, prm, cos_c, sin_c, layer):
    B, H, S, _ = kc.shape
    ng = S // CMP_STRIDE
    gw = CMP_STRIDE * HEAD_DIM
    g_k = kc.reshape(B, H, ng, gw)
    g_v = vc.reshape(B, H, ng, gw)
    gspec = pl.BlockSpec((1, 1, ng, gw), lambda b, h: (b, h, 0, 0))
    w1 = pl.BlockSpec((None, 2 * gw, CMP_HIDDEN), lambda b, h: (layer, 0, 0))
    w2 = pl.BlockSpec((None, CMP_HIDDEN, LANES), lambda b, h: (layer, 0, 0))
    pe = pl.BlockSpec((None, SUBLANES, 2 * gw), lambda b, h: (layer, 0, 0))
    tab = pl.BlockSpec((ng, LANES), lambda b, h: (0, 0))
    return pl.pallas_call(
        _nsa_cmp_kernel,
        out_shape=[jax.ShapeDtypeStruct((B, H, ng, HEAD_DIM), BF16), jax.ShapeDtypeStruct((B, H, HEAD_DIM, ng), BF16)],
        grid=(B, H),
        in_specs=[gspec, gspec, w1, w2, pe, w1,
                  pl.BlockSpec((None, HEAD_DIM, CMP_HIDDEN), lambda b, h: (layer, 0, 0)), pe, tab, tab],
        out_specs=[pl.BlockSpec((1, 1, ng, HEAD_DIM), lambda b, h: (b, h, 0, 0)),
                   pl.BlockSpec((1, 1, HEAD_DIM, ng), lambda b, h: (b, h, 0, 0))],
        compiler_params=_params("parallel", "parallel"),
        name="nsa_compress",
    )(g_k, g_v, prm["k_w1"], prm["k_w2"], prm["k_pe"], prm["v_w1"], prm["v_w2"], prm["v_pe"], cos_c, sin_c)


def _nsa_attn_kernel(q_ref, kcmp_ref, vcmp_ref, ks_ref, vs_ref, kw_ref, vw_ref, gt_ref, ov_ref, o_ref,
                     m_ref, l_ref, acc_ref, sb_ref, *, tq):
    G = NSA_GQA
    tk = tq
    qi = pl.program_id(2)
    q0 = qi * tq
    q_all = q_ref[0].reshape(G * tq, HEAD_DIM)
    pos = q0 + lax.broadcasted_iota(jnp.int32, (1, tq), 1)
    tiny = jnp.finfo(F32).tiny
    heads = lambda x: jnp.tile(x, (1, G))
    split = lambda x: [x[:, g * tq:(g + 1) * tq] for g in range(G)]

    ncp = kcmp_ref.shape[2]
    n_idx = lax.broadcasted_iota(jnp.int32, (ncp, 1), 0)
    cbias = jnp.where((n_idx * CMP_STRIDE + (CMP_BLOCK - 1)) <= pos, 0.0, NEG)
    any_cmp = jnp.where(pos >= CMP_BLOCK - 1, 1.0, 0.0)
    s = _dot_nt(kcmp_ref[0, 0], q_all) + heads(cbias)
    e = jnp.exp(s - jnp.max(s, axis=0, keepdims=True))
    p = e * (heads(any_cmp) / jnp.maximum(jnp.sum(e, axis=0, keepdims=True), tiny))
    o_cmp = split(_dot(vcmp_ref[0, 0], p.astype(BF16)))
    ps = split(p)
    psum = (ps[0] + ps[1]) + (ps[2] + ps[3])
    p_hi = psum.astype(BF16)
    p_lo = (psum - p_hi.astype(F32)).astype(BF16)
    imp = _dot(ov_ref[...], p_hi) + _dot(ov_ref[...], p_lo)

    wk = min(WINDOW + tq, ks_ref.shape[2])
    w0 = pl.multiple_of(jnp.maximum(q0 + tq - wk, 0), tk)
    wpos = w0 + lax.broadcasted_iota(jnp.int32, (wk, 1), 0)
    wbias = jnp.where((wpos <= pos) & (wpos > pos - WINDOW), 0.0, NEG)
    s = _dot_nt(kw_ref[0, 0, pl.ds(w0, wk), :], q_all) + heads(wbias)
    e = jnp.exp(s - jnp.max(s, axis=0, keepdims=True))
    pv = _dot(vw_ref[0, 0, :, pl.ds(w0, wk)], e.astype(BF16))
    o_win = split(pv / jnp.maximum(jnp.sum(e, axis=0, keepdims=True), tiny))

    n_sel = ov_ref.shape[0]
    blk = lax.broadcasted_iota(jnp.int32, (n_sel, 1), 0)
    cur = pos // SEL_BLOCK
    forced = (blk == 0) | (blk == cur) | (blk == cur - 1)
    valid = blk <= cur
    score = jnp.where(valid, jnp.where(forced, FORCE_SCORE, imp), -jnp.inf)
    groups = [score[g * SUBLANES:(g + 1) * SUBLANES, :] for g in range(n_sel // SUBLANES)]
    ranks = [jnp.zeros((SUBLANES, tq), F32) for _ in groups]
    sub = lax.broadcasted_iota(jnp.int32, (SUBLANES, 1), 0)
    for i in range(n_sel):
        ci = jnp.broadcast_to(score[i:i + 1, :], (SUBLANES, tq))
        for g, sg in enumerate(groups):
            if g > i // SUBLANES:
                beats = ci >= sg
            elif g < i // SUBLANES:
                beats = ci > sg
            else:
                beats = (ci > sg) | ((ci == sg) & (sub > i % SUBLANES))
            ranks[g] = ranks[g] + jnp.where(beats, 1.0, 0.0)
    rank = jnp.concatenate(ranks, axis=0)
    chosen = (rank < float(SEL_TOPK)) & valid
    sel_bias = jnp.where(chosen, 0.0, NEG)
    for j in range(n_sel):
        sb_ref[j] = jnp.broadcast_to(sel_bias[j:j + 1, :], (SUBLANES, tq))

    kcol = lax.broadcasted_iota(jnp.int32, (tk, 1), 0)

    def sel_update(kt, n, extra=None):
        bpt = n // SEL_BLOCK
        k0 = pl.multiple_of(kt * tk, tk)
        s = _dot_nt(ks_ref[0, 0, pl.ds(k0, n), :], q_all)
        b0 = kt * (tk // SEL_BLOCK)
        bias = jnp.concatenate([jnp.tile(sb_ref[b0 + j], (SEL_BLOCK // SUBLANES, 1)) for j in range(bpt)],
                               axis=0)
        if extra is not None:
            bias = bias + extra
        vt = vs_ref[0, 0, :, pl.ds(k0, n)]
        s = s + heads(bias)
        m_prev = m_ref[...]
        m_new = jnp.maximum(m_prev, jnp.max(s, axis=0, keepdims=True))
        p = jnp.exp(s - m_new)
        corr = jnp.exp(m_prev - m_new)
        l_ref[...] = corr * l_ref[...] + jnp.sum(p, axis=0, keepdims=True)
        acc_ref[...] = corr * acc_ref[...] + _dot(vt, p.astype(BF16))
        m_ref[...] = m_new

    m_ref[...] = jnp.full_like(m_ref, NEG)
    l_ref[...] = jnp.zeros_like(l_ref)
    acc_ref[...] = jnp.zeros_like(acc_ref)

    def sel_body(i, carry):
        sel_update(4 * i, 4 * tk)
        return carry

    lax.fori_loop(0, qi // 4, sel_body, 0)

    @pl.when(qi % 4 >= 2)
    def _():
        sel_update((qi // 4) * 4, 2 * tk)

    @pl.when(qi % 2 == 1)
    def _():
        sel_update(qi - 1, tk)

    causal = jnp.where(q0 + kcol <= pos, 0.0, NEG)
    sel_update(qi, tk, causal)
    o_sel = split(acc_ref[...] / jnp.maximum(l_ref[...], tiny))

    gt = gt_ref[0, 0]
    outs = [gt[g:g + 1] * o_cmp[g] + gt[G + g:G + g + 1] * o_sel[g] + gt[2 * G + g:2 * G + g + 1] * o_win[g]
            for g in range(G)]
    for half in range(G // 2):
        pair = jnp.concatenate(outs[2 * half:2 * half + 2], axis=0)
        o_ref[0, :, half * LANES:(half + 1) * LANES] = pair.T.astype(o_ref.dtype)


def _nsa_attention(q, k_cmp, v_cmp_t, ks, vs_t, kw, vw_t, gates_t, overlap_t, *, tq=256):
    B, _, S, _ = q.shape
    H, G = NSA_KV_HEADS, NSA_GQA
    tq = min(tq, S)
    assert WINDOW == 2 * tq or S <= tq
    ncp = k_cmp.shape[2]
    n_cmp = (S - CMP_BLOCK) // CMP_STRIDE + 1
    keys = pl.BlockSpec((1, 1, S, HEAD_DIM), lambda b, h, i: (b, h, 0, 0))
    vals = pl.BlockSpec((1, 1, HEAD_DIM, S), lambda b, h, i: (b, h, 0, 0))
    return pl.pallas_call(
        functools.partial(_nsa_attn_kernel, tq=tq),
        out_shape=jax.ShapeDtypeStruct((B, S, D_NSA), BF16),
        grid=(B, H, S // tq),
        in_specs=[
            pl.BlockSpec((1, G, tq, HEAD_DIM), lambda b, h, i: (b, h, i, 0)),
            pl.BlockSpec((1, 1, ncp, HEAD_DIM), lambda b, h, i: (b, h, 0, 0)),
            pl.BlockSpec((1, 1, HEAD_DIM, ncp), lambda b, h, i: (b, h, 0, 0)),
            keys, vals, keys, vals,
            pl.BlockSpec((1, 1, 12, tq), lambda b, h, i: (b, h, 0, i)),
            pl.BlockSpec(overlap_t.shape, lambda b, h, i: (0, 0)),
        ],
        out_specs=pl.BlockSpec((1, tq, G * HEAD_DIM), lambda b, h, i: (b, i, h)),
        scratch_shapes=[pltpu.VMEM((1, G * tq), F32), pltpu.VMEM((1, G * tq), F32),
                        pltpu.VMEM((HEAD_DIM, G * tq), F32), pltpu.VMEM((S // SEL_BLOCK, SUBLANES, tq), F32)],
        compiler_params=_params("parallel", "parallel", "arbitrary"),
        name="nsa_attention",
    )(q, k_cmp, v_cmp_t, ks, vs_t, kw, vw_t, gates_t, overlap_t)


def _rope_tables(pos, heads):
    half = ROPE_DIM // 2
    inv_freq = ROPE_THETA ** (-jnp.arange(half, dtype=F32) / half)
    ang = pos.astype(F32)[:, None] * inv_freq
    cos, sin = jnp.cos(ang), jnp.sin(ang)
    n = pos.shape[0]
    rest = HEAD_DIM - ROPE_DIM
    cos_h = jnp.concatenate([cos, cos, jnp.ones((n, rest), F32)], axis=1)
    sin_h = jnp.concatenate([-sin, sin, jnp.zeros((n, rest), F32)], axis=1)
    return jnp.tile(cos_h, (1, heads)), jnp.tile(sin_h, (1, heads))


def _overlap_matrix(S, ncp):
    n_cmp = (S - CMP_BLOCK) // CMP_STRIDE + 1
    n_sel = S // SEL_BLOCK
    cmp_start = np.arange(n_cmp) * CMP_STRIDE
    sel_start = np.arange(n_sel) * SEL_BLOCK
    ov = np.clip(np.minimum(cmp_start[:, None] + CMP_BLOCK, sel_start[None, :] + SEL_BLOCK)
                 - np.maximum(cmp_start[:, None], sel_start[None, :]), 0, None) / CMP_BLOCK
    full = np.zeros((n_sel, ncp), np.float32)
    full[:, :n_cmp] = ov.T
    return jnp.asarray(full, BF16)


RW_HEAD_ORDER = tuple(2 * (e % RW_PAIRS) + e // RW_PAIRS for e in range(RW_HEADS))


def _parity_major(a, axis):
    axis = axis % a.ndim
    shape = a.shape
    a = a.reshape(shape[:axis] + (RW_HEADS, HEAD_DIM) + shape[axis + 1:])
    a = jnp.take(a, jnp.asarray(RW_HEAD_ORDER), axis=axis)
    return a.reshape(shape)


def _block_ones(n):
    idx = np.arange(n) // HEAD_DIM
    return jnp.asarray((idx[:, None] == idx[None, :]).astype(np.float32), BF16)


def _w_in_layout_kernel(w_ref, g_ref, o_ref):
    nsa0 = RW_COLS + D_POOL
    o_ref[0, :, P_RW_OFF:P_RW_OFF + RW_COLS] = w_ref[0, :, 0:RW_COLS].astype(BF16)
    o_ref[0, :, P_POOL_OFF:P_POOL_OFF + D_POOL] = w_ref[0, :, RW_COLS:nsa0].astype(BF16)
    o_ref[0, :, P_NSA_OFF:P_NSA_OFF + D_NSA] = w_ref[0, :, nsa0:nsa0 + D_NSA].astype(BF16)
    rows = o_ref.shape[1]
    for i in range(6):
        src = nsa0 + D_NSA + i * NSA_KV
        part = jnp.concatenate([w_ref[0, :, src:src + NSA_KV], jnp.zeros((rows, KV_SLOT - NSA_KV), F32)], axis=1)
        o_ref[0, :, P_NSA_OFF + D_NSA + i * KV_SLOT:P_NSA_OFF + D_NSA + (i + 1) * KV_SLOT] = part.astype(BF16)
    o_ref[0, :, P_NSA_OFF + NSA_GATE_OFF:P_NSA_OFF + NSA_PAD] = g_ref[0].astype(BF16)


def _w_in_layout(w_in, *, tr=256):
    L = w_in.shape[0]
    gates = w_in[:, :, RW_COLS + D_POOL + D_NSA + 6 * NSA_KV:]
    gates = gates.reshape(L, D_MODEL, NSA_KV_HEADS, NSA_GQA, 3).transpose(0, 1, 2, 4, 3)
    gates = gates.reshape(L, D_MODEL, 3 * NSA_HEADS)
    gates = jnp.pad(gates, ((0, 0), (0, 0), (0, NSA_PAD - NSA_GATE_OFF - 3 * NSA_HEADS)))
    return pl.pallas_call(
        _w_in_layout_kernel,
        out_shape=jax.ShapeDtypeStruct((L, D_MODEL, P_COLS), BF16),
        grid=(L, D_MODEL // tr),
        in_specs=[pl.BlockSpec((1, tr, IN_COLS), lambda l, i: (l, i, 0)),
                  pl.BlockSpec((1, tr, NSA_PAD - NSA_GATE_OFF), lambda l, i: (l, i, 0))],
        out_specs=pl.BlockSpec((1, tr, P_COLS), lambda l, i: (l, i, 0)),
        compiler_params=_params("parallel", "parallel"),
        name="w_in_layout",
    )(w_in, gates)


def _ffn_up_layout_kernel(w_ref, a_ref, b_ref):
    pad = jnp.zeros((w_ref.shape[1], D_FF_PAD - D_FF), BF16)
    a_ref[0] = jnp.concatenate([w_ref[0, :, :D_FF].astype(BF16), pad], axis=1)
    b_ref[0] = jnp.concatenate([w_ref[0, :, D_FF:].astype(BF16), pad], axis=1)


def _ffn_up_layout(w, *, tr=128):
    L, D, _ = w.shape
    out = jax.ShapeDtypeStruct((L, D, D_FF_PAD), BF16)
    ospec = pl.BlockSpec((1, tr, D_FF_PAD), lambda l, i: (l, i, 0))
    return pl.pallas_call(
        _ffn_up_layout_kernel,
        out_shape=[out, out],
        grid=(L, D // tr),
        in_specs=[pl.BlockSpec((1, tr, 2 * D_FF), lambda l, i: (l, i, 0))],
        out_specs=[ospec, ospec],
        compiler_params=_params("parallel", "parallel"),
        name="ffn_up_layout",
    )(w)


def _gate_bias_layout(gate_b):
    L = gate_b.shape[0]
    gb = gate_b.reshape(L, NSA_KV_HEADS, NSA_GQA, 3).transpose(0, 1, 3, 2).reshape(L, 1, 3 * NSA_HEADS)
    return jnp.pad(gb, ((0, 0), (0, 0), (0, LANES - 3 * NSA_HEADS)))


def kernel(x, ffn1_w_up, ffn1_w_down, ln1_g, ln1_b, w_in, rw_mu, rw_w0, rw_w2, rw_a0, rw_a2, rw_g2, rw_k_k,
           rw_k_a, rw_r_k, rw_gn_g, rw_gn_b, pool_w, pool_b, pool_scale, nsa_cmp_pe_k, nsa_cmp_pe_v,
           nsa_cmp_k_w1, nsa_cmp_k_w2, nsa_cmp_v_w1, nsa_cmp_v_w2, nsa_gate_b, w_out, ln2_g, ln2_b,
           ffn2_w_up, ffn2_w_down, ln3_g, ln3_b):
    prm = _prepare(x.shape[1], ffn1_w_up, ffn1_w_down, ln1_g, ln1_b, w_in, rw_mu, rw_w0, rw_w2, rw_a0, rw_a2,
                   rw_g2, rw_k_k, rw_k_a, rw_r_k, rw_gn_g, rw_gn_b, pool_w, pool_b, pool_scale, nsa_cmp_pe_k,
                   nsa_cmp_pe_v, nsa_cmp_k_w1, nsa_cmp_k_w2, nsa_cmp_v_w1, nsa_cmp_v_w2, nsa_gate_b, w_out,
                   ln2_g, ln2_b, ffn2_w_up, ffn2_w_down, ln3_g, ln3_b)
    B, S, D = x.shape
    h = x.reshape(B * S, D)
    for l in range(w_in.shape[0]):
        h = _layer(h, prm, l, B, S)
    return h.reshape(B, S, D)


def _prepare(S, ffn1_w_up, ffn1_w_down, ln1_g, ln1_b, w_in, rw_mu, rw_w0, rw_w2, rw_a0, rw_a2, rw_g2, rw_k_k,
             rw_k_a, rw_r_k, rw_gn_g, rw_gn_b, pool_w, pool_b, pool_scale, nsa_cmp_pe_k, nsa_cmp_pe_v,
             nsa_cmp_k_w1, nsa_cmp_k_w2, nsa_cmp_v_w1, nsa_cmp_v_w2, nsa_gate_b, w_out, ln2_g, ln2_b,
             ffn2_w_up, ffn2_w_down, ln3_g, ln3_b):
    L = w_in.shape[0]
    fpad = D_FF_PAD - D_FF

    up = _ffn_up_layout

    def down(w):
        return jnp.pad(w, ((0, 0), (0, fpad), (0, 0))).astype(BF16)

    row = lambda v: v[:, None, :]
    f1a, f1b = up(ffn1_w_up)
    f2a, f2b = up(ffn2_w_up)
    w_out_b = w_out.astype(BF16)
    gw = CMP_BLOCK * HEAD_DIM
    pad_w2 = lambda w: jnp.pad(w, ((0, 0), (0, 0), (0, LANES - HEAD_DIM))).astype(BF16)
    pe_rows = lambda pe: jnp.broadcast_to(pe.reshape(L, 1, gw), (L, SUBLANES, gw)).astype(BF16)
    ncp = S // CMP_STRIDE
    cos_t, sin_t = _rope_tables(jnp.arange(S), 4)
    cos_c, sin_c = _rope_tables(jnp.arange(ncp) * CMP_STRIDE + (CMP_BLOCK - 1), 2)
    return dict(
        ffn1=(f1a, f1b, down(ffn1_w_down), row(ln1_g), row(ln1_b)),
        ffn2=(f2a, f2b, down(ffn2_w_down), row(ln3_g), row(ln3_b)),
        w_in=_w_in_layout(w_in),
        w_out=(_parity_major(w_out_b[:, :D_RWKV], 1), w_out_b[:, D_RWKV:D_RWKV + D_POOL],
               w_out_b[:, D_RWKV + D_POOL:]),
        ln2=(row(ln2_g), row(ln2_b)),
        rw=dict(mu=row(rw_mu), w0=row(rw_w0), w2=rw_w2.astype(BF16), a0=row(rw_a0), a2=rw_a2.astype(BF16),
                g2=_parity_major(rw_g2, -1).astype(BF16), k_k=row(rw_k_k), k_a=row(rw_k_a),
                r_k=rw_r_k.reshape(L, 1, D_RWKV), gn_g=row(_parity_major(rw_gn_g, -1)),
                gn_b=row(_parity_major(rw_gn_b, -1)), ones=_block_ones(D_RWKV)),
        cmp=dict(k_w1=nsa_cmp_k_w1.reshape(L, gw, CMP_HIDDEN).astype(BF16), k_w2=pad_w2(nsa_cmp_k_w2),
                 k_pe=pe_rows(nsa_cmp_pe_k),
                 v_w1=nsa_cmp_v_w1.reshape(L, gw, CMP_HIDDEN).astype(BF16), v_w2=nsa_cmp_v_w2.transpose(0, 2, 1).astype(BF16),
                 v_pe=pe_rows(nsa_cmp_pe_v)),
        gate_b=_gate_bias_layout(nsa_gate_b),
        pool=(pool_w.astype(BF16), row(pool_b), row(pool_scale)),
        rope=(cos_t, sin_t), rope_cmp=(cos_c, sin_c), overlap=_overlap_matrix(S, ncp))


def _mixers(p_all, prm, l):
    r, w, k, v, kk, be, bo, g = _rw_prep(p_all, prm["rw"], l)
    y_rw = _rw_scan(r, w, k, v, kk, be, bo, g, prm["rw"], l)
    y_pool = _pool_mix(p_all, *prm["pool"], l)
    q, kc, vc, ks, vs, kw, vw, gates = _nsa_prep(p_all, *prm["rope"], prm["gate_b"], l)
    k_cmp, v_cmp = _nsa_compress(kc, vc, prm["cmp"], *prm["rope_cmp"], l)
    y_nsa = _nsa_attention(q, k_cmp, v_cmp, ks, vs, kw, vw, gates, prm["overlap"])
    return y_rw, y_pool, y_nsa


def _layer(h, prm, l, B, S):
    T = B * S
    h = _ffn_ln(h, *prm["ffn1"], l)
    p_all = _in_proj(h, prm["w_in"], l).reshape(B, S, P_COLS)
    y_rw, y_pool, y_nsa = _mixers(p_all, prm, l)
    h = _out_proj_ln(h, y_rw.reshape(T, D_RWKV), y_pool.reshape(T, D_POOL), y_nsa.reshape(T, D_NSA),
                     prm["w_out"], *prm["ln2"], l)
    return _ffn_ln(h, *prm["ffn2"], l)
```

```python
import functools

import numpy as np
import jax
import jax.numpy as jnp
from jax import lax
from jax.experimental import pallas as pl
from jax.experimental.pallas import tpu as pltpu

F32 = jnp.float32
BF16 = jnp.bfloat16

D_MODEL = 2048
DEPTH = 4
HEAD_DIM = 64
D_RWKV = 768
D_POOL = 512
D_NSA = 768
RW_HEADS = 12
RW_DECAY_LORA = 64
RW_A_LORA = 64
RW_GATE_LORA = 128
RW_GN_EPS = 64e-5
RW_COLS = 3 * D_RWKV + RW_DECAY_LORA + RW_A_LORA + RW_GATE_LORA
POOL_WINDOWS = (2, 4, 8, 16)
POOL_GROUP = 128
NSA_HEADS = 12
NSA_KV_HEADS = 3
NSA_GQA = 4
NSA_KV = 192
NSA_COLS = D_NSA + 6 * NSA_KV + 3 * NSA_HEADS
CMP_BLOCK = 32
CMP_STRIDE = 16
CMP_HIDDEN = 256
SEL_BLOCK = 64
SEL_TOPK = 16
FORCE_SCORE = 1e9
WINDOW = 512
ROPE_THETA = 500000.0
ROPE_DIM = 16
D_FF = 5504
IN_COLS = RW_COLS + D_POOL + NSA_COLS
ALPHA = (2 * DEPTH) ** 0.25
LN_EPS = 1e-5

LANES = 128
SUBLANES = 8
VMEM_LIMIT = 56 * 1024 * 1024

KV_SLOT = 2 * LANES
NSA_PAD = 2560
P_RW_OFF = 0
P_NSA_OFF = RW_COLS
P_POOL_OFF = RW_COLS + NSA_PAD
P_COLS = RW_COLS + NSA_PAD + D_POOL
NSA_GATE_OFF = D_NSA + 6 * KV_SLOT
D_FF_PAD = 5632

NEG = -1e30


def _params(*sem):
    return pltpu.CompilerParams(dimension_semantics=sem, vmem_limit_bytes=VMEM_LIMIT)


def _layer_norm(z, g, b):
    mu = jnp.mean(z, axis=-1, keepdims=True)
    zc = z - mu
    var = jnp.mean(zc * zc, axis=-1, keepdims=True)
    return zc * lax.rsqrt(var + LN_EPS) * g + b


def _dot(a, b):
    return jnp.dot(a, b, preferred_element_type=F32)


def _dot_nt(a, b):
    return lax.dot_general(a, b, (((1,), (1,)), ((), ())), preferred_element_type=F32)


def _dot_split(x, w):
    hi = x.astype(BF16)
    lo = (x - hi.astype(F32)).astype(BF16)
    return _dot(hi, w) + _dot(lo, w)


FFN_LN_ROWS = 256
FFN_MM_ROWS = 512


def _ffn_kernel(x_ref, wa_ref, wb_ref, wd_ref, g_ref, b_ref, o_ref, xb_ref):
    k = pl.program_id(1)

    @pl.when(k == 0)
    def _():
        xb_ref[...] = x_ref[...].astype(BF16)
        o_ref[...] = jnp.zeros_like(o_ref)

    for c in range(o_ref.shape[0] // FFN_MM_ROWS):
        rows = slice(c * FFN_MM_ROWS, (c + 1) * FFN_MM_ROWS)
        xb = xb_ref[rows, :]
        a = _dot(xb, wa_ref[...])
        b = _dot(xb, wb_ref[...])
        h = (a / (1.0 + jnp.exp(-a))) * b
        o_ref[rows, :] += _dot(h.astype(BF16), wd_ref[...])

    @pl.when(k == pl.num_programs(1) - 1)
    def _():
        for c in range(o_ref.shape[0] // FFN_LN_ROWS):
            rows = slice(c * FFN_LN_ROWS, (c + 1) * FFN_LN_ROWS)
            z = ALPHA * x_ref[rows, :] + 0.5 * o_ref[rows, :]
            o_ref[rows, :] = _layer_norm(z, g_ref[...], b_ref[...])


def _ffn_ln(x, wa, wb, wd, g, b, layer, *, tm=1024, tf=512):
    T, D = x.shape
    fp = wa.shape[-1]
    tm = min(tm, T)
    return pl.pallas_call(
        _ffn_kernel,
        out_shape=jax.ShapeDtypeStruct((T, D), F32),
        grid=(T // tm, fp // tf),
        in_specs=[
            pl.BlockSpec((tm, D), lambda i, k: (i, 0)),
            pl.BlockSpec((None, D, tf), lambda i, k: (layer, 0, k)),
            pl.BlockSpec((None, D, tf), lambda i, k: (layer, 0, k)),
            pl.BlockSpec((None, tf, D), lambda i, k: (layer, k, 0)),
            pl.BlockSpec((None, 1, D), lambda i, k: (layer, 0, 0)),
            pl.BlockSpec((None, 1, D), lambda i, k: (layer, 0, 0)),
        ],
        out_specs=pl.BlockSpec((tm, D), lambda i, k: (i, 0)),
        scratch_shapes=[pltpu.VMEM((tm, D), BF16)],
        compiler_params=_params("parallel", "arbitrary"),
        name="ffn_ln",
    )(x, wa, wb, wd, g, b)


def _inproj_kernel(x_ref, w_ref, o_ref, xb_ref):
    @pl.when(pl.program_id(1) == 0)
    def _():
        xb_ref[...] = x_ref[...].astype(BF16)

    o_ref[...] = _dot(xb_ref[...], w_ref[...])


def _in_proj(x, w, layer, *, tm=1024, tn=P_COLS // 4):
    T, D = x.shape
    n = w.shape[-1]
    tm = min(tm, T)
    return pl.pallas_call(
        _inproj_kernel,
        out_shape=jax.ShapeDtypeStruct((T, n), F32),
        grid=(T // tm, n // tn),
        in_specs=[
            pl.BlockSpec((tm, D), lambda i, j: (i, 0)),
            pl.BlockSpec((None, D, tn), lambda i, j: (layer, 0, j)),
        ],
        out_specs=pl.BlockSpec((tm, tn), lambda i, j: (i, j)),
        scratch_shapes=[pltpu.VMEM((tm, D), BF16)],
        compiler_params=_params("parallel", "arbitrary"),
        name="in_proj",
    )(x, w)


def _outproj_kernel(x_ref, yr_ref, yp_ref, yn_ref, wr_ref, wp_ref, wn_ref, g_ref, b_ref, o_ref):
    y = _dot(yr_ref[...], wr_ref[...]) + _dot(yp_ref[...], wp_ref[...]) + _dot(yn_ref[...], wn_ref[...])
    o_ref[...] = _layer_norm(ALPHA * x_ref[...] + y, g_ref[...], b_ref[...])


def _out_proj_ln(x, y_rw, y_pool, y_nsa, w_out, g, b, layer, *, tm=512):
    T, D = x.shape
    return pl.pallas_call(
        _outproj_kernel,
        out_shape=jax.ShapeDtypeStruct((T, D), F32),
        grid=(T // tm,),
        in_specs=[
            pl.BlockSpec((tm, D), lambda i: (i, 0)),
            pl.BlockSpec((tm, D_RWKV), lambda i: (i, 0)),
            pl.BlockSpec((tm, D_POOL), lambda i: (i, 0)),
            pl.BlockSpec((tm, D_NSA), lambda i: (i, 0)),
            pl.BlockSpec((None, D_RWKV, D), lambda i: (layer, 0, 0)),
            pl.BlockSpec((None, D_POOL, D), lambda i: (layer, 0, 0)),
            pl.BlockSpec((None, D_NSA, D), lambda i: (layer, 0, 0)),
            pl.BlockSpec((None, 1, D), lambda i: (layer, 0, 0)),
            pl.BlockSpec((None, 1, D), lambda i: (layer, 0, 0)),
        ],
        out_specs=pl.BlockSpec((tm, D), lambda i: (i, 0)),
        compiler_params=_params("parallel"),
        name="out_proj_ln",
    )(x, y_rw, y_pool, y_nsa, w_out[0], w_out[1], w_out[2], g, b)


POOL_HALO = 16


def _pool_kernel(p_ref, halo_ref, w_ref, b_ref, sc_ref, o_ref, xs_ref, *, ts):
    s = pl.program_id(1)
    x = p_ref[0]
    halo = jnp.where(s > 0, halo_ref[0], 0.0)
    xs_ref[0:POOL_HALO, :] = halo
    xs_ref[POOL_HALO:POOL_HALO + ts, :] = x
    t1 = (s * ts + 1 + lax.broadcasted_iota(jnp.int32, (ts, 1), 0)).astype(F32)
    for gi, win in enumerate(POOL_WINDOWS):
        c0 = gi * POOL_GROUP
        acc = x[:, c0:c0 + POOL_GROUP]
        for j in range(1, win):
            acc = acc + xs_ref[POOL_HALO - j:POOL_HALO - j + ts, c0:c0 + POOL_GROUP]
        pooled = acc / jnp.minimum(t1, float(win)) - x[:, c0:c0 + POOL_GROUP]
        z = _dot(pooled.astype(BF16), w_ref[gi]) + b_ref[:, c0:c0 + POOL_GROUP]
        o_ref[0, :, c0:c0 + POOL_GROUP] = (z * sc_ref[:, c0:c0 + POOL_GROUP]).astype(o_ref.dtype)


def _pool_mix(p_all, pool_w, pool_b, pool_scale, layer, *, ts=512):
    B, S, _ = p_all.shape
    cb = P_POOL_OFF // D_POOL
    hb = ts // POOL_HALO
    return pl.pallas_call(
        functools.partial(_pool_kernel, ts=ts),
        out_shape=jax.ShapeDtypeStruct((B, S, D_POOL), BF16),
        grid=(B, S // ts),
        in_specs=[
            pl.BlockSpec((1, ts, D_POOL), lambda b, s: (b, s, cb)),
            pl.BlockSpec((1, POOL_HALO, D_POOL), lambda b, s: (b, jnp.maximum(s * hb - 1, 0), cb)),
            pl.BlockSpec((None, 4, POOL_GROUP, POOL_GROUP), lambda b, s: (layer, 0, 0, 0)),
            pl.BlockSpec((None, 1, D_POOL), lambda b, s: (layer, 0, 0)),
            pl.BlockSpec((None, 1, D_POOL), lambda b, s: (layer, 0, 0)),
        ],
        out_specs=pl.BlockSpec((1, ts, D_POOL), lambda b, s: (b, s, 0)),
        scratch_shapes=[pltpu.VMEM((ts + POOL_HALO, D_POOL), F32)],
        compiler_params=_params("parallel", "parallel"),
        name="pool_mix",
    )(p_all, p_all, pool_w, pool_b, pool_scale)


def _softplus(z):
    return jnp.maximum(z, 0.0) + jnp.log1p(jnp.exp(-jnp.abs(z)))


def _sigmoid(z):
    return 1.0 / (1.0 + jnp.exp(-z))


def _lanes_to_parity_major(x):
    n_pairs = x.shape[1] // LANES
    low = lax.broadcasted_iota(jnp.int32, (1, LANES), 1) < HEAD_DIM

    def chunk(e, to_low):
        pair, parity = e % n_pairs, e // n_pairs
        src = x[:, pair * LANES:(pair + 1) * LANES]
        return src if (parity == 0) == to_low else pltpu.roll(src, HEAD_DIM, 1)

    return jnp.concatenate([jnp.where(low, chunk(2 * d, True), chunk(2 * d + 1, False))
                            for d in range(n_pairs)], axis=1)


def _rw_prep_kernel(p_ref, prev_ref, mu_ref, w0_ref, w2_ref, a0_ref, a2_ref, g2_ref, kk_ref, ka_ref,
                    rk_ref, ones_ref,
                    r_o, w_o, k_o, v_o, kk_o, be_o, bo_o, g_o, *, ts):
    s = pl.program_id(1)
    x = p_ref[0]
    last = jnp.where(s > 0, prev_ref[0][SUBLANES - 1:SUBLANES, :], 0.0)
    row = lax.broadcasted_iota(jnp.int32, (ts, 1), 0)
    shifted = jnp.where(row == 0, last, pltpu.roll(x, 1, 0))
    xm = x + (shifted - x) * mu_ref[...]
    c = D_RWKV
    r = xm[:, 0:c]
    k = xm[:, c:2 * c]
    v = xm[:, 2 * c:3 * c]
    lora = xm[:, 3 * c:3 * c + LANES]
    wl = lora[:, :RW_DECAY_LORA]
    al = lora[:, RW_DECAY_LORA:]
    gl = xm[:, 3 * c + LANES:]
    w = -_softplus(-(w0_ref[...] + _dot(jnp.tanh(wl).astype(BF16), w2_ref[...]))) - 0.5
    decay = jnp.exp(-jnp.exp(w))
    a = _sigmoid(a0_ref[...] + _dot(al.astype(BF16), a2_ref[...]))
    g = _dot(_sigmoid(gl).astype(BF16), g2_ref[...])
    ones = ones_ref[...]
    kk = k * kk_ref[...]
    nrm = jnp.sqrt(_dot_split(kk * kk, ones))
    kk = kk / jnp.maximum(nrm, 1e-12)
    k_mod = k * (1.0 + (a - 1.0) * ka_ref[...])
    bonus = _dot_split(r * k_mod * rk_ref[...], ones) * v
    r_o[0] = r
    w_o[0] = decay
    k_o[0] = k_mod
    v_o[0] = v
    kk_o[0] = kk
    be_o[0] = kk * a
    bo_o[0] = _lanes_to_parity_major(bonus).astype(bo_o.dtype)
    g_o[0] = g.astype(g_o.dtype)


def _rw_prep(p_all, prm, layer, *, ts=512):
    B, S, _ = p_all.shape
    cb = P_RW_OFF // RW_COLS
    hb = ts // SUBLANES
    vec = lambda n: pl.BlockSpec((None, 1, n), lambda b, s: (layer, 0, 0))
    mat = lambda m, n: pl.BlockSpec((None, m, n), lambda b, s: (layer, 0, 0))
    out = jax.ShapeDtypeStruct((B, S, D_RWKV), F32)
    ospec = pl.BlockSpec((1, ts, D_RWKV), lambda b, s: (b, s, 0))
    return pl.pallas_call(
        functools.partial(_rw_prep_kernel, ts=ts),
        out_shape=[out] * 6 + [jax.ShapeDtypeStruct((B, S, D_RWKV), BF16)] * 2,
        grid=(B, S // ts),
        in_specs=[
            pl.BlockSpec((1, ts, RW_COLS), lambda b, s: (b, s, cb)),
            pl.BlockSpec((1, SUBLANES, RW_COLS), lambda b, s: (b, jnp.maximum(s * hb - 1, 0), cb)),
            vec(RW_COLS), vec(D_RWKV), mat(RW_DECAY_LORA, D_RWKV), vec(D_RWKV), mat(RW_A_LORA, D_RWKV),
            mat(RW_GATE_LORA, D_RWKV), vec(D_RWKV), vec(D_RWKV), vec(D_RWKV),
            pl.BlockSpec((D_RWKV, D_RWKV), lambda b, s: (0, 0)),
        ],
        out_specs=[ospec] * 8,
        compiler_params=_params("parallel", "parallel"),
        name="rw_prep",
    )(p_all, p_all, prm["mu"], prm["w0"], prm["w2"], prm["a0"], prm["a2"], prm["g2"], prm["k_k"],
      prm["k_a"], prm["r_k"], prm["ones"])


RW_PAIRS = D_RWKV // LANES
RW_STEPS = 8


def _rw_scan_kernel(r_ref, w_ref, k_ref, v_ref, kk_ref, be_ref, bo_ref, g_ref, gng_ref, gnb_ref, ones_ref,
                    pair_ref, half_ref, o_ref, st_ref, vc_ref, sr_ref, y_ref, *, ts, nb):
    s = pl.program_id(1)

    @pl.when(s == 0)
    def _():
        st_ref[...] = jnp.zeros_like(st_ref)

    side, stack = 2, RW_PAIRS // 2
    rows, width = stack * HEAD_DIM, side * LANES
    lane = lax.broadcasted_iota(jnp.int32, (rows, width), 1)
    row = lax.broadcasted_iota(jnp.int32, (rows, width), 0)
    eye = jnp.where(lane % HEAD_DIM == row % HEAD_DIM, 1.0, 0.0)
    ones_blk = pair_ref[...]

    def spread(x8, j):
        blocks = []
        for st in range(stack):
            lanes = jnp.concatenate([x8[j:j + 1, (sd * stack + st) * LANES:(sd * stack + st + 1) * LANES]
                                     for sd in range(side)], axis=1)
            blocks.append(jnp.broadcast_to(lanes, (HEAD_DIM, width)))
        return jnp.concatenate(blocks, axis=0)

    def step(i, carry):
        base = pl.multiple_of(i * RW_STEPS, RW_STEPS)
        tiles = [[ref[b, pl.ds(base, RW_STEPS), :] for ref in (kk_ref, w_ref, be_ref, k_ref, r_ref, v_ref)]
                 for b in range(nb)]
        for b in range(nb):
            lhs = jnp.concatenate([eye * spread(tiles[b][5], j) for j in range(RW_STEPS)], axis=0)
            vc_ref[b] = _dot(lhs.astype(BF16), ones_blk)
        sts = [st_ref[b] for b in range(nb)]
        for j in range(RW_STEPS):
            for b in range(nb):
                kk8, w8, be8, k8, r8, _ = tiles[b]
                st = sts[b]
                sa = _dot((st * spread(kk8, j)).astype(BF16), ones_blk)
                st = (st * spread(w8, j) - sa * spread(be8, j)
                      + vc_ref[b, j * rows:(j + 1) * rows, :] * spread(k8, j))
                sr = (st * spread(r8, j)).astype(BF16)
                for sd in range(side):
                    lo = (j * side + sd) * rows
                    sr_ref[b, lo:lo + rows, :] = sr[:, sd * LANES:(sd + 1) * LANES]
                sts[b] = st
        for b in range(nb):
            st_ref[b] = sts[b]
            yt = _dot_nt(half_ref[...], sr_ref[b])
            per = side * rows
            y_ref[b, pl.ds(base, RW_STEPS), :] = jnp.concatenate(
                [jnp.concatenate([yt[0:1, j * per:(j + 1) * per], yt[1:2, j * per:(j + 1) * per]], axis=1)
                 for j in range(RW_STEPS)], axis=0)
        return carry

    lax.fori_loop(0, ts // RW_STEPS, step, 0)

    ones = ones_ref[...]
    for b in range(nb):
        y = y_ref[b]
        mean = _dot_split(y, ones) * (1.0 / HEAD_DIM)
        yc = y - mean
        var = _dot_split(yc * yc, ones) * (1.0 / HEAD_DIM)
        yn = yc * lax.rsqrt(var + RW_GN_EPS) * gng_ref[...] + gnb_ref[...]
        o_ref[b] = ((yn + bo_ref[b]) * g_ref[b]).astype(o_ref.dtype)


def _rw_scan(r, w, k, v, kk, be, bo, g, prm, layer, *, ts=128):
    B, S, _ = r.shape
    nb = 4 if B % 4 == 0 else (2 if B % 2 == 0 else 1)
    rows, width = RW_PAIRS // 2 * HEAD_DIM, 2 * LANES
    seq = pl.BlockSpec((nb, ts, D_RWKV), lambda b, s: (b, s, 0))
    vec = pl.BlockSpec((None, 1, D_RWKV), lambda b, s: (layer, 0, 0))
    return pl.pallas_call(
        functools.partial(_rw_scan_kernel, ts=ts, nb=nb),
        out_shape=jax.ShapeDtypeStruct((B, S, D_RWKV), BF16),
        grid=(B // nb, S // ts),
        in_specs=[seq] * 8 + [vec, vec, pl.BlockSpec((D_RWKV, D_RWKV), lambda b, s: (0, 0)),
                              pl.BlockSpec((width, width), lambda b, s: (0, 0)),
                              pl.BlockSpec((SUBLANES, LANES), lambda b, s: (0, 0))],
        out_specs=seq,
        scratch_shapes=[pltpu.VMEM((nb, rows, width), F32), pltpu.VMEM((nb, RW_STEPS * rows, width), F32),
                        pltpu.VMEM((nb, RW_STEPS * RW_PAIRS * HEAD_DIM, LANES), BF16),
                        pltpu.VMEM((nb, ts, D_RWKV), F32)],
        compiler_params=_params("parallel", "arbitrary"),
        name="rw_scan",
    )(r, w, k, v, kk, be, bo, g, prm["gn_g"], prm["gn_b"], prm["ones"], prm["ones"][:width, :width],
      prm["ones"][:SUBLANES * HEAD_DIM:HEAD_DIM, :LANES])


def _rope(x, cos, sin):
    half = ROPE_DIM // 2
    w = x.shape[-1]
    d = lax.broadcasted_iota(jnp.int32, x.shape, 1) % HEAD_DIM
    partner = jnp.where(d < half, pltpu.roll(x, w - half, 1), pltpu.roll(x, half, 1))
    return x * cos + partner * sin


def _nsa_prep_kernel(p_ref, cos_ref, sin_ref, gb_ref, q_o, kc_o, vc_o, ks_o, vs_o, kw_o, vw_o, gt_o):
    cos = cos_ref[...]
    sin = sin_ref[...]
    for hq in range(NSA_HEADS // 4):
        x = p_ref[0, :, hq * KV_SLOT:(hq + 1) * KV_SLOT]
        qr = _rope(x, cos, sin) * (HEAD_DIM ** -0.5)
        for j in range(4):
            q_o[0, 4 * hq + j] = qr[:, j * HEAD_DIM:(j + 1) * HEAD_DIM].astype(BF16)
    for i, ref in enumerate((kc_o, vc_o, ks_o, vs_o, kw_o, vw_o)):
        part = p_ref[0, :, D_NSA + i * KV_SLOT:D_NSA + (i + 1) * KV_SLOT]
        if i in (2, 4):
            part = _rope(part, cos, sin)
        if i in (3, 5):
            part = part.T
            for h in range(NSA_KV_HEADS):
                ref[0, h] = part[h * HEAD_DIM:(h + 1) * HEAD_DIM, :].astype(BF16)
        else:
            for h in range(NSA_KV_HEADS):
                ref[0, h] = part[:, h * HEAD_DIM:(h + 1) * HEAD_DIM].astype(BF16)
    gts = _sigmoid(p_ref[0, :, NSA_GATE_OFF:NSA_GATE_OFF + LANES] + gb_ref[...]).T
    for h in range(NSA_KV_HEADS):
        gt_o[0, h] = gts[h * 12:(h + 1) * 12, :]


def _nsa_prep(p_all, cos, sin, gate_b, layer, *, ts=512):
    B, S, _ = p_all.shape
    H = NSA_KV_HEADS
    k_shape = jax.ShapeDtypeStruct((B, H, S, HEAD_DIM), BF16)
    k_spec = pl.BlockSpec((1, H, ts, HEAD_DIM), lambda b, s: (b, 0, s, 0))
    vt_shape = jax.ShapeDtypeStruct((B, H, HEAD_DIM, S), BF16)
    vt_spec = pl.BlockSpec((1, H, HEAD_DIM, ts), lambda b, s: (b, 0, 0, s))
    return pl.pallas_call(
        _nsa_prep_kernel,
        out_shape=[jax.ShapeDtypeStruct((B, NSA_HEADS, S, HEAD_DIM), BF16),
                   k_shape, k_shape, k_shape, vt_shape, k_shape, vt_shape,
                   jax.ShapeDtypeStruct((B, H, 12, S), F32)],
        grid=(B, S // ts),
        in_specs=[
            pl.BlockSpec((1, ts, NSA_PAD), lambda b, s: (b, s, P_NSA_OFF // NSA_PAD)),
            pl.BlockSpec((ts, KV_SLOT), lambda b, s: (s, 0)),
            pl.BlockSpec((ts, KV_SLOT), lambda b, s: (s, 0)),
            pl.BlockSpec((None, 1, LANES), lambda b, s: (layer, 0, 0)),
        ],
        out_specs=[pl.BlockSpec((1, NSA_HEADS, ts, HEAD_DIM), lambda b, s: (b, 0, s, 0)),
                   k_spec, k_spec, k_spec, vt_spec, k_spec, vt_spec,
                   pl.BlockSpec((1, H, 12, ts), lambda b, s: (b, 0, 0, s))],
        compiler_params=_params("parallel", "parallel"),
        name="nsa_prep",
    )(p_all, cos, sin, gate_b)


def _gelu_tanh(x):
    return 0.5 * x * (1.0 + jnp.tanh(float(np.sqrt(2.0 / np.pi)) * (x + 0.044715 * (x * x * x))))


def _nsa_cmp_kernel(kc_ref, vc_ref, kw1_ref, kw2_ref, kpe_ref, vw1_ref, vw2_ref, vpe_ref, cos_ref, sin_ref,
                    k_o, v_o):
    def hidden(g, w1_ref, pe_ref):
        half = CMP_STRIDE * HEAD_DIM
        first = _dot(g, w1_ref[:half, :])
        second = _dot(g, w1_ref[half:, :])
        n = first.shape[0]
        bias = _dot(pe_ref[...], w1_ref[...])[0:1]
        return _gelu_tanh(first + pltpu.roll(second, n - 1, 0) + bias).astype(BF16)

    k = _dot(hidden(kc_ref[0, 0], kw1_ref, kpe_ref), kw2_ref[...])
    k_o[0, 0] = _rope(k, cos_ref[...], sin_ref[...])[:, :HEAD_DIM].astype(BF16)
    v_o[0, 0] = _dot_nt(vw2_ref[...], hidden(vc_ref[0, 0], vw1_ref, vpe_ref)).astype(BF16)


def _nsa_compress(kc, vc, prm, cos_c, sin_c, layer):
    B, H, S, _ = kc.shape
    ng = S // CMP_STRIDE
    gw = CMP_STRIDE * HEAD_DIM
    g_k = kc.reshape(B, H, ng, gw)
    g_v = vc.reshape(B, H, ng, gw)
    gspec = pl.BlockSpec((1, 1, ng, gw), lambda b, h: (b, h, 0, 0))
    w1 = pl.BlockSpec((None, 2 * gw, CMP_HIDDEN), lambda b, h: (layer, 0, 0))
    w2 = pl.BlockSpec((None, CMP_HIDDEN, LANES), lambda b, h: (layer, 0, 0))
    pe = pl.BlockSpec((None, SUBLANES, 2 * gw), lambda b, h: (layer, 0, 0))
    tab = pl.BlockSpec((ng, LANES), lambda b, h: (0, 0))
    return pl.pallas_call(
        _nsa_cmp_kernel,
        out_shape=[jax.ShapeDtypeStruct((B, H, ng, HEAD_DIM), BF16), jax.ShapeDtypeStruct((B, H, HEAD_DIM, ng), BF16)],
        grid=(B, H),
        in_specs=[gspec, gspec, w1, w2, pe, w1,
                  pl.BlockSpec((None, HEAD_DIM, CMP_HIDDEN), lambda b, h: (layer, 0, 0)), pe, tab, tab],
        out_specs=[pl.BlockSpec((1, 1, ng, HEAD_DIM), lambda b, h: (b, h, 0, 0)),
                   pl.BlockSpec((1, 1, HEAD_DIM, ng), lambda b, h: (b, h, 0, 0))],
        compiler_params=_params("parallel", "parallel"),
        name="nsa_compress",
    )(g_k, g_v, prm["k_w1"], prm["k_w2"], prm["k_pe"], prm["v_w1"], prm["v_w2"], prm["v_pe"], cos_c, sin_c)


def _nsa_attn_kernel(q_ref, kcmp_ref, vcmp_ref, ks_ref, vs_ref, kw_ref, vw_ref, gt_ref, ov_ref, o_ref,
                     m_ref, l_ref, acc_ref, sb_ref, *, tq):
    G = NSA_GQA
    tk = tq
    qi = pl.program_id(2)
    q0 = qi * tq
    q_all = q_ref[0].reshape(G * tq, HEAD_DIM)
    pos = q0 + lax.broadcasted_iota(jnp.int32, (1, tq), 1)
    tiny = jnp.finfo(F32).tiny
    heads = lambda x: jnp.tile(x, (1, G))
    split = lambda x: [x[:, g * tq:(g + 1) * tq] for g in range(G)]

    ncp = kcmp_ref.shape[2]
    n_idx = lax.broadcasted_iota(jnp.int32, (ncp, 1), 0)
    cbias = jnp.where((n_idx * CMP_STRIDE + (CMP_BLOCK - 1)) <= pos, 0.0, NEG)
    any_cmp = jnp.where(pos >= CMP_BLOCK - 1, 1.0, 0.0)
    s = _dot_nt(kcmp_ref[0, 0], q_all) + heads(cbias)
    e = jnp.exp(s - jnp.max(s, axis=0, keepdims=True))
    p = e * (heads(any_cmp) / jnp.maximum(jnp.sum(e, axis=0, keepdims=True), tiny))
    o_cmp = split(_dot(vcmp_ref[0, 0], p.astype(BF16)))
    ps = split(p)
    psum = (ps[0] + ps[1]) + (ps[2] + ps[3])
    p_hi = psum.astype(BF16)
    p_lo = (psum - p_hi.astype(F32)).astype(BF16)
    imp = _dot(ov_ref[...], p_hi) + _dot(ov_ref[...], p_lo)

    wk = min(WINDOW + tq, ks_ref.shape[2])
    w0 = pl.multiple_of(jnp.maximum(q0 + tq - wk, 0), tk)
    wpos = w0 + lax.broadcasted_iota(jnp.int32, (wk, 1), 0)
    wbias = jnp.where((wpos <= pos) & (wpos > pos - WINDOW), 0.0, NEG)
    s = _dot_nt(kw_ref[0, 0, pl.ds(w0, wk), :], q_all) + heads(wbias)
    e = jnp.exp(s - jnp.max(s, axis=0, keepdims=True))
    pv = _dot(vw_ref[0, 0, :, pl.ds(w0, wk)], e.astype(BF16))
    o_win = split(pv / jnp.maximum(jnp.sum(e, axis=0, keepdims=True), tiny))

    n_sel = ov_ref.shape[0]
    blk = lax.broadcasted_iota(jnp.int32, (n_sel, 1), 0)
    cur = pos // SEL_BLOCK
    forced = (blk == 0) | (blk == cur) | (blk == cur - 1)
    valid = blk <= cur
    score = jnp.where(valid, jnp.where(forced, FORCE_SCORE, imp), -jnp.inf)
    groups = [score[g * SUBLANES:(g + 1) * SUBLANES, :] for g in range(n_sel // SUBLANES)]
    ranks = [jnp.zeros((SUBLANES, tq), F32) for _ in groups]
    sub = lax.broadcasted_iota(jnp.int32, (SUBLANES, 1), 0)
    for i in range(n_sel):
        ci = jnp.broadcast_to(score[i:i + 1, :], (SUBLANES, tq))
        for g, sg in enumerate(groups):
            if g > i // SUBLANES:
                beats = ci >= sg
            elif g < i // SUBLANES:
                beats = ci > sg
            else:
                beats = (ci > sg) | ((ci == sg) & (sub > i % SUBLANES))
            ranks[g] = ranks[g] + jnp.where(beats, 1.0, 0.0)
    rank = jnp.concatenate(ranks, axis=0)
    chosen = (rank < float(SEL_TOPK)) & valid
    sel_bias = jnp.where(chosen, 0.0, NEG)
    for j in range(n_sel):
        sb_ref[j] = jnp.broadcast_to(sel_bias[j:j + 1, :], (SUBLANES, tq))

    kcol = lax.broadcasted_iota(jnp.int32, (tk, 1), 0)

    def sel_update(kt, n, extra=None):
        bpt = n // SEL_BLOCK
        k0 = pl.multiple_of(kt * tk, tk)
        s = _dot_nt(ks_ref[0, 0, pl.ds(k0, n), :], q_all)
        b0 = kt * (tk // SEL_BLOCK)
        bias = jnp.concatenate([jnp.tile(sb_ref[b0 + j], (SEL_BLOCK // SUBLANES, 1)) for j in range(bpt)],
                               axis=0)
        if extra is not None:
            bias = bias + extra
        vt = vs_ref[0, 0, :, pl.ds(k0, n)]
        s = s + heads(bias)
        m_prev = m_ref[...]
        m_new = jnp.maximum(m_prev, jnp.max(s, axis=0, keepdims=True))
        p = jnp.exp(s - m_new)
        corr = jnp.exp(m_prev - m_new)
        l_ref[...] = corr * l_ref[...] + jnp.sum(p, axis=0, keepdims=True)
        acc_ref[...] = corr * acc_ref[...] + _dot(vt, p.astype(BF16))
        m_ref[...] = m_new

    m_ref[...] = jnp.full_like(m_ref, NEG)
    l_ref[...] = jnp.zeros_like(l_ref)
    acc_ref[...] = jnp.zeros_like(acc_ref)

    def sel_body(i, carry):
        sel_update(4 * i, 4 * tk)
        return carry

    lax.fori_loop(0, qi // 4, sel_body, 0)

    @pl.when(qi % 4 >= 2)
    def _():
        sel_update((qi // 4) * 4, 2 * tk)

    @pl.when(qi % 2 == 1)
    def _():
        sel_update(qi - 1, tk)

    causal = jnp.where(q0 + kcol <= pos, 0.0, NEG)
    sel_update(qi, tk, causal)
    o_sel = split(acc_ref[...] / jnp.maximum(l_ref[...], tiny))

    gt = gt_ref[0, 0]
    outs = [gt[g:g + 1] * o_cmp[g] + gt[G + g:G + g + 1] * o_sel[g] + gt[2 * G + g:2 * G + g + 1] * o_win[g]
            for g in range(G)]
    for half in range(G // 2):
        pair = jnp.concatenate(outs[2 * half:2 * half + 2], axis=0)
        o_ref[0, :, half * LANES:(half + 1) * LANES] = pair.T.astype(o_ref.dtype)


def _nsa_attention(q, k_cmp, v_cmp_t, ks, vs_t, kw, vw_t, gates_t, overlap_t, *, tq=256):
    B, _, S, _ = q.shape
    H, G = NSA_KV_HEADS, NSA_GQA
    tq = min(tq, S)
    assert WINDOW == 2 * tq or S <= tq
    ncp = k_cmp.shape[2]
    n_cmp = (S - CMP_BLOCK) // CMP_STRIDE + 1
    keys = pl.BlockSpec((1, 1, S, HEAD_DIM), lambda b, h, i: (b, h, 0, 0))
    vals = pl.BlockSpec((1, 1, HEAD_DIM, S), lambda b, h, i: (b, h, 0, 0))
    return pl.pallas_call(
        functools.partial(_nsa_attn_kernel, tq=tq),
        out_shape=jax.ShapeDtypeStruct((B, S, D_NSA), BF16),
        grid=(B, H, S // tq),
        in_specs=[
            pl.BlockSpec((1, G, tq, HEAD_DIM), lambda b, h, i: (b, h, i, 0)),
            pl.BlockSpec((1, 1, ncp, HEAD_DIM), lambda b, h, i: (b, h, 0, 0)),
            pl.BlockSpec((1, 1, HEAD_DIM, ncp), lambda b, h, i: (b, h, 0, 0)),
            keys, vals, keys, vals,
            pl.BlockSpec((1, 1, 12, tq), lambda b, h, i: (b, h, 0, i)),
            pl.BlockSpec(overlap_t.shape, lambda b, h, i: (0, 0)),
        ],
        out_specs=pl.BlockSpec((1, tq, G * HEAD_DIM), lambda b, h, i: (b, i, h)),
        scratch_shapes=[pltpu.VMEM((1, G * tq), F32), pltpu.VMEM((1, G * tq), F32),
                        pltpu.VMEM((HEAD_DIM, G * tq), F32), pltpu.VMEM((S // SEL_BLOCK, SUBLANES, tq), F32)],
        compiler_params=_params("parallel", "parallel", "arbitrary"),
        name="nsa_attention",
    )(q, k_cmp, v_cmp_t, ks, vs_t, kw, vw_t, gates_t, overlap_t)


def _rope_tables(pos, heads):
    half = ROPE_DIM // 2
    inv_freq = ROPE_THETA ** (-jnp.arange(half, dtype=F32) / half)
    ang = pos.astype(F32)[:, None] * inv_freq
    cos, sin = jnp.cos(ang), jnp.sin(ang)
    n = pos.shape[0]
    rest = HEAD_DIM - ROPE_DIM
    cos_h = jnp.concatenate([cos, cos, jnp.ones((n, rest), F32)], axis=1)
    sin_h = jnp.concatenate([-sin, sin, jnp.zeros((n, rest), F32)], axis=1)
    return jnp.tile(cos_h, (1, heads)), jnp.tile(sin_h, (1, heads))


def _overlap_matrix(S, ncp):
    n_cmp = (S - CMP_BLOCK) // CMP_STRIDE + 1
    n_sel = S // SEL_BLOCK
    cmp_start = np.arange(n_cmp) * CMP_STRIDE
    sel_start = np.arange(n_sel) * SEL_BLOCK
    ov = np.clip(np.minimum(cmp_start[:, None] + CMP_BLOCK, sel_start[None, :] + SEL_BLOCK)
                 - np.maximum(cmp_start[:, None], sel_start[None, :]), 0, None) / CMP_BLOCK
    full = np.zeros((n_sel, ncp), np.float32)
    full[:, :n_cmp] = ov.T
    return jnp.asarray(full, BF16)


RW_HEAD_ORDER = tuple(2 * (e % RW_PAIRS) + e // RW_PAIRS for e in range(RW_HEADS))


def _parity_major(a, axis):
    axis = axis % a.ndim
    shape = a.shape
    a = a.reshape(shape[:axis] + (RW_HEADS, HEAD_DIM) + shape[axis + 1:])
    a = jnp.take(a, jnp.asarray(RW_HEAD_ORDER), axis=axis)
    return a.reshape(shape)


def _block_ones(n):
    idx = np.arange(n) // HEAD_DIM
    return jnp.asarray((idx[:, None] == idx[None, :]).astype(np.float32), BF16)


def _w_in_layout_kernel(w_ref, g_ref, o_ref):
    nsa0 = RW_COLS + D_POOL
    o_ref[0, :, P_RW_OFF:P_RW_OFF + RW_COLS] = w_ref[0, :, 0:RW_COLS].astype(BF16)
    o_ref[0, :, P_POOL_OFF:P_POOL_OFF + D_POOL] = w_ref[0, :, RW_COLS:nsa0].astype(BF16)
    o_ref[0, :, P_NSA_OFF:P_NSA_OFF + D_NSA] = w_ref[0, :, nsa0:nsa0 + D_NSA].astype(BF16)
    rows = o_ref.shape[1]
    for i in range(6):
        src = nsa0 + D_NSA + i * NSA_KV
        part = jnp.concatenate([w_ref[0, :, src:src + NSA_KV], jnp.zeros((rows, KV_SLOT - NSA_KV), F32)], axis=1)
        o_ref[0, :, P_NSA_OFF + D_NSA + i * KV_SLOT:P_NSA_OFF + D_NSA + (i + 1) * KV_SLOT] = part.astype(BF16)
    g0 = nsa0 + D_NSA + 6 * NSA_KV
    gates = jnp.concatenate([w_ref[0, :, g0:g0 + 3 * NSA_HEADS], jnp.zeros((rows, LANES - 3 * NSA_HEADS), F32)],
                            axis=1).astype(BF16)
    o_ref[0, :, P_NSA_OFF + NSA_GATE_OFF:P_NSA_OFF + NSA_PAD] = _dot(gates, g_ref[...]).astype(BF16)


def _gate_column_permutation():
    m = np.zeros((LANES, NSA_PAD - NSA_GATE_OFF), np.float32)
    for kv in range(NSA_KV_HEADS):
        for g in range(NSA_GQA):
            for br in range(3):
                m[(kv * NSA_GQA + g) * 3 + br, (kv * 3 + br) * NSA_GQA + g] = 1.0
    return jnp.asarray(m, BF16)


def _w_in_layout(w_in, *, tr=256):
    L = w_in.shape[0]
    perm = _gate_column_permutation()
    return pl.pallas_call(
        _w_in_layout_kernel,
        out_shape=jax.ShapeDtypeStruct((L, D_MODEL, P_COLS), BF16),
        grid=(L, D_MODEL // tr),
        in_specs=[pl.BlockSpec((1, tr, IN_COLS), lambda l, i: (l, i, 0)),
                  pl.BlockSpec(perm.shape, lambda l, i: (0, 0))],
        out_specs=pl.BlockSpec((1, tr, P_COLS), lambda l, i: (l, i, 0)),
        compiler_params=_params("parallel", "parallel"),
        name="w_in_layout",
    )(w_in, perm)


def _ffn_up_layout_kernel(w_ref, a_ref, b_ref):
    pad = jnp.zeros((w_ref.shape[1], D_FF_PAD - D_FF), BF16)
    a_ref[0] = jnp.concatenate([w_ref[0, :, :D_FF].astype(BF16), pad], axis=1)
    b_ref[0] = jnp.concatenate([w_ref[0, :, D_FF:].astype(BF16), pad], axis=1)


def _ffn_up_layout(w, *, tr=128):
    L, D, _ = w.shape
    out = jax.ShapeDtypeStruct((L, D, D_FF_PAD), BF16)
    ospec = pl.BlockSpec((1, tr, D_FF_PAD), lambda l, i: (l, i, 0))
    return pl.pallas_call(
        _ffn_up_layout_kernel,
        out_shape=[out, out],
        grid=(L, D // tr),
        in_specs=[pl.BlockSpec((1, tr, 2 * D_FF), lambda l, i: (l, i, 0))],
        out_specs=[ospec, ospec],
        compiler_params=_params("parallel", "parallel"),
        name="ffn_up_layout",
    )(w)


def _ffn_down_layout_kernel(w_ref, o_ref):
    real = pl.program_id(1) < D_FF // o_ref.shape[1]
    o_ref[0] = jnp.where(real, w_ref[0], 0.0).astype(BF16)


def _ffn_down_layout(w, *, tr=128):
    L, _, D = w.shape
    last = D_FF // tr - 1
    return pl.pallas_call(
        _ffn_down_layout_kernel,
        out_shape=jax.ShapeDtypeStruct((L, D_FF_PAD, D), BF16),
        grid=(L, D_FF_PAD // tr),
        in_specs=[pl.BlockSpec((1, tr, D), lambda l, i: (l, jnp.minimum(i, last), 0))],
        out_specs=pl.BlockSpec((1, tr, D), lambda l, i: (l, i, 0)),
        compiler_params=_params("parallel", "parallel"),
        name="ffn_down_layout",
    )(w)


def _gate_bias_layout(gate_b):
    L = gate_b.shape[0]
    gb = gate_b.reshape(L, NSA_KV_HEADS, NSA_GQA, 3).transpose(0, 1, 3, 2).reshape(L, 1, 3 * NSA_HEADS)
    return jnp.pad(gb, ((0, 0), (0, 0), (0, LANES - 3 * NSA_HEADS)))


def kernel(x, ffn1_w_up, ffn1_w_down, ln1_g, ln1_b, w_in, rw_mu, rw_w0, rw_w2, rw_a0, rw_a2, rw_g2, rw_k_k,
           rw_k_a, rw_r_k, rw_gn_g, rw_gn_b, pool_w, pool_b, pool_scale, nsa_cmp_pe_k, nsa_cmp_pe_v,
           nsa_cmp_k_w1, nsa_cmp_k_w2, nsa_cmp_v_w1, nsa_cmp_v_w2, nsa_gate_b, w_out, ln2_g, ln2_b,
           ffn2_w_up, ffn2_w_down, ln3_g, ln3_b):
    prm = _prepare(x.shape[1], ffn1_w_up, ffn1_w_down, ln1_g, ln1_b, w_in, rw_mu, rw_w0, rw_w2, rw_a0, rw_a2,
                   rw_g2, rw_k_k, rw_k_a, rw_r_k, rw_gn_g, rw_gn_b, pool_w, pool_b, pool_scale, nsa_cmp_pe_k,
                   nsa_cmp_pe_v, nsa_cmp_k_w1, nsa_cmp_k_w2, nsa_cmp_v_w1, nsa_cmp_v_w2, nsa_gate_b, w_out,
                   ln2_g, ln2_b, ffn2_w_up, ffn2_w_down, ln3_g, ln3_b)
    B, S, D = x.shape
    h = x.reshape(B * S, D)
    for l in range(w_in.shape[0]):
        h = _layer(h, prm, l, B, S)
    return h.reshape(B, S, D)


def _prepare(S, ffn1_w_up, ffn1_w_down, ln1_g, ln1_b, w_in, rw_mu, rw_w0, rw_w2, rw_a0, rw_a2, rw_g2, rw_k_k,
             rw_k_a, rw_r_k, rw_gn_g, rw_gn_b, pool_w, pool_b, pool_scale, nsa_cmp_pe_k, nsa_cmp_pe_v,
             nsa_cmp_k_w1, nsa_cmp_k_w2, nsa_cmp_v_w1, nsa_cmp_v_w2, nsa_gate_b, w_out, ln2_g, ln2_b,
             ffn2_w_up, ffn2_w_down, ln3_g, ln3_b):
    L = w_in.shape[0]

    up = _ffn_up_layout

    down = _ffn_down_layout

    row = lambda v: v[:, None, :]
    f1a, f1b = up(ffn1_w_up)
    f2a, f2b = up(ffn2_w_up)
    w_out_b = w_out.astype(BF16)
    gw = CMP_BLOCK * HEAD_DIM
    pad_w2 = lambda w: jnp.pad(w, ((0, 0), (0, 0), (0, LANES - HEAD_DIM))).astype(BF16)
    pe_rows = lambda pe: jnp.broadcast_to(pe.reshape(L, 1, gw), (L, SUBLANES, gw)).astype(BF16)
    ncp = S // CMP_STRIDE
    cos_t, sin_t = _rope_tables(jnp.arange(S), 4)
    cos_c, sin_c = _rope_tables(jnp.arange(ncp) * CMP_STRIDE + (CMP_BLOCK - 1), 2)
    return dict(
        ffn1=(f1a, f1b, down(ffn1_w_down), row(ln1_g), row(ln1_b)),
        ffn2=(f2a, f2b, down(ffn2_w_down), row(ln3_g), row(ln3_b)),
        w_in=_w_in_layout(w_in),
        w_out=(_parity_major(w_out_b[:, :D_RWKV], 1), w_out_b[:, D_RWKV:D_RWKV + D_POOL],
               w_out_b[:, D_RWKV + D_POOL:]),
        ln2=(row(ln2_g), row(ln2_b)),
        rw=dict(mu=row(rw_mu), w0=row(rw_w0), w2=rw_w2.astype(BF16), a0=row(rw_a0), a2=rw_a2.astype(BF16),
                g2=_parity_major(rw_g2, -1).astype(BF16), k_k=row(rw_k_k), k_a=row(rw_k_a),
                r_k=rw_r_k.reshape(L, 1, D_RWKV), gn_g=row(_parity_major(rw_gn_g, -1)),
                gn_b=row(_parity_major(rw_gn_b, -1)), ones=_block_ones(D_RWKV)),
        cmp=dict(k_w1=nsa_cmp_k_w1.reshape(L, gw, CMP_HIDDEN).astype(BF16), k_w2=pad_w2(nsa_cmp_k_w2),
                 k_pe=pe_rows(nsa_cmp_pe_k),
                 v_w1=nsa_cmp_v_w1.reshape(L, gw, CMP_HIDDEN).astype(BF16), v_w2=nsa_cmp_v_w2.transpose(0, 2, 1).astype(BF16),
                 v_pe=pe_rows(nsa_cmp_pe_v)),
        gate_b=_gate_bias_layout(nsa_gate_b),
        pool=(pool_w.astype(BF16), row(pool_b), row(pool_scale)),
        rope=(cos_t, sin_t), rope_cmp=(cos_c, sin_c), overlap=_overlap_matrix(S, ncp))


def _mixers(p_all, prm, l):
    r, w, k, v, kk, be, bo, g = _rw_prep(p_all, prm["rw"], l)
    y_rw = _rw_scan(r, w, k, v, kk, be, bo, g, prm["rw"], l)
    y_pool = _pool_mix(p_all, *prm["pool"], l)
    q, kc, vc, ks, vs, kw, vw, gates = _nsa_prep(p_all, *prm["rope"], prm["gate_b"], l)
    k_cmp, v_cmp = _nsa_compress(kc, vc, prm["cmp"], *prm["rope_cmp"], l)
    y_nsa = _nsa_attention(q, k_cmp, v_cmp, ks, vs, kw, vw, gates, prm["overlap"])
    return y_rw, y_pool, y_nsa


def _layer(h, prm, l, B, S):
    T = B * S
    h = _ffn_ln(h, *prm["ffn1"], l)
    p_all = _in_proj(h, prm["w_in"], l).reshape(B, S, P_COLS)
    y_rw, y_pool, y_nsa = _mixers(p_all, prm, l)
    h = _out_proj_ln(h, y_rw.reshape(T, D_RWKV), y_pool.reshape(T, D_POOL), y_nsa.reshape(T, D_NSA),
                     prm["w_out"], *prm["ln2"], l)
    return _ffn_ln(h, *prm["ffn2"], l)
```

```python
import functools

import numpy as np
import jax
import jax.numpy as jnp
from jax import lax
from jax.experimental import pallas as pl
from jax.experimental.pallas import tpu as pltpu

F32 = jnp.float32
BF16 = jnp.bfloat16

D_MODEL = 2048
DEPTH = 4
HEAD_DIM = 64
D_RWKV = 768
D_POOL = 512
D_NSA = 768
RW_HEADS = 12
RW_DECAY_LORA = 64
RW_A_LORA = 64
RW_GATE_LORA = 128
RW_GN_EPS = 64e-5
RW_COLS = 3 * D_RWKV + RW_DECAY_LORA + RW_A_LORA + RW_GATE_LORA
POOL_WINDOWS = (2, 4, 8, 16)
POOL_GROUP = 128
NSA_HEADS = 12
NSA_KV_HEADS = 3
NSA_GQA = 4
NSA_KV = 192
NSA_COLS = D_NSA + 6 * NSA_KV + 3 * NSA_HEADS
CMP_BLOCK = 32
CMP_STRIDE = 16
CMP_HIDDEN = 256
SEL_BLOCK = 64
SEL_TOPK = 16
FORCE_SCORE = 1e9
WINDOW = 512
ROPE_THETA = 500000.0
ROPE_DIM = 16
D_FF = 5504
IN_COLS = RW_COLS + D_POOL + NSA_COLS
ALPHA = (2 * DEPTH) ** 0.25
LN_EPS = 1e-5

LANES = 128
SUBLANES = 8
VMEM_LIMIT = 56 * 1024 * 1024

KV_SLOT = 2 * LANES
NSA_PAD = 2560
P_RW_OFF = 0
P_NSA_OFF = RW_COLS
P_POOL_OFF = RW_COLS + NSA_PAD
P_COLS = RW_COLS + NSA_PAD + D_POOL
NSA_GATE_OFF = D_NSA + 6 * KV_SLOT
D_FF_PAD = 5632

NEG = -1e30


def _params(*sem):
    return pltpu.CompilerParams(dimension_semantics=sem, vmem_limit_bytes=VMEM_LIMIT)


def _layer_norm(z, g, b):
    mu = jnp.mean(z, axis=-1, keepdims=True)
    zc = z - mu
    var = jnp.mean(zc * zc, axis=-1, keepdims=True)
    return zc * lax.rsqrt(var + LN_EPS) * g + b


def _dot(a, b):
    return jnp.dot(a, b, preferred_element_type=F32)


def _dot_nt(a, b):
    return lax.dot_general(a, b, (((1,), (1,)), ((), ())), preferred_element_type=F32)


def _dot_split(x, w):
    hi = x.astype(BF16)
    lo = (x - hi.astype(F32)).astype(BF16)
    return _dot(hi, w) + _dot(lo, w)


FFN_LN_ROWS = 256
FFN_MM_ROWS = 512


def _ffn_kernel(x_ref, wa_ref, wb_ref, wd_ref, g_ref, b_ref, o_ref, xb_ref):
    k = pl.program_id(1)

    @pl.when(k == 0)
    def _():
        xb_ref[...] = x_ref[...].astype(BF16)
        o_ref[...] = jnp.zeros_like(o_ref)

    for c in range(o_ref.shape[0] // FFN_MM_ROWS):
        rows = slice(c * FFN_MM_ROWS, (c + 1) * FFN_MM_ROWS)
        xb = xb_ref[rows, :]
        a = _dot(xb, wa_ref[...])
        b = _dot(xb, wb_ref[...])
        h = (a / (1.0 + jnp.exp(-a))) * b
        o_ref[rows, :] += _dot(h.astype(BF16), wd_ref[...])

    @pl.when(k == pl.num_programs(1) - 1)
    def _():
        for c in range(o_ref.shape[0] // FFN_LN_ROWS):
            rows = slice(c * FFN_LN_ROWS, (c + 1) * FFN_LN_ROWS)
            z = ALPHA * x_ref[rows, :] + 0.5 * o_ref[rows, :]
            o_ref[rows, :] = _layer_norm(z, g_ref[...], b_ref[...])


def _ffn_ln(x, wa, wb, wd, g, b, layer, *, tm=1024, tf=512):
    T, D = x.shape
    fp = wa.shape[-1]
    tm = min(tm, T)
    return pl.pallas_call(
        _ffn_kernel,
        out_shape=jax.ShapeDtypeStruct((T, D), F32),
        grid=(T // tm, fp // tf),
        in_specs=[
            pl.BlockSpec((tm, D), lambda i, k: (i, 0)),
            pl.BlockSpec((None, D, tf), lambda i, k: (layer, 0, k)),
            pl.BlockSpec((None, D, tf), lambda i, k: (layer, 0, k)),
            pl.BlockSpec((None, tf, D), lambda i, k: (layer, k, 0)),
            pl.BlockSpec((None, 1, D), lambda i, k: (layer, 0, 0)),
            pl.BlockSpec((None, 1, D), lambda i, k: (layer, 0, 0)),
        ],
        out_specs=pl.BlockSpec((tm, D), lambda i, k: (i, 0)),
        scratch_shapes=[pltpu.VMEM((tm, D), BF16)],
        compiler_params=_params("parallel", "arbitrary"),
        name="ffn_ln",
    )(x, wa, wb, wd, g, b)


def _inproj_kernel(x_ref, w_ref, o_ref, xb_ref):
    @pl.when(pl.program_id(1) == 0)
    def _():
        xb_ref[...] = x_ref[...].astype(BF16)

    o_ref[...] = _dot(xb_ref[...], w_ref[...])


def _in_proj(x, w, layer, *, tm=1024, tn=P_COLS // 4):
    T, D = x.shape
    n = w.shape[-1]
    tm = min(tm, T)
    return pl.pallas_call(
        _inproj_kernel,
        out_shape=jax.ShapeDtypeStruct((T, n), F32),
        grid=(T // tm, n // tn),
        in_specs=[
            pl.BlockSpec((tm, D), lambda i, j: (i, 0)),
            pl.BlockSpec((None, D, tn), lambda i, j: (layer, 0, j)),
        ],
        out_specs=pl.BlockSpec((tm, tn), lambda i, j: (i, j)),
        scratch_shapes=[pltpu.VMEM((tm, D), BF16)],
        compiler_params=_params("parallel", "arbitrary"),
        name="in_proj",
    )(x, w)


def _outproj_kernel(x_ref, yr_ref, yp_ref, yn_ref, wr_ref, wp_ref, wn_ref, g_ref, b_ref, o_ref):
    y = _dot(yr_ref[...], wr_ref[...]) + _dot(yp_ref[...], wp_ref[...]) + _dot(yn_ref[...], wn_ref[...])
    o_ref[...] = _layer_norm(ALPHA * x_ref[...] + y, g_ref[...], b_ref[...])


def _out_proj_ln(x, y_rw, y_pool, y_nsa, w_out, g, b, layer, *, tm=512):
    T, D = x.shape
    return pl.pallas_call(
        _outproj_kernel,
        out_shape=jax.ShapeDtypeStruct((T, D), F32),
        grid=(T // tm,),
        in_specs=[
            pl.BlockSpec((tm, D), lambda i: (i, 0)),
            pl.BlockSpec((tm, D_RWKV), lambda i: (i, 0)),
            pl.BlockSpec((tm, D_POOL), lambda i: (i, 0)),
            pl.BlockSpec((tm, D_NSA), lambda i: (i, 0)),
            pl.BlockSpec((None, D_RWKV, D), lambda i: (layer, 0, 0)),
            pl.BlockSpec((None, D_POOL, D), lambda i: (layer, 0, 0)),
            pl.BlockSpec((None, D_NSA, D), lambda i: (layer, 0, 0)),
            pl.BlockSpec((None, 1, D), lambda i: (layer, 0, 0)),
            pl.BlockSpec((None, 1, D), lambda i: (layer, 0, 0)),
        ],
        out_specs=pl.BlockSpec((tm, D), lambda i: (i, 0)),
        compiler_params=_params("parallel"),
        name="out_proj_ln",
    )(x, y_rw, y_pool, y_nsa, w_out[0], w_out[1], w_out[2], g, b)


POOL_HALO = 16


def _pool_kernel(p_ref, halo_ref, w_ref, b_ref, sc_ref, o_ref, xs_ref, *, ts):
    s = pl.program_id(1)
    x = p_ref[0]
    halo = jnp.where(s > 0, halo_ref[0], 0.0)
    xs_ref[0:POOL_HALO, :] = halo
    xs_ref[POOL_HALO:POOL_HALO + ts, :] = x
    t1 = (s * ts + 1 + lax.broadcasted_iota(jnp.int32, (ts, 1), 0)).astype(F32)
    for gi, win in enumerate(POOL_WINDOWS):
        c0 = gi * POOL_GROUP
        acc = x[:, c0:c0 + POOL_GROUP]
        for j in range(1, win):
            acc = acc + xs_ref[POOL_HALO - j:POOL_HALO - j + ts, c0:c0 + POOL_GROUP]
        pooled = acc / jnp.minimum(t1, float(win)) - x[:, c0:c0 + POOL_GROUP]
        z = _dot(pooled.astype(BF16), w_ref[gi]) + b_ref[:, c0:c0 + POOL_GROUP]
        o_ref[0, :, c0:c0 + POOL_GROUP] = (z * sc_ref[:, c0:c0 + POOL_GROUP]).astype(o_ref.dtype)


def _pool_mix(p_all, pool_w, pool_b, pool_scale, layer, *, ts=512):
    B, S, _ = p_all.shape
    cb = P_POOL_OFF // D_POOL
    hb = ts // POOL_HALO
    return pl.pallas_call(
        functools.partial(_pool_kernel, ts=ts),
        out_shape=jax.ShapeDtypeStruct((B, S, D_POOL), BF16),
        grid=(B, S // ts),
        in_specs=[
            pl.BlockSpec((1, ts, D_POOL), lambda b, s: (b, s, cb)),
            pl.BlockSpec((1, POOL_HALO, D_POOL), lambda b, s: (b, jnp.maximum(s * hb - 1, 0), cb)),
            pl.BlockSpec((None, 4, POOL_GROUP, POOL_GROUP), lambda b, s: (layer, 0, 0, 0)),
            pl.BlockSpec((None, 1, D_POOL), lambda b, s: (layer, 0, 0)),
            pl.BlockSpec((None, 1, D_POOL), lambda b, s: (layer, 0, 0)),
        ],
        out_specs=pl.BlockSpec((1, ts, D_POOL), lambda b, s: (b, s, 0)),
        scratch_shapes=[pltpu.VMEM((ts + POOL_HALO, D_POOL), F32)],
        compiler_params=_params("parallel", "parallel"),
        name="pool_mix",
    )(p_all, p_all, pool_w, pool_b, pool_scale)


def _softplus(z):
    return jnp.maximum(z, 0.0) + jnp.log1p(jnp.exp(-jnp.abs(z)))


def _sigmoid(z):
    return 1.0 / (1.0 + jnp.exp(-z))


def _lanes_to_parity_major(x):
    n_pairs = x.shape[1] // LANES
    low = lax.broadcasted_iota(jnp.int32, (1, LANES), 1) < HEAD_DIM

    def chunk(e, to_low):
        pair, parity = e % n_pairs, e // n_pairs
        src = x[:, pair * LANES:(pair + 1) * LANES]
        return src if (parity == 0) == to_low else pltpu.roll(src, HEAD_DIM, 1)

    return jnp.concatenate([jnp.where(low, chunk(2 * d, True), chunk(2 * d + 1, False))
                            for d in range(n_pairs)], axis=1)


def _rw_prep_kernel(p_ref, prev_ref, mu_ref, w0_ref, w2_ref, a0_ref, a2_ref, g2_ref, kk_ref, ka_ref,
                    rk_ref, ones_ref,
                    r_o, w_o, k_o, v_o, kk_o, be_o, bo_o, g_o, *, ts):
    s = pl.program_id(1)
    x = p_ref[0]
    last = jnp.where(s > 0, prev_ref[0][SUBLANES - 1:SUBLANES, :], 0.0)
    row = lax.broadcasted_iota(jnp.int32, (ts, 1), 0)
    shifted = jnp.where(row == 0, last, pltpu.roll(x, 1, 0))
    xm = x + (shifted - x) * mu_ref[...]
    c = D_RWKV
    r = xm[:, 0:c]
    k = xm[:, c:2 * c]
    v = xm[:, 2 * c:3 * c]
    lora = xm[:, 3 * c:3 * c + LANES]
    wl = lora[:, :RW_DECAY_LORA]
    al = lora[:, RW_DECAY_LORA:]
    gl = xm[:, 3 * c + LANES:]
    w = -_softplus(-(w0_ref[...] + _dot(jnp.tanh(wl).astype(BF16), w2_ref[...]))) - 0.5
    decay = jnp.exp(-jnp.exp(w))
    a = _sigmoid(a0_ref[...] + _dot(al.astype(BF16), a2_ref[...]))
    g = _dot(_sigmoid(gl).astype(BF16), g2_ref[...])
    ones = ones_ref[...]
    kk = k * kk_ref[...]
    nrm = jnp.sqrt(_dot_split(kk * kk, ones))
    kk = kk / jnp.maximum(nrm, 1e-12)
    k_mod = k * (1.0 + (a - 1.0) * ka_ref[...])
    bonus = _dot_split(r * k_mod * rk_ref[...], ones) * v
    r_o[0] = r
    w_o[0] = decay
    k_o[0] = k_mod
    v_o[0] = v
    kk_o[0] = kk
    be_o[0] = kk * a
    bo_o[0] = _lanes_to_parity_major(bonus).astype(bo_o.dtype)
    g_o[0] = g.astype(g_o.dtype)


def _rw_prep(p_all, prm, layer, *, ts=512):
    B, S, _ = p_all.shape
    cb = P_RW_OFF // RW_COLS
    hb = ts // SUBLANES
    vec = lambda n: pl.BlockSpec((None, 1, n), lambda b, s: (layer, 0, 0))
    mat = lambda m, n: pl.BlockSpec((None, m, n), lambda b, s: (layer, 0, 0))
    out = jax.ShapeDtypeStruct((B, S, D_RWKV), F32)
    ospec = pl.BlockSpec((1, ts, D_RWKV), lambda b, s: (b, s, 0))
    return pl.pallas_call(
        functools.partial(_rw_prep_kernel, ts=ts),
        out_shape=[out] * 6 + [jax.ShapeDtypeStruct((B, S, D_RWKV), BF16)] * 2,
        grid=(B, S // ts),
        in_specs=[
            pl.BlockSpec((1, ts, RW_COLS), lambda b, s: (b, s, cb)),
            pl.BlockSpec((1, SUBLANES, RW_COLS), lambda b, s: (b, jnp.maximum(s * hb - 1, 0), cb)),
            vec(RW_COLS), vec(D_RWKV), mat(RW_DECAY_LORA, D_RWKV), vec(D_RWKV), mat(RW_A_LORA, D_RWKV),
            mat(RW_GATE_LORA, D_RWKV), vec(D_RWKV), vec(D_RWKV), vec(D_RWKV),
            pl.BlockSpec((D_RWKV, D_RWKV), lambda b, s: (0, 0)),
        ],
        out_specs=[ospec] * 8,
        compiler_params=_params("parallel", "parallel"),
        name="rw_prep",
    )(p_all, p_all, prm["mu"], prm["w0"], prm["w2"], prm["a0"], prm["a2"], prm["g2"], prm["k_k"],
      prm["k_a"], prm["r_k"], prm["ones"])


RW_PAIRS = D_RWKV // LANES
RW_STEPS = 16


def _rw_scan_kernel(r_ref, w_ref, k_ref, v_ref, kk_ref, be_ref, bo_ref, g_ref, gng_ref, gnb_ref, ones_ref,
                    pair_ref, half_ref, o_ref, st_ref, vc_ref, sr_ref, y_ref, *, ts, nb):
    s = pl.program_id(1)

    @pl.when(s == 0)
    def _():
        st_ref[...] = jnp.zeros_like(st_ref)

    side, stack = 2, RW_PAIRS // 2
    rows, width = stack * HEAD_DIM, side * LANES
    lane = lax.broadcasted_iota(jnp.int32, (rows, width), 1)
    row = lax.broadcasted_iota(jnp.int32, (rows, width), 0)
    eye = jnp.where(lane % HEAD_DIM == row % HEAD_DIM, 1.0, 0.0)
    ones_blk = pair_ref[...]

    def spread(x8, j):
        blocks = []
        for st in range(stack):
            lanes = jnp.concatenate([x8[j:j + 1, (sd * stack + st) * LANES:(sd * stack + st + 1) * LANES]
                                     for sd in range(side)], axis=1)
            blocks.append(jnp.broadcast_to(lanes, (HEAD_DIM, width)))
        return jnp.concatenate(blocks, axis=0)

    def step(i, carry):
        base = pl.multiple_of(i * RW_STEPS, RW_STEPS)
        tiles = [[ref[b, pl.ds(base, RW_STEPS), :] for ref in (kk_ref, w_ref, be_ref, k_ref, r_ref, v_ref)]
                 for b in range(nb)]
        for b in range(nb):
            lhs = jnp.concatenate([eye * spread(tiles[b][5], j) for j in range(RW_STEPS)], axis=0)
            vc_ref[b] = _dot(lhs.astype(BF16), ones_blk)
        sts = [st_ref[b] for b in range(nb)]
        for j in range(RW_STEPS):
            for b in range(nb):
                kk8, w8, be8, k8, r8, _ = tiles[b]
                st = sts[b]
                sa = _dot((st * spread(kk8, j)).astype(BF16), ones_blk)
                st = (st * spread(w8, j) - sa * spread(be8, j)
                      + vc_ref[b, j * rows:(j + 1) * rows, :] * spread(k8, j))
                sr = (st * spread(r8, j)).astype(BF16)
                for sd in range(side):
                    lo = (j * side + sd) * rows
                    sr_ref[b, lo:lo + rows, :] = sr[:, sd * LANES:(sd + 1) * LANES]
                sts[b] = st
        for b in range(nb):
            st_ref[b] = sts[b]
            yt = _dot_nt(half_ref[...], sr_ref[b])
            per = side * rows
            y_ref[b, pl.ds(base, RW_STEPS), :] = jnp.concatenate(
                [jnp.concatenate([yt[0:1, j * per:(j + 1) * per], yt[1:2, j * per:(j + 1) * per]], axis=1)
                 for j in range(RW_STEPS)], axis=0)
        return carry

    lax.fori_loop(0, ts // RW_STEPS, step, 0)

    ones = ones_ref[...]
    for b in range(nb):
        y = y_ref[b]
        mean = _dot_split(y, ones) * (1.0 / HEAD_DIM)
        yc = y - mean
        var = _dot_split(yc * yc, ones) * (1.0 / HEAD_DIM)
        yn = yc * lax.rsqrt(var + RW_GN_EPS) * gng_ref[...] + gnb_ref[...]
        o_ref[b] = ((yn + bo_ref[b]) * g_ref[b]).astype(o_ref.dtype)


def _rw_scan(r, w, k, v, kk, be, bo, g, prm, layer, *, ts=128):
    B, S, _ = r.shape
    nb = 4 if B % 4 == 0 else (2 if B % 2 == 0 else 1)
    rows, width = RW_PAIRS // 2 * HEAD_DIM, 2 * LANES
    seq = pl.BlockSpec((nb, ts, D_RWKV), lambda b, s: (b, s, 0))
    vec = pl.BlockSpec((None, 1, D_RWKV), lambda b, s: (layer, 0, 0))
    return pl.pallas_call(
        functools.partial(_rw_scan_kernel, ts=ts, nb=nb),
        out_shape=jax.ShapeDtypeStruct((B, S, D_RWKV), BF16),
        grid=(B // nb, S // ts),
        in_specs=[seq] * 8 + [vec, vec, pl.BlockSpec((D_RWKV, D_RWKV), lambda b, s: (0, 0)),
                              pl.BlockSpec((width, width), lambda b, s: (0, 0)),
                              pl.BlockSpec((SUBLANES, LANES), lambda b, s: (0, 0))],
        out_specs=seq,
        scratch_shapes=[pltpu.VMEM((nb, rows, width), F32), pltpu.VMEM((nb, RW_STEPS * rows, width), F32),
                        pltpu.VMEM((nb, RW_STEPS * RW_PAIRS * HEAD_DIM, LANES), BF16),
                        pltpu.VMEM((nb, ts, D_RWKV), F32)],
        compiler_params=_params("parallel", "arbitrary"),
        name="rw_scan",
    )(r, w, k, v, kk, be, bo, g, prm["gn_g"], prm["gn_b"], prm["ones"], prm["ones"][:width, :width],
      prm["ones"][:SUBLANES * HEAD_DIM:HEAD_DIM, :LANES])


def _rope(x, cos, sin):
    half = ROPE_DIM // 2
    w = x.shape[-1]
    d = lax.broadcasted_iota(jnp.int32, x.shape, 1) % HEAD_DIM
    partner = jnp.where(d < half, pltpu.roll(x, w - half, 1), pltpu.roll(x, half, 1))
    return x * cos + partner * sin


def _nsa_prep_kernel(p_ref, cos_ref, sin_ref, gb_ref, q_o, kc_o, vc_o, ks_o, vs_o, kw_o, vw_o, gt_o):
    cos = cos_ref[...]
    sin = sin_ref[...]
    for hq in range(NSA_HEADS // 4):
        x = p_ref[0, :, hq * KV_SLOT:(hq + 1) * KV_SLOT]
        qr = _rope(x, cos, sin) * (HEAD_DIM ** -0.5)
        for j in range(4):
            q_o[0, 4 * hq + j] = qr[:, j * HEAD_DIM:(j + 1) * HEAD_DIM].astype(BF16)
    for i, ref in enumerate((kc_o, vc_o, ks_o, vs_o, kw_o, vw_o)):
        part = p_ref[0, :, D_NSA + i * KV_SLOT:D_NSA + (i + 1) * KV_SLOT]
        if i in (2, 4):
            part = _rope(part, cos, sin)
        if i in (3, 5):
            part = part.T
            for h in range(NSA_KV_HEADS):
                ref[0, h] = part[h * HEAD_DIM:(h + 1) * HEAD_DIM, :].astype(BF16)
        else:
            for h in range(NSA_KV_HEADS):
                ref[0, h] = part[:, h * HEAD_DIM:(h + 1) * HEAD_DIM].astype(BF16)
    gts = _sigmoid(p_ref[0, :, NSA_GATE_OFF:NSA_GATE_OFF + LANES] + gb_ref[...]).T
    for h in range(NSA_KV_HEADS):
        gt_o[0, h] = gts[h * 12:(h + 1) * 12, :]


def _nsa_prep(p_all, cos, sin, gate_b, layer, *, ts=512):
    B, S, _ = p_all.shape
    H = NSA_KV_HEADS
    k_shape = jax.ShapeDtypeStruct((B, H, S, HEAD_DIM), BF16)
    k_spec = pl.BlockSpec((1, H, ts, HEAD_DIM), lambda b, s: (b, 0, s, 0))
    vt_shape = jax.ShapeDtypeStruct((B, H, HEAD_DIM, S), BF16)
    vt_spec = pl.BlockSpec((1, H, HEAD_DIM, ts), lambda b, s: (b, 0, 0, s))
    return pl.pallas_call(
        _nsa_prep_kernel,
        out_shape=[jax.ShapeDtypeStruct((B, NSA_HEADS, S, HEAD_DIM), BF16),
                   k_shape, k_shape, k_shape, vt_shape, k_shape, vt_shape,
                   jax.ShapeDtypeStruct((B, H, 12, S), F32)],
        grid=(B, S // ts),
        in_specs=[
            pl.BlockSpec((1, ts, NSA_PAD), lambda b, s: (b, s, P_NSA_OFF // NSA_PAD)),
            pl.BlockSpec((ts, KV_SLOT), lambda b, s: (s, 0)),
            pl.BlockSpec((ts, KV_SLOT), lambda b, s: (s, 0)),
            pl.BlockSpec((None, 1, LANES), lambda b, s: (layer, 0, 0)),
        ],
        out_specs=[pl.BlockSpec((1, NSA_HEADS, ts, HEAD_DIM), lambda b, s: (b, 0, s, 0)),
                   k_spec, k_spec, k_spec, vt_spec, k_spec, vt_spec,
                   pl.BlockSpec((1, H, 12, ts), lambda b, s: (b, 0, 0, s))],
        compiler_params=_params("parallel", "parallel"),
        name="nsa_prep",
    )(p_all, cos, sin, gate_b)


def _gelu_tanh(x):
    return 0.5 * x * (1.0 + jnp.tanh(float(np.sqrt(2.0 / np.pi)) * (x + 0.044715 * (x * x * x))))


def _nsa_cmp_kernel(kc_ref, vc_ref, kw1_ref, kw2_ref, kpe_ref, vw1_ref, vw2_ref, vpe_ref, cos_ref, sin_ref,
                    k_o, v_o):
    def hidden(g, w1_ref, pe_ref):
        half = CMP_STRIDE * HEAD_DIM
        first = _dot(g, w1_ref[:half, :])
        second = _dot(g, w1_ref[half:, :])
        n = first.shape[0]
        bias = _dot(pe_ref[...], w1_ref[...])[0:1]
        return _gelu_tanh(first + pltpu.roll(second, n - 1, 0) + bias).astype(BF16)

    k = _dot(hidden(kc_ref[0, 0], kw1_ref, kpe_ref), kw2_ref[...])
    k_o[0, 0] = _rope(k, cos_ref[...], sin_ref[...])[:, :HEAD_DIM].astype(BF16)
    v_o[0, 0] = _dot_nt(vw2_ref[...], hidden(vc_ref[0, 0], vw1_ref, vpe_ref)).astype(BF16)


def _nsa_compress(kc, vc, prm, cos_c, sin_c, layer):
    B, H, S, _ = kc.shape
    ng = S // CMP_STRIDE
    gw = CMP_STRIDE * HEAD_DIM
    g_k = kc.reshape(B, H, ng, gw)
    g_v = vc.reshape(B, H, ng, gw)
    gspec = pl.BlockSpec((1, 1, ng, gw), lambda b, h: (b, h, 0, 0))
    w1 = pl.BlockSpec((None, 2 * gw, CMP_HIDDEN), lambda b, h: (layer, 0, 0))
    w2 = pl.BlockSpec((None, CMP_HIDDEN, LANES), lambda b, h: (layer, 0, 0))
    pe = pl.BlockSpec((None, SUBLANES, 2 * gw), lambda b, h: (layer, 0, 0))
    tab = pl.BlockSpec((ng, LANES), lambda b, h: (0, 0))
    return pl.pallas_call(
        _nsa_cmp_kernel,
        out_shape=[jax.ShapeDtypeStruct((B, H, ng, HEAD_DIM), BF16), jax.ShapeDtypeStruct((B, H, HEAD_DIM, ng), BF16)],
        grid=(B, H),
        in_specs=[gspec, gspec, w1, w2, pe, w1,
                  pl.BlockSpec((None, HEAD_DIM, CMP_HIDDEN), lambda b, h: (layer, 0, 0)), pe, tab, tab],
        out_specs=[pl.BlockSpec((1, 1, ng, HEAD_DIM), lambda b, h: (b, h, 0, 0)),
                   pl.BlockSpec((1, 1, HEAD_DIM, ng), lambda b, h: (b, h, 0, 0))],
        compiler_params=_params("parallel", "parallel"),
        name="nsa_compress",
    )(g_k, g_v, prm["k_w1"], prm["k_w2"], prm["k_pe"], prm["v_w1"], prm["v_w2"], prm["v_pe"], cos_c, sin_c)


def _nsa_attn_kernel(q_ref, kcmp_ref, vcmp_ref, ks_ref, vs_ref, kw_ref, vw_ref, gt_ref, ov_ref, o_ref,
                     m_ref, l_ref, acc_ref, sb_ref, *, tq):
    G = NSA_GQA
    tk = tq
    qi = pl.program_id(2)
    q0 = qi * tq
    q_all = q_ref[0].reshape(G * tq, HEAD_DIM)
    pos = q0 + lax.broadcasted_iota(jnp.int32, (1, tq), 1)
    tiny = jnp.finfo(F32).tiny
    heads = lambda x: jnp.tile(x, (1, G))
    split = lambda x: [x[:, g * tq:(g + 1) * tq] for g in range(G)]

    ncp = kcmp_ref.shape[2]
    n_idx = lax.broadcasted_iota(jnp.int32, (ncp, 1), 0)
    cbias = jnp.where((n_idx * CMP_STRIDE + (CMP_BLOCK - 1)) <= pos, 0.0, NEG)
    any_cmp = jnp.where(pos >= CMP_BLOCK - 1, 1.0, 0.0)
    s = _dot_nt(kcmp_ref[0, 0], q_all) + heads(cbias)
    e = jnp.exp(s - jnp.max(s, axis=0, keepdims=True))
    p = e * (heads(any_cmp) / jnp.maximum(jnp.sum(e, axis=0, keepdims=True), tiny))
    o_cmp = split(_dot(vcmp_ref[0, 0], p.astype(BF16)))
    ps = split(p)
    psum = (ps[0] + ps[1]) + (ps[2] + ps[3])
    p_hi = psum.astype(BF16)
    p_lo = (psum - p_hi.astype(F32)).astype(BF16)
    imp = _dot(ov_ref[...], p_hi) + _dot(ov_ref[...], p_lo)

    wk = min(WINDOW + tq, ks_ref.shape[2])
    w0 = pl.multiple_of(jnp.maximum(q0 + tq - wk, 0), tk)
    wpos = w0 + lax.broadcasted_iota(jnp.int32, (wk, 1), 0)
    wbias = jnp.where((wpos <= pos) & (wpos > pos - WINDOW), 0.0, NEG)
    s = _dot_nt(kw_ref[0, 0, pl.ds(w0, wk), :], q_all) + heads(wbias)
    e = jnp.exp(s - jnp.max(s, axis=0, keepdims=True))
    pv = _dot(vw_ref[0, 0, :, pl.ds(w0, wk)], e.astype(BF16))
    o_win = split(pv / jnp.maximum(jnp.sum(e, axis=0, keepdims=True), tiny))

    n_sel = ov_ref.shape[0]
    blk = lax.broadcasted_iota(jnp.int32, (n_sel, 1), 0)
    cur = pos // SEL_BLOCK
    forced = (blk == 0) | (blk == cur) | (blk == cur - 1)
    valid = blk <= cur
    score = jnp.where(valid, jnp.where(forced, FORCE_SCORE, imp), -jnp.inf)
    groups = [score[g * SUBLANES:(g + 1) * SUBLANES, :] for g in range(n_sel // SUBLANES)]
    ranks = [jnp.zeros((SUBLANES, tq), F32) for _ in groups]
    sub = lax.broadcasted_iota(jnp.int32, (SUBLANES, 1), 0)
    for i in range(n_sel):
        ci = jnp.broadcast_to(score[i:i + 1, :], (SUBLANES, tq))
        for g, sg in enumerate(groups):
            if g > i // SUBLANES:
                beats = ci >= sg
            elif g < i // SUBLANES:
                beats = ci > sg
            else:
                beats = (ci > sg) | ((ci == sg) & (sub > i % SUBLANES))
            ranks[g] = ranks[g] + jnp.where(beats, 1.0, 0.0)
    rank = jnp.concatenate(ranks, axis=0)
    chosen = (rank < float(SEL_TOPK)) & valid
    sel_bias = jnp.where(chosen, 0.0, NEG)
    for j in range(n_sel):
        sb_ref[j] = jnp.broadcast_to(sel_bias[j:j + 1, :], (SUBLANES, tq))

    kcol = lax.broadcasted_iota(jnp.int32, (tk, 1), 0)

    def sel_update(kt, n, extra=None):
        bpt = n // SEL_BLOCK
        k0 = pl.multiple_of(kt * tk, tk)
        s = _dot_nt(ks_ref[0, 0, pl.ds(k0, n), :], q_all)
        b0 = kt * (tk // SEL_BLOCK)
        bias = jnp.concatenate([jnp.tile(sb_ref[b0 + j], (SEL_BLOCK // SUBLANES, 1)) for j in range(bpt)],
                               axis=0)
        if extra is not None:
            bias = bias + extra
        vt = vs_ref[0, 0, :, pl.ds(k0, n)]
        s = s + heads(bias)
        m_prev = m_ref[...]
        m_new = jnp.maximum(m_prev, jnp.max(s, axis=0, keepdims=True))
        p = jnp.exp(s - m_new)
        corr = jnp.exp(m_prev - m_new)
        l_ref[...] = corr * l_ref[...] + jnp.sum(p, axis=0, keepdims=True)
        acc_ref[...] = corr * acc_ref[...] + _dot(vt, p.astype(BF16))
        m_ref[...] = m_new

    m_ref[...] = jnp.full_like(m_ref, NEG)
    l_ref[...] = jnp.zeros_like(l_ref)
    acc_ref[...] = jnp.zeros_like(acc_ref)

    def sel_body(i, carry):
        sel_update(4 * i, 4 * tk)
        return carry

    lax.fori_loop(0, qi // 4, sel_body, 0)

    @pl.when(qi % 4 >= 2)
    def _():
        sel_update((qi // 4) * 4, 2 * tk)

    @pl.when(qi % 2 == 1)
    def _():
        sel_update(qi - 1, tk)

    causal = jnp.where(q0 + kcol <= pos, 0.0, NEG)
    sel_update(qi, tk, causal)
    o_sel = split(acc_ref[...] / jnp.maximum(l_ref[...], tiny))

    gt = gt_ref[0, 0]
    outs = [gt[g:g + 1] * o_cmp[g] + gt[G + g:G + g + 1] * o_sel[g] + gt[2 * G + g:2 * G + g + 1] * o_win[g]
            for g in range(G)]
    for half in range(G // 2):
        pair = jnp.concatenate(outs[2 * half:2 * half + 2], axis=0)
        o_ref[0, :, half * LANES:(half + 1) * LANES] = pair.T.astype(o_ref.dtype)


def _nsa_attention(q, k_cmp, v_cmp_t, ks, vs_t, kw, vw_t, gates_t, overlap_t, *, tq=256):
    B, _, S, _ = q.shape
    H, G = NSA_KV_HEADS, NSA_GQA
    tq = min(tq, S)
    assert WINDOW == 2 * tq or S <= tq
    ncp = k_cmp.shape[2]
    n_cmp = (S - CMP_BLOCK) // CMP_STRIDE + 1
    keys = pl.BlockSpec((1, 1, S, HEAD_DIM), lambda b, h, i: (b, h, 0, 0))
    vals = pl.BlockSpec((1, 1, HEAD_DIM, S), lambda b, h, i: (b, h, 0, 0))
    return pl.pallas_call(
        functools.partial(_nsa_attn_kernel, tq=tq),
        out_shape=jax.ShapeDtypeStruct((B, S, D_NSA), BF16),
        grid=(B, H, S // tq),
        in_specs=[
            pl.BlockSpec((1, G, tq, HEAD_DIM), lambda b, h, i: (b, h, i, 0)),
            pl.BlockSpec((1, 1, ncp, HEAD_DIM), lambda b, h, i: (b, h, 0, 0)),
            pl.BlockSpec((1, 1, HEAD_DIM, ncp), lambda b, h, i: (b, h, 0, 0)),
            keys, vals, keys, vals,
            pl.BlockSpec((1, 1, 12, tq), lambda b, h, i: (b, h, 0, i)),
            pl.BlockSpec(overlap_t.shape, lambda b, h, i: (0, 0)),
        ],
        out_specs=pl.BlockSpec((1, tq, G * HEAD_DIM), lambda b, h, i: (b, i, h)),
        scratch_shapes=[pltpu.VMEM((1, G * tq), F32), pltpu.VMEM((1, G * tq), F32),
                        pltpu.VMEM((HEAD_DIM, G * tq), F32), pltpu.VMEM((S // SEL_BLOCK, SUBLANES, tq), F32)],
        compiler_params=_params("parallel", "parallel", "arbitrary"),
        name="nsa_attention",
    )(q, k_cmp, v_cmp_t, ks, vs_t, kw, vw_t, gates_t, overlap_t)


def _rope_tables(pos, heads):
    half = ROPE_DIM // 2
    inv_freq = ROPE_THETA ** (-jnp.arange(half, dtype=F32) / half)
    ang = pos.astype(F32)[:, None] * inv_freq
    cos, sin = jnp.cos(ang), jnp.sin(ang)
    n = pos.shape[0]
    rest = HEAD_DIM - ROPE_DIM
    cos_h = jnp.concatenate([cos, cos, jnp.ones((n, rest), F32)], axis=1)
    sin_h = jnp.concatenate([-sin, sin, jnp.zeros((n, rest), F32)], axis=1)
    return jnp.tile(cos_h, (1, heads)), jnp.tile(sin_h, (1, heads))


def _overlap_matrix(S, ncp):
    n_cmp = (S - CMP_BLOCK) // CMP_STRIDE + 1
    n_sel = S // SEL_BLOCK
    cmp_start = np.arange(n_cmp) * CMP_STRIDE
    sel_start = np.arange(n_sel) * SEL_BLOCK
    ov = np.clip(np.minimum(cmp_start[:, None] + CMP_BLOCK, sel_start[None, :] + SEL_BLOCK)
                 - np.maximum(cmp_start[:, None], sel_start[None, :]), 0, None) / CMP_BLOCK
    full = np.zeros((n_sel, ncp), np.float32)
    full[:, :n_cmp] = ov.T
    return jnp.asarray(full, BF16)


RW_HEAD_ORDER = tuple(2 * (e % RW_PAIRS) + e // RW_PAIRS for e in range(RW_HEADS))


def _parity_major(a, axis):
    axis = axis % a.ndim
    shape = a.shape
    a = a.reshape(shape[:axis] + (RW_HEADS, HEAD_DIM) + shape[axis + 1:])
    a = jnp.take(a, jnp.asarray(RW_HEAD_ORDER), axis=axis)
    return a.reshape(shape)


def _block_ones(n):
    idx = np.arange(n) // HEAD_DIM
    return jnp.asarray((idx[:, None] == idx[None, :]).astype(np.float32), BF16)


def _w_in_layout_kernel(w_ref, g_ref, o_ref):
    nsa0 = RW_COLS + D_POOL
    o_ref[0, :, P_RW_OFF:P_RW_OFF + RW_COLS] = w_ref[0, :, 0:RW_COLS].astype(BF16)
    o_ref[0, :, P_POOL_OFF:P_POOL_OFF + D_POOL] = w_ref[0, :, RW_COLS:nsa0].astype(BF16)
    o_ref[0, :, P_NSA_OFF:P_NSA_OFF + D_NSA] = w_ref[0, :, nsa0:nsa0 + D_NSA].astype(BF16)
    rows = o_ref.shape[1]
    for i in range(6):
        src = nsa0 + D_NSA + i * NSA_KV
        part = jnp.concatenate([w_ref[0, :, src:src + NSA_KV], jnp.zeros((rows, KV_SLOT - NSA_KV), F32)], axis=1)
        o_ref[0, :, P_NSA_OFF + D_NSA + i * KV_SLOT:P_NSA_OFF + D_NSA + (i + 1) * KV_SLOT] = part.astype(BF16)
    g0 = nsa0 + D_NSA + 6 * NSA_KV
    gates = jnp.concatenate([w_ref[0, :, g0:g0 + 3 * NSA_HEADS], jnp.zeros((rows, LANES - 3 * NSA_HEADS), F32)],
                            axis=1).astype(BF16)
    o_ref[0, :, P_NSA_OFF + NSA_GATE_OFF:P_NSA_OFF + NSA_PAD] = _dot(gates, g_ref[...]).astype(BF16)


def _gate_column_permutation():
    m = np.zeros((LANES, NSA_PAD - NSA_GATE_OFF), np.float32)
    for kv in range(NSA_KV_HEADS):
        for g in range(NSA_GQA):
            for br in range(3):
                m[(kv * NSA_GQA + g) * 3 + br, (kv * 3 + br) * NSA_GQA + g] = 1.0
    return jnp.asarray(m, BF16)


def _w_in_layout(w_in, *, tr=256):
    L = w_in.shape[0]
    perm = _gate_column_permutation()
    return pl.pallas_call(
        _w_in_layout_kernel,
        out_shape=jax.ShapeDtypeStruct((L, D_MODEL, P_COLS), BF16),
        grid=(L, D_MODEL // tr),
        in_specs=[pl.BlockSpec((1, tr, IN_COLS), lambda l, i: (l, i, 0)),
                  pl.BlockSpec(perm.shape, lambda l, i: (0, 0))],
        out_specs=pl.BlockSpec((1, tr, P_COLS), lambda l, i: (l, i, 0)),
        compiler_params=_params("parallel", "parallel"),
        name="w_in_layout",
    )(w_in, perm)


def _ffn_up_layout_kernel(w_ref, a_ref, b_ref):
    pad = jnp.zeros((w_ref.shape[1], D_FF_PAD - D_FF), BF16)
    a_ref[0] = jnp.concatenate([w_ref[0, :, :D_FF].astype(BF16), pad], axis=1)
    b_ref[0] = jnp.concatenate([w_ref[0, :, D_FF:].astype(BF16), pad], axis=1)


def _ffn_up_layout(w, *, tr=128):
    L, D, _ = w.shape
    out = jax.ShapeDtypeStruct((L, D, D_FF_PAD), BF16)
    ospec = pl.BlockSpec((1, tr, D_FF_PAD), lambda l, i: (l, i, 0))
    return pl.pallas_call(
        _ffn_up_layout_kernel,
        out_shape=[out, out],
        grid=(L, D // tr),
        in_specs=[pl.BlockSpec((1, tr, 2 * D_FF), lambda l, i: (l, i, 0))],
        out_specs=[ospec, ospec],
        compiler_params=_params("parallel", "parallel"),
        name="ffn_up_layout",
    )(w)


def _ffn_down_layout_kernel(w_ref, o_ref):
    real = pl.program_id(1) < D_FF // o_ref.shape[1]
    o_ref[0] = jnp.where(real, w_ref[0], 0.0).astype(BF16)


def _ffn_down_layout(w, *, tr=128):
    L, _, D = w.shape
    last = D_FF // tr - 1
    return pl.pallas_call(
        _ffn_down_layout_kernel,
        out_shape=jax.ShapeDtypeStruct((L, D_FF_PAD, D), BF16),
        grid=(L, D_FF_PAD // tr),
        in_specs=[pl.BlockSpec((1, tr, D), lambda l, i: (l, jnp.minimum(i, last), 0))],
        out_specs=pl.BlockSpec((1, tr, D), lambda l, i: (l, i, 0)),
        compiler_params=_params("parallel", "parallel"),
        name="ffn_down_layout",
    )(w)


def _gate_bias_layout(gate_b):
    L = gate_b.shape[0]
    gb = gate_b.reshape(L, NSA_KV_HEADS, NSA_GQA, 3).transpose(0, 1, 3, 2).reshape(L, 1, 3 * NSA_HEADS)
    return jnp.pad(gb, ((0, 0), (0, 0), (0, LANES - 3 * NSA_HEADS)))


def kernel(x, ffn1_w_up, ffn1_w_down, ln1_g, ln1_b, w_in, rw_mu, rw_w0, rw_w2, rw_a0, rw_a2, rw_g2, rw_k_k,
           rw_k_a, rw_r_k, rw_gn_g, rw_gn_b, pool_w, pool_b, pool_scale, nsa_cmp_pe_k, nsa_cmp_pe_v,
           nsa_cmp_k_w1, nsa_cmp_k_w2, nsa_cmp_v_w1, nsa_cmp_v_w2, nsa_gate_b, w_out, ln2_g, ln2_b,
           ffn2_w_up, ffn2_w_down, ln3_g, ln3_b):
    prm = _prepare(x.shape[1], ffn1_w_up, ffn1_w_down, ln1_g, ln1_b, w_in, rw_mu, rw_w0, rw_w2, rw_a0, rw_a2,
                   rw_g2, rw_k_k, rw_k_a, rw_r_k, rw_gn_g, rw_gn_b, pool_w, pool_b, pool_scale, nsa_cmp_pe_k,
                   nsa_cmp_pe_v, nsa_cmp_k_w1, nsa_cmp_k_w2, nsa_cmp_v_w1, nsa_cmp_v_w2, nsa_gate_b, w_out,
                   ln2_g, ln2_b, ffn2_w_up, ffn2_w_down, ln3_g, ln3_b)
    B, S, D = x.shape
    h = x.reshape(B * S, D)
    for l in range(w_in.shape[0]):
        h = _layer(h, prm, l, B, S)
    return h.reshape(B, S, D)


def _prepare(S, ffn1_w_up, ffn1_w_down, ln1_g, ln1_b, w_in, rw_mu, rw_w0, rw_w2, rw_a0, rw_a2, rw_g2, rw_k_k,
             rw_k_a, rw_r_k, rw_gn_g, rw_gn_b, pool_w, pool_b, pool_scale, nsa_cmp_pe_k, nsa_cmp_pe_v,
             nsa_cmp_k_w1, nsa_cmp_k_w2, nsa_cmp_v_w1, nsa_cmp_v_w2, nsa_gate_b, w_out, ln2_g, ln2_b,
             ffn2_w_up, ffn2_w_down, ln3_g, ln3_b):
    L = w_in.shape[0]

    up = _ffn_up_layout

    down = _ffn_down_layout

    row = lambda v: v[:, None, :]
    f1a, f1b = up(ffn1_w_up)
    f2a, f2b = up(ffn2_w_up)
    w_out_b = w_out.astype(BF16)
    gw = CMP_BLOCK * HEAD_DIM
    pad_w2 = lambda w: jnp.pad(w, ((0, 0), (0, 0), (0, LANES - HEAD_DIM))).astype(BF16)
    pe_rows = lambda pe: jnp.broadcast_to(pe.reshape(L, 1, gw), (L, SUBLANES, gw)).astype(BF16)
    ncp = S // CMP_STRIDE
    cos_t, sin_t = _rope_tables(jnp.arange(S), 4)
    cos_c, sin_c = _rope_tables(jnp.arange(ncp) * CMP_STRIDE + (CMP_BLOCK - 1), 2)
    return dict(
        ffn1=(f1a, f1b, down(ffn1_w_down), row(ln1_g), row(ln1_b)),
        ffn2=(f2a, f2b, down(ffn2_w_down), row(ln3_g), row(ln3_b)),
        w_in=_w_in_layout(w_in),
        w_out=(_parity_major(w_out_b[:, :D_RWKV], 1), w_out_b[:, D_RWKV:D_RWKV + D_POOL],
               w_out_b[:, D_RWKV + D_POOL:]),
        ln2=(row(ln2_g), row(ln2_b)),
        rw=dict(mu=row(rw_mu), w0=row(rw_w0), w2=rw_w2.astype(BF16), a0=row(rw_a0), a2=rw_a2.astype(BF16),
                g2=_parity_major(rw_g2, -1).astype(BF16), k_k=row(rw_k_k), k_a=row(rw_k_a),
                r_k=rw_r_k.reshape(L, 1, D_RWKV), gn_g=row(_parity_major(rw_gn_g, -1)),
                gn_b=row(_parity_major(rw_gn_b, -1)), ones=_block_ones(D_RWKV)),
        cmp=dict(k_w1=nsa_cmp_k_w1.reshape(L, gw, CMP_HIDDEN).astype(BF16), k_w2=pad_w2(nsa_cmp_k_w2),
                 k_pe=pe_rows(nsa_cmp_pe_k),
                 v_w1=nsa_cmp_v_w1.reshape(L, gw, CMP_HIDDEN).astype(BF16), v_w2=nsa_cmp_v_w2.transpose(0, 2, 1).astype(BF16),
                 v_pe=pe_rows(nsa_cmp_pe_v)),
        gate_b=_gate_bias_layout(nsa_gate_b),
        pool=(pool_w.astype(BF16), row(pool_b), row(pool_scale)),
        rope=(cos_t, sin_t), rope_cmp=(cos_c, sin_c), overlap=_overlap_matrix(S, ncp))


def _mixers(p_all, prm, l):
    r, w, k, v, kk, be, bo, g = _rw_prep(p_all, prm["rw"], l)
    y_rw = _rw_scan(r, w, k, v, kk, be, bo, g, prm["rw"], l)
    y_pool = _pool_mix(p_all, *prm["pool"], l)
    q, kc, vc, ks, vs, kw, vw, gates = _nsa_prep(p_all, *prm["rope"], prm["gate_b"], l)
    k_cmp, v_cmp = _nsa_compress(kc, vc, prm["cmp"], *prm["rope_cmp"], l)
    y_nsa = _nsa_attention(q, k_cmp, v_cmp, ks, vs, kw, vw, gates, prm["overlap"])
    return y_rw, y_pool, y_nsa


def _layer(h, prm, l, B, S):
    T = B * S
    h = _ffn_ln(h, *prm["ffn1"], l)
    p_all = _in_proj(h, prm["w_in"], l).reshape(B, S, P_COLS)
    y_rw, y_pool, y_nsa = _mixers(p_all, prm, l)
    h = _out_proj_ln(h, y_rw.reshape(T, D_RWKV), y_pool.reshape(T, D_POOL), y_nsa.reshape(T, D_NSA),
                     prm["w_out"], *prm["ln2"], l)
    return _ffn_ln(h, *prm["ffn2"], l)
```

```python
import functools

import numpy as np
import jax
import jax.numpy as jnp
from jax import lax
from jax.experimental import pallas as pl
from jax.experimental.pallas import tpu as pltpu

F32 = jnp.float32
BF16 = jnp.bfloat16

D_MODEL = 2048
DEPTH = 4
HEAD_DIM = 64
D_RWKV = 768
D_POOL = 512
D_NSA = 768
RW_HEADS = 12
RW_DECAY_LORA = 64
RW_A_LORA = 64
RW_GATE_LORA = 128
RW_GN_EPS = 64e-5
RW_COLS = 3 * D_RWKV + RW_DECAY_LORA + RW_A_LORA + RW_GATE_LORA
POOL_WINDOWS = (2, 4, 8, 16)
POOL_GROUP = 128
NSA_HEADS = 12
NSA_KV_HEADS = 3
NSA_GQA = 4
NSA_KV = 192
NSA_COLS = D_NSA + 6 * NSA_KV + 3 * NSA_HEADS
CMP_BLOCK = 32
CMP_STRIDE = 16
CMP_HIDDEN = 256
SEL_BLOCK = 64
SEL_TOPK = 16
FORCE_SCORE = 1e9
WINDOW = 512
ROPE_THETA = 500000.0
ROPE_DIM = 16
D_FF = 5504
IN_COLS = RW_COLS + D_POOL + NSA_COLS
ALPHA = (2 * DEPTH) ** 0.25
LN_EPS = 1e-5

LANES = 128
SUBLANES = 8
VMEM_LIMIT = 56 * 1024 * 1024

KV_SLOT = 2 * LANES
NSA_PAD = 2560
P_RW_OFF = 0
P_NSA_OFF = RW_COLS
P_POOL_OFF = RW_COLS + NSA_PAD
P_COLS = RW_COLS + NSA_PAD + D_POOL
NSA_GATE_OFF = D_NSA + 6 * KV_SLOT
D_FF_PAD = 5632

NEG = -1e30


def _params(*sem):
    return pltpu.CompilerParams(dimension_semantics=sem, vmem_limit_bytes=VMEM_LIMIT)


def _layer_norm(z, g, b):
    mu = jnp.mean(z, axis=-1, keepdims=True)
    zc = z - mu
    var = jnp.mean(zc * zc, axis=-1, keepdims=True)
    return zc * lax.rsqrt(var + LN_EPS) * g + b


def _dot(a, b):
    return jnp.dot(a, b, preferred_element_type=F32)


def _dot_nt(a, b):
    return lax.dot_general(a, b, (((1,), (1,)), ((), ())), preferred_element_type=F32)


def _dot_split(x, w):
    hi = x.astype(BF16)
    lo = (x - hi.astype(F32)).astype(BF16)
    return _dot(hi, w) + _dot(lo, w)


FFN_LN_ROWS = 256
FFN_MM_ROWS = 512


def _ffn_kernel(x_ref, wa_ref, wb_ref, wd_ref, g_ref, b_ref, o_ref, xb_ref):
    k = pl.program_id(1)

    @pl.when(k == 0)
    def _():
        xb_ref[...] = x_ref[...].astype(BF16)
        o_ref[...] = jnp.zeros_like(o_ref)

    for c in range(o_ref.shape[0] // FFN_MM_ROWS):
        rows = slice(c * FFN_MM_ROWS, (c + 1) * FFN_MM_ROWS)
        xb = xb_ref[rows, :]
        a = _dot(xb, wa_ref[...])
        b = _dot(xb, wb_ref[...])
        h = (a / (1.0 + jnp.exp(-a))) * b
        o_ref[rows, :] += _dot(h.astype(BF16), wd_ref[...])

    @pl.when(k == pl.num_programs(1) - 1)
    def _():
        for c in range(o_ref.shape[0] // FFN_LN_ROWS):
            rows = slice(c * FFN_LN_ROWS, (c + 1) * FFN_LN_ROWS)
            z = ALPHA * x_ref[rows, :] + 0.5 * o_ref[rows, :]
            o_ref[rows, :] = _layer_norm(z, g_ref[...], b_ref[...])


def _ffn_ln(x, wa, wb, wd, g, b, layer, *, tm=1024, tf=512):
    T, D = x.shape
    fp = wa.shape[-1]
    tm = min(tm, T)
    return pl.pallas_call(
        _ffn_kernel,
        out_shape=jax.ShapeDtypeStruct((T, D), F32),
        grid=(T // tm, fp // tf),
        in_specs=[
            pl.BlockSpec((tm, D), lambda i, k: (i, 0)),
            pl.BlockSpec((None, D, tf), lambda i, k: (layer, 0, k)),
            pl.BlockSpec((None, D, tf), lambda i, k: (layer, 0, k)),
            pl.BlockSpec((None, tf, D), lambda i, k: (layer, k, 0)),
            pl.BlockSpec((None, 1, D), lambda i, k: (layer, 0, 0)),
            pl.BlockSpec((None, 1, D), lambda i, k: (layer, 0, 0)),
        ],
        out_specs=pl.BlockSpec((tm, D), lambda i, k: (i, 0)),
        scratch_shapes=[pltpu.VMEM((tm, D), BF16)],
        compiler_params=_params("parallel", "arbitrary"),
        name="ffn_ln",
    )(x, wa, wb, wd, g, b)


def _inproj_kernel(x_ref, w_ref, o_ref, xb_ref):
    @pl.when(pl.program_id(1) == 0)
    def _():
        xb_ref[...] = x_ref[...].astype(BF16)

    o_ref[...] = _dot(xb_ref[...], w_ref[...])


def _in_proj(x, w, layer, *, tm=1024, tn=P_COLS // 4):
    T, D = x.shape
    n = w.shape[-1]
    tm = min(tm, T)
    return pl.pallas_call(
        _inproj_kernel,
        out_shape=jax.ShapeDtypeStruct((T, n), F32),
        grid=(T // tm, n // tn),
        in_specs=[
            pl.BlockSpec((tm, D), lambda i, j: (i, 0)),
            pl.BlockSpec((None, D, tn), lambda i, j: (layer, 0, j)),
        ],
        out_specs=pl.BlockSpec((tm, tn), lambda i, j: (i, j)),
        scratch_shapes=[pltpu.VMEM((tm, D), BF16)],
        compiler_params=_params("parallel", "arbitrary"),
        name="in_proj",
    )(x, w)


def _outproj_kernel(x_ref, yr_ref, yp_ref, yn_ref, wr_ref, wp_ref, wn_ref, g_ref, b_ref, o_ref):
    y = _dot(yr_ref[...], wr_ref[...]) + _dot(yp_ref[...], wp_ref[...]) + _dot(yn_ref[...], wn_ref[...])
    o_ref[...] = _layer_norm(ALPHA * x_ref[...] + y, g_ref[...], b_ref[...])


def _out_proj_ln(x, y_rw, y_pool, y_nsa, w_out, g, b, layer, *, tm=512):
    T, D = x.shape
    return pl.pallas_call(
        _outproj_kernel,
        out_shape=jax.ShapeDtypeStruct((T, D), F32),
        grid=(T // tm,),
        in_specs=[
            pl.BlockSpec((tm, D), lambda i: (i, 0)),
            pl.BlockSpec((tm, D_RWKV), lambda i: (i, 0)),
            pl.BlockSpec((tm, D_POOL), lambda i: (i, 0)),
            pl.BlockSpec((tm, D_NSA), lambda i: (i, 0)),
            pl.BlockSpec((None, D_RWKV, D), lambda i: (layer, 0, 0)),
            pl.BlockSpec((None, D_POOL, D), lambda i: (layer, 0, 0)),
            pl.BlockSpec((None, D_NSA, D), lambda i: (layer, 0, 0)),
            pl.BlockSpec((None, 1, D), lambda i: (layer, 0, 0)),
            pl.BlockSpec((None, 1, D), lambda i: (layer, 0, 0)),
        ],
        out_specs=pl.BlockSpec((tm, D), lambda i: (i, 0)),
        compiler_params=_params("parallel"),
        name="out_proj_ln",
    )(x, y_rw, y_pool, y_nsa, w_out[0], w_out[1], w_out[2], g, b)


POOL_HALO = 16


def _pool_kernel(p_ref, halo_ref, w_ref, b_ref, sc_ref, o_ref, xs_ref, *, ts):
    s = pl.program_id(1)
    x = p_ref[0]
    halo = jnp.where(s > 0, halo_ref[0], 0.0)
    xs_ref[0:POOL_HALO, :] = halo
    xs_ref[POOL_HALO:POOL_HALO + ts, :] = x
    t1 = (s * ts + 1 + lax.broadcasted_iota(jnp.int32, (ts, 1), 0)).astype(F32)
    for gi, win in enumerate(POOL_WINDOWS):
        c0 = gi * POOL_GROUP
        acc = x[:, c0:c0 + POOL_GROUP]
        for j in range(1, win):
            acc = acc + xs_ref[POOL_HALO - j:POOL_HALO - j + ts, c0:c0 + POOL_GROUP]
        pooled = acc / jnp.minimum(t1, float(win)) - x[:, c0:c0 + POOL_GROUP]
        z = _dot(pooled.astype(BF16), w_ref[gi]) + b_ref[:, c0:c0 + POOL_GROUP]
        o_ref[0, :, c0:c0 + POOL_GROUP] = (z * sc_ref[:, c0:c0 + POOL_GROUP]).astype(o_ref.dtype)


def _pool_mix(p_all, pool_w, pool_b, pool_scale, layer, *, ts=512):
    B, S, _ = p_all.shape
    cb = P_POOL_OFF // D_POOL
    hb = ts // POOL_HALO
    return pl.pallas_call(
        functools.partial(_pool_kernel, ts=ts),
        out_shape=jax.ShapeDtypeStruct((B, S, D_POOL), BF16),
        grid=(B, S // ts),
        in_specs=[
            pl.BlockSpec((1, ts, D_POOL), lambda b, s: (b, s, cb)),
            pl.BlockSpec((1, POOL_HALO, D_POOL), lambda b, s: (b, jnp.maximum(s * hb - 1, 0), cb)),
            pl.BlockSpec((None, 4, POOL_GROUP, POOL_GROUP), lambda b, s: (layer, 0, 0, 0)),
            pl.BlockSpec((None, 1, D_POOL), lambda b, s: (layer, 0, 0)),
            pl.BlockSpec((None, 1, D_POOL), lambda b, s: (layer, 0, 0)),
        ],
        out_specs=pl.BlockSpec((1, ts, D_POOL), lambda b, s: (b, s, 0)),
        scratch_shapes=[pltpu.VMEM((ts + POOL_HALO, D_POOL), F32)],
        compiler_params=_params("parallel", "parallel"),
        name="pool_mix",
    )(p_all, p_all, pool_w, pool_b, pool_scale)


def _softplus(z):
    return jnp.maximum(z, 0.0) + jnp.log1p(jnp.exp(-jnp.abs(z)))


def _sigmoid(z):
    return 1.0 / (1.0 + jnp.exp(-z))


def _lanes_to_parity_major(x):
    n_pairs = x.shape[1] // LANES
    low = lax.broadcasted_iota(jnp.int32, (1, LANES), 1) < HEAD_DIM

    def chunk(e, to_low):
        pair, parity = e % n_pairs, e // n_pairs
        src = x[:, pair * LANES:(pair + 1) * LANES]
        return src if (parity == 0) == to_low else pltpu.roll(src, HEAD_DIM, 1)

    return jnp.concatenate([jnp.where(low, chunk(2 * d, True), chunk(2 * d + 1, False))
                            for d in range(n_pairs)], axis=1)


def _rw_prep_kernel(p_ref, prev_ref, mu_ref, w0_ref, w2_ref, a0_ref, a2_ref, g2_ref, kk_ref, ka_ref,
                    rk_ref, ones_ref,
                    r_o, w_o, k_o, v_o, kk_o, be_o, bo_o, g_o, *, ts):
    s = pl.program_id(1)
    x = p_ref[0]
    last = jnp.where(s > 0, prev_ref[0][SUBLANES - 1:SUBLANES, :], 0.0)
    row = lax.broadcasted_iota(jnp.int32, (ts, 1), 0)
    shifted = jnp.where(row == 0, last, pltpu.roll(x, 1, 0))
    xm = x + (shifted - x) * mu_ref[...]
    c = D_RWKV
    r = xm[:, 0:c]
    k = xm[:, c:2 * c]
    v = xm[:, 2 * c:3 * c]
    lora = xm[:, 3 * c:3 * c + LANES]
    wl = lora[:, :RW_DECAY_LORA]
    al = lora[:, RW_DECAY_LORA:]
    gl = xm[:, 3 * c + LANES:]
    w = -_softplus(-(w0_ref[...] + _dot(jnp.tanh(wl).astype(BF16), w2_ref[...]))) - 0.5
    decay = jnp.exp(-jnp.exp(w))
    a = _sigmoid(a0_ref[...] + _dot(al.astype(BF16), a2_ref[...]))
    g = _dot(_sigmoid(gl).astype(BF16), g2_ref[...])
    ones = ones_ref[...]
    kk = k * kk_ref[...]
    nrm = jnp.sqrt(_dot_split(kk * kk, ones))
    kk = kk / jnp.maximum(nrm, 1e-12)
    k_mod = k * (1.0 + (a - 1.0) * ka_ref[...])
    bonus = _dot_split(r * k_mod * rk_ref[...], ones) * v
    r_o[0] = r
    w_o[0] = decay
    k_o[0] = k_mod
    v_o[0] = v
    kk_o[0] = kk
    be_o[0] = kk * a
    bo_o[0] = _lanes_to_parity_major(bonus).astype(bo_o.dtype)
    g_o[0] = g.astype(g_o.dtype)


def _rw_prep(p_all, prm, layer, *, ts=512):
    B, S, _ = p_all.shape
    cb = P_RW_OFF // RW_COLS
    hb = ts // SUBLANES
    vec = lambda n: pl.BlockSpec((None, 1, n), lambda b, s: (layer, 0, 0))
    mat = lambda m, n: pl.BlockSpec((None, m, n), lambda b, s: (layer, 0, 0))
    out = jax.ShapeDtypeStruct((B, S, D_RWKV), F32)
    ospec = pl.BlockSpec((1, ts, D_RWKV), lambda b, s: (b, s, 0))
    return pl.pallas_call(
        functools.partial(_rw_prep_kernel, ts=ts),
        out_shape=[out] * 6 + [jax.ShapeDtypeStruct((B, S, D_RWKV), BF16)] * 2,
        grid=(B, S // ts),
        in_specs=[
            pl.BlockSpec((1, ts, RW_COLS), lambda b, s: (b, s, cb)),
            pl.BlockSpec((1, SUBLANES, RW_COLS), lambda b, s: (b, jnp.maximum(s * hb - 1, 0), cb)),
            vec(RW_COLS), vec(D_RWKV), mat(RW_DECAY_LORA, D_RWKV), vec(D_RWKV), mat(RW_A_LORA, D_RWKV),
            mat(RW_GATE_LORA, D_RWKV), vec(D_RWKV), vec(D_RWKV), vec(D_RWKV),
            pl.BlockSpec((D_RWKV, D_RWKV), lambda b, s: (0, 0)),
        ],
        out_specs=[ospec] * 8,
        compiler_params=_params("parallel", "parallel"),
        name="rw_prep",
    )(p_all, p_all, prm["mu"], prm["w0"], prm["w2"], prm["a0"], prm["a2"], prm["g2"], prm["k_k"],
      prm["k_a"], prm["r_k"], prm["ones"])


RW_PAIRS = D_RWKV // LANES
RW_STEPS = 16


def _rw_scan_kernel(r_ref, w_ref, k_ref, v_ref, kk_ref, be_ref, bo_ref, g_ref, gng_ref, gnb_ref,
                    pair_ref, half_ref, o_ref, st_ref, vc_ref, sr_ref, y_ref, *, ts, nb):
    s = pl.program_id(1)

    @pl.when(s == 0)
    def _():
        st_ref[...] = jnp.zeros_like(st_ref)

    side, stack = 2, RW_PAIRS // 2
    rows, width = stack * HEAD_DIM, side * LANES
    lane = lax.broadcasted_iota(jnp.int32, (rows, width), 1)
    row = lax.broadcasted_iota(jnp.int32, (rows, width), 0)
    eye = jnp.where(lane % HEAD_DIM == row % HEAD_DIM, 1.0, 0.0)
    ones_blk = pair_ref[...]

    def spread(x8, j):
        blocks = []
        for st in range(stack):
            lanes = jnp.concatenate([x8[j:j + 1, (sd * stack + st) * LANES:(sd * stack + st + 1) * LANES]
                                     for sd in range(side)], axis=1)
            blocks.append(jnp.broadcast_to(lanes, (HEAD_DIM, width)))
        return jnp.concatenate(blocks, axis=0)

    def step(i, carry):
        base = pl.multiple_of(i * RW_STEPS, RW_STEPS)
        tiles = [[ref[b, pl.ds(base, RW_STEPS), :] for ref in (kk_ref, w_ref, be_ref, k_ref, r_ref, v_ref)]
                 for b in range(nb)]
        for b in range(nb):
            lhs = jnp.concatenate([eye * spread(tiles[b][5], j) for j in range(RW_STEPS)], axis=0)
            vc_ref[b] = _dot(lhs.astype(BF16), ones_blk)
        sts = [st_ref[b] for b in range(nb)]
        for j in range(RW_STEPS):
            for b in range(nb):
                kk8, w8, be8, k8, r8, _ = tiles[b]
                st = sts[b]
                sa = _dot((st * spread(kk8, j)).astype(BF16), ones_blk)
                st = (st * spread(w8, j) - sa * spread(be8, j)
                      + vc_ref[b, j * rows:(j + 1) * rows, :] * spread(k8, j))
                sr = (st * spread(r8, j)).astype(BF16)
                for sd in range(side):
                    lo = (j * side + sd) * rows
                    sr_ref[b, lo:lo + rows, :] = sr[:, sd * LANES:(sd + 1) * LANES]
                sts[b] = st
        for b in range(nb):
            st_ref[b] = sts[b]
            yt = _dot_nt(half_ref[...], sr_ref[b])
            per = side * rows
            y_ref[b, pl.ds(base, RW_STEPS), :] = jnp.concatenate(
                [jnp.concatenate([yt[0:1, j * per:(j + 1) * per], yt[1:2, j * per:(j + 1) * per]], axis=1)
                 for j in range(RW_STEPS)], axis=0)
        return carry

    lax.fori_loop(0, ts // RW_STEPS, step, 0)

    def head_sums(x):
        n = x.shape[0]
        hi = x.astype(BF16)
        both = jnp.concatenate([hi, (x - hi.astype(F32)).astype(BF16)], axis=0)
        out = jnp.concatenate([_dot(both[:, c * width:(c + 1) * width], ones_blk)
                               for c in range(D_RWKV // width)], axis=1)
        return out[:n] + out[n:]

    y = y_ref[...].reshape(nb * ts, D_RWKV)
    mean = head_sums(y) * (1.0 / HEAD_DIM)
    yc = y - mean
    var = head_sums(yc * yc) * (1.0 / HEAD_DIM)
    yn = yc * lax.rsqrt(var + RW_GN_EPS) * gng_ref[...] + gnb_ref[...]
    out = (yn + bo_ref[...].reshape(nb * ts, D_RWKV)) * g_ref[...].reshape(nb * ts, D_RWKV)
    o_ref[...] = out.reshape(nb, ts, D_RWKV).astype(o_ref.dtype)


def _rw_scan(r, w, k, v, kk, be, bo, g, prm, layer, *, ts=128):
    B, S, _ = r.shape
    nb = 4 if B % 4 == 0 else (2 if B % 2 == 0 else 1)
    rows, width = RW_PAIRS // 2 * HEAD_DIM, 2 * LANES
    seq = pl.BlockSpec((nb, ts, D_RWKV), lambda b, s: (b, s, 0))
    vec = pl.BlockSpec((None, 1, D_RWKV), lambda b, s: (layer, 0, 0))
    return pl.pallas_call(
        functools.partial(_rw_scan_kernel, ts=ts, nb=nb),
        out_shape=jax.ShapeDtypeStruct((B, S, D_RWKV), BF16),
        grid=(B // nb, S // ts),
        in_specs=[seq] * 8 + [vec, vec, pl.BlockSpec((width, width), lambda b, s: (0, 0)),
                              pl.BlockSpec((SUBLANES, LANES), lambda b, s: (0, 0))],
        out_specs=seq,
        scratch_shapes=[pltpu.VMEM((nb, rows, width), F32), pltpu.VMEM((nb, RW_STEPS * rows, width), F32),
                        pltpu.VMEM((nb, RW_STEPS * RW_PAIRS * HEAD_DIM, LANES), BF16),
                        pltpu.VMEM((nb, ts, D_RWKV), F32)],
        compiler_params=_params("parallel", "arbitrary"),
        name="rw_scan",
    )(r, w, k, v, kk, be, bo, g, prm["gn_g"], prm["gn_b"], prm["ones"][:width, :width],
      prm["ones"][:SUBLANES * HEAD_DIM:HEAD_DIM, :LANES])


def _rope(x, cos, sin):
    half = ROPE_DIM // 2
    w = x.shape[-1]
    d = lax.broadcasted_iota(jnp.int32, x.shape, 1) % HEAD_DIM
    partner = jnp.where(d < half, pltpu.roll(x, w - half, 1), pltpu.roll(x, half, 1))
    return x * cos + partner * sin


def _nsa_prep_kernel(p_ref, cos_ref, sin_ref, gb_ref, q_o, kc_o, vc_o, ks_o, vs_o, kw_o, vw_o, gt_o):
    cos = cos_ref[...]
    sin = sin_ref[...]
    for hq in range(NSA_HEADS // 4):
        x = p_ref[0, :, hq * KV_SLOT:(hq + 1) * KV_SLOT]
        qr = _rope(x, cos, sin) * (HEAD_DIM ** -0.5)
        for j in range(4):
            q_o[0, 4 * hq + j] = qr[:, j * HEAD_DIM:(j + 1) * HEAD_DIM].astype(BF16)
    for i, ref in enumerate((kc_o, vc_o, ks_o, vs_o, kw_o, vw_o)):
        part = p_ref[0, :, D_NSA + i * KV_SLOT:D_NSA + (i + 1) * KV_SLOT]
        if i in (2, 4):
            part = _rope(part, cos, sin)
        if i in (3, 5):
            part = part.T
            for h in range(NSA_KV_HEADS):
                ref[0, h] = part[h * HEAD_DIM:(h + 1) * HEAD_DIM, :].astype(BF16)
        else:
            for h in range(NSA_KV_HEADS):
                ref[0, h] = part[:, h * HEAD_DIM:(h + 1) * HEAD_DIM].astype(BF16)
    gts = _sigmoid(p_ref[0, :, NSA_GATE_OFF:NSA_GATE_OFF + LANES] + gb_ref[...]).T
    for h in range(NSA_KV_HEADS):
        gt_o[0, h] = gts[h * 12:(h + 1) * 12, :]


def _nsa_prep(p_all, cos, sin, gate_b, layer, *, ts=512):
    B, S, _ = p_all.shape
    H = NSA_KV_HEADS
    k_shape = jax.ShapeDtypeStruct((B, H, S, HEAD_DIM), BF16)
    k_spec = pl.BlockSpec((1, H, ts, HEAD_DIM), lambda b, s: (b, 0, s, 0))
    vt_shape = jax.ShapeDtypeStruct((B, H, HEAD_DIM, S), BF16)
    vt_spec = pl.BlockSpec((1, H, HEAD_DIM, ts), lambda b, s: (b, 0, 0, s))
    return pl.pallas_call(
        _nsa_prep_kernel,
        out_shape=[jax.ShapeDtypeStruct((B, NSA_HEADS, S, HEAD_DIM), BF16),
                   k_shape, k_shape, k_shape, vt_shape, k_shape, vt_shape,
                   jax.ShapeDtypeStruct((B, H, 12, S), F32)],
        grid=(B, S // ts),
        in_specs=[
            pl.BlockSpec((1, ts, NSA_PAD), lambda b, s: (b, s, P_NSA_OFF // NSA_PAD)),
            pl.BlockSpec((ts, KV_SLOT), lambda b, s: (s, 0)),
            pl.BlockSpec((ts, KV_SLOT), lambda b, s: (s, 0)),
            pl.BlockSpec((None, 1, LANES), lambda b, s: (layer, 0, 0)),
        ],
        out_specs=[pl.BlockSpec((1, NSA_HEADS, ts, HEAD_DIM), lambda b, s: (b, 0, s, 0)),
                   k_spec, k_spec, k_spec, vt_spec, k_spec, vt_spec,
                   pl.BlockSpec((1, H, 12, ts), lambda b, s: (b, 0, 0, s))],
        compiler_params=_params("parallel", "parallel"),
        name="nsa_prep",
    )(p_all, cos, sin, gate_b)


def _gelu_tanh(x):
    return 0.5 * x * (1.0 + jnp.tanh(float(np.sqrt(2.0 / np.pi)) * (x + 0.044715 * (x * x * x))))


def _nsa_cmp_kernel(kc_ref, vc_ref, kw1_ref, kw2_ref, kpe_ref, vw1_ref, vw2_ref, vpe_ref, cos_ref, sin_ref,
                    k_o, v_o):
    def hidden(g, w1_ref, pe_ref):
        half = CMP_STRIDE * HEAD_DIM
        first = _dot(g, w1_ref[:half, :])
        second = _dot(g, w1_ref[half:, :])
        n = first.shape[0]
        bias = _dot(pe_ref[...], w1_ref[...])[0:1]
        return _gelu_tanh(first + pltpu.roll(second, n - 1, 0) + bias).astype(BF16)

    k = _dot(hidden(kc_ref[0, 0], kw1_ref, kpe_ref), kw2_ref[...])
    k_o[0, 0] = _rope(k, cos_ref[...], sin_ref[...])[:, :HEAD_DIM].astype(BF16)
    v_o[0, 0] = _dot_nt(vw2_ref[...], hidden(vc_ref[0, 0], vw1_ref, vpe_ref)).astype(BF16)


def _nsa_compress(kc, vc, prm, cos_c, sin_c, layer):
    B, H, S, _ = kc.shape
    ng = S // CMP_STRIDE
    gw = CMP_STRIDE * HEAD_DIM
    g_k = kc.reshape(B, H, ng, gw)
    g_v = vc.reshape(B, H, ng, gw)
    gspec = pl.BlockSpec((1, 1, ng, gw), lambda b, h: (b, h, 0, 0))
    w1 = pl.BlockSpec((None, 2 * gw, CMP_HIDDEN), lambda b, h: (layer, 0, 0))
    w2 = pl.BlockSpec((None, CMP_HIDDEN, LANES), lambda b, h: (layer, 0, 0))
    pe = pl.BlockSpec((None, SUBLANES, 2 * gw), lambda b, h: (layer, 0, 0))
    tab = pl.BlockSpec((ng, LANES), lambda b, h: (0, 0))
    return pl.pallas_call(
        _nsa_cmp_kernel,
        out_shape=[jax.ShapeDtypeStruct((B, H, ng, HEAD_DIM), BF16), jax.ShapeDtypeStruct((B, H, HEAD_DIM, ng), BF16)],
        grid=(B, H),
        in_specs=[gspec, gspec, w1, w2, pe, w1,
                  pl.BlockSpec((None, HEAD_DIM, CMP_HIDDEN), lambda b, h: (layer, 0, 0)), pe, tab, tab],
        out_specs=[pl.BlockSpec((1, 1, ng, HEAD_DIM), lambda b, h: (b, h, 0, 0)),
                   pl.BlockSpec((1, 1, HEAD_DIM, ng), lambda b, h: (b, h, 0, 0))],
        compiler_params=_params("parallel", "parallel"),
        name="nsa_compress",
    )(g_k, g_v, prm["k_w1"], prm["k_w2"], prm["k_pe"], prm["v_w1"], prm["v_w2"], prm["v_pe"], cos_c, sin_c)


def _nsa_attn_kernel(q_ref, kcmp_ref, vcmp_ref, ks_ref, vs_ref, kw_ref, vw_ref, gt_ref, ov_ref, o_ref,
                     m_ref, l_ref, acc_ref, sb_ref, *, tq):
    G = NSA_GQA
    tk = tq
    qi = pl.program_id(2)
    q0 = qi * tq
    q_all = q_ref[0].reshape(G * tq, HEAD_DIM)
    pos = q0 + lax.broadcasted_iota(jnp.int32, (1, tq), 1)
    tiny = jnp.finfo(F32).tiny
    heads = lambda x: jnp.tile(x, (1, G))
    split = lambda x: [x[:, g * tq:(g + 1) * tq] for g in range(G)]

    ncp = kcmp_ref.shape[2]
    n_idx = lax.broadcasted_iota(jnp.int32, (ncp, 1), 0)
    cbias = jnp.where((n_idx * CMP_STRIDE + (CMP_BLOCK - 1)) <= pos, 0.0, NEG)
    any_cmp = jnp.where(pos >= CMP_BLOCK - 1, 1.0, 0.0)
    s = _dot_nt(kcmp_ref[0, 0], q_all) + heads(cbias)
    e = jnp.exp(s - jnp.max(s, axis=0, keepdims=True))
    p = e * (heads(any_cmp) / jnp.maximum(jnp.sum(e, axis=0, keepdims=True), tiny))
    o_cmp = split(_dot(vcmp_ref[0, 0], p.astype(BF16)))
    ps = split(p)
    psum = (ps[0] + ps[1]) + (ps[2] + ps[3])
    p_hi = psum.astype(BF16)
    p_lo = (psum - p_hi.astype(F32)).astype(BF16)
    imp2 = _dot(ov_ref[...], jnp.concatenate([p_hi, p_lo], axis=1))
    imp = imp2[:, :tq] + imp2[:, tq:]

    wk = min(WINDOW + tq, ks_ref.shape[2])
    w0 = pl.multiple_of(jnp.maximum(q0 + tq - wk, 0), tk)
    wpos = w0 + lax.broadcasted_iota(jnp.int32, (wk, 1), 0)
    wbias = jnp.where((wpos <= pos) & (wpos > pos - WINDOW), 0.0, NEG)
    s = _dot_nt(kw_ref[0, 0, pl.ds(w0, wk), :], q_all) + heads(wbias)
    e = jnp.exp(s - jnp.max(s, axis=0, keepdims=True))
    pv = _dot(vw_ref[0, 0, :, pl.ds(w0, wk)], e.astype(BF16))
    o_win = split(pv / jnp.maximum(jnp.sum(e, axis=0, keepdims=True), tiny))

    n_sel = ov_ref.shape[0]
    blk = lax.broadcasted_iota(jnp.int32, (n_sel, 1), 0)
    cur = pos // SEL_BLOCK
    forced = (blk == 0) | (blk == cur) | (blk == cur - 1)
    valid = blk <= cur
    score = jnp.where(valid, jnp.where(forced, FORCE_SCORE, imp), -jnp.inf)
    groups = [score[g * SUBLANES:(g + 1) * SUBLANES, :] for g in range(n_sel // SUBLANES)]
    ranks = [jnp.zeros((SUBLANES, tq), F32) for _ in groups]
    sub = lax.broadcasted_iota(jnp.int32, (SUBLANES, 1), 0)
    for i in range(n_sel):
        ci = jnp.broadcast_to(score[i:i + 1, :], (SUBLANES, tq))
        for g, sg in enumerate(groups):
            if g > i // SUBLANES:
                beats = ci >= sg
            elif g < i // SUBLANES:
                beats = ci > sg
            else:
                beats = (ci > sg) | ((ci == sg) & (sub > i % SUBLANES))
            ranks[g] = ranks[g] + jnp.where(beats, 1.0, 0.0)
    rank = jnp.concatenate(ranks, axis=0)
    chosen = (rank < float(SEL_TOPK)) & valid
    sel_bias = jnp.where(chosen, 0.0, NEG)
    for j in range(n_sel):
        sb_ref[j] = jnp.broadcast_to(sel_bias[j:j + 1, :], (SUBLANES, tq))

    kcol = lax.broadcasted_iota(jnp.int32, (tk, 1), 0)

    def sel_update(kt, n, extra=None):
        bpt = n // SEL_BLOCK
        k0 = pl.multiple_of(kt * tk, tk)
        s = _dot_nt(ks_ref[0, 0, pl.ds(k0, n), :], q_all)
        b0 = kt * (tk // SEL_BLOCK)
        bias = jnp.concatenate([jnp.tile(sb_ref[b0 + j], (SEL_BLOCK // SUBLANES, 1)) for j in range(bpt)],
                               axis=0)
        if extra is not None:
            bias = bias + extra
        vt = vs_ref[0, 0, :, pl.ds(k0, n)]
        s = s + heads(bias)
        m_prev = m_ref[...]
        m_new = jnp.maximum(m_prev, jnp.max(s, axis=0, keepdims=True))
        p = jnp.exp(s - m_new)
        corr = jnp.exp(m_prev - m_new)
        l_ref[...] = corr * l_ref[...] + jnp.sum(p, axis=0, keepdims=True)
        acc_ref[...] = corr * acc_ref[...] + _dot(vt, p.astype(BF16))
        m_ref[...] = m_new

    m_ref[...] = jnp.full_like(m_ref, NEG)
    l_ref[...] = jnp.zeros_like(l_ref)
    acc_ref[...] = jnp.zeros_like(acc_ref)

    def sel_body(i, carry):
        sel_update(4 * i, 4 * tk)
        return carry

    lax.fori_loop(0, qi // 4, sel_body, 0)

    @pl.when(qi % 4 >= 2)
    def _():
        sel_update((qi // 4) * 4, 2 * tk)

    @pl.when(qi % 2 == 1)
    def _():
        sel_update(qi - 1, tk)

    causal = jnp.where(q0 + kcol <= pos, 0.0, NEG)
    sel_update(qi, tk, causal)
    o_sel = split(acc_ref[...] / jnp.maximum(l_ref[...], tiny))

    gt = gt_ref[0, 0]
    outs = [gt[g:g + 1] * o_cmp[g] + gt[G + g:G + g + 1] * o_sel[g] + gt[2 * G + g:2 * G + g + 1] * o_win[g]
            for g in range(G)]
    for half in range(G // 2):
        pair = jnp.concatenate(outs[2 * half:2 * half + 2], axis=0)
        o_ref[0, :, half * LANES:(half + 1) * LANES] = pair.T.astype(o_ref.dtype)


def _nsa_attention(q, k_cmp, v_cmp_t, ks, vs_t, kw, vw_t, gates_t, overlap_t, *, tq=256):
    B, _, S, _ = q.shape
    H, G = NSA_KV_HEADS, NSA_GQA
    tq = min(tq, S)
    assert WINDOW == 2 * tq or S <= tq
    ncp = k_cmp.shape[2]
    n_cmp = (S - CMP_BLOCK) // CMP_STRIDE + 1
    keys = pl.BlockSpec((1, 1, S, HEAD_DIM), lambda b, h, i: (b, h, 0, 0))
    vals = pl.BlockSpec((1, 1, HEAD_DIM, S), lambda b, h, i: (b, h, 0, 0))
    return pl.pallas_call(
        functools.partial(_nsa_attn_kernel, tq=tq),
        out_shape=jax.ShapeDtypeStruct((B, S, D_NSA), BF16),
        grid=(B, H, S // tq),
        in_specs=[
            pl.BlockSpec((1, G, tq, HEAD_DIM), lambda b, h, i: (b, h, i, 0)),
            pl.BlockSpec((1, 1, ncp, HEAD_DIM), lambda b, h, i: (b, h, 0, 0)),
            pl.BlockSpec((1, 1, HEAD_DIM, ncp), lambda b, h, i: (b, h, 0, 0)),
            keys, vals, keys, vals,
            pl.BlockSpec((1, 1, 12, tq), lambda b, h, i: (b, h, 0, i)),
            pl.BlockSpec(overlap_t.shape, lambda b, h, i: (0, 0)),
        ],
        out_specs=pl.BlockSpec((1, tq, G * HEAD_DIM), lambda b, h, i: (b, i, h)),
        scratch_shapes=[pltpu.VMEM((1, G * tq), F32), pltpu.VMEM((1, G * tq), F32),
                        pltpu.VMEM((HEAD_DIM, G * tq), F32), pltpu.VMEM((S // SEL_BLOCK, SUBLANES, tq), F32)],
        compiler_params=_params("parallel", "parallel", "arbitrary"),
        name="nsa_attention",
    )(q, k_cmp, v_cmp_t, ks, vs_t, kw, vw_t, gates_t, overlap_t)


def _rope_tables(pos, heads):
    half = ROPE_DIM // 2
    inv_freq = ROPE_THETA ** (-jnp.arange(half, dtype=F32) / half)
    ang = pos.astype(F32)[:, None] * inv_freq
    cos, sin = jnp.cos(ang), jnp.sin(ang)
    n = pos.shape[0]
    rest = HEAD_DIM - ROPE_DIM
    cos_h = jnp.concatenate([cos, cos, jnp.ones((n, rest), F32)], axis=1)
    sin_h = jnp.concatenate([-sin, sin, jnp.zeros((n, rest), F32)], axis=1)
    return jnp.tile(cos_h, (1, heads)), jnp.tile(sin_h, (1, heads))


def _overlap_matrix(S, ncp):
    n_cmp = (S - CMP_BLOCK) // CMP_STRIDE + 1
    n_sel = S // SEL_BLOCK
    cmp_start = np.arange(n_cmp) * CMP_STRIDE
    sel_start = np.arange(n_sel) * SEL_BLOCK
    ov = np.clip(np.minimum(cmp_start[:, None] + CMP_BLOCK, sel_start[None, :] + SEL_BLOCK)
                 - np.maximum(cmp_start[:, None], sel_start[None, :]), 0, None) / CMP_BLOCK
    full = np.zeros((n_sel, ncp), np.float32)
    full[:, :n_cmp] = ov.T
    return jnp.asarray(full, BF16)


RW_HEAD_ORDER = tuple(2 * (e % RW_PAIRS) + e // RW_PAIRS for e in range(RW_HEADS))


def _parity_major(a, axis):
    axis = axis % a.ndim
    shape = a.shape
    a = a.reshape(shape[:axis] + (RW_HEADS, HEAD_DIM) + shape[axis + 1:])
    a = jnp.take(a, jnp.asarray(RW_HEAD_ORDER), axis=axis)
    return a.reshape(shape)


def _block_ones(n):
    idx = np.arange(n) // HEAD_DIM
    return jnp.asarray((idx[:, None] == idx[None, :]).astype(np.float32), BF16)


def _w_in_layout_kernel(w_ref, g_ref, o_ref):
    nsa0 = RW_COLS + D_POOL
    o_ref[0, :, P_RW_OFF:P_RW_OFF + RW_COLS] = w_ref[0, :, 0:RW_COLS].astype(BF16)
    o_ref[0, :, P_POOL_OFF:P_POOL_OFF + D_POOL] = w_ref[0, :, RW_COLS:nsa0].astype(BF16)
    o_ref[0, :, P_NSA_OFF:P_NSA_OFF + D_NSA] = w_ref[0, :, nsa0:nsa0 + D_NSA].astype(BF16)
    rows = o_ref.shape[1]
    for i in range(6):
        src = nsa0 + D_NSA + i * NSA_KV
        part = jnp.concatenate([w_ref[0, :, src:src + NSA_KV], jnp.zeros((rows, KV_SLOT - NSA_KV), F32)], axis=1)
        o_ref[0, :, P_NSA_OFF + D_NSA + i * KV_SLOT:P_NSA_OFF + D_NSA + (i + 1) * KV_SLOT] = part.astype(BF16)
    g0 = nsa0 + D_NSA + 6 * NSA_KV
    gates = jnp.concatenate([w_ref[0, :, g0:g0 + 3 * NSA_HEADS], jnp.zeros((rows, LANES - 3 * NSA_HEADS), F32)],
                            axis=1).astype(BF16)
    o_ref[0, :, P_NSA_OFF + NSA_GATE_OFF:P_NSA_OFF + NSA_PAD] = _dot(gates, g_ref[...]).astype(BF16)


def _gate_column_permutation():
    m = np.zeros((LANES, NSA_PAD - NSA_GATE_OFF), np.float32)
    for kv in range(NSA_KV_HEADS):
        for g in range(NSA_GQA):
            for br in range(3):
                m[(kv * NSA_GQA + g) * 3 + br, (kv * 3 + br) * NSA_GQA + g] = 1.0
    return jnp.asarray(m, BF16)


def _w_in_layout(w_in, *, tr=256):
    L = w_in.shape[0]
    perm = _gate_column_permutation()
    return pl.pallas_call(
        _w_in_layout_kernel,
        out_shape=jax.ShapeDtypeStruct((L, D_MODEL, P_COLS), BF16),
        grid=(L, D_MODEL // tr),
        in_specs=[pl.BlockSpec((1, tr, IN_COLS), lambda l, i: (l, i, 0)),
                  pl.BlockSpec(perm.shape, lambda l, i: (0, 0))],
        out_specs=pl.BlockSpec((1, tr, P_COLS), lambda l, i: (l, i, 0)),
        compiler_params=_params("parallel", "parallel"),
        name="w_in_layout",
    )(w_in, perm)


def _ffn_up_layout_kernel(w_ref, a_ref, b_ref):
    pad = jnp.zeros((w_ref.shape[1], D_FF_PAD - D_FF), BF16)
    a_ref[0] = jnp.concatenate([w_ref[0, :, :D_FF].astype(BF16), pad], axis=1)
    b_ref[0] = jnp.concatenate([w_ref[0, :, D_FF:].astype(BF16), pad], axis=1)


def _ffn_up_layout(w, *, tr=128):
    L, D, _ = w.shape
    out = jax.ShapeDtypeStruct((L, D, D_FF_PAD), BF16)
    ospec = pl.BlockSpec((1, tr, D_FF_PAD), lambda l, i: (l, i, 0))
    return pl.pallas_call(
        _ffn_up_layout_kernel,
        out_shape=[out, out],
        grid=(L, D // tr),
        in_specs=[pl.BlockSpec((1, tr, 2 * D_FF), lambda l, i: (l, i, 0))],
        out_specs=[ospec, ospec],
        compiler_params=_params("parallel", "parallel"),
        name="ffn_up_layout",
    )(w)


def _ffn_down_layout_kernel(w_ref, o_ref):
    real = pl.program_id(1) < D_FF // o_ref.shape[1]
    o_ref[0] = jnp.where(real, w_ref[0], 0.0).astype(BF16)


def _ffn_down_layout(w, *, tr=128):
    L, _, D = w.shape
    last = D_FF // tr - 1
    return pl.pallas_call(
        _ffn_down_layout_kernel,
        out_shape=jax.ShapeDtypeStruct((L, D_FF_PAD, D), BF16),
        grid=(L, D_FF_PAD // tr),
        in_specs=[pl.BlockSpec((1, tr, D), lambda l, i: (l, jnp.minimum(i, last), 0))],
        out_specs=pl.BlockSpec((1, tr, D), lambda l, i: (l, i, 0)),
        compiler_params=_params("parallel", "parallel"),
        name="ffn_down_layout",
    )(w)


def _gate_bias_layout(gate_b):
    L = gate_b.shape[0]
    gb = gate_b.reshape(L, NSA_KV_HEADS, NSA_GQA, 3).transpose(0, 1, 3, 2).reshape(L, 1, 3 * NSA_HEADS)
    return jnp.pad(gb, ((0, 0), (0, 0), (0, LANES - 3 * NSA_HEADS)))


def kernel(x, ffn1_w_up, ffn1_w_down, ln1_g, ln1_b, w_in, rw_mu, rw_w0, rw_w2, rw_a0, rw_a2, rw_g2, rw_k_k,
           rw_k_a, rw_r_k, rw_gn_g, rw_gn_b, pool_w, pool_b, pool_scale, nsa_cmp_pe_k, nsa_cmp_pe_v,
           nsa_cmp_k_w1, nsa_cmp_k_w2, nsa_cmp_v_w1, nsa_cmp_v_w2, nsa_gate_b, w_out, ln2_g, ln2_b,
           ffn2_w_up, ffn2_w_down, ln3_g, ln3_b):
    prm = _prepare(x.shape[1], ffn1_w_up, ffn1_w_down, ln1_g, ln1_b, w_in, rw_mu, rw_w0, rw_w2, rw_a0, rw_a2,
                   rw_g2, rw_k_k, rw_k_a, rw_r_k, rw_gn_g, rw_gn_b, pool_w, pool_b, pool_scale, nsa_cmp_pe_k,
                   nsa_cmp_pe_v, nsa_cmp_k_w1, nsa_cmp_k_w2, nsa_cmp_v_w1, nsa_cmp_v_w2, nsa_gate_b, w_out,
                   ln2_g, ln2_b, ffn2_w_up, ffn2_w_down, ln3_g, ln3_b)
    B, S, D = x.shape
    h = x.reshape(B * S, D)
    for l in range(w_in.shape[0]):
        h = _layer(h, prm, l, B, S)
    return h.reshape(B, S, D)


def _prepare(S, ffn1_w_up, ffn1_w_down, ln1_g, ln1_b, w_in, rw_mu, rw_w0, rw_w2, rw_a0, rw_a2, rw_g2, rw_k_k,
             rw_k_a, rw_r_k, rw_gn_g, rw_gn_b, pool_w, pool_b, pool_scale, nsa_cmp_pe_k, nsa_cmp_pe_v,
             nsa_cmp_k_w1, nsa_cmp_k_w2, nsa_cmp_v_w1, nsa_cmp_v_w2, nsa_gate_b, w_out, ln2_g, ln2_b,
             ffn2_w_up, ffn2_w_down, ln3_g, ln3_b):
    L = w_in.shape[0]

    up = _ffn_up_layout

    down = _ffn_down_layout

    row = lambda v: v[:, None, :]
    f1a, f1b = up(ffn1_w_up)
    f2a, f2b = up(ffn2_w_up)
    w_out_b = w_out.astype(BF16)
    gw = CMP_BLOCK * HEAD_DIM
    pad_w2 = lambda w: jnp.pad(w, ((0, 0), (0, 0), (0, LANES - HEAD_DIM))).astype(BF16)
    pe_rows = lambda pe: jnp.broadcast_to(pe.reshape(L, 1, gw), (L, SUBLANES, gw)).astype(BF16)
    ncp = S // CMP_STRIDE
    cos_t, sin_t = _rope_tables(jnp.arange(S), 4)
    cos_c, sin_c = _rope_tables(jnp.arange(ncp) * CMP_STRIDE + (CMP_BLOCK - 1), 2)
    return dict(
        ffn1=(f1a, f1b, down(ffn1_w_down), row(ln1_g), row(ln1_b)),
        ffn2=(f2a, f2b, down(ffn2_w_down), row(ln3_g), row(ln3_b)),
        w_in=_w_in_layout(w_in),
        w_out=(_parity_major(w_out_b[:, :D_RWKV], 1), w_out_b[:, D_RWKV:D_RWKV + D_POOL],
               w_out_b[:, D_RWKV + D_POOL:]),
        ln2=(row(ln2_g), row(ln2_b)),
        rw=dict(mu=row(rw_mu), w0=row(rw_w0), w2=rw_w2.astype(BF16), a0=row(rw_a0), a2=rw_a2.astype(BF16),
                g2=_parity_major(rw_g2, -1).astype(BF16), k_k=row(rw_k_k), k_a=row(rw_k_a),
                r_k=rw_r_k.reshape(L, 1, D_RWKV), gn_g=row(_parity_major(rw_gn_g, -1)),
                gn_b=row(_parity_major(rw_gn_b, -1)), ones=_block_ones(D_RWKV)),
        cmp=dict(k_w1=nsa_cmp_k_w1.reshape(L, gw, CMP_HIDDEN).astype(BF16), k_w2=pad_w2(nsa_cmp_k_w2),
                 k_pe=pe_rows(nsa_cmp_pe_k),
                 v_w1=nsa_cmp_v_w1.reshape(L, gw, CMP_HIDDEN).astype(BF16), v_w2=nsa_cmp_v_w2.transpose(0, 2, 1).astype(BF16),
                 v_pe=pe_rows(nsa_cmp_pe_v)),
        gate_b=_gate_bias_layout(nsa_gate_b),
        pool=(pool_w.astype(BF16), row(pool_b), row(pool_scale)),
        rope=(cos_t, sin_t), rope_cmp=(cos_c, sin_c), overlap=_overlap_matrix(S, ncp))


def _mixers(p_all, prm, l):
    r, w, k, v, kk, be, bo, g = _rw_prep(p_all, prm["rw"], l)
    y_rw = _rw_scan(r, w, k, v, kk, be, bo, g, prm["rw"], l)
    y_pool = _pool_mix(p_all, *prm["pool"], l)
    q, kc, vc, ks, vs, kw, vw, gates = _nsa_prep(p_all, *prm["rope"], prm["gate_b"], l)
    k_cmp, v_cmp = _nsa_compress(kc, vc, prm["cmp"], *prm["rope_cmp"], l)
    y_nsa = _nsa_attention(q, k_cmp, v_cmp, ks, vs, kw, vw, gates, prm["overlap"])
    return y_rw, y_pool, y_nsa


def _layer(h, prm, l, B, S):
    T = B * S
    h = _ffn_ln(h, *prm["ffn1"], l)
    p_all = _in_proj(h, prm["w_in"], l).reshape(B, S, P_COLS)
    y_rw, y_pool, y_nsa = _mixers(p_all, prm, l)
    h = _out_proj_ln(h, y_rw.reshape(T, D_RWKV), y_pool.reshape(T, D_POOL), y_nsa.reshape(T, D_NSA),
                     prm["w_out"], *prm["ln2"], l)
    return _ffn_ln(h, *prm["ffn2"], l)
```

```python
import functools

import numpy as np
import jax
import jax.numpy as jnp
from jax import lax
from jax.experimental import pallas as pl
from jax.experimental.pallas import tpu as pltpu

F32 = jnp.float32
BF16 = jnp.bfloat16

D_MODEL = 2048
DEPTH = 4
HEAD_DIM = 64
D_RWKV = 768
D_POOL = 512
D_NSA = 768
RW_HEADS = 12
RW_DECAY_LORA = 64
RW_A_LORA = 64
RW_GATE_LORA = 128
RW_GN_EPS = 64e-5
RW_COLS = 3 * D_RWKV + RW_DECAY_LORA + RW_A_LORA + RW_GATE_LORA
POOL_WINDOWS = (2, 4, 8, 16)
POOL_GROUP = 128
NSA_HEADS = 12
NSA_KV_HEADS = 3
NSA_GQA = 4
NSA_KV = 192
NSA_COLS = D_NSA + 6 * NSA_KV + 3 * NSA_HEADS
CMP_BLOCK = 32
CMP_STRIDE = 16
CMP_HIDDEN = 256
SEL_BLOCK = 64
SEL_TOPK = 16
FORCE_SCORE = 1e9
WINDOW = 512
ROPE_THETA = 500000.0
ROPE_DIM = 16
D_FF = 5504
IN_COLS = RW_COLS + D_POOL + NSA_COLS
ALPHA = (2 * DEPTH) ** 0.25
LN_EPS = 1e-5

LANES = 128
SUBLANES = 8
VMEM_LIMIT = 56 * 1024 * 1024

KV_SLOT = 2 * LANES
NSA_PAD = 2560
P_RW_OFF = 0
P_NSA_OFF = RW_COLS
P_POOL_OFF = RW_COLS + NSA_PAD
P_COLS = RW_COLS + NSA_PAD + D_POOL
NSA_GATE_OFF = D_NSA + 6 * KV_SLOT
D_FF_PAD = 5632

NEG = -1e30


def _params(*sem):
    return pltpu.CompilerParams(dimension_semantics=sem, vmem_limit_bytes=VMEM_LIMIT)


def _layer_norm(z, g, b):
    mu = jnp.mean(z, axis=-1, keepdims=True)
    zc = z - mu
    var = jnp.mean(zc * zc, axis=-1, keepdims=True)
    return zc * lax.rsqrt(var + LN_EPS) * g + b


def _dot(a, b):
    return jnp.dot(a, b, preferred_element_type=F32)


def _dot_nt(a, b):
    return lax.dot_general(a, b, (((1,), (1,)), ((), ())), preferred_element_type=F32)


def _dot_split(x, w):
    hi = x.astype(BF16)
    lo = (x - hi.astype(F32)).astype(BF16)
    return _dot(hi, w) + _dot(lo, w)


FFN_LN_ROWS = 256
FFN_MM_ROWS = 512


def _ffn_kernel(x_ref, wa_ref, wb_ref, wd_ref, g_ref, b_ref, o_ref, xb_ref):
    k = pl.program_id(1)

    @pl.when(k == 0)
    def _():
        xb_ref[...] = x_ref[...].astype(BF16)
        o_ref[...] = jnp.zeros_like(o_ref)

    for c in range(o_ref.shape[0] // FFN_MM_ROWS):
        rows = slice(c * FFN_MM_ROWS, (c + 1) * FFN_MM_ROWS)
        xb = xb_ref[rows, :]
        a = _dot(xb, wa_ref[...])
        b = _dot(xb, wb_ref[...])
        h = (a / (1.0 + jnp.exp(-a))) * b
        o_ref[rows, :] += _dot(h.astype(BF16), wd_ref[...])

    @pl.when(k == pl.num_programs(1) - 1)
    def _():
        for c in range(o_ref.shape[0] // FFN_LN_ROWS):
            rows = slice(c * FFN_LN_ROWS, (c + 1) * FFN_LN_ROWS)
            z = ALPHA * x_ref[rows, :] + 0.5 * o_ref[rows, :]
            o_ref[rows, :] = _layer_norm(z, g_ref[...], b_ref[...])


def _ffn_ln(x, wa, wb, wd, g, b, layer, *, tm=1024, tf=512):
    T, D = x.shape
    fp = wa.shape[-1]
    tm = min(tm, T)
    return pl.pallas_call(
        _ffn_kernel,
        out_shape=jax.ShapeDtypeStruct((T, D), F32),
        grid=(T // tm, fp // tf),
        in_specs=[
            pl.BlockSpec((tm, D), lambda i, k: (i, 0)),
            pl.BlockSpec((None, D, tf), lambda i, k: (layer, 0, k)),
            pl.BlockSpec((None, D, tf), lambda i, k: (layer, 0, k)),
            pl.BlockSpec((None, tf, D), lambda i, k: (layer, k, 0)),
            pl.BlockSpec((None, 1, D), lambda i, k: (layer, 0, 0)),
            pl.BlockSpec((None, 1, D), lambda i, k: (layer, 0, 0)),
        ],
        out_specs=pl.BlockSpec((tm, D), lambda i, k: (i, 0)),
        scratch_shapes=[pltpu.VMEM((tm, D), BF16)],
        compiler_params=_params("parallel", "arbitrary"),
        name="ffn_ln",
    )(x, wa, wb, wd, g, b)


def _inproj_kernel(x_ref, w_ref, o_ref, xb_ref):
    @pl.when(pl.program_id(1) == 0)
    def _():
        xb_ref[...] = x_ref[...].astype(BF16)

    o_ref[...] = _dot(xb_ref[...], w_ref[...])


def _in_proj(x, w, layer, *, tm=1024, tn=P_COLS // 4):
    T, D = x.shape
    n = w.shape[-1]
    tm = min(tm, T)
    return pl.pallas_call(
        _inproj_kernel,
        out_shape=jax.ShapeDtypeStruct((T, n), F32),
        grid=(T // tm, n // tn),
        in_specs=[
            pl.BlockSpec((tm, D), lambda i, j: (i, 0)),
            pl.BlockSpec((None, D, tn), lambda i, j: (layer, 0, j)),
        ],
        out_specs=pl.BlockSpec((tm, tn), lambda i, j: (i, j)),
        scratch_shapes=[pltpu.VMEM((tm, D), BF16)],
        compiler_params=_params("parallel", "arbitrary"),
        name="in_proj",
    )(x, w)


def _outproj_kernel(x_ref, yr_ref, yp_ref, yn_ref, wr_ref, wp_ref, wn_ref, g_ref, b_ref, o_ref):
    y = _dot(yr_ref[...], wr_ref[...]) + _dot(yp_ref[...], wp_ref[...]) + _dot(yn_ref[...], wn_ref[...])
    o_ref[...] = _layer_norm(ALPHA * x_ref[...] + y, g_ref[...], b_ref[...])


def _out_proj_ln(x, y_rw, y_pool, y_nsa, w_out, g, b, layer, *, tm=512):
    T, D = x.shape
    return pl.pallas_call(
        _outproj_kernel,
        out_shape=jax.ShapeDtypeStruct((T, D), F32),
        grid=(T // tm,),
        in_specs=[
            pl.BlockSpec((tm, D), lambda i: (i, 0)),
            pl.BlockSpec((tm, D_RWKV), lambda i: (i, 0)),
            pl.BlockSpec((tm, D_POOL), lambda i: (i, 0)),
            pl.BlockSpec((tm, D_NSA), lambda i: (i, 0)),
            pl.BlockSpec((None, D_RWKV, D), lambda i: (layer, 0, 0)),
            pl.BlockSpec((None, D_POOL, D), lambda i: (layer, 0, 0)),
            pl.BlockSpec((None, D_NSA, D), lambda i: (layer, 0, 0)),
            pl.BlockSpec((None, 1, D), lambda i: (layer, 0, 0)),
            pl.BlockSpec((None, 1, D), lambda i: (layer, 0, 0)),
        ],
        out_specs=pl.BlockSpec((tm, D), lambda i: (i, 0)),
        compiler_params=_params("parallel"),
        name="out_proj_ln",
    )(x, y_rw, y_pool, y_nsa, w_out[0], w_out[1], w_out[2], g, b)


POOL_HALO = 16


def _pool_kernel(p_ref, halo_ref, w_ref, b_ref, sc_ref, o_ref, xs_ref, *, ts):
    s = pl.program_id(1)
    x = p_ref[0]
    halo = jnp.where(s > 0, halo_ref[0], 0.0)
    xs_ref[0:POOL_HALO, :] = halo
    xs_ref[POOL_HALO:POOL_HALO + ts, :] = x
    t1 = (s * ts + 1 + lax.broadcasted_iota(jnp.int32, (ts, 1), 0)).astype(F32)
    for gi, win in enumerate(POOL_WINDOWS):
        c0 = gi * POOL_GROUP
        acc = x[:, c0:c0 + POOL_GROUP]
        for j in range(1, win):
            acc = acc + xs_ref[POOL_HALO - j:POOL_HALO - j + ts, c0:c0 + POOL_GROUP]
        pooled = acc / jnp.minimum(t1, float(win)) - x[:, c0:c0 + POOL_GROUP]
        z = _dot(pooled.astype(BF16), w_ref[gi]) + b_ref[:, c0:c0 + POOL_GROUP]
        o_ref[0, :, c0:c0 + POOL_GROUP] = (z * sc_ref[:, c0:c0 + POOL_GROUP]).astype(o_ref.dtype)


def _pool_mix(p_all, pool_w, pool_b, pool_scale, layer, *, ts=512):
    B, S, _ = p_all.shape
    cb = P_POOL_OFF // D_POOL
    hb = ts // POOL_HALO
    return pl.pallas_call(
        functools.partial(_pool_kernel, ts=ts),
        out_shape=jax.ShapeDtypeStruct((B, S, D_POOL), BF16),
        grid=(B, S // ts),
        in_specs=[
            pl.BlockSpec((1, ts, D_POOL), lambda b, s: (b, s, cb)),
            pl.BlockSpec((1, POOL_HALO, D_POOL), lambda b, s: (b, jnp.maximum(s * hb - 1, 0), cb)),
            pl.BlockSpec((None, 4, POOL_GROUP, POOL_GROUP), lambda b, s: (layer, 0, 0, 0)),
            pl.BlockSpec((None, 1, D_POOL), lambda b, s: (layer, 0, 0)),
            pl.BlockSpec((None, 1, D_POOL), lambda b, s: (layer, 0, 0)),
        ],
        out_specs=pl.BlockSpec((1, ts, D_POOL), lambda b, s: (b, s, 0)),
        scratch_shapes=[pltpu.VMEM((ts + POOL_HALO, D_POOL), F32)],
        compiler_params=_params("parallel", "parallel"),
        name="pool_mix",
    )(p_all, p_all, pool_w, pool_b, pool_scale)


def _softplus(z):
    return jnp.maximum(z, 0.0) + jnp.log1p(jnp.exp(-jnp.abs(z)))


def _sigmoid(z):
    return 1.0 / (1.0 + jnp.exp(-z))


def _lanes_to_parity_major(x):
    n_pairs = x.shape[1] // LANES
    low = lax.broadcasted_iota(jnp.int32, (1, LANES), 1) < HEAD_DIM

    def chunk(e, to_low):
        pair, parity = e % n_pairs, e // n_pairs
        src = x[:, pair * LANES:(pair + 1) * LANES]
        return src if (parity == 0) == to_low else pltpu.roll(src, HEAD_DIM, 1)

    return jnp.concatenate([jnp.where(low, chunk(2 * d, True), chunk(2 * d + 1, False))
                            for d in range(n_pairs)], axis=1)


def _rw_prep_kernel(p_ref, prev_ref, mu_ref, w0_ref, w2_ref, a0_ref, a2_ref, g2_ref, kk_ref, ka_ref,
                    rk_ref, ones_ref,
                    r_o, w_o, k_o, v_o, kk_o, be_o, bo_o, g_o, *, ts):
    s = pl.program_id(1)
    x = p_ref[0]
    last = jnp.where(s > 0, prev_ref[0][SUBLANES - 1:SUBLANES, :], 0.0)
    row = lax.broadcasted_iota(jnp.int32, (ts, 1), 0)
    shifted = jnp.where(row == 0, last, pltpu.roll(x, 1, 0))
    xm = x + (shifted - x) * mu_ref[...]
    c = D_RWKV
    r = xm[:, 0:c]
    k = xm[:, c:2 * c]
    v = xm[:, 2 * c:3 * c]
    lora = xm[:, 3 * c:3 * c + LANES]
    wl = lora[:, :RW_DECAY_LORA]
    al = lora[:, RW_DECAY_LORA:]
    gl = xm[:, 3 * c + LANES:]
    w = -_softplus(-(w0_ref[...] + _dot(jnp.tanh(wl).astype(BF16), w2_ref[...]))) - 0.5
    decay = jnp.exp(-jnp.exp(w))
    a = _sigmoid(a0_ref[...] + _dot(al.astype(BF16), a2_ref[...]))
    g = _dot(_sigmoid(gl).astype(BF16), g2_ref[...])
    ones = ones_ref[...]
    kk = k * kk_ref[...]
    nrm = jnp.sqrt(_dot_split(kk * kk, ones))
    kk = kk / jnp.maximum(nrm, 1e-12)
    k_mod = k * (1.0 + (a - 1.0) * ka_ref[...])
    bonus = _dot_split(r * k_mod * rk_ref[...], ones) * v
    r_o[0] = r
    w_o[0] = decay
    k_o[0] = k_mod
    v_o[0] = v
    kk_o[0] = kk
    be_o[0] = kk * a
    bo_o[0] = _lanes_to_parity_major(bonus).astype(bo_o.dtype)
    g_o[0] = g.astype(g_o.dtype)


def _rw_prep(p_all, prm, layer, *, ts=512):
    B, S, _ = p_all.shape
    cb = P_RW_OFF // RW_COLS
    hb = ts // SUBLANES
    vec = lambda n: pl.BlockSpec((None, 1, n), lambda b, s: (layer, 0, 0))
    mat = lambda m, n: pl.BlockSpec((None, m, n), lambda b, s: (layer, 0, 0))
    out = jax.ShapeDtypeStruct((B, S, D_RWKV), F32)
    ospec = pl.BlockSpec((1, ts, D_RWKV), lambda b, s: (b, s, 0))
    return pl.pallas_call(
        functools.partial(_rw_prep_kernel, ts=ts),
        out_shape=[out] * 6 + [jax.ShapeDtypeStruct((B, S, D_RWKV), BF16)] * 2,
        grid=(B, S // ts),
        in_specs=[
            pl.BlockSpec((1, ts, RW_COLS), lambda b, s: (b, s, cb)),
            pl.BlockSpec((1, SUBLANES, RW_COLS), lambda b, s: (b, jnp.maximum(s * hb - 1, 0), cb)),
            vec(RW_COLS), vec(D_RWKV), mat(RW_DECAY_LORA, D_RWKV), vec(D_RWKV), mat(RW_A_LORA, D_RWKV),
            mat(RW_GATE_LORA, D_RWKV), vec(D_RWKV), vec(D_RWKV), vec(D_RWKV),
            pl.BlockSpec((D_RWKV, D_RWKV), lambda b, s: (0, 0)),
        ],
        out_specs=[ospec] * 8,
        compiler_params=_params("parallel", "parallel"),
        name="rw_prep",
    )(p_all, p_all, prm["mu"], prm["w0"], prm["w2"], prm["a0"], prm["a2"], prm["g2"], prm["k_k"],
      prm["k_a"], prm["r_k"], prm["ones"])


RW_PAIRS = D_RWKV // LANES
RW_STEPS = 16


def _rw_scan_kernel(r_ref, w_ref, k_ref, v_ref, kk_ref, be_ref, bo_ref, g_ref, gng_ref, gnb_ref,
                    pair_ref, half_ref, o_ref, st_ref, vc_ref, sr_ref, y_ref, *, ts, nb):
    s = pl.program_id(1)

    @pl.when(s == 0)
    def _():
        st_ref[...] = jnp.zeros_like(st_ref)

    side, stack = 2, RW_PAIRS // 2
    rows, width = stack * HEAD_DIM, side * LANES
    lane = lax.broadcasted_iota(jnp.int32, (rows, width), 1)
    row = lax.broadcasted_iota(jnp.int32, (rows, width), 0)
    eye = jnp.where(lane % HEAD_DIM == row % HEAD_DIM, 1.0, 0.0)
    ones_blk = pair_ref[...]

    def spread(x8, j):
        blocks = []
        for st in range(stack):
            lanes = jnp.concatenate([x8[j:j + 1, (sd * stack + st) * LANES:(sd * stack + st + 1) * LANES]
                                     for sd in range(side)], axis=1)
            blocks.append(jnp.broadcast_to(lanes, (HEAD_DIM, width)))
        return jnp.concatenate(blocks, axis=0)

    def step(i, carry):
        base = pl.multiple_of(i * RW_STEPS, RW_STEPS)
        tiles = [[ref[b, pl.ds(base, RW_STEPS), :] for ref in (kk_ref, w_ref, be_ref, k_ref, r_ref, v_ref)]
                 for b in range(nb)]
        for b in range(nb):
            lhs = jnp.concatenate([eye * spread(tiles[b][5], j) for j in range(RW_STEPS)], axis=0)
            vc_ref[b] = _dot(lhs.astype(BF16), ones_blk)
        sts = [st_ref[b] for b in range(nb)]
        for j in range(RW_STEPS):
            for b in range(nb):
                kk8, w8, be8, k8, r8, _ = tiles[b]
                st = sts[b]
                sa = _dot((st * spread(kk8, j)).astype(BF16), ones_blk)
                st = (st * spread(w8, j) - sa * spread(be8, j)
                      + vc_ref[b, j * rows:(j + 1) * rows, :] * spread(k8, j))
                sr = (st * spread(r8, j)).astype(BF16)
                for sd in range(side):
                    lo = (j * side + sd) * rows
                    sr_ref[b, lo:lo + rows, :] = sr[:, sd * LANES:(sd + 1) * LANES]
                sts[b] = st
        for b in range(nb):
            st_ref[b] = sts[b]
            yt = _dot_nt(half_ref[...], sr_ref[b])
            per = side * rows
            y_ref[b, pl.ds(base, RW_STEPS), :] = jnp.concatenate(
                [jnp.concatenate([yt[0:1, j * per:(j + 1) * per], yt[1:2, j * per:(j + 1) * per]], axis=1)
                 for j in range(RW_STEPS)], axis=0)
        return carry

    lax.fori_loop(0, ts // RW_STEPS, step, 0)

    def head_sums(x):
        n = x.shape[0]
        hi = x.astype(BF16)
        both = jnp.concatenate([hi, (x - hi.astype(F32)).astype(BF16)], axis=0)
        out = jnp.concatenate([_dot(both[:, c * width:(c + 1) * width], ones_blk)
                               for c in range(D_RWKV // width)], axis=1)
        return out[:n] + out[n:]

    y = y_ref[...].reshape(nb * ts, D_RWKV)
    mean = head_sums(y) * (1.0 / HEAD_DIM)
    yc = y - mean
    var = head_sums(yc * yc) * (1.0 / HEAD_DIM)
    yn = yc * lax.rsqrt(var + RW_GN_EPS) * gng_ref[...] + gnb_ref[...]
    out = (yn + bo_ref[...].reshape(nb * ts, D_RWKV)) * g_ref[...].reshape(nb * ts, D_RWKV)
    o_ref[...] = out.reshape(nb, ts, D_RWKV).astype(o_ref.dtype)


def _rw_scan(r, w, k, v, kk, be, bo, g, prm, layer, *, ts=128):
    B, S, _ = r.shape
    nb = 4 if B % 4 == 0 else (2 if B % 2 == 0 else 1)
    rows, width = RW_PAIRS // 2 * HEAD_DIM, 2 * LANES
    seq = pl.BlockSpec((nb, ts, D_RWKV), lambda b, s: (b, s, 0))
    vec = pl.BlockSpec((None, 1, D_RWKV), lambda b, s: (layer, 0, 0))
    return pl.pallas_call(
        functools.partial(_rw_scan_kernel, ts=ts, nb=nb),
        out_shape=jax.ShapeDtypeStruct((B, S, D_RWKV), BF16),
        grid=(B // nb, S // ts),
        in_specs=[seq] * 8 + [vec, vec, pl.BlockSpec((width, width), lambda b, s: (0, 0)),
                              pl.BlockSpec((SUBLANES, LANES), lambda b, s: (0, 0))],
        out_specs=seq,
        scratch_shapes=[pltpu.VMEM((nb, rows, width), F32), pltpu.VMEM((nb, RW_STEPS * rows, width), F32),
                        pltpu.VMEM((nb, RW_STEPS * RW_PAIRS * HEAD_DIM, LANES), BF16),
                        pltpu.VMEM((nb, ts, D_RWKV), F32)],
        compiler_params=_params("parallel", "arbitrary"),
        name="rw_scan",
    )(r, w, k, v, kk, be, bo, g, prm["gn_g"], prm["gn_b"], prm["ones"][:width, :width],
      prm["ones"][:SUBLANES * HEAD_DIM:HEAD_DIM, :LANES])


def _rope(x, cos, sin):
    half = ROPE_DIM // 2
    w = x.shape[-1]
    d = lax.broadcasted_iota(jnp.int32, x.shape, 1) % HEAD_DIM
    partner = jnp.where(d < half, pltpu.roll(x, w - half, 1), pltpu.roll(x, half, 1))
    return x * cos + partner * sin


def _nsa_prep_kernel(p_ref, cos_ref, sin_ref, gb_ref, q_o, kc_o, vc_o, ks_o, vs_o, kw_o, vw_o, gt_o):
    cos = cos_ref[...]
    sin = sin_ref[...]
    for hq in range(NSA_HEADS // 4):
        x = p_ref[0, :, hq * KV_SLOT:(hq + 1) * KV_SLOT]
        qr = _rope(x, cos, sin) * (HEAD_DIM ** -0.5)
        for j in range(4):
            q_o[0, 4 * hq + j] = qr[:, j * HEAD_DIM:(j + 1) * HEAD_DIM].astype(BF16)
    for i, ref in enumerate((kc_o, vc_o, ks_o, vs_o, kw_o, vw_o)):
        part = p_ref[0, :, D_NSA + i * KV_SLOT:D_NSA + (i + 1) * KV_SLOT]
        if i in (2, 4):
            part = _rope(part, cos, sin)
        if i in (3, 5):
            part = part.T
            for h in range(NSA_KV_HEADS):
                ref[0, h] = part[h * HEAD_DIM:(h + 1) * HEAD_DIM, :].astype(BF16)
        else:
            for h in range(NSA_KV_HEADS):
                ref[0, h] = part[:, h * HEAD_DIM:(h + 1) * HEAD_DIM].astype(BF16)
    gts = _sigmoid(p_ref[0, :, NSA_GATE_OFF:NSA_GATE_OFF + LANES] + gb_ref[...]).T
    for h in range(NSA_KV_HEADS):
        gt_o[0, h] = gts[h * 12:(h + 1) * 12, :]


def _nsa_prep(p_all, cos, sin, gate_b, layer, *, ts=512):
    B, S, _ = p_all.shape
    H = NSA_KV_HEADS
    k_shape = jax.ShapeDtypeStruct((B, H, S, HEAD_DIM), BF16)
    k_spec = pl.BlockSpec((1, H, ts, HEAD_DIM), lambda b, s: (b, 0, s, 0))
    vt_shape = jax.ShapeDtypeStruct((B, H, HEAD_DIM, S), BF16)
    vt_spec = pl.BlockSpec((1, H, HEAD_DIM, ts), lambda b, s: (b, 0, 0, s))
    return pl.pallas_call(
        _nsa_prep_kernel,
        out_shape=[jax.ShapeDtypeStruct((B, NSA_HEADS, S, HEAD_DIM), BF16),
                   k_shape, k_shape, k_shape, vt_shape, k_shape, vt_shape,
                   jax.ShapeDtypeStruct((B, H, 12, S), F32)],
        grid=(B, S // ts),
        in_specs=[
            pl.BlockSpec((1, ts, NSA_PAD), lambda b, s: (b, s, P_NSA_OFF // NSA_PAD)),
            pl.BlockSpec((ts, KV_SLOT), lambda b, s: (s, 0)),
            pl.BlockSpec((ts, KV_SLOT), lambda b, s: (s, 0)),
            pl.BlockSpec((None, 1, LANES), lambda b, s: (layer, 0, 0)),
        ],
        out_specs=[pl.BlockSpec((1, NSA_HEADS, ts, HEAD_DIM), lambda b, s: (b, 0, s, 0)),
                   k_spec, k_spec, k_spec, vt_spec, k_spec, vt_spec,
                   pl.BlockSpec((1, H, 12, ts), lambda b, s: (b, 0, 0, s))],
        compiler_params=_params("parallel", "parallel"),
        name="nsa_prep",
    )(p_all, cos, sin, gate_b)


def _gelu_tanh(x):
    return 0.5 * x * (1.0 + jnp.tanh(float(np.sqrt(2.0 / np.pi)) * (x + 0.044715 * (x * x * x))))


def _nsa_cmp_kernel(kc_ref, vc_ref, kw1_ref, kw2_ref, kpe_ref, vw1_ref, vw2_ref, vpe_ref, cos_ref, sin_ref,
                    k_o, v_o):
    def hidden(g, w1_ref, pe_ref):
        half = CMP_STRIDE * HEAD_DIM
        first = _dot(g, w1_ref[:half, :])
        second = _dot(g, w1_ref[half:, :])
        n = first.shape[0]
        bias = _dot(pe_ref[...], w1_ref[...])[0:1]
        return _gelu_tanh(first + pltpu.roll(second, n - 1, 0) + bias).astype(BF16)

    k = _dot(hidden(kc_ref[0, 0], kw1_ref, kpe_ref), kw2_ref[...])
    k_o[0, 0] = _rope(k, cos_ref[...], sin_ref[...])[:, :HEAD_DIM].astype(BF16)
    v_o[0, 0] = _dot_nt(vw2_ref[...], hidden(vc_ref[0, 0], vw1_ref, vpe_ref)).astype(BF16)


def _nsa_compress(kc, vc, prm, cos_c, sin_c, layer):
    B, H, S, _ = kc.shape
    ng = S // CMP_STRIDE
    gw = CMP_STRIDE * HEAD_DIM
    g_k = kc.reshape(B, H, ng, gw)
    g_v = vc.reshape(B, H, ng, gw)
    gspec = pl.BlockSpec((1, 1, ng, gw), lambda b, h: (b, h, 0, 0))
    w1 = pl.BlockSpec((None, 2 * gw, CMP_HIDDEN), lambda b, h: (layer, 0, 0))
    w2 = pl.BlockSpec((None, CMP_HIDDEN, LANES), lambda b, h: (layer, 0, 0))
    pe = pl.BlockSpec((None, SUBLANES, 2 * gw), lambda b, h: (layer, 0, 0))
    tab = pl.BlockSpec((ng, LANES), lambda b, h: (0, 0))
    return pl.pallas_call(
        _nsa_cmp_kernel,
        out_shape=[jax.ShapeDtypeStruct((B, H, ng, HEAD_DIM), BF16), jax.ShapeDtypeStruct((B, H, HEAD_DIM, ng), BF16)],
        grid=(B, H),
        in_specs=[gspec, gspec, w1, w2, pe, w1,
                  pl.BlockSpec((None, HEAD_DIM, CMP_HIDDEN), lambda b, h: (layer, 0, 0)), pe, tab, tab],
        out_specs=[pl.BlockSpec((1, 1, ng, HEAD_DIM), lambda b, h: (b, h, 0, 0)),
                   pl.BlockSpec((1, 1, HEAD_DIM, ng), lambda b, h: (b, h, 0, 0))],
        compiler_params=_params("parallel", "parallel"),
        name="nsa_compress",
    )(g_k, g_v, prm["k_w1"], prm["k_w2"], prm["k_pe"], prm["v_w1"], prm["v_w2"], prm["v_pe"], cos_c, sin_c)


def _nsa_attn_kernel(q_ref, kcmp_ref, vcmp_ref, ks_ref, vs_ref, kw_ref, vw_ref, gt_ref, ov_ref, o_ref,
                     m_ref, l_ref, acc_ref, sb_ref, *, tq):
    G = NSA_GQA
    qi = pl.program_id(1)
    for h in range(NSA_KV_HEADS):
        one = lambda ref: ref.at[:, h:h + 1]
        _nsa_head(q_ref.at[:, h * G:(h + 1) * G], one(kcmp_ref), one(vcmp_ref), one(ks_ref), one(vs_ref),
                  one(kw_ref), one(vw_ref), one(gt_ref), ov_ref,
                  o_ref.at[:, :, h * G * HEAD_DIM:(h + 1) * G * HEAD_DIM],
                  m_ref, l_ref, acc_ref, sb_ref, tq=tq, qi=qi)


def _nsa_head(q_ref, kcmp_ref, vcmp_ref, ks_ref, vs_ref, kw_ref, vw_ref, gt_ref, ov_ref, o_ref,
              m_ref, l_ref, acc_ref, sb_ref, *, tq, qi):
    G = NSA_GQA
    tk = tq
    q0 = qi * tq
    q_all = q_ref[0].reshape(G * tq, HEAD_DIM)
    pos = q0 + lax.broadcasted_iota(jnp.int32, (1, tq), 1)
    tiny = jnp.finfo(F32).tiny
    heads = lambda x: jnp.tile(x, (1, G))
    split = lambda x: [x[:, g * tq:(g + 1) * tq] for g in range(G)]

    ncp = kcmp_ref.shape[2]
    n_idx = lax.broadcasted_iota(jnp.int32, (ncp, 1), 0)
    cbias = jnp.where((n_idx * CMP_STRIDE + (CMP_BLOCK - 1)) <= pos, 0.0, NEG)
    any_cmp = jnp.where(pos >= CMP_BLOCK - 1, 1.0, 0.0)
    s = _dot_nt(kcmp_ref[0, 0], q_all) + heads(cbias)
    e = jnp.exp(s - jnp.max(s, axis=0, keepdims=True))
    p = e * (heads(any_cmp) / jnp.maximum(jnp.sum(e, axis=0, keepdims=True), tiny))
    o_cmp = split(_dot(vcmp_ref[0, 0], p.astype(BF16)))
    ps = split(p)
    psum = (ps[0] + ps[1]) + (ps[2] + ps[3])
    p_hi = psum.astype(BF16)
    p_lo = (psum - p_hi.astype(F32)).astype(BF16)
    imp2 = _dot(ov_ref[...], jnp.concatenate([p_hi, p_lo], axis=1))
    imp = imp2[:, :tq] + imp2[:, tq:]

    wk = min(WINDOW + tq, ks_ref.shape[2])
    w0 = pl.multiple_of(jnp.maximum(q0 + tq - wk, 0), tk)
    wpos = w0 + lax.broadcasted_iota(jnp.int32, (wk, 1), 0)
    wbias = jnp.where((wpos <= pos) & (wpos > pos - WINDOW), 0.0, NEG)
    s = _dot_nt(kw_ref[0, 0, pl.ds(w0, wk), :], q_all) + heads(wbias)
    e = jnp.exp(s - jnp.max(s, axis=0, keepdims=True))
    pv = _dot(vw_ref[0, 0, :, pl.ds(w0, wk)], e.astype(BF16))
    o_win = split(pv / jnp.maximum(jnp.sum(e, axis=0, keepdims=True), tiny))

    n_sel = ov_ref.shape[0]
    blk = lax.broadcasted_iota(jnp.int32, (n_sel, 1), 0)
    cur = pos // SEL_BLOCK
    forced = (blk == 0) | (blk == cur) | (blk == cur - 1)
    valid = blk <= cur
    score = jnp.where(valid, jnp.where(forced, FORCE_SCORE, imp), -jnp.inf)
    groups = [score[g * SUBLANES:(g + 1) * SUBLANES, :] for g in range(n_sel // SUBLANES)]
    ranks = [jnp.zeros((SUBLANES, tq), F32) for _ in groups]
    sub = lax.broadcasted_iota(jnp.int32, (SUBLANES, 1), 0)
    for i in range(n_sel):
        ci = jnp.broadcast_to(score[i:i + 1, :], (SUBLANES, tq))
        for g, sg in enumerate(groups):
            if g > i // SUBLANES:
                beats = ci >= sg
            elif g < i // SUBLANES:
                beats = ci > sg
            else:
                beats = (ci > sg) | ((ci == sg) & (sub > i % SUBLANES))
            ranks[g] = ranks[g] + jnp.where(beats, 1.0, 0.0)
    rank = jnp.concatenate(ranks, axis=0)
    chosen = (rank < float(SEL_TOPK)) & valid
    sel_bias = jnp.where(chosen, 0.0, NEG)
    for j in range(n_sel):
        sb_ref[j] = jnp.broadcast_to(sel_bias[j:j + 1, :], (SUBLANES, tq))

    kcol = lax.broadcasted_iota(jnp.int32, (tk, 1), 0)

    def sel_update(kt, n, extra=None):
        bpt = n // SEL_BLOCK
        k0 = pl.multiple_of(kt * tk, tk)
        s = _dot_nt(ks_ref[0, 0, pl.ds(k0, n), :], q_all)
        b0 = kt * (tk // SEL_BLOCK)
        bias = jnp.concatenate([jnp.tile(sb_ref[b0 + j], (SEL_BLOCK // SUBLANES, 1)) for j in range(bpt)],
                               axis=0)
        if extra is not None:
            bias = bias + extra
        vt = vs_ref[0, 0, :, pl.ds(k0, n)]
        s = s + heads(bias)
        m_prev = m_ref[...]
        m_new = jnp.maximum(m_prev, jnp.max(s, axis=0, keepdims=True))
        p = jnp.exp(s - m_new)
        corr = jnp.exp(m_prev - m_new)
        l_ref[...] = corr * l_ref[...] + jnp.sum(p, axis=0, keepdims=True)
        acc_ref[...] = corr * acc_ref[...] + _dot(vt, p.astype(BF16))
        m_ref[...] = m_new

    m_ref[...] = jnp.full_like(m_ref, NEG)
    l_ref[...] = jnp.zeros_like(l_ref)
    acc_ref[...] = jnp.zeros_like(acc_ref)

    def sel_body(i, carry):
        sel_update(4 * i, 4 * tk)
        return carry

    lax.fori_loop(0, qi // 4, sel_body, 0)

    @pl.when(qi % 4 >= 2)
    def _():
        sel_update((qi // 4) * 4, 2 * tk)

    @pl.when(qi % 2 == 1)
    def _():
        sel_update(qi - 1, tk)

    causal = jnp.where(q0 + kcol <= pos, 0.0, NEG)
    sel_update(qi, tk, causal)
    o_sel = split(acc_ref[...] / jnp.maximum(l_ref[...], tiny))

    gt = gt_ref[0, 0]
    outs = [gt[g:g + 1] * o_cmp[g] + gt[G + g:G + g + 1] * o_sel[g] + gt[2 * G + g:2 * G + g + 1] * o_win[g]
            for g in range(G)]
    for half in range(G // 2):
        pair = jnp.concatenate(outs[2 * half:2 * half + 2], axis=0)
        o_ref[0, :, half * LANES:(half + 1) * LANES] = pair.T.astype(o_ref.dtype)


def _nsa_attention(q, k_cmp, v_cmp_t, ks, vs_t, kw, vw_t, gates_t, overlap_t, *, tq=256):
    B, _, S, _ = q.shape
    H, G = NSA_KV_HEADS, NSA_GQA
    tq = min(tq, S)
    assert WINDOW == 2 * tq or S <= tq
    ncp = k_cmp.shape[2]
    n_cmp = (S - CMP_BLOCK) // CMP_STRIDE + 1
    keys = pl.BlockSpec((1, H, S, HEAD_DIM), lambda b, i: (b, 0, 0, 0))
    vals = pl.BlockSpec((1, H, HEAD_DIM, S), lambda b, i: (b, 0, 0, 0))
    return pl.pallas_call(
        functools.partial(_nsa_attn_kernel, tq=tq),
        out_shape=jax.ShapeDtypeStruct((B, S, D_NSA), BF16),
        grid=(B, S // tq),
        in_specs=[
            pl.BlockSpec((1, H * G, tq, HEAD_DIM), lambda b, i: (b, 0, i, 0)),
            pl.BlockSpec((1, H, ncp, HEAD_DIM), lambda b, i: (b, 0, 0, 0)),
            pl.BlockSpec((1, H, HEAD_DIM, ncp), lambda b, i: (b, 0, 0, 0)),
            keys, vals, keys, vals,
            pl.BlockSpec((1, H, 12, tq), lambda b, i: (b, 0, 0, i)),
            pl.BlockSpec(overlap_t.shape, lambda b, i: (0, 0)),
        ],
        out_specs=pl.BlockSpec((1, tq, D_NSA), lambda b, i: (b, i, 0)),
        scratch_shapes=[pltpu.VMEM((1, G * tq), F32), pltpu.VMEM((1, G * tq), F32),
                        pltpu.VMEM((HEAD_DIM, G * tq), F32), pltpu.VMEM((S // SEL_BLOCK, SUBLANES, tq), F32)],
        compiler_params=_params("parallel", "arbitrary"),
        name="nsa_attention",
    )(q, k_cmp, v_cmp_t, ks, vs_t, kw, vw_t, gates_t, overlap_t)


def _rope_tables(pos, heads):
    half = ROPE_DIM // 2
    inv_freq = ROPE_THETA ** (-jnp.arange(half, dtype=F32) / half)
    ang = pos.astype(F32)[:, None] * inv_freq
    cos, sin = jnp.cos(ang), jnp.sin(ang)
    n = pos.shape[0]
    rest = HEAD_DIM - ROPE_DIM
    cos_h = jnp.concatenate([cos, cos, jnp.ones((n, rest), F32)], axis=1)
    sin_h = jnp.concatenate([-sin, sin, jnp.zeros((n, rest), F32)], axis=1)
    return jnp.tile(cos_h, (1, heads)), jnp.tile(sin_h, (1, heads))


def _overlap_matrix(S, ncp):
    n_cmp = (S - CMP_BLOCK) // CMP_STRIDE + 1
    n_sel = S // SEL_BLOCK
    cmp_start = np.arange(n_cmp) * CMP_STRIDE
    sel_start = np.arange(n_sel) * SEL_BLOCK
    ov = np.clip(np.minimum(cmp_start[:, None] + CMP_BLOCK, sel_start[None, :] + SEL_BLOCK)
                 - np.maximum(cmp_start[:, None], sel_start[None, :]), 0, None) / CMP_BLOCK
    full = np.zeros((n_sel, ncp), np.float32)
    full[:, :n_cmp] = ov.T
    return jnp.asarray(full, BF16)


RW_HEAD_ORDER = tuple(2 * (e % RW_PAIRS) + e // RW_PAIRS for e in range(RW_HEADS))


def _parity_major(a, axis):
    axis = axis % a.ndim
    shape = a.shape
    a = a.reshape(shape[:axis] + (RW_HEADS, HEAD_DIM) + shape[axis + 1:])
    a = jnp.take(a, jnp.asarray(RW_HEAD_ORDER), axis=axis)
    return a.reshape(shape)


def _block_ones(n):
    idx = np.arange(n) // HEAD_DIM
    return jnp.asarray((idx[:, None] == idx[None, :]).astype(np.float32), BF16)


def _w_in_layout_kernel(w_ref, g_ref, o_ref):
    nsa0 = RW_COLS + D_POOL
    o_ref[0, :, P_RW_OFF:P_RW_OFF + RW_COLS] = w_ref[0, :, 0:RW_COLS].astype(BF16)
    o_ref[0, :, P_POOL_OFF:P_POOL_OFF + D_POOL] = w_ref[0, :, RW_COLS:nsa0].astype(BF16)
    o_ref[0, :, P_NSA_OFF:P_NSA_OFF + D_NSA] = w_ref[0, :, nsa0:nsa0 + D_NSA].astype(BF16)
    rows = o_ref.shape[1]
    for i in range(6):
        src = nsa0 + D_NSA + i * NSA_KV
        part = jnp.concatenate([w_ref[0, :, src:src + NSA_KV], jnp.zeros((rows, KV_SLOT - NSA_KV), F32)], axis=1)
        o_ref[0, :, P_NSA_OFF + D_NSA + i * KV_SLOT:P_NSA_OFF + D_NSA + (i + 1) * KV_SLOT] = part.astype(BF16)
    g0 = nsa0 + D_NSA + 6 * NSA_KV
    gates = jnp.concatenate([w_ref[0, :, g0:g0 + 3 * NSA_HEADS], jnp.zeros((rows, LANES - 3 * NSA_HEADS), F32)],
                            axis=1).astype(BF16)
    o_ref[0, :, P_NSA_OFF + NSA_GATE_OFF:P_NSA_OFF + NSA_PAD] = _dot(gates, g_ref[...]).astype(BF16)


def _gate_column_permutation():
    m = np.zeros((LANES, NSA_PAD - NSA_GATE_OFF), np.float32)
    for kv in range(NSA_KV_HEADS):
        for g in range(NSA_GQA):
            for br in range(3):
                m[(kv * NSA_GQA + g) * 3 + br, (kv * 3 + br) * NSA_GQA + g] = 1.0
    return jnp.asarray(m, BF16)


def _w_in_layout(w_in, *, tr=256):
    L = w_in.shape[0]
    perm = _gate_column_permutation()
    return pl.pallas_call(
        _w_in_layout_kernel,
        out_shape=jax.ShapeDtypeStruct((L, D_MODEL, P_COLS), BF16),
        grid=(L, D_MODEL // tr),
        in_specs=[pl.BlockSpec((1, tr, IN_COLS), lambda l, i: (l, i, 0)),
                  pl.BlockSpec(perm.shape, lambda l, i: (0, 0))],
        out_specs=pl.BlockSpec((1, tr, P_COLS), lambda l, i: (l, i, 0)),
        compiler_params=_params("parallel", "parallel"),
        name="w_in_layout",
    )(w_in, perm)


def _ffn_up_layout_kernel(w_ref, a_ref, b_ref):
    pad = jnp.zeros((w_ref.shape[1], D_FF_PAD - D_FF), BF16)
    a_ref[0] = jnp.concatenate([w_ref[0, :, :D_FF].astype(BF16), pad], axis=1)
    b_ref[0] = jnp.concatenate([w_ref[0, :, D_FF:].astype(BF16), pad], axis=1)


def _ffn_up_layout(w, *, tr=128):
    L, D, _ = w.shape
    out = jax.ShapeDtypeStruct((L, D, D_FF_PAD), BF16)
    ospec = pl.BlockSpec((1, tr, D_FF_PAD), lambda l, i: (l, i, 0))
    return pl.pallas_call(
        _ffn_up_layout_kernel,
        out_shape=[out, out],
        grid=(L, D // tr),
        in_specs=[pl.BlockSpec((1, tr, 2 * D_FF), lambda l, i: (l, i, 0))],
        out_specs=[ospec, ospec],
        compiler_params=_params("parallel", "parallel"),
        name="ffn_up_layout",
    )(w)


def _ffn_down_layout_kernel(w_ref, o_ref):
    real = pl.program_id(1) < D_FF // o_ref.shape[1]
    o_ref[0] = jnp.where(real, w_ref[0], 0.0).astype(BF16)


def _ffn_down_layout(w, *, tr=128):
    L, _, D = w.shape
    last = D_FF // tr - 1
    return pl.pallas_call(
        _ffn_down_layout_kernel,
        out_shape=jax.ShapeDtypeStruct((L, D_FF_PAD, D), BF16),
        grid=(L, D_FF_PAD // tr),
        in_specs=[pl.BlockSpec((1, tr, D), lambda l, i: (l, jnp.minimum(i, last), 0))],
        out_specs=pl.BlockSpec((1, tr, D), lambda l, i: (l, i, 0)),
        compiler_params=_params("parallel", "parallel"),
        name="ffn_down_layout",
    )(w)


def _gate_bias_layout(gate_b):
    L = gate_b.shape[0]
    gb = gate_b.reshape(L, NSA_KV_HEADS, NSA_GQA, 3).transpose(0, 1, 3, 2).reshape(L, 1, 3 * NSA_HEADS)
    return jnp.pad(gb, ((0, 0), (0, 0), (0, LANES - 3 * NSA_HEADS)))


def kernel(x, ffn1_w_up, ffn1_w_down, ln1_g, ln1_b, w_in, rw_mu, rw_w0, rw_w2, rw_a0, rw_a2, rw_g2, rw_k_k,
           rw_k_a, rw_r_k, rw_gn_g, rw_gn_b, pool_w, pool_b, pool_scale, nsa_cmp_pe_k, nsa_cmp_pe_v,
           nsa_cmp_k_w1, nsa_cmp_k_w2, nsa_cmp_v_w1, nsa_cmp_v_w2, nsa_gate_b, w_out, ln2_g, ln2_b,
           ffn2_w_up, ffn2_w_down, ln3_g, ln3_b):
    prm = _prepare(x.shape[1], ffn1_w_up, ffn1_w_down, ln1_g, ln1_b, w_in, rw_mu, rw_w0, rw_w2, rw_a0, rw_a2,
                   rw_g2, rw_k_k, rw_k_a, rw_r_k, rw_gn_g, rw_gn_b, pool_w, pool_b, pool_scale, nsa_cmp_pe_k,
                   nsa_cmp_pe_v, nsa_cmp_k_w1, nsa_cmp_k_w2, nsa_cmp_v_w1, nsa_cmp_v_w2, nsa_gate_b, w_out,
                   ln2_g, ln2_b, ffn2_w_up, ffn2_w_down, ln3_g, ln3_b)
    B, S, D = x.shape
    h = x.reshape(B * S, D)
    for l in range(w_in.shape[0]):
        h = _layer(h, prm, l, B, S)
    return h.reshape(B, S, D)


def _prepare(S, ffn1_w_up, ffn1_w_down, ln1_g, ln1_b, w_in, rw_mu, rw_w0, rw_w2, rw_a0, rw_a2, rw_g2, rw_k_k,
             rw_k_a, rw_r_k, rw_gn_g, rw_gn_b, pool_w, pool_b, pool_scale, nsa_cmp_pe_k, nsa_cmp_pe_v,
             nsa_cmp_k_w1, nsa_cmp_k_w2, nsa_cmp_v_w1, nsa_cmp_v_w2, nsa_gate_b, w_out, ln2_g, ln2_b,
             ffn2_w_up, ffn2_w_down, ln3_g, ln3_b):
    L = w_in.shape[0]

    up = _ffn_up_layout

    down = _ffn_down_layout

    row = lambda v: v[:, None, :]
    f1a, f1b = up(ffn1_w_up)
    f2a, f2b = up(ffn2_w_up)
    w_out_b = w_out.astype(BF16)
    gw = CMP_BLOCK * HEAD_DIM
    pad_w2 = lambda w: jnp.pad(w, ((0, 0), (0, 0), (0, LANES - HEAD_DIM))).astype(BF16)
    pe_rows = lambda pe: jnp.broadcast_to(pe.reshape(L, 1, gw), (L, SUBLANES, gw)).astype(BF16)
    ncp = S // CMP_STRIDE
    cos_t, sin_t = _rope_tables(jnp.arange(S), 4)
    cos_c, sin_c = _rope_tables(jnp.arange(ncp) * CMP_STRIDE + (CMP_BLOCK - 1), 2)
    return dict(
        ffn1=(f1a, f1b, down(ffn1_w_down), row(ln1_g), row(ln1_b)),
        ffn2=(f2a, f2b, down(ffn2_w_down), row(ln3_g), row(ln3_b)),
        w_in=_w_in_layout(w_in),
        w_out=(_parity_major(w_out_b[:, :D_RWKV], 1), w_out_b[:, D_RWKV:D_RWKV + D_POOL],
               w_out_b[:, D_RWKV + D_POOL:]),
        ln2=(row(ln2_g), row(ln2_b)),
        rw=dict(mu=row(rw_mu), w0=row(rw_w0), w2=rw_w2.astype(BF16), a0=row(rw_a0), a2=rw_a2.astype(BF16),
                g2=_parity_major(rw_g2, -1).astype(BF16), k_k=row(rw_k_k), k_a=row(rw_k_a),
                r_k=rw_r_k.reshape(L, 1, D_RWKV), gn_g=row(_parity_major(rw_gn_g, -1)),
                gn_b=row(_parity_major(rw_gn_b, -1)), ones=_block_ones(D_RWKV)),
        cmp=dict(k_w1=nsa_cmp_k_w1.reshape(L, gw, CMP_HIDDEN).astype(BF16), k_w2=pad_w2(nsa_cmp_k_w2),
                 k_pe=pe_rows(nsa_cmp_pe_k),
                 v_w1=nsa_cmp_v_w1.reshape(L, gw, CMP_HIDDEN).astype(BF16), v_w2=nsa_cmp_v_w2.transpose(0, 2, 1).astype(BF16),
                 v_pe=pe_rows(nsa_cmp_pe_v)),
        gate_b=_gate_bias_layout(nsa_gate_b),
        pool=(pool_w.astype(BF16), row(pool_b), row(pool_scale)),
        rope=(cos_t, sin_t), rope_cmp=(cos_c, sin_c), overlap=_overlap_matrix(S, ncp))


def _mixers(p_all, prm, l):
    r, w, k, v, kk, be, bo, g = _rw_prep(p_all, prm["rw"], l)
    y_rw = _rw_scan(r, w, k, v, kk, be, bo, g, prm["rw"], l)
    y_pool = _pool_mix(p_all, *prm["pool"], l)
    q, kc, vc, ks, vs, kw, vw, gates = _nsa_prep(p_all, *prm["rope"], prm["gate_b"], l)
    k_cmp, v_cmp = _nsa_compress(kc, vc, prm["cmp"], *prm["rope_cmp"], l)
    y_nsa = _nsa_attention(q, k_cmp, v_cmp, ks, vs, kw, vw, gates, prm["overlap"])
    return y_rw, y_pool, y_nsa


def _layer(h, prm, l, B, S):
    T = B * S
    h = _ffn_ln(h, *prm["ffn1"], l)
    p_all = _in_proj(h, prm["w_in"], l).reshape(B, S, P_COLS)
    y_rw, y_pool, y_nsa = _mixers(p_all, prm, l)
    h = _out_proj_ln(h, y_rw.reshape(T, D_RWKV), y_pool.reshape(T, D_POOL), y_nsa.reshape(T, D_NSA),
                     prm["w_out"], *prm["ln2"], l)
    return _ffn_ln(h, *prm["ffn2"], l)
```

```python
import functools

import numpy as np
import jax
import jax.numpy as jnp
from jax import lax
from jax.experimental import pallas as pl
from jax.experimental.pallas import tpu as pltpu

F32 = jnp.float32
BF16 = jnp.bfloat16

D_MODEL = 2048
DEPTH = 4
HEAD_DIM = 64
D_RWKV = 768
D_POOL = 512
D_NSA = 768
RW_HEADS = 12
RW_DECAY_LORA = 64
RW_A_LORA = 64
RW_GATE_LORA = 128
RW_GN_EPS = 64e-5
RW_COLS = 3 * D_RWKV + RW_DECAY_LORA + RW_A_LORA + RW_GATE_LORA
POOL_WINDOWS = (2, 4, 8, 16)
POOL_GROUP = 128
NSA_HEADS = 12
NSA_KV_HEADS = 3
NSA_GQA = 4
NSA_KV = 192
NSA_COLS = D_NSA + 6 * NSA_KV + 3 * NSA_HEADS
CMP_BLOCK = 32
CMP_STRIDE = 16
CMP_HIDDEN = 256
SEL_BLOCK = 64
SEL_TOPK = 16
FORCE_SCORE = 1e9
WINDOW = 512
ROPE_THETA = 500000.0
ROPE_DIM = 16
D_FF = 5504
IN_COLS = RW_COLS + D_POOL + NSA_COLS
ALPHA = (2 * DEPTH) ** 0.25
LN_EPS = 1e-5

LANES = 128
SUBLANES = 8
VMEM_LIMIT = 56 * 1024 * 1024

KV_SLOT = 2 * LANES
NSA_PAD = 2560
P_RW_OFF = 0
P_NSA_OFF = RW_COLS
P_POOL_OFF = RW_COLS + NSA_PAD
P_COLS = RW_COLS + NSA_PAD + D_POOL
NSA_GATE_OFF = D_NSA + 6 * KV_SLOT
D_FF_PAD = 5632

NEG = -1e30


def _params(*sem):
    return pltpu.CompilerParams(dimension_semantics=sem, vmem_limit_bytes=VMEM_LIMIT)


def _layer_norm(z, g, b):
    mu = jnp.mean(z, axis=-1, keepdims=True)
    zc = z - mu
    var = jnp.mean(zc * zc, axis=-1, keepdims=True)
    return zc * lax.rsqrt(var + LN_EPS) * g + b


def _dot(a, b):
    return jnp.dot(a, b, preferred_element_type=F32)


def _dot_nt(a, b):
    return lax.dot_general(a, b, (((1,), (1,)), ((), ())), preferred_element_type=F32)


def _dot_split(x, w):
    hi = x.astype(BF16)
    lo = (x - hi.astype(F32)).astype(BF16)
    return _dot(hi, w) + _dot(lo, w)


FFN_LN_ROWS = 256
FFN_MM_ROWS = 512


def _ffn_kernel(x_ref, wa_ref, wb_ref, wd_ref, g_ref, b_ref, o_ref, xb_ref):
    k = pl.program_id(1)

    @pl.when(k == 0)
    def _():
        xb_ref[...] = x_ref[...].astype(BF16)
        o_ref[...] = jnp.zeros_like(o_ref)

    for c in range(o_ref.shape[0] // FFN_MM_ROWS):
        rows = slice(c * FFN_MM_ROWS, (c + 1) * FFN_MM_ROWS)
        xb = xb_ref[rows, :]
        a = _dot(xb, wa_ref[...])
        b = _dot(xb, wb_ref[...])
        h = (a / (1.0 + jnp.exp(-a))) * b
        o_ref[rows, :] += _dot(h.astype(BF16), wd_ref[...])

    @pl.when(k == pl.num_programs(1) - 1)
    def _():
        for c in range(o_ref.shape[0] // FFN_LN_ROWS):
            rows = slice(c * FFN_LN_ROWS, (c + 1) * FFN_LN_ROWS)
            z = ALPHA * x_ref[rows, :] + 0.5 * o_ref[rows, :]
            o_ref[rows, :] = _layer_norm(z, g_ref[...], b_ref[...])


def _ffn_ln(x, wa, wb, wd, g, b, layer, *, tm=1024, tf=512):
    T, D = x.shape
    fp = wa.shape[-1]
    tm = min(tm, T)
    return pl.pallas_call(
        _ffn_kernel,
        out_shape=jax.ShapeDtypeStruct((T, D), F32),
        grid=(T // tm, fp // tf),
        in_specs=[
            pl.BlockSpec((tm, D), lambda i, k: (i, 0)),
            pl.BlockSpec((None, D, tf), lambda i, k: (layer, 0, k)),
            pl.BlockSpec((None, D, tf), lambda i, k: (layer, 0, k)),
            pl.BlockSpec((None, tf, D), lambda i, k: (layer, k, 0)),
            pl.BlockSpec((None, 1, D), lambda i, k: (layer, 0, 0)),
            pl.BlockSpec((None, 1, D), lambda i, k: (layer, 0, 0)),
        ],
        out_specs=pl.BlockSpec((tm, D), lambda i, k: (i, 0)),
        scratch_shapes=[pltpu.VMEM((tm, D), BF16)],
        compiler_params=_params("parallel", "arbitrary"),
        name="ffn_ln",
    )(x, wa, wb, wd, g, b)


def _inproj_kernel(x_ref, w_ref, o_ref, xb_ref):
    @pl.when(pl.program_id(1) == 0)
    def _():
        xb_ref[...] = x_ref[...].astype(BF16)

    o_ref[...] = _dot(xb_ref[...], w_ref[...])


def _in_proj(x, w, layer, *, tm=1024, tn=P_COLS // 4):
    T, D = x.shape
    n = w.shape[-1]
    tm = min(tm, T)
    return pl.pallas_call(
        _inproj_kernel,
        out_shape=jax.ShapeDtypeStruct((T, n), F32),
        grid=(T // tm, n // tn),
        in_specs=[
            pl.BlockSpec((tm, D), lambda i, j: (i, 0)),
            pl.BlockSpec((None, D, tn), lambda i, j: (layer, 0, j)),
        ],
        out_specs=pl.BlockSpec((tm, tn), lambda i, j: (i, j)),
        scratch_shapes=[pltpu.VMEM((tm, D), BF16)],
        compiler_params=_params("parallel", "arbitrary"),
        name="in_proj",
    )(x, w)


def _outproj_kernel(x_ref, yr_ref, yp_ref, yn_ref, wr_ref, wp_ref, wn_ref, g_ref, b_ref, o_ref):
    y = _dot(yr_ref[...], wr_ref[...]) + _dot(yp_ref[...], wp_ref[...]) + _dot(yn_ref[...], wn_ref[...])
    o_ref[...] = _layer_norm(ALPHA * x_ref[...] + y, g_ref[...], b_ref[...])


def _out_proj_ln(x, y_rw, y_pool, y_nsa, w_out, g, b, layer, *, tm=512):
    T, D = x.shape
    return pl.pallas_call(
        _outproj_kernel,
        out_shape=jax.ShapeDtypeStruct((T, D), F32),
        grid=(T // tm,),
        in_specs=[
            pl.BlockSpec((tm, D), lambda i: (i, 0)),
            pl.BlockSpec((tm, D_RWKV), lambda i: (i, 0)),
            pl.BlockSpec((tm, D_POOL), lambda i: (i, 0)),
            pl.BlockSpec((tm, D_NSA), lambda i: (i, 0)),
            pl.BlockSpec((None, D_RWKV, D), lambda i: (layer, 0, 0)),
            pl.BlockSpec((None, D_POOL, D), lambda i: (layer, 0, 0)),
            pl.BlockSpec((None, D_NSA, D), lambda i: (layer, 0, 0)),
            pl.BlockSpec((None, 1, D), lambda i: (layer, 0, 0)),
            pl.BlockSpec((None, 1, D), lambda i: (layer, 0, 0)),
        ],
        out_specs=pl.BlockSpec((tm, D), lambda i: (i, 0)),
        compiler_params=_params("parallel"),
        name="out_proj_ln",
    )(x, y_rw, y_pool, y_nsa, w_out[0], w_out[1], w_out[2], g, b)


POOL_HALO = 16


def _pool_kernel(p_ref, halo_ref, w_ref, b_ref, sc_ref, o_ref, xs_ref, *, ts):
    s = pl.program_id(1)
    x = p_ref[0]
    halo = jnp.where(s > 0, halo_ref[0], 0.0)
    xs_ref[0:POOL_HALO, :] = halo
    xs_ref[POOL_HALO:POOL_HALO + ts, :] = x
    t1 = (s * ts + 1 + lax.broadcasted_iota(jnp.int32, (ts, 1), 0)).astype(F32)
    for gi, win in enumerate(POOL_WINDOWS):
        c0 = gi * POOL_GROUP
        acc = x[:, c0:c0 + POOL_GROUP]
        for j in range(1, win):
            acc = acc + xs_ref[POOL_HALO - j:POOL_HALO - j + ts, c0:c0 + POOL_GROUP]
        pooled = acc / jnp.minimum(t1, float(win)) - x[:, c0:c0 + POOL_GROUP]
        z = _dot(pooled.astype(BF16), w_ref[gi]) + b_ref[:, c0:c0 + POOL_GROUP]
        o_ref[0, :, c0:c0 + POOL_GROUP] = (z * sc_ref[:, c0:c0 + POOL_GROUP]).astype(o_ref.dtype)


def _pool_mix(p_all, pool_w, pool_b, pool_scale, layer, *, ts=1024):
    B, S, _ = p_all.shape
    ts = min(ts, S)
    cb = P_POOL_OFF // D_POOL
    hb = ts // POOL_HALO
    return pl.pallas_call(
        functools.partial(_pool_kernel, ts=ts),
        out_shape=jax.ShapeDtypeStruct((B, S, D_POOL), BF16),
        grid=(B, S // ts),
        in_specs=[
            pl.BlockSpec((1, ts, D_POOL), lambda b, s: (b, s, cb)),
            pl.BlockSpec((1, POOL_HALO, D_POOL), lambda b, s: (b, jnp.maximum(s * hb - 1, 0), cb)),
            pl.BlockSpec((None, 4, POOL_GROUP, POOL_GROUP), lambda b, s: (layer, 0, 0, 0)),
            pl.BlockSpec((None, 1, D_POOL), lambda b, s: (layer, 0, 0)),
            pl.BlockSpec((None, 1, D_POOL), lambda b, s: (layer, 0, 0)),
        ],
        out_specs=pl.BlockSpec((1, ts, D_POOL), lambda b, s: (b, s, 0)),
        scratch_shapes=[pltpu.VMEM((ts + POOL_HALO, D_POOL), F32)],
        compiler_params=_params("parallel", "parallel"),
        name="pool_mix",
    )(p_all, p_all, pool_w, pool_b, pool_scale)


def _softplus(z):
    return jnp.maximum(z, 0.0) + jnp.log1p(jnp.exp(-jnp.abs(z)))


def _sigmoid(z):
    return 1.0 / (1.0 + jnp.exp(-z))


def _lanes_to_parity_major(x):
    n_pairs = x.shape[1] // LANES
    low = lax.broadcasted_iota(jnp.int32, (1, LANES), 1) < HEAD_DIM

    def chunk(e, to_low):
        pair, parity = e % n_pairs, e // n_pairs
        src = x[:, pair * LANES:(pair + 1) * LANES]
        return src if (parity == 0) == to_low else pltpu.roll(src, HEAD_DIM, 1)

    return jnp.concatenate([jnp.where(low, chunk(2 * d, True), chunk(2 * d + 1, False))
                            for d in range(n_pairs)], axis=1)


def _rw_prep_kernel(p_ref, prev_ref, mu_ref, w0_ref, w2_ref, a0_ref, a2_ref, g2_ref, kk_ref, ka_ref,
                    rk_ref, ones_ref,
                    r_o, w_o, k_o, v_o, kk_o, be_o, bo_o, g_o, *, ts):
    s = pl.program_id(1)
    x = p_ref[0]
    last = jnp.where(s > 0, prev_ref[0][SUBLANES - 1:SUBLANES, :], 0.0)
    row = lax.broadcasted_iota(jnp.int32, (ts, 1), 0)
    shifted = jnp.where(row == 0, last, pltpu.roll(x, 1, 0))
    xm = x + (shifted - x) * mu_ref[...]
    c = D_RWKV
    r = xm[:, 0:c]
    k = xm[:, c:2 * c]
    v = xm[:, 2 * c:3 * c]
    lora = xm[:, 3 * c:3 * c + LANES]
    wl = lora[:, :RW_DECAY_LORA]
    al = lora[:, RW_DECAY_LORA:]
    gl = xm[:, 3 * c + LANES:]
    w = -_softplus(-(w0_ref[...] + _dot(jnp.tanh(wl).astype(BF16), w2_ref[...]))) - 0.5
    decay = jnp.exp(-jnp.exp(w))
    a = _sigmoid(a0_ref[...] + _dot(al.astype(BF16), a2_ref[...]))
    g = _dot(_sigmoid(gl).astype(BF16), g2_ref[...])
    ones = ones_ref[...]
    kk = k * kk_ref[...]
    nrm = jnp.sqrt(_dot_split(kk * kk, ones))
    kk = kk / jnp.maximum(nrm, 1e-12)
    k_mod = k * (1.0 + (a - 1.0) * ka_ref[...])
    bonus = _dot_split(r * k_mod * rk_ref[...], ones) * v
    r_o[0] = r
    w_o[0] = decay
    k_o[0] = k_mod
    v_o[0] = v
    kk_o[0] = kk
    be_o[0] = kk * a
    bo_o[0] = _lanes_to_parity_major(bonus).astype(bo_o.dtype)
    g_o[0] = g.astype(g_o.dtype)


def _rw_prep(p_all, prm, layer, *, ts=512):
    B, S, _ = p_all.shape
    cb = P_RW_OFF // RW_COLS
    hb = ts // SUBLANES
    vec = lambda n: pl.BlockSpec((None, 1, n), lambda b, s: (layer, 0, 0))
    mat = lambda m, n: pl.BlockSpec((None, m, n), lambda b, s: (layer, 0, 0))
    out = jax.ShapeDtypeStruct((B, S, D_RWKV), F32)
    ospec = pl.BlockSpec((1, ts, D_RWKV), lambda b, s: (b, s, 0))
    return pl.pallas_call(
        functools.partial(_rw_prep_kernel, ts=ts),
        out_shape=[out] * 6 + [jax.ShapeDtypeStruct((B, S, D_RWKV), BF16)] * 2,
        grid=(B, S // ts),
        in_specs=[
            pl.BlockSpec((1, ts, RW_COLS), lambda b, s: (b, s, cb)),
            pl.BlockSpec((1, SUBLANES, RW_COLS), lambda b, s: (b, jnp.maximum(s * hb - 1, 0), cb)),
            vec(RW_COLS), vec(D_RWKV), mat(RW_DECAY_LORA, D_RWKV), vec(D_RWKV), mat(RW_A_LORA, D_RWKV),
            mat(RW_GATE_LORA, D_RWKV), vec(D_RWKV), vec(D_RWKV), vec(D_RWKV),
            pl.BlockSpec((D_RWKV, D_RWKV), lambda b, s: (0, 0)),
        ],
        out_specs=[ospec] * 8,
        compiler_params=_params("parallel", "parallel"),
        name="rw_prep",
    )(p_all, p_all, prm["mu"], prm["w0"], prm["w2"], prm["a0"], prm["a2"], prm["g2"], prm["k_k"],
      prm["k_a"], prm["r_k"], prm["ones"])


RW_PAIRS = D_RWKV // LANES
RW_STEPS = 16


def _rw_scan_kernel(r_ref, w_ref, k_ref, v_ref, kk_ref, be_ref, bo_ref, g_ref, gng_ref, gnb_ref,
                    pair_ref, half_ref, o_ref, st_ref, vc_ref, sr_ref, y_ref, *, ts, nb):
    s = pl.program_id(1)

    @pl.when(s == 0)
    def _():
        st_ref[...] = jnp.zeros_like(st_ref)

    side, stack = 2, RW_PAIRS // 2
    rows, width = stack * HEAD_DIM, side * LANES
    lane = lax.broadcasted_iota(jnp.int32, (rows, width), 1)
    row = lax.broadcasted_iota(jnp.int32, (rows, width), 0)
    eye = jnp.where(lane % HEAD_DIM == row % HEAD_DIM, 1.0, 0.0)
    ones_blk = pair_ref[...]

    def spread(x8, j):
        blocks = []
        for st in range(stack):
            lanes = jnp.concatenate([x8[j:j + 1, (sd * stack + st) * LANES:(sd * stack + st + 1) * LANES]
                                     for sd in range(side)], axis=1)
            blocks.append(jnp.broadcast_to(lanes, (HEAD_DIM, width)))
        return jnp.concatenate(blocks, axis=0)

    def step(i, carry):
        base = pl.multiple_of(i * RW_STEPS, RW_STEPS)
        tiles = [[ref[b, pl.ds(base, RW_STEPS), :] for ref in (kk_ref, w_ref, be_ref, k_ref, r_ref, v_ref)]
                 for b in range(nb)]
        for b in range(nb):
            lhs = jnp.concatenate([eye * spread(tiles[b][5], j) for j in range(RW_STEPS)], axis=0)
            vc_ref[b] = _dot(lhs.astype(BF16), ones_blk)
        sts = [st_ref[b] for b in range(nb)]
        for j in range(RW_STEPS):
            for b in range(nb):
                kk8, w8, be8, k8, r8, _ = tiles[b]
                st = sts[b]
                sa = _dot((st * spread(kk8, j)).astype(BF16), ones_blk)
                st = (st * spread(w8, j) - sa * spread(be8, j)
                      + vc_ref[b, j * rows:(j + 1) * rows, :] * spread(k8, j))
                sr = (st * spread(r8, j)).astype(BF16)
                for sd in range(side):
                    lo = (j * side + sd) * rows
                    sr_ref[b, lo:lo + rows, :] = sr[:, sd * LANES:(sd + 1) * LANES]
                sts[b] = st
        for b in range(nb):
            st_ref[b] = sts[b]
            yt = _dot_nt(half_ref[...], sr_ref[b])
            per = side * rows
            y_ref[b, pl.ds(base, RW_STEPS), :] = jnp.concatenate(
                [jnp.concatenate([yt[0:1, j * per:(j + 1) * per], yt[1:2, j * per:(j + 1) * per]], axis=1)
                 for j in range(RW_STEPS)], axis=0)
        return carry

    lax.fori_loop(0, ts // RW_STEPS, step, 0)

    def head_sums(x):
        n = x.shape[0]
        hi = x.astype(BF16)
        both = jnp.concatenate([hi, (x - hi.astype(F32)).astype(BF16)], axis=0)
        out = jnp.concatenate([_dot(both[:, c * width:(c + 1) * width], ones_blk)
                               for c in range(D_RWKV // width)], axis=1)
        return out[:n] + out[n:]

    y = y_ref[...].reshape(nb * ts, D_RWKV)
    mean = head_sums(y) * (1.0 / HEAD_DIM)
    yc = y - mean
    var = head_sums(yc * yc) * (1.0 / HEAD_DIM)
    yn = yc * lax.rsqrt(var + RW_GN_EPS) * gng_ref[...] + gnb_ref[...]
    out = (yn + bo_ref[...].reshape(nb * ts, D_RWKV)) * g_ref[...].reshape(nb * ts, D_RWKV)
    o_ref[...] = out.reshape(nb, ts, D_RWKV).astype(o_ref.dtype)


def _rw_scan(r, w, k, v, kk, be, bo, g, prm, layer, *, ts=128):
    B, S, _ = r.shape
    nb = 4 if B % 4 == 0 else (2 if B % 2 == 0 else 1)
    rows, width = RW_PAIRS // 2 * HEAD_DIM, 2 * LANES
    seq = pl.BlockSpec((nb, ts, D_RWKV), lambda b, s: (b, s, 0))
    vec = pl.BlockSpec((None, 1, D_RWKV), lambda b, s: (layer, 0, 0))
    return pl.pallas_call(
        functools.partial(_rw_scan_kernel, ts=ts, nb=nb),
        out_shape=jax.ShapeDtypeStruct((B, S, D_RWKV), BF16),
        grid=(B // nb, S // ts),
        in_specs=[seq] * 8 + [vec, vec, pl.BlockSpec((width, width), lambda b, s: (0, 0)),
                              pl.BlockSpec((SUBLANES, LANES), lambda b, s: (0, 0))],
        out_specs=seq,
        scratch_shapes=[pltpu.VMEM((nb, rows, width), F32), pltpu.VMEM((nb, RW_STEPS * rows, width), F32),
                        pltpu.VMEM((nb, RW_STEPS * RW_PAIRS * HEAD_DIM, LANES), BF16),
                        pltpu.VMEM((nb, ts, D_RWKV), F32)],
        compiler_params=_params("parallel", "arbitrary"),
        name="rw_scan",
    )(r, w, k, v, kk, be, bo, g, prm["gn_g"], prm["gn_b"], prm["ones"][:width, :width],
      prm["ones"][:SUBLANES * HEAD_DIM:HEAD_DIM, :LANES])


def _rope(x, cos, sin):
    half = ROPE_DIM // 2
    w = x.shape[-1]
    d = lax.broadcasted_iota(jnp.int32, x.shape, 1) % HEAD_DIM
    partner = jnp.where(d < half, pltpu.roll(x, w - half, 1), pltpu.roll(x, half, 1))
    return x * cos + partner * sin


def _nsa_prep_kernel(p_ref, cos_ref, sin_ref, gb_ref, q_o, kc_o, vc_o, ks_o, vs_o, kw_o, vw_o, gt_o):
    cos = cos_ref[...]
    sin = sin_ref[...]
    for hq in range(NSA_HEADS // 4):
        x = p_ref[0, :, hq * KV_SLOT:(hq + 1) * KV_SLOT]
        qr = _rope(x, cos, sin) * (HEAD_DIM ** -0.5)
        for j in range(4):
            q_o[0, 4 * hq + j] = qr[:, j * HEAD_DIM:(j + 1) * HEAD_DIM].astype(BF16)
    for i, ref in enumerate((kc_o, vc_o, ks_o, vs_o, kw_o, vw_o)):
        part = p_ref[0, :, D_NSA + i * KV_SLOT:D_NSA + (i + 1) * KV_SLOT]
        if i in (2, 4):
            part = _rope(part, cos, sin)
        if i in (3, 5):
            part = part.T
            for h in range(NSA_KV_HEADS):
                ref[0, h] = part[h * HEAD_DIM:(h + 1) * HEAD_DIM, :].astype(BF16)
        else:
            for h in range(NSA_KV_HEADS):
                ref[0, h] = part[:, h * HEAD_DIM:(h + 1) * HEAD_DIM].astype(BF16)
    gts = _sigmoid(p_ref[0, :, NSA_GATE_OFF:NSA_GATE_OFF + LANES] + gb_ref[...]).T
    for h in range(NSA_KV_HEADS):
        gt_o[0, h] = gts[h * 12:(h + 1) * 12, :]


def _nsa_prep(p_all, cos, sin, gate_b, layer, *, ts=1024):
    B, S, _ = p_all.shape
    ts = min(ts, S)
    H = NSA_KV_HEADS
    k_shape = jax.ShapeDtypeStruct((B, H, S, HEAD_DIM), BF16)
    k_spec = pl.BlockSpec((1, H, ts, HEAD_DIM), lambda b, s: (b, 0, s, 0))
    vt_shape = jax.ShapeDtypeStruct((B, H, HEAD_DIM, S), BF16)
    vt_spec = pl.BlockSpec((1, H, HEAD_DIM, ts), lambda b, s: (b, 0, 0, s))
    return pl.pallas_call(
        _nsa_prep_kernel,
        out_shape=[jax.ShapeDtypeStruct((B, NSA_HEADS, S, HEAD_DIM), BF16),
                   k_shape, k_shape, k_shape, vt_shape, k_shape, vt_shape,
                   jax.ShapeDtypeStruct((B, H, 12, S), F32)],
        grid=(B, S // ts),
        in_specs=[
            pl.BlockSpec((1, ts, NSA_PAD), lambda b, s: (b, s, P_NSA_OFF // NSA_PAD)),
            pl.BlockSpec((ts, KV_SLOT), lambda b, s: (s, 0)),
            pl.BlockSpec((ts, KV_SLOT), lambda b, s: (s, 0)),
            pl.BlockSpec((None, 1, LANES), lambda b, s: (layer, 0, 0)),
        ],
        out_specs=[pl.BlockSpec((1, NSA_HEADS, ts, HEAD_DIM), lambda b, s: (b, 0, s, 0)),
                   k_spec, k_spec, k_spec, vt_spec, k_spec, vt_spec,
                   pl.BlockSpec((1, H, 12, ts), lambda b, s: (b, 0, 0, s))],
        compiler_params=_params("parallel", "parallel"),
        name="nsa_prep",
    )(p_all, cos, sin, gate_b)


def _gelu_tanh(x):
    return 0.5 * x * (1.0 + jnp.tanh(float(np.sqrt(2.0 / np.pi)) * (x + 0.044715 * (x * x * x))))


def _nsa_cmp_kernel(kc_ref, vc_ref, kw1_ref, kw2_ref, kpe_ref, vw1_ref, vw2_ref, vpe_ref, cos_ref, sin_ref,
                    k_o, v_o):
    def hidden(g, w1_ref, pe_ref):
        half = CMP_STRIDE * HEAD_DIM
        first = _dot(g, w1_ref[:half, :])
        second = _dot(g, w1_ref[half:, :])
        n = first.shape[0]
        bias = _dot(pe_ref[...], w1_ref[...])[0:1]
        return _gelu_tanh(first + pltpu.roll(second, n - 1, 0) + bias).astype(BF16)

    k = _dot(hidden(kc_ref[0, 0], kw1_ref, kpe_ref), kw2_ref[...])
    k_o[0, 0] = _rope(k, cos_ref[...], sin_ref[...])[:, :HEAD_DIM].astype(BF16)
    v_o[0, 0] = _dot_nt(vw2_ref[...], hidden(vc_ref[0, 0], vw1_ref, vpe_ref)).astype(BF16)


def _nsa_compress(kc, vc, prm, cos_c, sin_c, layer):
    B, H, S, _ = kc.shape
    ng = S // CMP_STRIDE
    gw = CMP_STRIDE * HEAD_DIM
    g_k = kc.reshape(B, H, ng, gw)
    g_v = vc.reshape(B, H, ng, gw)
    gspec = pl.BlockSpec((1, 1, ng, gw), lambda b, h: (b, h, 0, 0))
    w1 = pl.BlockSpec((None, 2 * gw, CMP_HIDDEN), lambda b, h: (layer, 0, 0))
    w2 = pl.BlockSpec((None, CMP_HIDDEN, LANES), lambda b, h: (layer, 0, 0))
    pe = pl.BlockSpec((None, SUBLANES, 2 * gw), lambda b, h: (layer, 0, 0))
    tab = pl.BlockSpec((ng, LANES), lambda b, h: (0, 0))
    return pl.pallas_call(
        _nsa_cmp_kernel,
        out_shape=[jax.ShapeDtypeStruct((B, H, ng, HEAD_DIM), BF16), jax.ShapeDtypeStruct((B, H, HEAD_DIM, ng), BF16)],
        grid=(B, H),
        in_specs=[gspec, gspec, w1, w2, pe, w1,
                  pl.BlockSpec((None, HEAD_DIM, CMP_HIDDEN), lambda b, h: (layer, 0, 0)), pe, tab, tab],
        out_specs=[pl.BlockSpec((1, 1, ng, HEAD_DIM), lambda b, h: (b, h, 0, 0)),
                   pl.BlockSpec((1, 1, HEAD_DIM, ng), lambda b, h: (b, h, 0, 0))],
        compiler_params=_params("parallel", "parallel"),
        name="nsa_compress",
    )(g_k, g_v, prm["k_w1"], prm["k_w2"], prm["k_pe"], prm["v_w1"], prm["v_w2"], prm["v_pe"], cos_c, sin_c)


def _nsa_attn_kernel(q_ref, kcmp_ref, vcmp_ref, ks_ref, vs_ref, kw_ref, vw_ref, gt_ref, ov_ref, o_ref,
                     m_ref, l_ref, acc_ref, sb_ref, *, tq):
    G = NSA_GQA
    qi = pl.program_id(1)
    for h in range(NSA_KV_HEADS):
        one = lambda ref: ref.at[:, h:h + 1]
        _nsa_head(q_ref.at[:, h * G:(h + 1) * G], one(kcmp_ref), one(vcmp_ref), one(ks_ref), one(vs_ref),
                  one(kw_ref), one(vw_ref), one(gt_ref), ov_ref,
                  o_ref.at[:, :, h * G * HEAD_DIM:(h + 1) * G * HEAD_DIM],
                  m_ref, l_ref, acc_ref, sb_ref, tq=tq, qi=qi)


def _nsa_head(q_ref, kcmp_ref, vcmp_ref, ks_ref, vs_ref, kw_ref, vw_ref, gt_ref, ov_ref, o_ref,
              m_ref, l_ref, acc_ref, sb_ref, *, tq, qi):
    G = NSA_GQA
    tk = tq
    q0 = qi * tq
    q_all = q_ref[0].reshape(G * tq, HEAD_DIM)
    pos = q0 + lax.broadcasted_iota(jnp.int32, (1, tq), 1)
    tiny = jnp.finfo(F32).tiny
    heads = lambda x: jnp.tile(x, (1, G))
    split = lambda x: [x[:, g * tq:(g + 1) * tq] for g in range(G)]

    ncp = kcmp_ref.shape[2]
    n_idx = lax.broadcasted_iota(jnp.int32, (ncp, 1), 0)
    cbias = jnp.where((n_idx * CMP_STRIDE + (CMP_BLOCK - 1)) <= pos, 0.0, NEG)
    any_cmp = jnp.where(pos >= CMP_BLOCK - 1, 1.0, 0.0)
    s = _dot_nt(kcmp_ref[0, 0], q_all) + heads(cbias)
    e = jnp.exp(s - jnp.max(s, axis=0, keepdims=True))
    p = e * (heads(any_cmp) / jnp.maximum(jnp.sum(e, axis=0, keepdims=True), tiny))
    o_cmp = split(_dot(vcmp_ref[0, 0], p.astype(BF16)))
    ps = split(p)
    psum = (ps[0] + ps[1]) + (ps[2] + ps[3])
    p_hi = psum.astype(BF16)
    p_lo = (psum - p_hi.astype(F32)).astype(BF16)
    imp2 = _dot(ov_ref[...], jnp.concatenate([p_hi, p_lo], axis=1))
    imp = imp2[:, :tq] + imp2[:, tq:]

    wk = min(WINDOW + tq, ks_ref.shape[2])
    w0 = pl.multiple_of(jnp.maximum(q0 + tq - wk, 0), tk)
    wpos = w0 + lax.broadcasted_iota(jnp.int32, (wk, 1), 0)
    wbias = jnp.where((wpos <= pos) & (wpos > pos - WINDOW), 0.0, NEG)
    s = _dot_nt(kw_ref[0, 0, pl.ds(w0, wk), :], q_all) + heads(wbias)
    e = jnp.exp(s - jnp.max(s, axis=0, keepdims=True))
    pv = _dot(vw_ref[0, 0, :, pl.ds(w0, wk)], e.astype(BF16))
    o_win = split(pv / jnp.maximum(jnp.sum(e, axis=0, keepdims=True), tiny))

    n_sel = ov_ref.shape[0]
    blk = lax.broadcasted_iota(jnp.int32, (n_sel, 1), 0)
    cur = pos // SEL_BLOCK
    forced = (blk == 0) | (blk == cur) | (blk == cur - 1)
    valid = blk <= cur
    score = jnp.where(valid, jnp.where(forced, FORCE_SCORE, imp), -jnp.inf)
    groups = [score[g * SUBLANES:(g + 1) * SUBLANES, :] for g in range(n_sel // SUBLANES)]
    ranks = [jnp.zeros((SUBLANES, tq), F32) for _ in groups]
    sub = lax.broadcasted_iota(jnp.int32, (SUBLANES, 1), 0)
    for i in range(n_sel):
        ci = jnp.broadcast_to(score[i:i + 1, :], (SUBLANES, tq))
        for g, sg in enumerate(groups):
            if g > i // SUBLANES:
                beats = ci >= sg
            elif g < i // SUBLANES:
                beats = ci > sg
            else:
                beats = (ci > sg) | ((ci == sg) & (sub > i % SUBLANES))
            ranks[g] = ranks[g] + jnp.where(beats, 1.0, 0.0)
    rank = jnp.concatenate(ranks, axis=0)
    chosen = (rank < float(SEL_TOPK)) & valid
    sel_bias = jnp.where(chosen, 0.0, NEG)
    for j in range(n_sel):
        sb_ref[j] = jnp.broadcast_to(sel_bias[j:j + 1, :], (SUBLANES, tq))

    kcol = lax.broadcasted_iota(jnp.int32, (tk, 1), 0)

    def sel_update(kt, n, extra=None):
        bpt = n // SEL_BLOCK
        k0 = pl.multiple_of(kt * tk, tk)
        s = _dot_nt(ks_ref[0, 0, pl.ds(k0, n), :], q_all)
        b0 = kt * (tk // SEL_BLOCK)
        bias = jnp.concatenate([jnp.tile(sb_ref[b0 + j], (SEL_BLOCK // SUBLANES, 1)) for j in range(bpt)],
                               axis=0)
        if extra is not None:
            bias = bias + extra
        vt = vs_ref[0, 0, :, pl.ds(k0, n)]
        s = s + heads(bias)
        m_prev = m_ref[...]
        m_new = jnp.maximum(m_prev, jnp.max(s, axis=0, keepdims=True))
        p = jnp.exp(s - m_new)
        corr = jnp.exp(m_prev - m_new)
        l_ref[...] = corr * l_ref[...] + jnp.sum(p, axis=0, keepdims=True)
        acc_ref[...] = corr * acc_ref[...] + _dot(vt, p.astype(BF16))
        m_ref[...] = m_new

    m_ref[...] = jnp.full_like(m_ref, NEG)
    l_ref[...] = jnp.zeros_like(l_ref)
    acc_ref[...] = jnp.zeros_like(acc_ref)

    def sel_body(i, carry):
        sel_update(4 * i, 4 * tk)
        return carry

    lax.fori_loop(0, qi // 4, sel_body, 0)

    @pl.when(qi % 4 >= 2)
    def _():
        sel_update((qi // 4) * 4, 2 * tk)

    @pl.when(qi % 2 == 1)
    def _():
        sel_update(qi - 1, tk)

    causal = jnp.where(q0 + kcol <= pos, 0.0, NEG)
    sel_update(qi, tk, causal)
    o_sel = split(acc_ref[...] / jnp.maximum(l_ref[...], tiny))

    gt = gt_ref[0, 0]
    outs = [gt[g:g + 1] * o_cmp[g] + gt[G + g:G + g + 1] * o_sel[g] + gt[2 * G + g:2 * G + g + 1] * o_win[g]
            for g in range(G)]
    for half in range(G // 2):
        pair = jnp.concatenate(outs[2 * half:2 * half + 2], axis=0)
        o_ref[0, :, half * LANES:(half + 1) * LANES] = pair.T.astype(o_ref.dtype)


def _nsa_attention(q, k_cmp, v_cmp_t, ks, vs_t, kw, vw_t, gates_t, overlap_t, *, tq=256):
    B, _, S, _ = q.shape
    H, G = NSA_KV_HEADS, NSA_GQA
    tq = min(tq, S)
    assert WINDOW == 2 * tq or S <= tq
    ncp = k_cmp.shape[2]
    n_cmp = (S - CMP_BLOCK) // CMP_STRIDE + 1
    keys = pl.BlockSpec((1, H, S, HEAD_DIM), lambda b, i: (b, 0, 0, 0))
    vals = pl.BlockSpec((1, H, HEAD_DIM, S), lambda b, i: (b, 0, 0, 0))
    return pl.pallas_call(
        functools.partial(_nsa_attn_kernel, tq=tq),
        out_shape=jax.ShapeDtypeStruct((B, S, D_NSA), BF16),
        grid=(B, S // tq),
        in_specs=[
            pl.BlockSpec((1, H * G, tq, HEAD_DIM), lambda b, i: (b, 0, i, 0)),
            pl.BlockSpec((1, H, ncp, HEAD_DIM), lambda b, i: (b, 0, 0, 0)),
            pl.BlockSpec((1, H, HEAD_DIM, ncp), lambda b, i: (b, 0, 0, 0)),
            keys, vals, keys, vals,
            pl.BlockSpec((1, H, 12, tq), lambda b, i: (b, 0, 0, i)),
            pl.BlockSpec(overlap_t.shape, lambda b, i: (0, 0)),
        ],
        out_specs=pl.BlockSpec((1, tq, D_NSA), lambda b, i: (b, i, 0)),
        scratch_shapes=[pltpu.VMEM((1, G * tq), F32), pltpu.VMEM((1, G * tq), F32),
                        pltpu.VMEM((HEAD_DIM, G * tq), F32), pltpu.VMEM((S // SEL_BLOCK, SUBLANES, tq), F32)],
        compiler_params=_params("parallel", "arbitrary"),
        name="nsa_attention",
    )(q, k_cmp, v_cmp_t, ks, vs_t, kw, vw_t, gates_t, overlap_t)


def _rope_tables(pos, heads):
    half = ROPE_DIM // 2
    inv_freq = ROPE_THETA ** (-jnp.arange(half, dtype=F32) / half)
    ang = pos.astype(F32)[:, None] * inv_freq
    cos, sin = jnp.cos(ang), jnp.sin(ang)
    n = pos.shape[0]
    rest = HEAD_DIM - ROPE_DIM
    cos_h = jnp.concatenate([cos, cos, jnp.ones((n, rest), F32)], axis=1)
    sin_h = jnp.concatenate([-sin, sin, jnp.zeros((n, rest), F32)], axis=1)
    return jnp.tile(cos_h, (1, heads)), jnp.tile(sin_h, (1, heads))


def _overlap_matrix(S, ncp):
    n_cmp = (S - CMP_BLOCK) // CMP_STRIDE + 1
    n_sel = S // SEL_BLOCK
    cmp_start = np.arange(n_cmp) * CMP_STRIDE
    sel_start = np.arange(n_sel) * SEL_BLOCK
    ov = np.clip(np.minimum(cmp_start[:, None] + CMP_BLOCK, sel_start[None, :] + SEL_BLOCK)
                 - np.maximum(cmp_start[:, None], sel_start[None, :]), 0, None) / CMP_BLOCK
    full = np.zeros((n_sel, ncp), np.float32)
    full[:, :n_cmp] = ov.T
    return jnp.asarray(full, BF16)


RW_HEAD_ORDER = tuple(2 * (e % RW_PAIRS) + e // RW_PAIRS for e in range(RW_HEADS))


def _parity_major(a, axis):
    axis = axis % a.ndim
    shape = a.shape
    a = a.reshape(shape[:axis] + (RW_HEADS, HEAD_DIM) + shape[axis + 1:])
    a = jnp.take(a, jnp.asarray(RW_HEAD_ORDER), axis=axis)
    return a.reshape(shape)


def _block_ones(n):
    idx = np.arange(n) // HEAD_DIM
    return jnp.asarray((idx[:, None] == idx[None, :]).astype(np.float32), BF16)


def _w_in_layout_kernel(w_ref, g_ref, o_ref):
    nsa0 = RW_COLS + D_POOL
    o_ref[0, :, P_RW_OFF:P_RW_OFF + RW_COLS] = w_ref[0, :, 0:RW_COLS].astype(BF16)
    o_ref[0, :, P_POOL_OFF:P_POOL_OFF + D_POOL] = w_ref[0, :, RW_COLS:nsa0].astype(BF16)
    o_ref[0, :, P_NSA_OFF:P_NSA_OFF + D_NSA] = w_ref[0, :, nsa0:nsa0 + D_NSA].astype(BF16)
    rows = o_ref.shape[1]
    for i in range(6):
        src = nsa0 + D_NSA + i * NSA_KV
        part = jnp.concatenate([w_ref[0, :, src:src + NSA_KV], jnp.zeros((rows, KV_SLOT - NSA_KV), F32)], axis=1)
        o_ref[0, :, P_NSA_OFF + D_NSA + i * KV_SLOT:P_NSA_OFF + D_NSA + (i + 1) * KV_SLOT] = part.astype(BF16)
    g0 = nsa0 + D_NSA + 6 * NSA_KV
    gates = jnp.concatenate([w_ref[0, :, g0:g0 + 3 * NSA_HEADS], jnp.zeros((rows, LANES - 3 * NSA_HEADS), F32)],
                            axis=1).astype(BF16)
    o_ref[0, :, P_NSA_OFF + NSA_GATE_OFF:P_NSA_OFF + NSA_PAD] = _dot(gates, g_ref[...]).astype(BF16)


def _gate_column_permutation():
    m = np.zeros((LANES, NSA_PAD - NSA_GATE_OFF), np.float32)
    for kv in range(NSA_KV_HEADS):
        for g in range(NSA_GQA):
            for br in range(3):
                m[(kv * NSA_GQA + g) * 3 + br, (kv * 3 + br) * NSA_GQA + g] = 1.0
    return jnp.asarray(m, BF16)


def _w_in_layout(w_in, *, tr=256):
    L = w_in.shape[0]
    perm = _gate_column_permutation()
    return pl.pallas_call(
        _w_in_layout_kernel,
        out_shape=jax.ShapeDtypeStruct((L, D_MODEL, P_COLS), BF16),
        grid=(L, D_MODEL // tr),
        in_specs=[pl.BlockSpec((1, tr, IN_COLS), lambda l, i: (l, i, 0)),
                  pl.BlockSpec(perm.shape, lambda l, i: (0, 0))],
        out_specs=pl.BlockSpec((1, tr, P_COLS), lambda l, i: (l, i, 0)),
        compiler_params=_params("parallel", "parallel"),
        name="w_in_layout",
    )(w_in, perm)


def _ffn_up_layout_kernel(w_ref, a_ref, b_ref):
    pad = jnp.zeros((w_ref.shape[1], D_FF_PAD - D_FF), BF16)
    a_ref[0] = jnp.concatenate([w_ref[0, :, :D_FF].astype(BF16), pad], axis=1)
    b_ref[0] = jnp.concatenate([w_ref[0, :, D_FF:].astype(BF16), pad], axis=1)


def _ffn_up_layout(w, *, tr=128):
    L, D, _ = w.shape
    out = jax.ShapeDtypeStruct((L, D, D_FF_PAD), BF16)
    ospec = pl.BlockSpec((1, tr, D_FF_PAD), lambda l, i: (l, i, 0))
    return pl.pallas_call(
        _ffn_up_layout_kernel,
        out_shape=[out, out],
        grid=(L, D // tr),
        in_specs=[pl.BlockSpec((1, tr, 2 * D_FF), lambda l, i: (l, i, 0))],
        out_specs=[ospec, ospec],
        compiler_params=_params("parallel", "parallel"),
        name="ffn_up_layout",
    )(w)


def _ffn_down_layout_kernel(w_ref, o_ref):
    real = pl.program_id(1) < D_FF // o_ref.shape[1]
    o_ref[0] = jnp.where(real, w_ref[0], 0.0).astype(BF16)


def _ffn_down_layout(w, *, tr=128):
    L, _, D = w.shape
    last = D_FF // tr - 1
    return pl.pallas_call(
        _ffn_down_layout_kernel,
        out_shape=jax.ShapeDtypeStruct((L, D_FF_PAD, D), BF16),
        grid=(L, D_FF_PAD // tr),
        in_specs=[pl.BlockSpec((1, tr, D), lambda l, i: (l, jnp.minimum(i, last), 0))],
        out_specs=pl.BlockSpec((1, tr, D), lambda l, i: (l, i, 0)),
        compiler_params=_params("parallel", "parallel"),
        name="ffn_down_layout",
    )(w)


def _gate_bias_layout(gate_b):
    L = gate_b.shape[0]
    gb = gate_b.reshape(L, NSA_KV_HEADS, NSA_GQA, 3).transpose(0, 1, 3, 2).reshape(L, 1, 3 * NSA_HEADS)
    return jnp.pad(gb, ((0, 0), (0, 0), (0, LANES - 3 * NSA_HEADS)))


def kernel(x, ffn1_w_up, ffn1_w_down, ln1_g, ln1_b, w_in, rw_mu, rw_w0, rw_w2, rw_a0, rw_a2, rw_g2, rw_k_k,
           rw_k_a, rw_r_k, rw_gn_g, rw_gn_b, pool_w, pool_b, pool_scale, nsa_cmp_pe_k, nsa_cmp_pe_v,
           nsa_cmp_k_w1, nsa_cmp_k_w2, nsa_cmp_v_w1, nsa_cmp_v_w2, nsa_gate_b, w_out, ln2_g, ln2_b,
           ffn2_w_up, ffn2_w_down, ln3_g, ln3_b):
    prm = _prepare(x.shape[1], ffn1_w_up, ffn1_w_down, ln1_g, ln1_b, w_in, rw_mu, rw_w0, rw_w2, rw_a0, rw_a2,
                   rw_g2, rw_k_k, rw_k_a, rw_r_k, rw_gn_g, rw_gn_b, pool_w, pool_b, pool_scale, nsa_cmp_pe_k,
                   nsa_cmp_pe_v, nsa_cmp_k_w1, nsa_cmp_k_w2, nsa_cmp_v_w1, nsa_cmp_v_w2, nsa_gate_b, w_out,
                   ln2_g, ln2_b, ffn2_w_up, ffn2_w_down, ln3_g, ln3_b)
    B, S, D = x.shape
    h = x.reshape(B * S, D)
    for l in range(w_in.shape[0]):
        h = _layer(h, prm, l, B, S)
    return h.reshape(B, S, D)


def _prepare(S, ffn1_w_up, ffn1_w_down, ln1_g, ln1_b, w_in, rw_mu, rw_w0, rw_w2, rw_a0, rw_a2, rw_g2, rw_k_k,
             rw_k_a, rw_r_k, rw_gn_g, rw_gn_b, pool_w, pool_b, pool_scale, nsa_cmp_pe_k, nsa_cmp_pe_v,
             nsa_cmp_k_w1, nsa_cmp_k_w2, nsa_cmp_v_w1, nsa_cmp_v_w2, nsa_gate_b, w_out, ln2_g, ln2_b,
             ffn2_w_up, ffn2_w_down, ln3_g, ln3_b):
    L = w_in.shape[0]

    up = _ffn_up_layout

    down = _ffn_down_layout

    row = lambda v: v[:, None, :]
    f1a, f1b = up(ffn1_w_up)
    f2a, f2b = up(ffn2_w_up)
    w_out_b = w_out.astype(BF16)
    gw = CMP_BLOCK * HEAD_DIM
    pad_w2 = lambda w: jnp.pad(w, ((0, 0), (0, 0), (0, LANES - HEAD_DIM))).astype(BF16)
    pe_rows = lambda pe: jnp.broadcast_to(pe.reshape(L, 1, gw), (L, SUBLANES, gw)).astype(BF16)
    ncp = S // CMP_STRIDE
    cos_t, sin_t = _rope_tables(jnp.arange(S), 4)
    cos_c, sin_c = _rope_tables(jnp.arange(ncp) * CMP_STRIDE + (CMP_BLOCK - 1), 2)
    return dict(
        ffn1=(f1a, f1b, down(ffn1_w_down), row(ln1_g), row(ln1_b)),
        ffn2=(f2a, f2b, down(ffn2_w_down), row(ln3_g), row(ln3_b)),
        w_in=_w_in_layout(w_in),
        w_out=(_parity_major(w_out_b[:, :D_RWKV], 1), w_out_b[:, D_RWKV:D_RWKV + D_POOL],
               w_out_b[:, D_RWKV + D_POOL:]),
        ln2=(row(ln2_g), row(ln2_b)),
        rw=dict(mu=row(rw_mu), w0=row(rw_w0), w2=rw_w2.astype(BF16), a0=row(rw_a0), a2=rw_a2.astype(BF16),
                g2=_parity_major(rw_g2, -1).astype(BF16), k_k=row(rw_k_k), k_a=row(rw_k_a),
                r_k=rw_r_k.reshape(L, 1, D_RWKV), gn_g=row(_parity_major(rw_gn_g, -1)),
                gn_b=row(_parity_major(rw_gn_b, -1)), ones=_block_ones(D_RWKV)),
        cmp=dict(k_w1=nsa_cmp_k_w1.reshape(L, gw, CMP_HIDDEN).astype(BF16), k_w2=pad_w2(nsa_cmp_k_w2),
                 k_pe=pe_rows(nsa_cmp_pe_k),
                 v_w1=nsa_cmp_v_w1.reshape(L, gw, CMP_HIDDEN).astype(BF16), v_w2=nsa_cmp_v_w2.transpose(0, 2, 1).astype(BF16),
                 v_pe=pe_rows(nsa_cmp_pe_v)),
        gate_b=_gate_bias_layout(nsa_gate_b),
        pool=(pool_w.astype(BF16), row(pool_b), row(pool_scale)),
        rope=(cos_t, sin_t), rope_cmp=(cos_c, sin_c), overlap=_overlap_matrix(S, ncp))


def _mixers(p_all, prm, l):
    r, w, k, v, kk, be, bo, g = _rw_prep(p_all, prm["rw"], l)
    y_rw = _rw_scan(r, w, k, v, kk, be, bo, g, prm["rw"], l)
    y_pool = _pool_mix(p_all, *prm["pool"], l)
    q, kc, vc, ks, vs, kw, vw, gates = _nsa_prep(p_all, *prm["rope"], prm["gate_b"], l)
    k_cmp, v_cmp = _nsa_compress(kc, vc, prm["cmp"], *prm["rope_cmp"], l)
    y_nsa = _nsa_attention(q, k_cmp, v_cmp, ks, vs, kw, vw, gates, prm["overlap"])
    return y_rw, y_pool, y_nsa


def _layer(h, prm, l, B, S):
    T = B * S
    h = _ffn_ln(h, *prm["ffn1"], l)
    p_all = _in_proj(h, prm["w_in"], l).reshape(B, S, P_COLS)
    y_rw, y_pool, y_nsa = _mixers(p_all, prm, l)
    h = _out_proj_ln(h, y_rw.reshape(T, D_RWKV), y_pool.reshape(T, D_POOL), y_nsa.reshape(T, D_NSA),
                     prm["w_out"], *prm["ln2"], l)
    return _ffn_ln(h, *prm["ffn2"], l)
```
